```python
import math
import jax, jax.numpy as jnp
from jax import lax
import numpy as np

D_MODEL = 2048
BATCH = 4
SEQ = 2048
DEPTH = 2
DEC_BATCH = 8
DEC_SEQ = 4
PAST_LEN = 16384
PAGE_SIZE = 128

N_BRANCH = 4
RMS_EPS = 1e-6
CONV_WIDTH = 4
POOL_WINDOWS = (2, 4, 8, 16)
POOL_GROUPS = 4
POOL_WIDTH = D_MODEL // 4
POOL_GW = POOL_WIDTH // POOL_GROUPS
POOL_BUF = max(POOL_WINDOWS) - 1
DN_HEADS = 4
DN_DK = 128
DN_DV = 128
DN_QK = DN_HEADS * DN_DK
DN_VW = DN_HEADS * DN_DV
DN_CONV_DIM = 2 * DN_QK + DN_VW
DN_CHUNK = 64
SB_HEADS = 4
SB_HEAD_DIM = 128
SB_WIDTH = SB_HEADS * SB_HEAD_DIM
SB_BLOCK = 128
SB_BIAS_INIT = -6.0
SSM_D_INNER = D_MODEL // 2
SSM_HEAD_DIM = 64
SSM_HEADS = SSM_D_INNER // SSM_HEAD_DIM
SSM_GROUPS = 2
SSM_HPG = SSM_HEADS // SSM_GROUPS
SSM_STATE = 128
SSM_CONV_DIM = SSM_D_INNER + 2 * SSM_GROUPS * SSM_STATE
SSM_CHUNK = 64
D_FF = 4 * D_MODEL
IN_SPLITS = (POOL_WIDTH, DN_CONV_DIM, DN_VW, DN_HEADS, DN_HEADS, 3 * SB_WIDTH, SSM_D_INNER, SSM_CONV_DIM, SSM_HEADS, N_BRANCH * D_MODEL)
IN_COLS = sum(IN_SPLITS)
BRANCH_WIDTHS = (POOL_WIDTH, DN_VW, SB_WIDTH, SSM_D_INNER)
BRANCH_ROWS = sum(BRANCH_WIDTHS)

kernel_name = 'hybrid_pool_delta_stickbreak_ssd_step'


def rmsnorm(x, g):
    xf = x.astype(jnp.float32)
    y = xf * lax.rsqrt(jnp.mean(xf * xf, axis=-1, keepdims=True) + RMS_EPS)
    return (y * g.astype(jnp.float32)).astype(x.dtype)


def split_sizes(a, sizes, axis):
    return jnp.split(a, np.cumsum(sizes)[:-1].tolist(), axis=axis)


def causal_conv_silu(u, buf, w, b):
    L = u.shape[1]
    ext = jnp.concatenate([buf.astype(u.dtype), u], axis=1)
    ef = ext.astype(jnp.float32)
    wf = w.astype(jnp.float32)
    out = sum(ef[:, i:i + L] * wf[i] for i in range(CONV_WIDTH))
    if b is not None:
        out = out + b.astype(jnp.float32)
    return jax.nn.silu(out), ext[:, L:]


def pool_mixer(u, buf, pos0, w_grp, scale):
    B, L, _ = u.shape
    uf = u.astype(jnp.float32)
    ext = jnp.concatenate([buf.astype(jnp.float32), uf], axis=1)
    csum = jnp.concatenate([jnp.zeros((B, 1, POOL_WIDTH), jnp.float32), jnp.cumsum(ext, axis=1)], axis=1)
    hi = csum[:, POOL_BUF + 1:]
    pos = pos0 + jnp.arange(L)
    outs = []
    for gi, w in enumerate(POOL_WINDOWS):
        cols = slice(gi * POOL_GW, (gi + 1) * POOL_GW)
        lo = csum[:, POOL_BUF + 1 - w:POOL_BUF + 1 - w + L, cols]
        cnt = jnp.minimum(pos + 1, w).astype(jnp.float32)[None, :, None]
        outs.append((hi[..., cols] - lo) / cnt - uf[..., cols])
    y = jnp.stack(outs, axis=2)
    y = jnp.einsum('blgc,gcd->blgd', y, w_grp.astype(jnp.float32)).reshape(B, L, POOL_WIDTH)
    y = y * scale.astype(jnp.float32)
    return y.astype(u.dtype), ext[:, L:].astype(u.dtype)


def l2norm(x):
    return x * lax.rsqrt(jnp.sum(x * x, axis=-1, keepdims=True) + 1e-6)


def gated_delta_chunked(q, k, v, g, beta, s0):
    B, L, H, _ = q.shape
    DV = v.shape[-1]
    C = math.gcd(L, DN_CHUNK)
    n = L // C

    def chunks(t):
        return jnp.moveaxis(t.reshape((B, n, C, H) + t.shape[3:]), (1, 3), (0, 2))

    tri = jnp.tril(jnp.ones((C, C), bool))
    stri = jnp.tril(jnp.ones((C, C), bool), -1)

    def step(S, inp):
        qc, kc, vc, gc, bc = inp
        gcum = jnp.cumsum(gc, axis=-1)
        decay = jnp.exp(jnp.where(tri, gcum[..., :, None] - gcum[..., None, :], -jnp.inf))
        kb = kc * bc[..., None]
        lmat = jnp.where(stri, jnp.einsum('bhcd,bhed->bhce', kb, kc) * decay, 0.0)
        rhs = jnp.concatenate([vc * bc[..., None], kb * jnp.exp(gcum)[..., None]], axis=-1)
        sol = lax.linalg.triangular_solve(lmat, rhs, left_side=True, lower=True, unit_diagonal=True)
        u_c, w_c = sol[..., :DV], sol[..., DV:]
        v_new = u_c - jnp.einsum('bhck,bhkv->bhcv', w_c, S)
        attn = jnp.einsum('bhck,bhek->bhce', qc, kc) * decay
        o = jnp.einsum('bhck,bhkv->bhcv', qc * jnp.exp(gcum)[..., None], S) + jnp.einsum('bhce,bhev->bhcv', attn, v_new)
        glast = gcum[..., -1:]
        S = S * jnp.exp(glast)[..., None] + jnp.einsum('bhck,bhcv->bhkv', kc * jnp.exp(glast - gcum)[..., None], v_new)
        return S, o

    S, o = lax.scan(step, s0, (chunks(q), chunks(k), chunks(v), chunks(g), chunks(beta)))
    o = jnp.moveaxis(o, (0, 2), (1, 3)).reshape(B, L, H, DV)
    return o, S


def deltanet_branch(u_qkv, u_z, u_b, u_a, conv_buf, s0, conv_w, a_log, dt_bias, norm_w):
    B, L, _ = u_qkv.shape
    qkv, conv_new = causal_conv_silu(u_qkv, conv_buf, conv_w, None)
    q, k, v = split_sizes(qkv, (DN_QK, DN_QK, DN_VW), -1)
    q = l2norm(q.reshape(B, L, DN_HEADS, DN_DK)) * (DN_DK ** -0.5)
    k = l2norm(k.reshape(B, L, DN_HEADS, DN_DK))
    v = v.reshape(B, L, DN_HEADS, DN_DV)
    beta = jax.nn.sigmoid(u_b.astype(jnp.float32))
    g = -jnp.exp(a_log.astype(jnp.float32)) * jax.nn.softplus(u_a.astype(jnp.float32) + dt_bias.astype(jnp.float32))
    o, s_new = gated_delta_chunked(q, k, v, g, beta, s0.astype(jnp.float32))
    zf = u_z.astype(jnp.float32).reshape(B, L, DN_HEADS, DN_DV)
    o = rmsnorm(o, norm_w) * jax.nn.silu(zf)
    dt_ = u_qkv.dtype
    return o.reshape(B, L, DN_VW).astype(dt_), conv_new.astype(dt_), s_new.astype(dt_)


def ssd_chunked(x, dt, a, bm, cm, h0):
    B, L = x.shape[:2]
    C = math.gcd(L, SSM_CHUNK)
    n = L // C

    def chunks(t):
        return jnp.moveaxis(t.reshape((B, n, C) + t.shape[2:]), 1, 0)

    tri = jnp.tril(jnp.ones((C, C), bool))[None, :, :, None, None]

    def step(h, inp):
        xc, dtc, bc, cc = inp
        cum = jnp.cumsum(dtc * a, axis=1)
        lmat = jnp.exp(jnp.where(tri, cum[:, :, None] - cum[:, None, :], -jnp.inf))
        cb = jnp.einsum('btgn,bsgn->btsg', cc, bc)
        wts = cb[..., None] * lmat * dtc[:, None]
        y = jnp.einsum('btsgr,bsgrp->btgrp', wts, xc)
        y = y + jnp.einsum('btgn,bgrpn->btgrp', cc, h) * jnp.exp(cum)[..., None]
        last = cum[:, -1]
        h = h * jnp.exp(last)[..., None, None] + jnp.einsum('bsgr,bsgrp,bsgn->bgrpn', jnp.exp(last[:, None] - cum) * dtc, xc, bc)
        return h, y

    h, y = lax.scan(step, h0, (chunks(x), chunks(dt), chunks(bm), chunks(cm)))
    return jnp.moveaxis(y, 0, 1).reshape(x.shape), h


def mamba_branch(u_z, u_xbc, u_dt, conv_buf, h0, conv_w, conv_b, a_log, dt_bias, d_skip, norm_w):
    B, L, _ = u_xbc.shape
    xbc, conv_new = causal_conv_silu(u_xbc, conv_buf, conv_w, conv_b)
    xs, bm, cm = split_sizes(xbc, (SSM_D_INNER, SSM_GROUPS * SSM_STATE, SSM_GROUPS * SSM_STATE), -1)
    xs = xs.reshape(B, L, SSM_GROUPS, SSM_HPG, SSM_HEAD_DIM)
    bm = bm.reshape(B, L, SSM_GROUPS, SSM_STATE)
    cm = cm.reshape(B, L, SSM_GROUPS, SSM_STATE)
    dt = jax.nn.softplus(u_dt.astype(jnp.float32) + dt_bias.astype(jnp.float32)).reshape(B, L, SSM_GROUPS, SSM_HPG)
    a = -jnp.exp(a_log.astype(jnp.float32)).reshape(SSM_GROUPS, SSM_HPG)
    h0f = h0.astype(jnp.float32).reshape(B, SSM_GROUPS, SSM_HPG, SSM_HEAD_DIM, SSM_STATE)
    y, h = ssd_chunked(xs, dt, a, bm, cm, h0f)
    y = y + d_skip.astype(jnp.float32).reshape(SSM_GROUPS, SSM_HPG)[..., None] * xs
    y = y.reshape(B, L, SSM_D_INNER) * jax.nn.silu(u_z.astype(jnp.float32))
    gw = SSM_D_INNER // SSM_GROUPS
    y = rmsnorm(y.reshape(B, L, SSM_GROUPS, gw), norm_w.reshape(SSM_GROUPS, gw)).reshape(B, L, SSM_D_INNER)
    dt_ = u_xbc.dtype
    return y.astype(dt_), conv_new.astype(dt_), h.reshape(B, SSM_HEADS, SSM_HEAD_DIM, SSM_STATE).astype(dt_)


def stick_breaking_block(q, qpos, k, v, kpos, bias):
    z = jnp.einsum('bqhd,bkhd->bhqk', q.astype(jnp.float32), k.astype(jnp.float32)) * (SB_HEAD_DIM ** -0.5)
    z = z + bias.astype(jnp.float32)[None, :, None, None]
    valid = kpos[None, :] < qpos[:, None]
    la = jnp.where(valid, jax.nn.log_sigmoid(-z), 0.0)
    surv = lax.cumsum(la, axis=3, reverse=True) - la
    att = jnp.where(valid, jnp.exp(jax.nn.log_sigmoid(z) + surv), 0.0)
    return jnp.einsum('bhqk,bkhd->bqhd', att, v.astype(jnp.float32))


def stick_breaking(q, k, v, qpos, kpos, bias):
    B, Lq, H, Dh = q.shape
    blk = SB_BLOCK if Lq % SB_BLOCK == 0 else Lq
    nb = Lq // blk
    qb = jnp.moveaxis(q.reshape(B, nb, blk, H, Dh), 1, 0)
    pb = qpos.reshape(nb, blk)
    o = lax.map(lambda qp: stick_breaking_block(qp[0], qp[1], k, v, kpos, bias), (qb, pb))
    return jnp.moveaxis(o, 0, 1).reshape(B, Lq, H * Dh)


def trunk_layer(x, past_k, past_v, pool_buf, dn_conv, dn_s, ssm_conv, ssm_h, lw):
    (n_mix_pre, n_mix_post, n_mlp_pre, n_mlp_post, w_in, pool_w, pool_scale,
     dn_conv_w, dn_a_log, dn_dt_bias, dn_norm_w, sb_bias, ssm_conv_w, ssm_conv_b,
     ssm_a_log, ssm_dt_bias, ssm_d, ssm_norm_w, w_branch, w_out, w_up, w_down) = lw
    B, L, _ = x.shape
    past = 0 if past_k is None else past_k.shape[1]
    h = rmsnorm(x, n_mix_pre)
    u = jnp.einsum('bld,dc->blc', h, w_in)
    (u_pool, u_dn_qkv, u_dn_z, u_dn_b, u_dn_a, u_sb, u_ss_z, u_ss_xbc, u_ss_dt, u_gate) = split_sizes(u, IN_SPLITS, -1)
    o_pool, pool_new = pool_mixer(u_pool, pool_buf, past, pool_w, pool_scale)
    o_dn, dn_conv_new, dn_s_new = deltanet_branch(u_dn_qkv, u_dn_z, u_dn_b, u_dn_a, dn_conv, dn_s, dn_conv_w, dn_a_log, dn_dt_bias, dn_norm_w)
    q, k, v = [t.reshape(B, L, SB_HEADS, SB_HEAD_DIM) for t in jnp.split(u_sb, 3, axis=-1)]
    k_all = k if past_k is None else jnp.concatenate([past_k.astype(k.dtype), k], axis=1)
    v_all = v if past_v is None else jnp.concatenate([past_v.astype(v.dtype), v], axis=1)
    o_sb = stick_breaking(q, k_all, v_all, past + jnp.arange(L), jnp.arange(past + L), sb_bias).astype(x.dtype)
    o_ss, ss_conv_new, ss_h_new = mamba_branch(u_ss_z, u_ss_xbc, u_ss_dt, ssm_conv, ssm_h, ssm_conv_w, ssm_conv_b, ssm_a_log, ssm_dt_bias, ssm_d, ssm_norm_w)
    gates = jax.nn.sigmoid(u_gate.astype(jnp.float32)).reshape(B, L, N_BRANCH, D_MODEL)
    w_rows = split_sizes(w_branch, BRANCH_WIDTHS, 0)
    outs = (o_pool, o_dn, o_sb, o_ss)
    merged = sum(gates[:, :, i] * jnp.einsum('blc,cd->bld', o, wr).astype(jnp.float32) for i, (o, wr) in enumerate(zip(outs, w_rows)))
    mix = jnp.einsum('bld,de->ble', merged.astype(x.dtype), w_out)
    x = x + rmsnorm(mix, n_mix_post)
    h2 = rmsnorm(x, n_mlp_pre)
    f = jnp.einsum('blf,fd->bld', jnp.square(jax.nn.relu(jnp.einsum('bld,df->blf', h2, w_up))), w_down)
    x = x + rmsnorm(f, n_mlp_post)
    return x, (k, v, pool_new, dn_conv_new, dn_s_new, ss_conv_new, ss_h_new)


def _dt_bias_init(key, shape):
    dt = jnp.exp(jax.random.uniform(key, shape, jnp.float32, minval=math.log(1e-3), maxval=math.log(1e-1)))
    return dt + jnp.log(-jnp.expm1(-dt))


def setup_inputs(seed: int = 0) -> dict:
    key = jax.random.key(seed)
    ks = jax.random.split(key, 40)
    kit = iter(range(40))
    f32 = jnp.float32

    def nrm(shape, s):
        return jax.random.normal(ks[next(kit)], shape, f32) * s

    n_pages = PAST_LEN // PAGE_SIZE
    n_used = DEC_BATCH * n_pages
    n_pool = n_used + n_used // 4
    x_prompt = nrm((BATCH, SEQ, D_MODEL), 1.0)
    x_sample = nrm((DEC_BATCH, DEC_SEQ, D_MODEL), 1.0)
    cache_sb_k = nrm((DEPTH, n_pool, PAGE_SIZE, SB_HEADS, SB_HEAD_DIM), 1.0)
    cache_sb_v = nrm((DEPTH, n_pool, PAGE_SIZE, SB_HEADS, SB_HEAD_DIM), 1.0)
    state_pool = nrm((DEPTH, DEC_BATCH, POOL_BUF, POOL_WIDTH), 1.0)
    state_dn_conv = nrm((DEPTH, DEC_BATCH, CONV_WIDTH - 1, DN_CONV_DIM), 1.0)
    state_dn_s = nrm((DEPTH, DEC_BATCH, DN_HEADS, DN_DK, DN_DV), 0.1)
    state_ssm_conv = nrm((DEPTH, DEC_BATCH, CONV_WIDTH - 1, SSM_CONV_DIM), 1.0)
    state_ssm_h = nrm((DEPTH, DEC_BATCH, SSM_HEADS, SSM_HEAD_DIM, SSM_STATE), 0.1)
    page_table = jax.random.permutation(ks[next(kit)], n_pool)[:n_used].reshape(DEC_BATCH, n_pages).astype(jnp.int32)
    norm_mix_pre = 1.0 + nrm((DEPTH, D_MODEL), 0.02)
    norm_mix_post = 1.0 + nrm((DEPTH, D_MODEL), 0.02)
    norm_mlp_pre = 1.0 + nrm((DEPTH, D_MODEL), 0.02)
    norm_mlp_post = 1.0 + nrm((DEPTH, D_MODEL), 0.02)
    w_in = nrm((DEPTH, D_MODEL, IN_COLS), D_MODEL ** -0.5)
    pool_w = nrm((DEPTH, POOL_GROUPS, POOL_GW, POOL_GW), POOL_GW ** -0.5)
    pool_scale = 1.0 + nrm((DEPTH, POOL_WIDTH), 0.1)
    dn_conv_w = nrm((DEPTH, CONV_WIDTH, DN_CONV_DIM), CONV_WIDTH ** -0.5)
    dn_a_log = jnp.log(jax.random.uniform(ks[next(kit)], (DEPTH, DN_HEADS), f32, minval=1.0, maxval=16.0))
    dn_dt_bias = _dt_bias_init(ks[next(kit)], (DEPTH, DN_HEADS))
    dn_norm_w = 1.0 + nrm((DEPTH, DN_DV), 0.02)
    sb_bias = SB_BIAS_INIT + nrm((DEPTH, SB_HEADS), 0.5)
    ssm_conv_w = nrm((DEPTH, CONV_WIDTH, SSM_CONV_DIM), CONV_WIDTH ** -0.5)
    ssm_conv_b = nrm((DEPTH, SSM_CONV_DIM), 0.01)
    ssm_a_log = jnp.log(jax.random.uniform(ks[next(kit)], (DEPTH, SSM_HEADS), f32, minval=1.0, maxval=16.0))
    ssm_dt_bias = _dt_bias_init(ks[next(kit)], (DEPTH, SSM_HEADS))
    ssm_d = 1.0 + nrm((DEPTH, SSM_HEADS), 0.1)
    ssm_norm_w = 1.0 + nrm((DEPTH, SSM_D_INNER), 0.02)
    w_branch = nrm((DEPTH, BRANCH_ROWS, D_MODEL), POOL_WIDTH ** -0.5)
    w_out = nrm((DEPTH, D_MODEL, D_MODEL), D_MODEL ** -0.5)
    w_up = nrm((DEPTH, D_MODEL, D_FF), D_MODEL ** -0.5)
    w_down = nrm((DEPTH, D_FF, D_MODEL), D_FF ** -0.5)
    return {'x_prompt': x_prompt, 'x_sample': x_sample, 'cache_sb_k': cache_sb_k, 'cache_sb_v': cache_sb_v,
            'state_pool': state_pool, 'state_dn_conv': state_dn_conv, 'state_dn_s': state_dn_s,
            'state_ssm_conv': state_ssm_conv, 'state_ssm_h': state_ssm_h, 'page_table': page_table,
            'norm_mix_pre': norm_mix_pre, 'norm_mix_post': norm_mix_post, 'norm_mlp_pre': norm_mlp_pre,
            'norm_mlp_post': norm_mlp_post, 'w_in': w_in, 'pool_w': pool_w, 'pool_scale': pool_scale,
            'dn_conv_w': dn_conv_w, 'dn_a_log': dn_a_log, 'dn_dt_bias': dn_dt_bias, 'dn_norm_w': dn_norm_w,
            'sb_bias': sb_bias,
            'ssm_conv_w': ssm_conv_w, 'ssm_conv_b': ssm_conv_b, 'ssm_a_log': ssm_a_log, 'ssm_dt_bias': ssm_dt_bias,
            'ssm_d': ssm_d, 'ssm_norm_w': ssm_norm_w, 'w_branch': w_branch, 'w_out': w_out,
            'w_up': w_up, 'w_down': w_down}


def reference(x_prompt, x_sample, cache_sb_k, cache_sb_v, state_pool, state_dn_conv, state_dn_s,
              state_ssm_conv, state_ssm_h, page_table, norm_mix_pre, norm_mix_post, norm_mlp_pre,
              norm_mlp_post, w_in, pool_w, pool_scale, dn_conv_w, dn_a_log, dn_dt_bias, dn_norm_w,
              sb_bias, ssm_conv_w, ssm_conv_b, ssm_a_log, ssm_dt_bias, ssm_d, ssm_norm_w, w_branch, w_out,
              w_up, w_down):
    bp = x_prompt.shape[0]
    bs = page_table.shape[0]
    dt_ = x_prompt.dtype
    zero_states = (jnp.zeros((bp, POOL_BUF, POOL_WIDTH), dt_),
                   jnp.zeros((bp, CONV_WIDTH - 1, DN_CONV_DIM), dt_),
                   jnp.zeros((bp, DN_HEADS, DN_DK, DN_DV), dt_),
                   jnp.zeros((bp, CONV_WIDTH - 1, SSM_CONV_DIM), dt_),
                   jnp.zeros((bp, SSM_HEADS, SSM_HEAD_DIM, SSM_STATE), dt_))
    y_prompt = x_prompt
    y_sample = x_sample
    new_p = []
    new_s = []
    for l in range(DEPTH):
        lw = (norm_mix_pre[l], norm_mix_post[l], norm_mlp_pre[l], norm_mlp_post[l], w_in[l], pool_w[l],
              pool_scale[l], dn_conv_w[l], dn_a_log[l], dn_dt_bias[l], dn_norm_w[l], sb_bias[l], ssm_conv_w[l],
              ssm_conv_b[l], ssm_a_log[l], ssm_dt_bias[l], ssm_d[l], ssm_norm_w[l], w_branch[l],
              w_out[l], w_up[l], w_down[l])
        y_prompt, st_p = trunk_layer(y_prompt, None, None, *zero_states, lw)
        past_k = jnp.take(cache_sb_k[l], page_table, axis=0).reshape(bs, -1, SB_HEADS, SB_HEAD_DIM)
        past_v = jnp.take(cache_sb_v[l], page_table, axis=0).reshape(bs, -1, SB_HEADS, SB_HEAD_DIM)
        y_sample, st_s = trunk_layer(y_sample, past_k, past_v, state_pool[l], state_dn_conv[l], state_dn_s[l],
                                     state_ssm_conv[l], state_ssm_h[l], lw)
        new_p.append(st_p)
        new_s.append(st_s)
    k_p, v_p, pool_p, dnc_p, dns_p, ssc_p, ssh_p = [jnp.stack(t) for t in zip(*new_p)]
    k_s, v_s, pool_s, dnc_s, dns_s, ssc_s, ssh_s = [jnp.stack(t) for t in zip(*new_s)]
    return (y_prompt, y_sample, k_p, v_p, pool_p, dnc_p, dns_p, ssc_p, ssh_p,
            k_s, v_s, pool_s, dnc_s, dns_s, ssc_s, ssh_s)
```

```python
import functools
import math

import jax
import jax.numpy as jnp
from jax import lax
from jax.experimental import pallas as pl
from jax.experimental.pallas import tpu as pltpu

F32 = jnp.float32
BF16 = jnp.bfloat16

D_MODEL = 2048
N_BRANCH = 4
RMS_EPS = 1e-6
CONV_WIDTH = 4
POOL_WINDOWS = (2, 4, 8, 16)
POOL_WIDTH = D_MODEL // 4
POOL_GW = POOL_WIDTH // 4
POOL_BUF = max(POOL_WINDOWS) - 1
DN_HEADS = 4
DN_DK = 128
DN_DV = 128
DN_QK = DN_HEADS * DN_DK
DN_VW = DN_HEADS * DN_DV
DN_CONV_DIM = 2 * DN_QK + DN_VW
DN_CHUNK = 64
SB_HEADS = 4
SB_HEAD_DIM = 128
SB_WIDTH = SB_HEADS * SB_HEAD_DIM
SSM_D_INNER = D_MODEL // 2
SSM_HEAD_DIM = 64
SSM_HEADS = SSM_D_INNER // SSM_HEAD_DIM
SSM_GROUPS = 2
SSM_HPG = SSM_HEADS // SSM_GROUPS
SSM_STATE = 128
SSM_CONV_DIM = SSM_D_INNER + 2 * SSM_GROUPS * SSM_STATE
SSM_CHUNK = 64
D_FF = 4 * D_MODEL

LANES = 128
SUBLANES = 8

C_POOL = 0
C_DNQ, C_DNK, C_DNV, C_DNZ = 512, 1024, 1536, 2048
C_SBQ, C_SBK, C_SBV = 2560, 3072, 3584
C_SSZ, C_SSX, C_SSBC = 4096, 5120, 6144
C_GATE = 6656
C_SMALL = 14848
NU = 15360
_O_DNB = POOL_WIDTH + DN_CONV_DIM + DN_VW
_O_SB = _O_DNB + 2 * DN_HEADS
_O_SSDT = _O_SB + 3 * SB_WIDTH + SSM_D_INNER + SSM_CONV_DIM
_O_GATE = _O_SSDT + SSM_HEADS
_O_END = _O_GATE + N_BRANCH * D_MODEL
SM_BETA, SM_DECAY, SM_DT = 0, DN_HEADS, 2 * DN_HEADS

VMEM_LIMIT_MB = 56


def _cparams(sem, vmem_mb=VMEM_LIMIT_MB):
    return pltpu.CompilerParams(dimension_semantics=sem, vmem_limit_bytes=vmem_mb * 1024 * 1024)


def _sigmoid(x):
    return 1.0 / (1.0 + jnp.exp(-x))


def _silu(x):
    return x * _sigmoid(x)


def _softplus(x):
    return jnp.maximum(x, 0.0) + jnp.log1p(jnp.exp(-jnp.abs(x)))


def _log_sigmoid(x):
    return jnp.minimum(x, 0.0) - jnp.log1p(jnp.exp(-jnp.abs(x)))


def _dot(a, b):
    return jnp.dot(a.astype(BF16), b.astype(BF16), preferred_element_type=F32)


def _dot_nt(a, b):
    return lax.dot_general(a.astype(BF16), b.astype(BF16), (((1,), (1,)), ((), ())), preferred_element_type=F32)


def _dot_tn(a, b):
    return lax.dot_general(a.astype(BF16), b.astype(BF16), (((0,), (0,)), ((), ())), preferred_element_type=F32)


def _split2(a):
    hi = a.astype(BF16)
    lo = (a - hi.astype(F32)).astype(BF16)
    return hi, lo


def _dot3(a, b):
    ah, al = _split2(a)
    bh, bl = _split2(b)
    d = lambda x, y: jnp.dot(x, y, preferred_element_type=F32)
    return d(ah, bh) + (d(ah, bl) + d(al, bh))


def _rms_rows(x, g):
    return x * lax.rsqrt(jnp.mean(x * x, axis=-1, keepdims=True) + RMS_EPS) * g


def _norm_rows_to(h_ref, x_ref, g_ref, tm):
    ch = min(tm, 256)

    def body(r, c):
        rs = pl.ds(pl.multiple_of(r * ch, ch), ch)
        h_ref[rs, :] = _rms_rows(x_ref[rs, :], g_ref[...]).astype(h_ref.dtype)
        return c

    lax.fori_loop(0, tm // ch, body, 0)


def _inproj_kernel(x_ref, g_ref, w_ref, o_ref, h_ref, *, tm):
    @pl.when(pl.program_id(1) == 0)
    def _():
        _norm_rows_to(h_ref, x_ref, g_ref, tm)

    o_ref[...] = jnp.dot(h_ref[...], w_ref[...], preferred_element_type=F32)


def _inproj(x2, g, w_p, *, tm, tn=1024):
    m = x2.shape[0]
    return pl.pallas_call(
        functools.partial(_inproj_kernel, tm=tm),
        grid=(m // tm, NU // tn),
        in_specs=[
            pl.BlockSpec((tm, D_MODEL), lambda i, j: (i, 0)),
            pl.BlockSpec((1, D_MODEL), lambda i, j: (0, 0)),
            pl.BlockSpec((D_MODEL, tn), lambda i, j: (0, j)),
        ],
        out_specs=pl.BlockSpec((tm, tn), lambda i, j: (i, j)),
        out_shape=jax.ShapeDtypeStruct((m, NU), F32),
        scratch_shapes=[pltpu.VMEM((tm, D_MODEL), BF16)],
        compiler_params=_cparams(("parallel", "arbitrary")),
        name="inproj",
    )(x2, g, w_p)


def _pool_kernel(u_ref, buf_ref, w_ref, sc_ref, o_ref, new_ref, ext_ref, *, lin, lp, pos0):
    ext_ref[0:16, :] = jnp.zeros((16, POOL_WIDTH), F32)
    ext_ref[1:16, :] = buf_ref[0]
    if lin < lp:
        ext_ref[16:16 + lp, :] = jnp.zeros((lp, POOL_WIDTH), F32)
    ext_ref[16:16 + lin, :] = u_ref[0]
    ch = min(lp, 256)
    for c0 in range(0, lp, ch):
        pos = pos0 + c0 + lax.broadcasted_iota(jnp.int32, (ch, 1), 0)
        for gi, w in enumerate(POOL_WINDOWS):
            cols = slice(gi * POOL_GW, (gi + 1) * POOL_GW)
            s = ext_ref[16 + c0:16 + c0 + ch, cols]
            tot = s
            for k in range(1, w):
                tot = tot + ext_ref[16 + c0 - k:16 + c0 - k + ch, cols]
            cnt = jnp.minimum(pos + 1, w).astype(F32)
            y = tot / cnt - s
            yo = _dot(y, w_ref[gi]) * sc_ref[:, cols]
            n = min(ch, lin - c0)
            o_ref[0, c0:c0 + n, cols] = yo[:n]
    new_ref[0] = ext_ref[1 + lin:16 + lin, :]


def _pool(u3, buf, w, sc, *, pos0):
    b, lin, _ = u3.shape
    lp = max(lin, SUBLANES)
    return pl.pallas_call(
        functools.partial(_pool_kernel, lin=lin, lp=lp, pos0=pos0),
        grid=(b,),
        in_specs=[
            pl.BlockSpec((1, lin, POOL_WIDTH), lambda i: (i, 0, C_POOL // POOL_WIDTH)),
            pl.BlockSpec((1, POOL_BUF, POOL_WIDTH), lambda i: (i, 0, 0)),
            pl.BlockSpec((4, POOL_GW, POOL_GW), lambda i: (0, 0, 0)),
            pl.BlockSpec((1, POOL_WIDTH), lambda i: (0, 0)),
        ],
        out_specs=[
            pl.BlockSpec((1, lin, POOL_WIDTH), lambda i: (i, 0, 0)),
            pl.BlockSpec((1, POOL_BUF, POOL_WIDTH), lambda i: (i, 0, 0)),
        ],
        out_shape=[
            jax.ShapeDtypeStruct((b, lin, POOL_WIDTH), F32),
            jax.ShapeDtypeStruct((b, POOL_BUF, POOL_WIDTH), F32),
        ],
        scratch_shapes=[pltpu.VMEM((16 + lp, POOL_WIDTH), F32)],
        compiler_params=_cparams(("parallel",)),
        name="pool",
    )(u3, buf, w, sc)


ROWS = 64


def _fill_ext(ext_ref, cbuf_ref, parts, *, lin, lp):
    l = pl.program_id(1)

    @pl.when(l == 0)
    def _():
        ext_ref[5:8, :] = cbuf_ref[0]

    @pl.when(l > 0)
    def _():
        ext_ref[5:8, :] = ext_ref[5 + lin:8 + lin, :]

    if lin < lp:
        ext_ref[8:8 + lp, :] = jnp.zeros((lp, ext_ref.shape[1]), F32)
    for ref, c0, width in parts:
        ext_ref[8:8 + lin, c0:c0 + width] = ref[0]


def _conv_silu(ext_ref, r0, n, cols, w_ref, b_ref):
    acc = None
    for i in range(CONV_WIDTH):
        part = ext_ref[5 + r0 + i:5 + r0 + i + n, cols] * w_ref[i:i + 1, cols]
        acc = part if acc is None else acc + part
    if b_ref is not None:
        acc = acc + b_ref[:, cols]
    return _silu(acc)


def _pad_rows(dst_ref, src_ref, *, lin, lp):
    if lin < lp:
        dst_ref[...] = jnp.zeros(dst_ref.shape, F32)
    dst_ref[0:lin, :] = src_ref[0]


def _row_valid(r0, n, lin):
    if r0 + n <= lin:
        return None
    return (r0 + lax.broadcasted_iota(jnp.int32, (n, 1), 0)) < lin


def _chunk_cumsum(x, chunk):
    rin = lax.broadcasted_iota(jnp.int32, x.shape, 0) % chunk
    s = 1
    while s < chunk:
        x = x + jnp.where(rin >= s, pltpu.roll(x, s, 0), 0.0)
        s *= 2
    return x


def _tri_inv(at):
    c = at.shape[0]
    ng = c // SUBLANES
    rid = lax.broadcasted_iota(jnp.int32, (SUBLANES, c), 0)
    cid = lax.broadcasted_iota(jnp.int32, (SUBLANES, c), 1)
    tg = [(cid == rid + g * SUBLANES).astype(F32) for g in range(ng)]
    for i in range(1, c):
        gi, ri = divmod(i, SUBLANES)
        acc = None
        for g in range(gi + 1):
            term = at[g * SUBLANES:(g + 1) * SUBLANES, i:i + 1] * tg[g]
            acc = term if acc is None else acc + term
        r = jnp.sum(acc, axis=0, keepdims=True)
        tg[gi] = jnp.where(rid == ri, tg[gi] - r, tg[gi])
    return jnp.concatenate(tg, axis=0)


def _dn_kernel(q_ref, k_ref, v_ref, z_ref, sm_ref, cbuf_ref, s0_ref, cw_ref, par_ref, nw_ref,
               o_ref, cnew_ref, snew_ref,
               ext_ref, qn_ref, kn_ref, vv_ref, zz_ref, smp_ref, be_ref, gc_ref, gt_ref, oo_ref, s_ref,
               *, lin, lp, chunk):
    l = pl.program_id(1)
    nch = lp // chunk

    @pl.when(l == 0)
    def _():
        s_ref[...] = s0_ref[0]

    _fill_ext(ext_ref, cbuf_ref, ((q_ref, 0, DN_QK), (k_ref, DN_QK, DN_QK), (v_ref, 2 * DN_QK, DN_VW)), lin=lin, lp=lp)
    _pad_rows(zz_ref, z_ref, lin=lin, lp=lp)
    _pad_rows(smp_ref, sm_ref, lin=lin, lp=lp)

    for r0 in range(0, lp, ROWS):
        valid = _row_valid(r0, ROWS, lin)
        for part, dst in enumerate((qn_ref, kn_ref, vv_ref)):
            for h in range(DN_HEADS):
                cols = slice(part * DN_QK + h * DN_DK, part * DN_QK + (h + 1) * DN_DK)
                y = _conv_silu(ext_ref, r0, ROWS, cols, cw_ref, None)
                if part < 2:
                    y = y * lax.rsqrt(jnp.sum(y * y, axis=-1, keepdims=True) + 1e-6)
                if part == 0:
                    y = y * (DN_DK ** -0.5)
                if valid is not None:
                    y = jnp.where(valid, y, 0.0)
                dst[r0:r0 + ROWS, h * DN_DK:(h + 1) * DN_DK] = y

    sm = smp_ref[...]
    beta = _sigmoid(sm)
    g = -jnp.exp(par_ref[0:1, :]) * _softplus(sm + par_ref[1:2, :])
    valid = _row_valid(0, lp, lin)
    if valid is not None:
        beta = jnp.where(valid, beta, 0.0)
        g = jnp.where(valid, g, 0.0)
    gcum = _chunk_cumsum(g, chunk)
    be_ref[...] = beta
    gc_ref[...] = gcum
    gt = gcum.T
    for c in range(nch):
        gt_ref[c] = gt[:, c * chunk:(c + 1) * chunk]

    rid = lax.broadcasted_iota(jnp.int32, (chunk, chunk), 0)
    cid = lax.broadcasted_iota(jnp.int32, (chunk, chunk), 1)
    upper = cid > rid
    lower = cid <= rid

    def chunk_body(c, carry):
        r0 = pl.multiple_of(c * chunk, chunk)
        rs = pl.ds(r0, chunk)
        gcb = gc_ref[rs, :]
        beb = be_ref[rs, :]
        glast_row = gc_ref[pl.ds(r0 + chunk - 1, 1), :]
        for h in range(DN_HEADS):
            hs = slice(h * DN_DK, (h + 1) * DN_DK)
            qh = qn_ref[rs, hs]
            kh = kn_ref[rs, hs]
            vh = vv_ref[rs, hs]
            gc = gcb[:, SM_DECAY + h:SM_DECAY + h + 1]
            bt = beb[:, SM_BETA + h:SM_BETA + h + 1]
            gr = gt_ref[c, SM_DECAY + h:SM_DECAY + h + 1, :]
            glast = glast_row[:, SM_DECAY + h:SM_DECAY + h + 1]
            kb = kh * bt
            at = _dot_nt(kh, kb) * jnp.exp(jnp.where(upper, gr - gc, -jnp.inf))
            t = _tri_inv(at)
            eg = jnp.exp(gc)
            sol = _dot3(t, jnp.concatenate([vh * bt, kb * eg], axis=1))
            u_c = sol[:, :DN_DV]
            w_c = sol[:, DN_DV:]
            sh = s_ref[h]
            v_new = u_c - _dot(w_c, sh)
            attn = _dot_nt(qh, kh) * jnp.exp(jnp.where(lower, gc - gr, -jnp.inf))
            o = _dot(qh * eg, sh) + _dot(attn, v_new)
            s_ref[h] = sh * jnp.exp(glast) + _dot_tn(kh * jnp.exp(glast - gc), v_new)
            zh = zz_ref[rs, hs]
            oo_ref[rs, hs] = _rms_rows(o, nw_ref[...]) * _silu(zh)
        return carry

    lax.fori_loop(0, nch, chunk_body, 0)
    o_ref[0] = oo_ref[0:lin, :]

    @pl.when(l == pl.num_programs(1) - 1)
    def _():
        cnew_ref[0] = ext_ref[5 + lin:8 + lin, :]
        snew_ref[0] = s_ref[...]


def _deltanet(u3, cbuf, s0, cw, par, nw, *, lc, lp):
    b, l, _ = u3.shape
    nl = l // lc
    blk = lambda c0: pl.BlockSpec((1, lc, 512), lambda i, j: (i, j, c0 // 512))
    return pl.pallas_call(
        functools.partial(_dn_kernel, lin=lc, lp=lp, chunk=DN_CHUNK),
        grid=(b, nl),
        in_specs=[
            blk(C_DNQ), blk(C_DNK), blk(C_DNV), blk(C_DNZ),
            pl.BlockSpec((1, lc, LANES), lambda i, j: (i, j, C_SMALL // LANES)),
            pl.BlockSpec((1, CONV_WIDTH - 1, DN_CONV_DIM), lambda i, j: (i, 0, 0)),
            pl.BlockSpec((1, DN_HEADS, DN_DK, DN_DV), lambda i, j: (i, 0, 0, 0)),
            pl.BlockSpec((CONV_WIDTH, DN_CONV_DIM), lambda i, j: (0, 0)),
            pl.BlockSpec((SUBLANES, LANES), lambda i, j: (0, 0)),
            pl.BlockSpec((1, DN_DV), lambda i, j: (0, 0)),
        ],
        out_specs=[
            pl.BlockSpec((1, lc, DN_VW), lambda i, j: (i, j, 0)),
            pl.BlockSpec((1, CONV_WIDTH - 1, DN_CONV_DIM), lambda i, j: (i, 0, 0)),
            pl.BlockSpec((1, DN_HEADS, DN_DK, DN_DV), lambda i, j: (i, 0, 0, 0)),
        ],
        out_shape=[
            jax.ShapeDtypeStruct((b, l, DN_VW), F32),
            jax.ShapeDtypeStruct((b, CONV_WIDTH - 1, DN_CONV_DIM), F32),
            jax.ShapeDtypeStruct((b, DN_HEADS, DN_DK, DN_DV), F32),
        ],
        scratch_shapes=[
            pltpu.VMEM((8 + lp, DN_CONV_DIM), F32),
            pltpu.VMEM((lp, DN_QK), F32), pltpu.VMEM((lp, DN_QK), F32), pltpu.VMEM((lp, DN_VW), F32),
            pltpu.VMEM((lp, DN_VW), F32), pltpu.VMEM((lp, LANES), F32),
            pltpu.VMEM((lp, LANES), F32), pltpu.VMEM((lp, LANES), F32),
            pltpu.VMEM((lp // DN_CHUNK, LANES, DN_CHUNK), F32),
            pltpu.VMEM((lp, DN_VW), F32),
            pltpu.VMEM((DN_HEADS, DN_DK, DN_DV), F32),
        ],
        compiler_params=_cparams(("parallel", "arbitrary")),
        name="deltanet",
    )(u3, u3, u3, u3, u3, cbuf, s0, cw, par, nw)


def _ssd_kernel(z_ref, x_ref, bc_ref, sm_ref, cbuf_ref, h0_ref, cw_ref, cb_ref, par_ref, nw_ref, d_ref,
                o_ref, cnew_ref, hnew_ref,
                ext_ref, xs_ref, bm_ref, cm_ref, zz_ref, smp_ref, dt_ref, cum_ref, ct_ref, yy_ref, hs_ref,
                *, lin, lp, chunk):
    l = pl.program_id(1)
    nch = lp // chunk
    gn = SSM_GROUPS * SSM_STATE

    @pl.when(l == 0)
    def _():
        hs_ref[...] = h0_ref[0]

    _fill_ext(ext_ref, cbuf_ref, ((x_ref, 0, SSM_D_INNER), (bc_ref, SSM_D_INNER, 2 * gn)), lin=lin, lp=lp)
    _pad_rows(zz_ref, z_ref, lin=lin, lp=lp)
    _pad_rows(smp_ref, sm_ref, lin=lin, lp=lp)

    for r0 in range(0, lp, ROWS):
        valid = _row_valid(r0, ROWS, lin)
        for c0 in range(0, SSM_CONV_DIM, LANES):
            y = _conv_silu(ext_ref, r0, ROWS, slice(c0, c0 + LANES), cw_ref, cb_ref)
            if valid is not None:
                y = jnp.where(valid, y, 0.0)
            if c0 < SSM_D_INNER:
                xs_ref[r0:r0 + ROWS, c0:c0 + LANES] = y
            elif c0 < SSM_D_INNER + gn:
                bm_ref[r0:r0 + ROWS, c0 - SSM_D_INNER:c0 - SSM_D_INNER + LANES] = y
            else:
                cm_ref[r0:r0 + ROWS, c0 - SSM_D_INNER - gn:c0 - SSM_D_INNER - gn + LANES] = y

    sm = smp_ref[...]
    dt = _softplus(sm + par_ref[1:2, :])
    valid = _row_valid(0, lp, lin)
    if valid is not None:
        dt = jnp.where(valid, dt, 0.0)
    cum = _chunk_cumsum(dt * (-jnp.exp(par_ref[0:1, :])), chunk)
    dt_ref[...] = dt
    cum_ref[...] = cum
    ct = cum.T
    for c in range(nch):
        ct_ref[c] = ct[:, c * chunk:(c + 1) * chunk]

    rid = lax.broadcasted_iota(jnp.int32, (chunk, chunk), 0)
    cid = lax.broadcasted_iota(jnp.int32, (chunk, chunk), 1)
    lower = cid <= rid

    def chunk_body(c, carry):
        r0 = pl.multiple_of(c * chunk, chunk)
        rs = pl.ds(r0, chunk)
        cumb = cum_ref[rs, :]
        dtb = dt_ref[rs, :]
        last_row = cum_ref[pl.ds(r0 + chunk - 1, 1), :]
        for g in range(SSM_GROUPS):
            bg = bm_ref[rs, g * SSM_STATE:(g + 1) * SSM_STATE]
            cg = cm_ref[rs, g * SSM_STATE:(g + 1) * SSM_STATE]
            cb = _dot_nt(cg, bg)
            for r in range(SSM_HPG):
                hd = g * SSM_HPG + r
                lane = SM_DT + hd
                cc = cumb[:, lane:lane + 1]
                cr = ct_ref[c, lane:lane + 1, :]
                dtc = dtb[:, lane:lane + 1]
                last = last_row[:, lane:lane + 1]
                lm = jnp.exp(jnp.where(lower, cc - cr, -jnp.inf))
                xr = xs_ref[rs, hd * SSM_HEAD_DIM:(hd + 1) * SSM_HEAD_DIM]
                hh = hs_ref[hd]
                y = _dot(cb * lm, xr * dtc) + _dot_nt(cg, hh) * jnp.exp(cc)
                yy_ref[rs, hd * SSM_HEAD_DIM:(hd + 1) * SSM_HEAD_DIM] = y + d_ref[hd] * xr
                hs_ref[hd] = hh * jnp.exp(last) + _dot_tn(xr * (jnp.exp(last - cc) * dtc), bg)
        return carry

    lax.fori_loop(0, nch, chunk_body, 0)

    gw = SSM_D_INNER // SSM_GROUPS
    for r0 in range(0, lp, ROWS):
        n = min(ROWS, lin - r0)
        if n <= 0:
            break
        for g in range(SSM_GROUPS):
            cols = slice(g * gw, (g + 1) * gw)
            t = yy_ref[r0:r0 + ROWS, cols] * _silu(zz_ref[r0:r0 + ROWS, cols])
            t = _rms_rows(t, nw_ref[:, cols])
            o_ref[0, r0:r0 + n, cols] = t[:n]

    @pl.when(l == pl.num_programs(1) - 1)
    def _():
        cnew_ref[0] = ext_ref[5 + lin:8 + lin, :]
        hnew_ref[0] = hs_ref[...]


def _ssd(u3, cbuf, h0, cw, cb, par, nw, d, *, lc, lp):
    b, l, _ = u3.shape
    nl = l // lc
    gn2 = 2 * SSM_GROUPS * SSM_STATE
    return pl.pallas_call(
        functools.partial(_ssd_kernel, lin=lc, lp=lp, chunk=SSM_CHUNK),
        grid=(b, nl),
        in_specs=[
            pl.BlockSpec((1, lc, SSM_D_INNER), lambda i, j: (i, j, C_SSZ // SSM_D_INNER)),
            pl.BlockSpec((1, lc, SSM_D_INNER), lambda i, j: (i, j, C_SSX // SSM_D_INNER)),
            pl.BlockSpec((1, lc, gn2), lambda i, j: (i, j, C_SSBC // gn2)),
            pl.BlockSpec((1, lc, LANES), lambda i, j: (i, j, C_SMALL // LANES)),
            pl.BlockSpec((1, CONV_WIDTH - 1, SSM_CONV_DIM), lambda i, j: (i, 0, 0)),
            pl.BlockSpec((1, SSM_HEADS, SSM_HEAD_DIM, SSM_STATE), lambda i, j: (i, 0, 0, 0)),
            pl.BlockSpec((CONV_WIDTH, SSM_CONV_DIM), lambda i, j: (0, 0)),
            pl.BlockSpec((1, SSM_CONV_DIM), lambda i, j: (0, 0)),
            pl.BlockSpec((SUBLANES, LANES), lambda i, j: (0, 0)),
            pl.BlockSpec((1, SSM_D_INNER), lambda i, j: (0, 0)),
            pl.BlockSpec(memory_space=pltpu.SMEM),
        ],
        out_specs=[
            pl.BlockSpec((1, lc, SSM_D_INNER), lambda i, j: (i, j, 0)),
            pl.BlockSpec((1, CONV_WIDTH - 1, SSM_CONV_DIM), lambda i, j: (i, 0, 0)),
            pl.BlockSpec((1, SSM_HEADS, SSM_HEAD_DIM, SSM_STATE), lambda i, j: (i, 0, 0, 0)),
        ],
        out_shape=[
            jax.ShapeDtypeStruct((b, l, SSM_D_INNER), F32),
            jax.ShapeDtypeStruct((b, CONV_WIDTH - 1, SSM_CONV_DIM), F32),
            jax.ShapeDtypeStruct((b, SSM_HEADS, SSM_HEAD_DIM, SSM_STATE), F32),
        ],
        scratch_shapes=[
            pltpu.VMEM((8 + lp, SSM_CONV_DIM), F32),
            pltpu.VMEM((lp, SSM_D_INNER), F32),
            pltpu.VMEM((lp, SSM_GROUPS * SSM_STATE), F32), pltpu.VMEM((lp, SSM_GROUPS * SSM_STATE), F32),
            pltpu.VMEM((lp, SSM_D_INNER), F32), pltpu.VMEM((lp, LANES), F32),
            pltpu.VMEM((lp, LANES), F32), pltpu.VMEM((lp, LANES), F32),
            pltpu.VMEM((lp // SSM_CHUNK, LANES, SSM_CHUNK), F32),
            pltpu.VMEM((lp, SSM_D_INNER), F32),
            pltpu.VMEM((SSM_HEADS, SSM_HEAD_DIM, SSM_STATE), F32),
        ],
        compiler_params=_cparams(("parallel", "arbitrary")),
        name="ssd",
    )(u3, u3, u3, u3, cbuf, h0, cw, cb, par, nw, d)


def _strict_upper_stack(n):
    j = lax.broadcasted_iota(jnp.int32, (2 * n, n), 0) % n
    s = lax.broadcasted_iota(jnp.int32, (2 * n, n), 1)
    return jnp.where(j > s, 1.0, 0.0).astype(BF16)


def _rev_excl_cumsum(la, uu):
    hi, lo = _split2(la)
    return jnp.dot(jnp.concatenate([hi, lo], axis=1), uu, preferred_element_type=F32)


def _sbp_kernel(bias_ref, q_ref, k_ref, v_ref, o_ref, *, tq, scale):
    h = pl.program_id(1)
    qi = pl.program_id(2)
    bias = bias_ref[h]
    q = q_ref[0].astype(BF16)
    uu = _strict_upper_stack(tq)
    qpos = qi * tq + lax.broadcasted_iota(jnp.int32, (tq, 1), 0)
    lane = lax.broadcasted_iota(jnp.int32, (1, tq), 1)

    def body(t, carry):
        c, acc = carry
        kj = qi - t
        ks = pl.ds(pl.multiple_of(kj * tq, tq), tq)
        z = _dot_nt(q, k_ref[0, ks, :]) * scale + bias
        valid = (kj * tq + lane) < qpos
        ls = _log_sigmoid(z)
        la = jnp.where(valid, ls - z, 0.0)
        surv = _rev_excl_cumsum(la, uu) + c
        att = jnp.where(valid, jnp.exp(ls + surv), 0.0)
        acc = acc + _dot(att, v_ref[0, ks, :])
        c = c + jnp.sum(la, axis=1, keepdims=True)
        return c, acc

    _, acc = lax.fori_loop(0, qi + 1, body, (jnp.zeros((tq, 1), F32), jnp.zeros((tq, SB_HEAD_DIM), F32)))
    o_ref[0] = acc


def _sb_prompt(u3, bias, *, tq=256):
    b, l, _ = u3.shape
    tq = min(tq, l)
    kv = lambda c0: pl.BlockSpec((1, l, SB_HEAD_DIM), lambda i, h, j: (i, 0, c0 // SB_HEAD_DIM + h))
    return pl.pallas_call(
        functools.partial(_sbp_kernel, tq=tq, scale=SB_HEAD_DIM ** -0.5),
        grid=(b, SB_HEADS, l // tq),
        in_specs=[
            pl.BlockSpec(memory_space=pltpu.SMEM),
            pl.BlockSpec((1, tq, SB_HEAD_DIM), lambda i, h, j: (i, j, C_SBQ // SB_HEAD_DIM + h)),
            kv(C_SBK), kv(C_SBV),
        ],
        out_specs=pl.BlockSpec((1, tq, SB_HEAD_DIM), lambda i, h, j: (i, j, h)),
        out_shape=jax.ShapeDtypeStruct((b, l, SB_WIDTH), F32),
        compiler_params=_cparams(("parallel", "parallel", "arbitrary")),
        name="sb_prompt",
    )(bias, u3, u3, u3)


def _sbs_kernel(pt_ref, qbd_ref, kc_ref, vc_ref, bias_ref, *rest, pp, tq, page, scale):
    k_refs = rest[:pp]
    v_refs = rest[pp:2 * pp]
    o_ref = rest[2 * pp]
    c_ref = rest[2 * pp + 1]
    p = pl.program_id(1)
    hq = qbd_ref.shape[1]
    qbd = qbd_ref[0].astype(BF16)
    uu = _strict_upper_stack(page)
    bias = bias_ref[...]

    @pl.when(p == 0)
    def _():
        z = _dot_nt(qbd, kc_ref[0]) * scale + bias
        t = lax.broadcasted_iota(jnp.int32, (hq, page), 0) % tq
        s = lax.broadcasted_iota(jnp.int32, (hq, page), 1)
        valid = s < t
        ls = _log_sigmoid(z)
        la = jnp.where(valid, ls - z, 0.0)
        att = jnp.where(valid, jnp.exp(ls + _rev_excl_cumsum(la, uu)), 0.0)
        o_ref[0] = _dot(att, vc_ref[0])
        c_ref[...] = jnp.sum(la, axis=1, keepdims=True)

    z = jnp.concatenate([_dot_nt(qbd, k_refs[j][...]) for j in range(pp)], axis=0)
    z = z * scale + jnp.concatenate([bias] * pp, axis=0)
    ls = _log_sigmoid(z)
    la = ls - z
    surv = _rev_excl_cumsum(la, uu)
    tot = jnp.sum(la, axis=1, keepdims=True)
    cur = c_ref[...]
    cs = []
    for j in range(pp):
        cs.append(cur)
        cur = cur + tot[j * hq:(j + 1) * hq]
    c_ref[...] = cur
    att = jnp.exp(ls + surv + jnp.concatenate(cs, axis=0))
    acc = o_ref[0]
    for j in range(pp):
        acc = acc + _dot(att[j * hq:(j + 1) * hq], v_refs[j][...])
    o_ref[0] = acc


def _sb_sample(qbd, kpad, vpad, bias_rows, cache_k, cache_v, page_table, layer, *, pp=8):
    b, hq, _ = qbd.shape
    n_pages = page_table.shape[1]
    page = cache_k.shape[2]
    pp = math.gcd(pp, n_pages)
    tq = hq // SB_HEADS

    def page_spec(j):
        return pl.BlockSpec((None, None, page, SB_WIDTH),
                            lambda i, p, pt: (layer, pt[i, n_pages - 1 - (p * pp + j)], 0, 0))

    grid_spec = pltpu.PrefetchScalarGridSpec(
        num_scalar_prefetch=1,
        grid=(b, n_pages // pp),
        in_specs=[
            pl.BlockSpec((1, hq, SB_WIDTH), lambda i, p, pt: (i, 0, 0)),
            pl.BlockSpec((1, page, SB_WIDTH), lambda i, p, pt: (i, 0, 0)),
            pl.BlockSpec((1, page, SB_WIDTH), lambda i, p, pt: (i, 0, 0)),
            pl.BlockSpec((hq, LANES), lambda i, p, pt: (0, 0)),
        ] + [page_spec(j) for j in range(pp)] + [page_spec(j) for j in range(pp)],
        out_specs=pl.BlockSpec((1, hq, SB_WIDTH), lambda i, p, pt: (i, 0, 0)),
        scratch_shapes=[pltpu.VMEM((hq, 1), F32)],
    )
    return pl.pallas_call(
        functools.partial(_sbs_kernel, pp=pp, tq=tq, page=page, scale=SB_HEAD_DIM ** -0.5),
        grid_spec=grid_spec,
        out_shape=jax.ShapeDtypeStruct((b, hq, SB_WIDTH), F32),
        compiler_params=_cparams(("parallel", "arbitrary")),
        name="sb_sample",
    )(page_table, qbd, kpad, vpad, bias_rows, *([cache_k] * pp), *([cache_v] * pp))


def _merge_kernel(op_ref, od_ref, os_ref, oa_ref, ob_ref, g0_ref, g1_ref, g2_ref, g3_ref,
                  w0_ref, w1_ref, w2_ref, w3_ref, w4_ref, o_ref):
    acc = _sigmoid(g0_ref[...]) * _dot(op_ref[...], w0_ref[...])
    acc = acc + _sigmoid(g1_ref[...]) * _dot(od_ref[...], w1_ref[...])
    acc = acc + _sigmoid(g2_ref[...]) * _dot(os_ref[...], w2_ref[...])
    acc = acc + _sigmoid(g3_ref[...]) * (_dot(oa_ref[...], w3_ref[...]) + _dot(ob_ref[...], w4_ref[...]))
    o_ref[...] = acc.astype(o_ref.dtype)


def _merge(o_pool, o_dn, o_sb, o_ss, u2, w_br, *, tm, tn=512):
    m = u2.shape[0]
    row = lambda: pl.BlockSpec((tm, 512), lambda i, j: (i, 0))
    gate = lambda k: pl.BlockSpec((tm, tn), lambda i, j: (i, (C_GATE + k * D_MODEL) // tn + j))
    wrow = lambda k: pl.BlockSpec((512, tn), lambda i, j: (k, j))
    return pl.pallas_call(
        _merge_kernel,
        grid=(m // tm, D_MODEL // tn),
        in_specs=[row(), row(), row(), row(), pl.BlockSpec((tm, 512), lambda i, j: (i, 1)),
                  gate(0), gate(1), gate(2), gate(3),
                  wrow(0), wrow(1), wrow(2), wrow(3), wrow(4)],
        out_specs=pl.BlockSpec((tm, tn), lambda i, j: (i, j)),
        out_shape=jax.ShapeDtypeStruct((m, D_MODEL), BF16),
        compiler_params=_cparams(("parallel", "arbitrary")),
        name="merge",
    )(o_pool, o_dn, o_sb, o_ss, o_ss, u2, u2, u2, u2, w_br, w_br, w_br, w_br, w_br)


def _outproj_kernel(m_ref, x_ref, w_ref, g_ref, o_ref):
    mix = jnp.dot(m_ref[...], w_ref[...], preferred_element_type=F32)
    o_ref[...] = x_ref[...] + _rms_rows(mix, g_ref[...])


def _outproj(merged, x2, w, g, *, tm):
    m = x2.shape[0]
    return pl.pallas_call(
        _outproj_kernel,
        grid=(m // tm,),
        in_specs=[
            pl.BlockSpec((tm, D_MODEL), lambda i: (i, 0)),
            pl.BlockSpec((tm, D_MODEL), lambda i: (i, 0)),
            pl.BlockSpec((D_MODEL, D_MODEL), lambda i: (0, 0)),
            pl.BlockSpec((1, D_MODEL), lambda i: (0, 0)),
        ],
        out_specs=pl.BlockSpec((tm, D_MODEL), lambda i: (i, 0)),
        out_shape=jax.ShapeDtypeStruct((m, D_MODEL), F32),
        compiler_params=_cparams(("parallel",)),
        name="outproj",
    )(merged, x2, w, g)


def _mlp_kernel(x_ref, g1_ref, wu_ref, wd_ref, g2_ref, o_ref, h_ref, acc_ref, *, tm):
    f = pl.program_id(1)

    @pl.when(f == 0)
    def _():
        _norm_rows_to(h_ref, x_ref, g1_ref, tm)
        acc_ref[...] = jnp.zeros(acc_ref.shape, F32)

    a = jnp.dot(h_ref[...], wu_ref[...], preferred_element_type=F32)
    a = jnp.square(jnp.maximum(a, 0.0)).astype(BF16)
    acc_ref[...] += jnp.dot(a, wd_ref[...], preferred_element_type=F32)

    @pl.when(f == pl.num_programs(1) - 1)
    def _():
        o_ref[...] = x_ref[...] + _rms_rows(acc_ref[...], g2_ref[...])


def _mlp(x2, g1, wu, wd, g2, *, tm, tf=1024):
    m = x2.shape[0]
    return pl.pallas_call(
        functools.partial(_mlp_kernel, tm=tm),
        grid=(m // tm, D_FF // tf),
        in_specs=[
            pl.BlockSpec((tm, D_MODEL), lambda i, f: (i, 0)),
            pl.BlockSpec((1, D_MODEL), lambda i, f: (0, 0)),
            pl.BlockSpec((D_MODEL, tf), lambda i, f: (0, f)),
            pl.BlockSpec((tf, D_MODEL), lambda i, f: (f, 0)),
            pl.BlockSpec((1, D_MODEL), lambda i, f: (0, 0)),
        ],
        out_specs=pl.BlockSpec((tm, D_MODEL), lambda i, f: (i, 0)),
        out_shape=jax.ShapeDtypeStruct((m, D_MODEL), F32),
        scratch_shapes=[pltpu.VMEM((tm, D_MODEL), BF16), pltpu.VMEM((tm, D_MODEL), F32)],
        compiler_params=_cparams(("parallel", "arbitrary")),
        name="mlp",
    )(x2, g1, wu, wd, g2)


def _lane_row(vals, offset):
    return jnp.zeros((LANES,), F32).at[offset:offset + vals.shape[0]].set(vals.astype(F32))


def _layer_params(layer, p):
    w_in = p["w_in"][layer]
    w_in_p = jnp.concatenate(
        [w_in[:, :_O_DNB], w_in[:, _O_SB:_O_SSDT], w_in[:, _O_GATE:_O_END],
         w_in[:, _O_DNB:_O_SB], w_in[:, _O_SSDT:_O_GATE],
         jnp.zeros((D_MODEL, NU - _O_END), w_in.dtype)], axis=1).astype(BF16)
    zrow = jnp.zeros((LANES,), F32)
    dn_par = jnp.stack([_lane_row(p["dn_a_log"][layer], SM_DECAY), _lane_row(p["dn_dt_bias"][layer], SM_DECAY)] + [zrow] * 6)
    ss_par = jnp.stack([_lane_row(p["ssm_a_log"][layer], SM_DT), _lane_row(p["ssm_dt_bias"][layer], SM_DT)] + [zrow] * 6)
    row = lambda a: a[layer].reshape(1, -1)
    return dict(
        w_in=w_in_p, n_mix_pre=row(p["norm_mix_pre"]), n_mix_post=row(p["norm_mix_post"]),
        n_mlp_pre=row(p["norm_mlp_pre"]), n_mlp_post=row(p["norm_mlp_post"]),
        pool_w=p["pool_w"][layer].astype(BF16), pool_scale=row(p["pool_scale"]),
        dn_conv_w=p["dn_conv_w"][layer], dn_par=dn_par, dn_norm_w=row(p["dn_norm_w"]),
        sb_bias=p["sb_bias"][layer],
        ssm_conv_w=p["ssm_conv_w"][layer], ssm_conv_b=row(p["ssm_conv_b"]), ss_par=ss_par,
        ssm_norm_w=row(p["ssm_norm_w"]), ssm_d=p["ssm_d"][layer],
        w_branch=p["w_branch"][layer].astype(BF16), w_out=p["w_out"][layer].astype(BF16),
        w_up=p["w_up"][layer].astype(BF16), w_down=p["w_down"][layer].astype(BF16),
    )


def _trunk_layer(x, states, lw, *, past=None):
    b, l, _ = x.shape
    m = b * l
    pool_buf, dn_conv, dn_s, ssm_conv, ssm_h = states
    prompt = past is None
    tm_big = min(m, 1024)
    tm = min(m, 512)
    lc = min(l, 256)
    lp = max(lc, 2 * DN_CHUNK)

    x2 = x.reshape(m, D_MODEL)
    u2 = _inproj(x2, lw["n_mix_pre"], lw["w_in"], tm=tm_big)
    u3 = u2.reshape(b, l, NU)

    pos0 = 0 if prompt else past[2].shape[1] * past[0].shape[2]
    o_pool, pool_new = _pool(u3, pool_buf, lw["pool_w"], lw["pool_scale"], pos0=pos0)
    o_dn, dn_conv_new, dn_s_new = _deltanet(u3, dn_conv, dn_s, lw["dn_conv_w"], lw["dn_par"], lw["dn_norm_w"], lc=lc, lp=lp)
    o_ss, ss_conv_new, ss_h_new = _ssd(u3, ssm_conv, ssm_h, lw["ssm_conv_w"], lw["ssm_conv_b"], lw["ss_par"],
                                       lw["ssm_norm_w"], lw["ssm_d"], lc=lc, lp=lp)

    k_new = u3[:, :, C_SBK:C_SBK + SB_WIDTH]
    v_new = u3[:, :, C_SBV:C_SBV + SB_WIDTH]
    if prompt:
        o_sb = _sb_prompt(u3, lw["sb_bias"])
    else:
        cache_k, cache_v, page_table, layer = past
        page = cache_k.shape[2]
        q = u3[:, :, C_SBQ:C_SBQ + SB_WIDTH].reshape(b, l, SB_HEADS, SB_HEAD_DIM)
        eye = jnp.eye(SB_HEADS, dtype=F32)
        qbd = jnp.transpose(q[:, :, :, None, :] * eye[None, None, :, :, None], (0, 2, 1, 3, 4))
        qbd = qbd.reshape(b, SB_HEADS * l, SB_WIDTH)
        kpad = jnp.pad(k_new, ((0, 0), (0, page - l), (0, 0)))
        vpad = jnp.pad(v_new, ((0, 0), (0, page - l), (0, 0)))
        bias_rows = jnp.broadcast_to(jnp.repeat(lw["sb_bias"], l)[:, None], (SB_HEADS * l, LANES)).astype(F32)
        acc = _sb_sample(qbd, kpad, vpad, bias_rows, cache_k, cache_v, page_table, layer)
        acc = acc.reshape(b, SB_HEADS, l, SB_HEADS, SB_HEAD_DIM)
        o_sb = jnp.stack([acc[:, h, :, h, :] for h in range(SB_HEADS)], axis=2).reshape(b, l, SB_WIDTH)

    merged = _merge(o_pool.reshape(m, -1), o_dn.reshape(m, -1), o_sb.reshape(m, -1), o_ss.reshape(m, -1),
                    u2, lw["w_branch"], tm=tm)
    x2 = _outproj(merged, x2, lw["w_out"], lw["n_mix_post"], tm=tm)
    x2 = _mlp(x2, lw["n_mlp_pre"], lw["w_up"], lw["w_down"], lw["n_mlp_post"], tm=tm)
    new_states = (k_new.reshape(b, l, SB_HEADS, SB_HEAD_DIM), v_new.reshape(b, l, SB_HEADS, SB_HEAD_DIM),
                  pool_new, dn_conv_new, dn_s_new, ss_conv_new, ss_h_new)
    return x2.reshape(b, l, D_MODEL), new_states


def kernel(x_prompt, x_sample, cache_sb_k, cache_sb_v, state_pool, state_dn_conv, state_dn_s, state_ssm_conv, state_ssm_h, page_table, norm_mix_pre, norm_mix_post, norm_mlp_pre, norm_mlp_post, w_in, pool_w, pool_scale, dn_conv_w, dn_a_log, dn_dt_bias, dn_norm_w, sb_bias, ssm_conv_w, ssm_conv_b, ssm_a_log, ssm_dt_bias, ssm_d, ssm_norm_w, w_branch, w_out, w_up, w_down):
    params = dict(norm_mix_pre=norm_mix_pre, norm_mix_post=norm_mix_post, norm_mlp_pre=norm_mlp_pre,
                  norm_mlp_post=norm_mlp_post, w_in=w_in, pool_w=pool_w, pool_scale=pool_scale,
                  dn_conv_w=dn_conv_w, dn_a_log=dn_a_log, dn_dt_bias=dn_dt_bias, dn_norm_w=dn_norm_w,
                  sb_bias=sb_bias, ssm_conv_w=ssm_conv_w, ssm_conv_b=ssm_conv_b, ssm_a_log=ssm_a_log,
                  ssm_dt_bias=ssm_dt_bias, ssm_d=ssm_d, ssm_norm_w=ssm_norm_w, w_branch=w_branch,
                  w_out=w_out, w_up=w_up, w_down=w_down)
    depth = w_in.shape[0]
    bp = x_prompt.shape[0]
    dt_ = x_prompt.dtype
    zero_states = (jnp.zeros((bp, POOL_BUF, POOL_WIDTH), dt_),
                   jnp.zeros((bp, CONV_WIDTH - 1, DN_CONV_DIM), dt_),
                   jnp.zeros((bp, DN_HEADS, DN_DK, DN_DV), dt_),
                   jnp.zeros((bp, CONV_WIDTH - 1, SSM_CONV_DIM), dt_),
                   jnp.zeros((bp, SSM_HEADS, SSM_HEAD_DIM, SSM_STATE), dt_))
    n_pool, page = cache_sb_k.shape[1], cache_sb_k.shape[2]
    cache_k = cache_sb_k.reshape(depth, n_pool, page, SB_WIDTH)
    cache_v = cache_sb_v.reshape(depth, n_pool, page, SB_WIDTH)
    y_prompt, y_sample = x_prompt, x_sample
    new_p, new_s = [], []
    for layer in range(depth):
        lw = _layer_params(layer, params)
        y_prompt, st_p = _trunk_layer(y_prompt, zero_states, lw)
        sample_states = (state_pool[layer], state_dn_conv[layer], state_dn_s[layer],
                         state_ssm_conv[layer], state_ssm_h[layer])
        y_sample, st_s = _trunk_layer(y_sample, sample_states, lw, past=(cache_k, cache_v, page_table, layer))
        new_p.append(st_p)
        new_s.append(st_s)
    outs_p = [jnp.stack(t) for t in zip(*new_p)]
    outs_s = [jnp.stack(t) for t in zip(*new_s)]
    return (y_prompt, y_sample, *outs_p, *outs_s)
```

```python
import functools
import math

import jax
import jax.numpy as jnp
from jax import lax
from jax.experimental import pallas as pl
from jax.experimental.pallas import tpu as pltpu

F32 = jnp.float32
BF16 = jnp.bfloat16

D_MODEL = 2048
N_BRANCH = 4
RMS_EPS = 1e-6
CONV_WIDTH = 4
POOL_WINDOWS = (2, 4, 8, 16)
POOL_WIDTH = D_MODEL // 4
POOL_GW = POOL_WIDTH // 4
POOL_BUF = max(POOL_WINDOWS) - 1
DN_HEADS = 4
DN_DK = 128
DN_DV = 128
DN_QK = DN_HEADS * DN_DK
DN_VW = DN_HEADS * DN_DV
DN_CONV_DIM = 2 * DN_QK + DN_VW
DN_CHUNK = 64
SB_HEADS = 4
SB_HEAD_DIM = 128
SB_WIDTH = SB_HEADS * SB_HEAD_DIM
SSM_D_INNER = D_MODEL // 2
SSM_HEAD_DIM = 64
SSM_HEADS = SSM_D_INNER // SSM_HEAD_DIM
SSM_GROUPS = 2
SSM_HPG = SSM_HEADS // SSM_GROUPS
SSM_STATE = 128
SSM_CONV_DIM = SSM_D_INNER + 2 * SSM_GROUPS * SSM_STATE
SSM_CHUNK = 64
D_FF = 4 * D_MODEL

LANES = 128
SUBLANES = 8

C_POOL = 0
C_DNQ, C_DNK, C_DNV, C_DNZ = 512, 1024, 1536, 2048
C_SBQ, C_SBK, C_SBV = 2560, 3072, 3584
C_SSZ, C_SSX, C_SSBC = 4096, 5120, 6144
C_GATE = 6656
C_SMALL = 14848
NU = 15360
_O_DNB = POOL_WIDTH + DN_CONV_DIM + DN_VW
_O_SB = _O_DNB + 2 * DN_HEADS
_O_SSDT = _O_SB + 3 * SB_WIDTH + SSM_D_INNER + SSM_CONV_DIM
_O_GATE = _O_SSDT + SSM_HEADS
_O_END = _O_GATE + N_BRANCH * D_MODEL
SM_BETA, SM_DECAY, SM_DT = 0, DN_HEADS, 2 * DN_HEADS

VMEM_LIMIT_MB = 56


def _cparams(sem, vmem_mb=VMEM_LIMIT_MB):
    return pltpu.CompilerParams(dimension_semantics=sem, vmem_limit_bytes=vmem_mb * 1024 * 1024)


def _sigmoid(x):
    return 1.0 / (1.0 + jnp.exp(-x))


def _silu(x):
    return x * _sigmoid(x)


def _softplus(x):
    return jnp.maximum(x, 0.0) + jnp.log1p(jnp.exp(-jnp.abs(x)))


def _log_sigmoid(x):
    return jnp.minimum(x, 0.0) - jnp.log1p(jnp.exp(-jnp.abs(x)))


def _dot(a, b):
    return jnp.dot(a.astype(BF16), b.astype(BF16), preferred_element_type=F32)


def _dot_nt(a, b):
    return lax.dot_general(a.astype(BF16), b.astype(BF16), (((1,), (1,)), ((), ())), preferred_element_type=F32)


def _dot_tn(a, b):
    return lax.dot_general(a.astype(BF16), b.astype(BF16), (((0,), (0,)), ((), ())), preferred_element_type=F32)


def _split2(a):
    hi = a.astype(BF16)
    lo = (a - hi.astype(F32)).astype(BF16)
    return hi, lo


def _dot3(a, b):
    ah, al = _split2(a)
    bh, bl = _split2(b)
    d = lambda x, y: jnp.dot(x, y, preferred_element_type=F32)
    return d(ah, bh) + (d(ah, bl) + d(al, bh))


def _rms_rows(x, g):
    return x * lax.rsqrt(jnp.mean(x * x, axis=-1, keepdims=True) + RMS_EPS) * g


def _norm_rows_to(h_ref, x_ref, g_ref, tm):
    ch = min(tm, 256)

    def body(r, c):
        rs = pl.ds(pl.multiple_of(r * ch, ch), ch)
        h_ref[rs, :] = _rms_rows(x_ref[rs, :], g_ref[...]).astype(h_ref.dtype)
        return c

    lax.fori_loop(0, tm // ch, body, 0)


def _inproj_kernel(x_ref, g_ref, w_ref, o_ref, h_ref, *, tm):
    @pl.when(pl.program_id(1) == 0)
    def _():
        _norm_rows_to(h_ref, x_ref, g_ref, tm)

    o_ref[...] = jnp.dot(h_ref[...], w_ref[...], preferred_element_type=F32)


def _inproj(x2, g, w_p, *, tm, tn=1024):
    m = x2.shape[0]
    return pl.pallas_call(
        functools.partial(_inproj_kernel, tm=tm),
        grid=(m // tm, NU // tn),
        in_specs=[
            pl.BlockSpec((tm, D_MODEL), lambda i, j: (i, 0)),
            pl.BlockSpec((1, D_MODEL), lambda i, j: (0, 0)),
            pl.BlockSpec((D_MODEL, tn), lambda i, j: (0, j)),
        ],
        out_specs=pl.BlockSpec((tm, tn), lambda i, j: (i, j)),
        out_shape=jax.ShapeDtypeStruct((m, NU), F32),
        scratch_shapes=[pltpu.VMEM((tm, D_MODEL), BF16)],
        compiler_params=_cparams(("parallel", "arbitrary")),
        name="inproj",
    )(x2, g, w_p)


def _pool_kernel(u_ref, buf_ref, w_ref, sc_ref, o_ref, new_ref, ext_ref, *, lin, lp, pos0):
    ext_ref[0:16, :] = jnp.zeros((16, POOL_WIDTH), F32)
    ext_ref[1:16, :] = buf_ref[0]
    if lin < lp:
        ext_ref[16:16 + lp, :] = jnp.zeros((lp, POOL_WIDTH), F32)
    ext_ref[16:16 + lin, :] = u_ref[0]
    ch = min(lp, 256)
    for c0 in range(0, lp, ch):
        pos = pos0 + c0 + lax.broadcasted_iota(jnp.int32, (ch, 1), 0)
        for gi, w in enumerate(POOL_WINDOWS):
            cols = slice(gi * POOL_GW, (gi + 1) * POOL_GW)
            s = ext_ref[16 + c0:16 + c0 + ch, cols]
            tot = s
            for k in range(1, w):
                tot = tot + ext_ref[16 + c0 - k:16 + c0 - k + ch, cols]
            cnt = jnp.minimum(pos + 1, w).astype(F32)
            y = tot / cnt - s
            yo = _dot(y, w_ref[gi]) * sc_ref[:, cols]
            n = min(ch, lin - c0)
            o_ref[0, c0:c0 + n, cols] = yo[:n]
    new_ref[0] = ext_ref[1 + lin:16 + lin, :]


def _pool(u3, buf, w, sc, *, pos0):
    b, lin, _ = u3.shape
    lp = max(lin, SUBLANES)
    return pl.pallas_call(
        functools.partial(_pool_kernel, lin=lin, lp=lp, pos0=pos0),
        grid=(b,),
        in_specs=[
            pl.BlockSpec((1, lin, POOL_WIDTH), lambda i: (i, 0, C_POOL // POOL_WIDTH)),
            pl.BlockSpec((1, POOL_BUF, POOL_WIDTH), lambda i: (i, 0, 0)),
            pl.BlockSpec((4, POOL_GW, POOL_GW), lambda i: (0, 0, 0)),
            pl.BlockSpec((1, POOL_WIDTH), lambda i: (0, 0)),
        ],
        out_specs=[
            pl.BlockSpec((1, lin, POOL_WIDTH), lambda i: (i, 0, 0)),
            pl.BlockSpec((1, POOL_BUF, POOL_WIDTH), lambda i: (i, 0, 0)),
        ],
        out_shape=[
            jax.ShapeDtypeStruct((b, lin, POOL_WIDTH), F32),
            jax.ShapeDtypeStruct((b, POOL_BUF, POOL_WIDTH), F32),
        ],
        scratch_shapes=[pltpu.VMEM((16 + lp, POOL_WIDTH), F32)],
        compiler_params=_cparams(("parallel",)),
        name="pool",
    )(u3, buf, w, sc)


ROWS = 64


def _fill_ext(ext_ref, cbuf_ref, parts, *, lin, lp):
    l = pl.program_id(1)

    @pl.when(l == 0)
    def _():
        ext_ref[5:8, :] = cbuf_ref[0]

    @pl.when(l > 0)
    def _():
        ext_ref[5:8, :] = ext_ref[5 + lin:8 + lin, :]

    if lin < lp:
        ext_ref[8:8 + lp, :] = jnp.zeros((lp, ext_ref.shape[1]), F32)
    for ref, c0, width in parts:
        ext_ref[8:8 + lin, c0:c0 + width] = ref[0]


def _conv_silu(ext_ref, r0, n, cols, w_ref, b_ref):
    acc = None
    for i in range(CONV_WIDTH):
        part = ext_ref[5 + r0 + i:5 + r0 + i + n, cols] * w_ref[i:i + 1, cols]
        acc = part if acc is None else acc + part
    if b_ref is not None:
        acc = acc + b_ref[:, cols]
    return _silu(acc)


def _pad_rows(dst_ref, src_ref, *, lin, lp):
    if lin < lp:
        dst_ref[...] = jnp.zeros(dst_ref.shape, F32)
    dst_ref[0:lin, :] = src_ref[0]


def _row_valid(r0, n, lin):
    if r0 + n <= lin:
        return None
    return (r0 + lax.broadcasted_iota(jnp.int32, (n, 1), 0)) < lin


def _chunk_cumsum(x, chunk):
    rin = lax.broadcasted_iota(jnp.int32, x.shape, 0) % chunk
    s = 1
    while s < chunk:
        x = x + jnp.where(rin >= s, pltpu.roll(x, s, 0), 0.0)
        s *= 2
    return x


def _diag_inv2(a0, a1):
    c = a0.shape[0]
    hb = c // 2
    ng = hb // SUBLANES
    sub = lax.broadcasted_iota(jnp.int32, (SUBLANES, 2 * c), 0)
    lane = lax.broadcasted_iota(jnp.int32, (SUBLANES, 2 * c), 1)
    base = (lane // hb) * hb
    lmod = lane - base
    a01 = jnp.concatenate([a0, a1], axis=1)
    odd = ((lax.broadcasted_iota(jnp.int32, (hb, 2 * c), 1) // hb) % 2) == 1
    packed = jnp.where(odd, a01[hb:], a01[:hb])
    racc = [jnp.zeros((SUBLANES, 2 * c), F32) for _ in range(ng)]
    tg = [jnp.zeros((SUBLANES, 2 * c), F32) for _ in range(ng)]
    for j in range(hb):
        gj, rj = divmod(j, SUBLANES)
        t_j = jnp.where(lmod[0:1] == j, 1.0, 0.0) - racc[gj][rj:rj + 1, :]
        tg[gj] = jnp.where(sub == rj, t_j, tg[gj])
        if j == hb - 1:
            break
        for g in range(gj, ng):
            col = jnp.take_along_axis(packed[g * SUBLANES:(g + 1) * SUBLANES], base + j, axis=1)
            racc[g] = racc[g] + col * t_j
    dinv = jnp.concatenate(tg, axis=0)
    bd = jnp.concatenate([jnp.where(odd, 0.0, dinv), jnp.where(odd, dinv, 0.0)], axis=0)
    return [bd[:, :c], bd[:, c:]]


def _dn_kernel(q_ref, k_ref, v_ref, z_ref, sm_ref, cbuf_ref, s0_ref, cw_ref, par_ref, nw_ref,
               o_ref, cnew_ref, snew_ref,
               ext_ref, qn_ref, kn_ref, vv_ref, zz_ref, smp_ref, be_ref, gc_ref, gt_ref, oo_ref, s_ref,
               uc_ref, wc_ref, qe_ref, kt_ref, at_ref, *, lin, lp, chunk):
    l = pl.program_id(1)
    nch = lp // chunk

    @pl.when(l == 0)
    def _():
        s_ref[...] = s0_ref[0]

    _fill_ext(ext_ref, cbuf_ref, ((q_ref, 0, DN_QK), (k_ref, DN_QK, DN_QK), (v_ref, 2 * DN_QK, DN_VW)), lin=lin, lp=lp)
    _pad_rows(zz_ref, z_ref, lin=lin, lp=lp)
    _pad_rows(smp_ref, sm_ref, lin=lin, lp=lp)

    for r0 in range(0, lp, ROWS):
        valid = _row_valid(r0, ROWS, lin)
        for part, dst in enumerate((qn_ref, kn_ref, vv_ref)):
            for h in range(DN_HEADS):
                cols = slice(part * DN_QK + h * DN_DK, part * DN_QK + (h + 1) * DN_DK)
                y = _conv_silu(ext_ref, r0, ROWS, cols, cw_ref, None)
                if part < 2:
                    y = y * lax.rsqrt(jnp.sum(y * y, axis=-1, keepdims=True) + 1e-6)
                if part == 0:
                    y = y * (DN_DK ** -0.5)
                if valid is not None:
                    y = jnp.where(valid, y, 0.0)
                dst[r0:r0 + ROWS, h * DN_DK:(h + 1) * DN_DK] = y

    sm = smp_ref[...]
    beta = _sigmoid(sm)
    g = -jnp.exp(par_ref[0:1, :]) * _softplus(sm + par_ref[1:2, :])
    valid = _row_valid(0, lp, lin)
    if valid is not None:
        beta = jnp.where(valid, beta, 0.0)
        g = jnp.where(valid, g, 0.0)
    gcum = _chunk_cumsum(g, chunk)
    be_ref[...] = beta
    gc_ref[...] = gcum
    gt = gcum.T
    for c in range(nch):
        gt_ref[c] = gt[:, c * chunk:(c + 1) * chunk]

    rid = lax.broadcasted_iota(jnp.int32, (chunk, chunk), 0)
    cid = lax.broadcasted_iota(jnp.int32, (chunk, chunk), 1)
    strict_lower = cid < rid
    lower = cid <= rid
    off = (rid >= chunk // 2) & (cid < chunk // 2)
    heads = range(DN_HEADS)
    hsl = [slice(h * DN_DK, (h + 1) * DN_DK) for h in heads]

    def prep_body(c, carry):
        r0 = pl.multiple_of(c * chunk, chunk)
        rs = pl.ds(r0, chunk)
        gcb = gc_ref[rs, :]
        beb = be_ref[rs, :]
        glast_row = gc_ref[pl.ds(r0 + chunk - 1, 1), :]
        gc = [gcb[:, SM_DECAY + h:SM_DECAY + h + 1] for h in heads]
        bt = [beb[:, SM_BETA + h:SM_BETA + h + 1] for h in heads]
        gr = [gt_ref[c, SM_DECAY + h:SM_DECAY + h + 1, :] for h in heads]
        kh = [kn_ref[rs, hsl[h]] for h in heads]
        qh = [qn_ref[rs, hsl[h]] for h in heads]
        kb = [kh[h] * bt[h] for h in heads]
        eg = [jnp.exp(gc[h]) for h in heads]
        amat = [_dot_nt(kb[h], kh[h]) * jnp.exp(jnp.where(strict_lower, gc[h] - gr[h], -jnp.inf)) for h in heads]
        for h in heads:
            at_ref[c * DN_HEADS + h] = _dot_nt(qh[h], kh[h]) * jnp.exp(jnp.where(lower, gc[h] - gr[h], -jnp.inf))
            qe_ref[rs, hsl[h]] = qh[h] * eg[h]
            kt_ref[rs, hsl[h]] = kh[h] * jnp.exp(glast_row[:, SM_DECAY + h:SM_DECAY + h + 1] - gc[h])
        dinv = _diag_inv2(amat[0], amat[1]) + _diag_inv2(amat[2], amat[3])
        inner = [_dot3(jnp.where(off, amat[h], 0.0), dinv[h]) for h in heads]
        tmat = [dinv[h] - _dot3(dinv[h], inner[h]) for h in heads]
        rhs = [jnp.concatenate([vv_ref[rs, hsl[h]] * bt[h], kb[h] * eg[h]], axis=1) for h in heads]
        sol = [_dot3(tmat[h], rhs[h]) for h in heads]
        for h in heads:
            uc_ref[rs, hsl[h]] = sol[h][:, :DN_DV]
            wc_ref[rs, hsl[h]] = sol[h][:, DN_DV:]
        return carry

    lax.fori_loop(0, nch, prep_body, 0)

    def scan_body(c, carry):
        r0 = pl.multiple_of(c * chunk, chunk)
        rs = pl.ds(r0, chunk)
        glast_row = gc_ref[pl.ds(r0 + chunk - 1, 1), :]
        sh = [s_ref[h] for h in heads]
        ws = [_dot(wc_ref[rs, hsl[h]], sh[h]) for h in heads]
        qs = [_dot(qe_ref[rs, hsl[h]], sh[h]) for h in heads]
        v_new = [uc_ref[rs, hsl[h]] - ws[h] for h in heads]
        o2 = [_dot(at_ref[c * DN_HEADS + h], v_new[h]) for h in heads]
        kv = [_dot_tn(kt_ref[rs, hsl[h]], v_new[h]) for h in heads]
        for h in heads:
            s_ref[h] = sh[h] * jnp.exp(glast_row[:, SM_DECAY + h:SM_DECAY + h + 1]) + kv[h]
            oo_ref[rs, hsl[h]] = _rms_rows(qs[h] + o2[h], nw_ref[...]) * _silu(zz_ref[rs, hsl[h]])
        return carry

    lax.fori_loop(0, nch, scan_body, 0)
    o_ref[0] = oo_ref[0:lin, :]

    @pl.when(l == pl.num_programs(1) - 1)
    def _():
        cnew_ref[0] = ext_ref[5 + lin:8 + lin, :]
        snew_ref[0] = s_ref[...]


def _deltanet(u3, cbuf, s0, cw, par, nw, *, lc, lp):
    b, l, _ = u3.shape
    nl = l // lc
    blk = lambda c0: pl.BlockSpec((1, lc, 512), lambda i, j: (i, j, c0 // 512))
    return pl.pallas_call(
        functools.partial(_dn_kernel, lin=lc, lp=lp, chunk=DN_CHUNK),
        grid=(b, nl),
        in_specs=[
            blk(C_DNQ), blk(C_DNK), blk(C_DNV), blk(C_DNZ),
            pl.BlockSpec((1, lc, LANES), lambda i, j: (i, j, C_SMALL // LANES)),
            pl.BlockSpec((1, CONV_WIDTH - 1, DN_CONV_DIM), lambda i, j: (i, 0, 0)),
            pl.BlockSpec((1, DN_HEADS, DN_DK, DN_DV), lambda i, j: (i, 0, 0, 0)),
            pl.BlockSpec((CONV_WIDTH, DN_CONV_DIM), lambda i, j: (0, 0)),
            pl.BlockSpec((SUBLANES, LANES), lambda i, j: (0, 0)),
            pl.BlockSpec((1, DN_DV), lambda i, j: (0, 0)),
        ],
        out_specs=[
            pl.BlockSpec((1, lc, DN_VW), lambda i, j: (i, j, 0)),
            pl.BlockSpec((1, CONV_WIDTH - 1, DN_CONV_DIM), lambda i, j: (i, 0, 0)),
            pl.BlockSpec((1, DN_HEADS, DN_DK, DN_DV), lambda i, j: (i, 0, 0, 0)),
        ],
        out_shape=[
            jax.ShapeDtypeStruct((b, l, DN_VW), F32),
            jax.ShapeDtypeStruct((b, CONV_WIDTH - 1, DN_CONV_DIM), F32),
            jax.ShapeDtypeStruct((b, DN_HEADS, DN_DK, DN_DV), F32),
        ],
        scratch_shapes=[
            pltpu.VMEM((8 + lp, DN_CONV_DIM), F32),
            pltpu.VMEM((lp, DN_QK), F32), pltpu.VMEM((lp, DN_QK), F32), pltpu.VMEM((lp, DN_VW), F32),
            pltpu.VMEM((lp, DN_VW), F32), pltpu.VMEM((lp, LANES), F32),
            pltpu.VMEM((lp, LANES), F32), pltpu.VMEM((lp, LANES), F32),
            pltpu.VMEM((lp // DN_CHUNK, LANES, DN_CHUNK), F32),
            pltpu.VMEM((lp, DN_VW), F32),
            pltpu.VMEM((DN_HEADS, DN_DK, DN_DV), F32),
            pltpu.VMEM((lp, DN_VW), F32), pltpu.VMEM((lp, DN_QK), F32),
            pltpu.VMEM((lp, DN_QK), F32), pltpu.VMEM((lp, DN_QK), F32),
            pltpu.VMEM((lp // DN_CHUNK * DN_HEADS, DN_CHUNK, DN_CHUNK), F32),
        ],
        compiler_params=_cparams(("parallel", "arbitrary")),
        name="deltanet",
    )(u3, u3, u3, u3, u3, cbuf, s0, cw, par, nw)


def _ssd_kernel(z_ref, x_ref, bc_ref, sm_ref, cbuf_ref, h0_ref, cw_ref, cb_ref, par_ref, nw_ref, d_ref,
                o_ref, cnew_ref, hnew_ref,
                ext_ref, xs_ref, bm_ref, cm_ref, zz_ref, smp_ref, dt_ref, cum_ref, ct_ref, yy_ref, hs_ref,
                *, lin, lp, chunk):
    l = pl.program_id(1)
    nch = lp // chunk
    gn = SSM_GROUPS * SSM_STATE

    @pl.when(l == 0)
    def _():
        hs_ref[...] = h0_ref[0]

    _fill_ext(ext_ref, cbuf_ref, ((x_ref, 0, SSM_D_INNER), (bc_ref, SSM_D_INNER, 2 * gn)), lin=lin, lp=lp)
    _pad_rows(zz_ref, z_ref, lin=lin, lp=lp)
    _pad_rows(smp_ref, sm_ref, lin=lin, lp=lp)

    for r0 in range(0, lp, ROWS):
        valid = _row_valid(r0, ROWS, lin)
        for c0 in range(0, SSM_CONV_DIM, LANES):
            y = _conv_silu(ext_ref, r0, ROWS, slice(c0, c0 + LANES), cw_ref, cb_ref)
            if valid is not None:
                y = jnp.where(valid, y, 0.0)
            if c0 < SSM_D_INNER:
                xs_ref[r0:r0 + ROWS, c0:c0 + LANES] = y
            elif c0 < SSM_D_INNER + gn:
                bm_ref[r0:r0 + ROWS, c0 - SSM_D_INNER:c0 - SSM_D_INNER + LANES] = y
            else:
                cm_ref[r0:r0 + ROWS, c0 - SSM_D_INNER - gn:c0 - SSM_D_INNER - gn + LANES] = y

    sm = smp_ref[...]
    dt = _softplus(sm + par_ref[1:2, :])
    valid = _row_valid(0, lp, lin)
    if valid is not None:
        dt = jnp.where(valid, dt, 0.0)
    cum = _chunk_cumsum(dt * (-jnp.exp(par_ref[0:1, :])), chunk)
    dt_ref[...] = dt
    cum_ref[...] = cum
    ct = cum.T
    for c in range(nch):
        ct_ref[c] = ct[:, c * chunk:(c + 1) * chunk]

    rid = lax.broadcasted_iota(jnp.int32, (chunk, chunk), 0)
    cid = lax.broadcasted_iota(jnp.int32, (chunk, chunk), 1)
    lower = cid <= rid

    def chunk_body(c, carry):
        r0 = pl.multiple_of(c * chunk, chunk)
        rs = pl.ds(r0, chunk)
        cumb = cum_ref[rs, :]
        dtb = dt_ref[rs, :]
        last_row = cum_ref[pl.ds(r0 + chunk - 1, 1), :]
        for g in range(SSM_GROUPS):
            bg = bm_ref[rs, g * SSM_STATE:(g + 1) * SSM_STATE]
            cg = cm_ref[rs, g * SSM_STATE:(g + 1) * SSM_STATE]
            cb = _dot_nt(cg, bg)
            for r in range(SSM_HPG):
                hd = g * SSM_HPG + r
                lane = SM_DT + hd
                cc = cumb[:, lane:lane + 1]
                cr = ct_ref[c, lane:lane + 1, :]
                dtc = dtb[:, lane:lane + 1]
                last = last_row[:, lane:lane + 1]
                lm = jnp.exp(jnp.where(lower, cc - cr, -jnp.inf))
                xr = xs_ref[rs, hd * SSM_HEAD_DIM:(hd + 1) * SSM_HEAD_DIM]
                hh = hs_ref[hd]
                y = _dot(cb * lm, xr * dtc) + _dot_nt(cg, hh) * jnp.exp(cc)
                yy_ref[rs, hd * SSM_HEAD_DIM:(hd + 1) * SSM_HEAD_DIM] = y + d_ref[hd] * xr
                hs_ref[hd] = hh * jnp.exp(last) + _dot_tn(xr * (jnp.exp(last - cc) * dtc), bg)
        return carry

    lax.fori_loop(0, nch, chunk_body, 0)

    gw = SSM_D_INNER // SSM_GROUPS
    for r0 in range(0, lp, ROWS):
        n = min(ROWS, lin - r0)
        if n <= 0:
            break
        for g in range(SSM_GROUPS):
            cols = slice(g * gw, (g + 1) * gw)
            t = yy_ref[r0:r0 + ROWS, cols] * _silu(zz_ref[r0:r0 + ROWS, cols])
            t = _rms_rows(t, nw_ref[:, cols])
            o_ref[0, r0:r0 + n, cols] = t[:n]

    @pl.when(l == pl.num_programs(1) - 1)
    def _():
        cnew_ref[0] = ext_ref[5 + lin:8 + lin, :]
        hnew_ref[0] = hs_ref[...]


def _ssd(u3, cbuf, h0, cw, cb, par, nw, d, *, lc, lp):
    b, l, _ = u3.shape
    nl = l // lc
    gn2 = 2 * SSM_GROUPS * SSM_STATE
    return pl.pallas_call(
        functools.partial(_ssd_kernel, lin=lc, lp=lp, chunk=SSM_CHUNK),
        grid=(b, nl),
        in_specs=[
            pl.BlockSpec((1, lc, SSM_D_INNER), lambda i, j: (i, j, C_SSZ // SSM_D_INNER)),
            pl.BlockSpec((1, lc, SSM_D_INNER), lambda i, j: (i, j, C_SSX // SSM_D_INNER)),
            pl.BlockSpec((1, lc, gn2), lambda i, j: (i, j, C_SSBC // gn2)),
            pl.BlockSpec((1, lc, LANES), lambda i, j: (i, j, C_SMALL // LANES)),
            pl.BlockSpec((1, CONV_WIDTH - 1, SSM_CONV_DIM), lambda i, j: (i, 0, 0)),
            pl.BlockSpec((1, SSM_HEADS, SSM_HEAD_DIM, SSM_STATE), lambda i, j: (i, 0, 0, 0)),
            pl.BlockSpec((CONV_WIDTH, SSM_CONV_DIM), lambda i, j: (0, 0)),
            pl.BlockSpec((1, SSM_CONV_DIM), lambda i, j: (0, 0)),
            pl.BlockSpec((SUBLANES, LANES), lambda i, j: (0, 0)),
            pl.BlockSpec((1, SSM_D_INNER), lambda i, j: (0, 0)),
            pl.BlockSpec(memory_space=pltpu.SMEM),
        ],
        out_specs=[
            pl.BlockSpec((1, lc, SSM_D_INNER), lambda i, j: (i, j, 0)),
            pl.BlockSpec((1, CONV_WIDTH - 1, SSM_CONV_DIM), lambda i, j: (i, 0, 0)),
            pl.BlockSpec((1, SSM_HEADS, SSM_HEAD_DIM, SSM_STATE), lambda i, j: (i, 0, 0, 0)),
        ],
        out_shape=[
            jax.ShapeDtypeStruct((b, l, SSM_D_INNER), F32),
            jax.ShapeDtypeStruct((b, CONV_WIDTH - 1, SSM_CONV_DIM), F32),
            jax.ShapeDtypeStruct((b, SSM_HEADS, SSM_HEAD_DIM, SSM_STATE), F32),
        ],
        scratch_shapes=[
            pltpu.VMEM((8 + lp, SSM_CONV_DIM), F32),
            pltpu.VMEM((lp, SSM_D_INNER), F32),
            pltpu.VMEM((lp, SSM_GROUPS * SSM_STATE), F32), pltpu.VMEM((lp, SSM_GROUPS * SSM_STATE), F32),
            pltpu.VMEM((lp, SSM_D_INNER), F32), pltpu.VMEM((lp, LANES), F32),
            pltpu.VMEM((lp, LANES), F32), pltpu.VMEM((lp, LANES), F32),
            pltpu.VMEM((lp // SSM_CHUNK, LANES, SSM_CHUNK), F32),
            pltpu.VMEM((lp, SSM_D_INNER), F32),
            pltpu.VMEM((SSM_HEADS, SSM_HEAD_DIM, SSM_STATE), F32),
        ],
        compiler_params=_cparams(("parallel", "arbitrary")),
        name="ssd",
    )(u3, u3, u3, u3, cbuf, h0, cw, cb, par, nw, d)


def _strict_upper_stack(n):
    j = lax.broadcasted_iota(jnp.int32, (2 * n, n), 0) % n
    s = lax.broadcasted_iota(jnp.int32, (2 * n, n), 1)
    return jnp.where(j > s, 1.0, 0.0).astype(BF16)


def _rev_excl_cumsum(la, uu):
    hi, lo = _split2(la)
    return jnp.dot(jnp.concatenate([hi, lo], axis=1), uu, preferred_element_type=F32)


def _sbp_kernel(bias_ref, q_ref, k_ref, v_ref, o_ref, *, tq, scale):
    h = pl.program_id(1)
    qi = pl.program_id(2)
    bias = bias_ref[h]
    q = q_ref[0].astype(BF16)
    uu = _strict_upper_stack(tq)
    qpos = qi * tq + lax.broadcasted_iota(jnp.int32, (tq, 1), 0)
    lane = lax.broadcasted_iota(jnp.int32, (1, tq), 1)

    def body(t, carry):
        c, acc = carry
        kj = qi - t
        ks = pl.ds(pl.multiple_of(kj * tq, tq), tq)
        z = _dot_nt(q, k_ref[0, ks, :]) * scale + bias
        valid = (kj * tq + lane) < qpos
        ls = _log_sigmoid(z)
        la = jnp.where(valid, ls - z, 0.0)
        surv = _rev_excl_cumsum(la, uu) + c
        att = jnp.where(valid, jnp.exp(ls + surv), 0.0)
        acc = acc + _dot(att, v_ref[0, ks, :])
        c = c + jnp.sum(la, axis=1, keepdims=True)
        return c, acc

    _, acc = lax.fori_loop(0, qi + 1, body, (jnp.zeros((tq, 1), F32), jnp.zeros((tq, SB_HEAD_DIM), F32)))
    o_ref[0] = acc


def _sb_prompt(u3, bias, *, tq=256):
    b, l, _ = u3.shape
    tq = min(tq, l)
    kv = lambda c0: pl.BlockSpec((1, l, SB_HEAD_DIM), lambda i, h, j: (i, 0, c0 // SB_HEAD_DIM + h))
    return pl.pallas_call(
        functools.partial(_sbp_kernel, tq=tq, scale=SB_HEAD_DIM ** -0.5),
        grid=(b, SB_HEADS, l // tq),
        in_specs=[
            pl.BlockSpec(memory_space=pltpu.SMEM),
            pl.BlockSpec((1, tq, SB_HEAD_DIM), lambda i, h, j: (i, j, C_SBQ // SB_HEAD_DIM + h)),
            kv(C_SBK), kv(C_SBV),
        ],
        out_specs=pl.BlockSpec((1, tq, SB_HEAD_DIM), lambda i, h, j: (i, j, h)),
        out_shape=jax.ShapeDtypeStruct((b, l, SB_WIDTH), F32),
        compiler_params=_cparams(("parallel", "parallel", "arbitrary")),
        name="sb_prompt",
    )(bias, u3, u3, u3)


def _sbs_kernel(pt_ref, q_ref, kc_ref, vc_ref, bias_ref, uu_ref, *rest, pp, tq, scale):
    k_refs = rest[:pp]
    v_refs = rest[pp:2 * pp]
    o_ref = rest[2 * pp]
    c_ref = rest[2 * pp + 1]
    p = pl.program_id(1)
    hq = q_ref.shape[1]
    ncol = kc_ref.shape[1]
    q = q_ref[0].astype(BF16)
    bias = bias_ref[...]
    row_head = lax.broadcasted_iota(jnp.int32, (hq, ncol), 0) // tq
    col = lax.broadcasted_iota(jnp.int32, (hq, ncol), 1)
    own = (col % SB_HEADS) == row_head

    def rev_cumsum(la):
        hi, lo = _split2(la)
        return jnp.dot(jnp.concatenate([hi, lo], axis=1), uu_ref[...], preferred_element_type=F32)

    @pl.when(p == 0)
    def _():
        z = _dot_nt(q, kc_ref[0]) * scale + bias
        t = lax.broadcasted_iota(jnp.int32, (hq, ncol), 0) % tq
        valid = own & ((col // SB_HEADS) < t)
        ls = _log_sigmoid(z)
        la = jnp.where(valid, ls - z, 0.0)
        att = jnp.where(valid, jnp.exp(ls + rev_cumsum(la)), 0.0)
        o_ref[0] = _dot(att, vc_ref[0])
        c_ref[...] = jnp.sum(la, axis=1, keepdims=True)

    z = jnp.concatenate([_dot_nt(q, k_refs[j][...]) for j in range(pp)], axis=0)
    z = z * scale + jnp.concatenate([bias] * pp, axis=0)
    valid = jnp.concatenate([own] * pp, axis=0)
    ls = _log_sigmoid(z)
    la = jnp.where(valid, ls - z, 0.0)
    surv = rev_cumsum(la)
    tot = jnp.sum(la, axis=1, keepdims=True)
    cur = c_ref[...]
    cs = []
    for j in range(pp):
        cs.append(cur)
        cur = cur + tot[j * hq:(j + 1) * hq]
    c_ref[...] = cur
    att = jnp.where(valid, jnp.exp(ls + surv + jnp.concatenate(cs, axis=0)), 0.0)
    acc = o_ref[0]
    for j in range(pp):
        acc = acc + _dot(att[j * hq:(j + 1) * hq], v_refs[j][...])
    o_ref[0] = acc


def _sb_sample(q_rows, k_cur, v_cur, bias_rows, cache_k, cache_v, page_table, layer, *, pp=8):
    b, hq, _ = q_rows.shape
    n_pages = page_table.shape[1]
    ncol = cache_k.shape[2]
    pp = math.gcd(pp, n_pages)
    tq = hq // SB_HEADS
    jj = lax.broadcasted_iota(jnp.int32, (2 * ncol, ncol), 0) % ncol
    ss = lax.broadcasted_iota(jnp.int32, (2 * ncol, ncol), 1)
    uu = jnp.where(jj > ss, 1.0, 0.0).astype(BF16)

    def page_spec(j):
        return pl.BlockSpec((None, None, ncol, SB_HEAD_DIM),
                            lambda i, p, pt: (layer, pt[i, n_pages - 1 - (p * pp + j)], 0, 0))

    grid_spec = pltpu.PrefetchScalarGridSpec(
        num_scalar_prefetch=1,
        grid=(b, n_pages // pp),
        in_specs=[
            pl.BlockSpec((1, hq, SB_HEAD_DIM), lambda i, p, pt: (i, 0, 0)),
            pl.BlockSpec((1, ncol, SB_HEAD_DIM), lambda i, p, pt: (i, 0, 0)),
            pl.BlockSpec((1, ncol, SB_HEAD_DIM), lambda i, p, pt: (i, 0, 0)),
            pl.BlockSpec((hq, ncol), lambda i, p, pt: (0, 0)),
            pl.BlockSpec((2 * ncol, ncol), lambda i, p, pt: (0, 0)),
        ] + [page_spec(j) for j in range(pp)] + [page_spec(j) for j in range(pp)],
        out_specs=pl.BlockSpec((1, hq, SB_HEAD_DIM), lambda i, p, pt: (i, 0, 0)),
        scratch_shapes=[pltpu.VMEM((hq, 1), F32)],
    )
    return pl.pallas_call(
        functools.partial(_sbs_kernel, pp=pp, tq=tq, scale=SB_HEAD_DIM ** -0.5),
        grid_spec=grid_spec,
        out_shape=jax.ShapeDtypeStruct((b, hq, SB_HEAD_DIM), F32),
        compiler_params=_cparams(("parallel", "arbitrary")),
        name="sb_sample",
    )(page_table, q_rows, k_cur, v_cur, bias_rows, uu, *([cache_k] * pp), *([cache_v] * pp))


def _merge_kernel(op_ref, od_ref, os_ref, oa_ref, ob_ref, g0_ref, g1_ref, g2_ref, g3_ref,
                  w0_ref, w1_ref, w2_ref, w3_ref, w4_ref, o_ref):
    acc = _sigmoid(g0_ref[...]) * _dot(op_ref[...], w0_ref[...])
    acc = acc + _sigmoid(g1_ref[...]) * _dot(od_ref[...], w1_ref[...])
    acc = acc + _sigmoid(g2_ref[...]) * _dot(os_ref[...], w2_ref[...])
    acc = acc + _sigmoid(g3_ref[...]) * (_dot(oa_ref[...], w3_ref[...]) + _dot(ob_ref[...], w4_ref[...]))
    o_ref[...] = acc.astype(o_ref.dtype)


def _merge(o_pool, o_dn, o_sb, o_ss, u2, w_br, *, tm, tn=512):
    m = u2.shape[0]
    row = lambda: pl.BlockSpec((tm, 512), lambda i, j: (i, 0))
    gate = lambda k: pl.BlockSpec((tm, tn), lambda i, j: (i, (C_GATE + k * D_MODEL) // tn + j))
    wrow = lambda k: pl.BlockSpec((512, tn), lambda i, j: (k, j))
    return pl.pallas_call(
        _merge_kernel,
        grid=(m // tm, D_MODEL // tn),
        in_specs=[row(), row(), row(), row(), pl.BlockSpec((tm, 512), lambda i, j: (i, 1)),
                  gate(0), gate(1), gate(2), gate(3),
                  wrow(0), wrow(1), wrow(2), wrow(3), wrow(4)],
        out_specs=pl.BlockSpec((tm, tn), lambda i, j: (i, j)),
        out_shape=jax.ShapeDtypeStruct((m, D_MODEL), BF16),
        compiler_params=_cparams(("parallel", "arbitrary")),
        name="merge",
    )(o_pool, o_dn, o_sb, o_ss, o_ss, u2, u2, u2, u2, w_br, w_br, w_br, w_br, w_br)


def _outproj_kernel(m_ref, x_ref, w_ref, g_ref, o_ref):
    mix = jnp.dot(m_ref[...], w_ref[...], preferred_element_type=F32)
    o_ref[...] = x_ref[...] + _rms_rows(mix, g_ref[...])


def _outproj(merged, x2, w, g, *, tm):
    m = x2.shape[0]
    return pl.pallas_call(
        _outproj_kernel,
        grid=(m // tm,),
        in_specs=[
            pl.BlockSpec((tm, D_MODEL), lambda i: (i, 0)),
            pl.BlockSpec((tm, D_MODEL), lambda i: (i, 0)),
            pl.BlockSpec((D_MODEL, D_MODEL), lambda i: (0, 0)),
            pl.BlockSpec((1, D_MODEL), lambda i: (0, 0)),
        ],
        out_specs=pl.BlockSpec((tm, D_MODEL), lambda i: (i, 0)),
        out_shape=jax.ShapeDtypeStruct((m, D_MODEL), F32),
        compiler_params=_cparams(("parallel",)),
        name="outproj",
    )(merged, x2, w, g)


def _mlp_kernel(x_ref, g1_ref, wu_ref, wd_ref, g2_ref, o_ref, h_ref, acc_ref, *, tm):
    f = pl.program_id(1)

    @pl.when(f == 0)
    def _():
        _norm_rows_to(h_ref, x_ref, g1_ref, tm)
        acc_ref[...] = jnp.zeros(acc_ref.shape, F32)

    a = jnp.dot(h_ref[...], wu_ref[...], preferred_element_type=F32)
    a = jnp.square(jnp.maximum(a, 0.0)).astype(BF16)
    acc_ref[...] += jnp.dot(a, wd_ref[...], preferred_element_type=F32)

    @pl.when(f == pl.num_programs(1) - 1)
    def _():
        o_ref[...] = x_ref[...] + _rms_rows(acc_ref[...], g2_ref[...])


def _mlp(x2, g1, wu, wd, g2, *, tm, tf=1024):
    m = x2.shape[0]
    return pl.pallas_call(
        functools.partial(_mlp_kernel, tm=tm),
        grid=(m // tm, D_FF // tf),
        in_specs=[
            pl.BlockSpec((tm, D_MODEL), lambda i, f: (i, 0)),
            pl.BlockSpec((1, D_MODEL), lambda i, f: (0, 0)),
            pl.BlockSpec((D_MODEL, tf), lambda i, f: (0, f)),
            pl.BlockSpec((tf, D_MODEL), lambda i, f: (f, 0)),
            pl.BlockSpec((1, D_MODEL), lambda i, f: (0, 0)),
        ],
        out_specs=pl.BlockSpec((tm, D_MODEL), lambda i, f: (i, 0)),
        out_shape=jax.ShapeDtypeStruct((m, D_MODEL), F32),
        scratch_shapes=[pltpu.VMEM((tm, D_MODEL), BF16), pltpu.VMEM((tm, D_MODEL), F32)],
        compiler_params=_cparams(("parallel", "arbitrary")),
        name="mlp",
    )(x2, g1, wu, wd, g2)


def _lane_row(vals, offset):
    return jnp.zeros((LANES,), F32).at[offset:offset + vals.shape[0]].set(vals.astype(F32))


def _layer_params(layer, p):
    w_in = p["w_in"][layer]
    w_in_p = jnp.concatenate(
        [w_in[:, :_O_DNB], w_in[:, _O_SB:_O_SSDT], w_in[:, _O_GATE:_O_END],
         w_in[:, _O_DNB:_O_SB], w_in[:, _O_SSDT:_O_GATE],
         jnp.zeros((D_MODEL, NU - _O_END), w_in.dtype)], axis=1).astype(BF16)
    zrow = jnp.zeros((LANES,), F32)
    dn_par = jnp.stack([_lane_row(p["dn_a_log"][layer], SM_DECAY), _lane_row(p["dn_dt_bias"][layer], SM_DECAY)] + [zrow] * 6)
    ss_par = jnp.stack([_lane_row(p["ssm_a_log"][layer], SM_DT), _lane_row(p["ssm_dt_bias"][layer], SM_DT)] + [zrow] * 6)
    row = lambda a: a[layer].reshape(1, -1)
    return dict(
        w_in=w_in_p, n_mix_pre=row(p["norm_mix_pre"]), n_mix_post=row(p["norm_mix_post"]),
        n_mlp_pre=row(p["norm_mlp_pre"]), n_mlp_post=row(p["norm_mlp_post"]),
        pool_w=p["pool_w"][layer].astype(BF16), pool_scale=row(p["pool_scale"]),
        dn_conv_w=p["dn_conv_w"][layer], dn_par=dn_par, dn_norm_w=row(p["dn_norm_w"]),
        sb_bias=p["sb_bias"][layer],
        ssm_conv_w=p["ssm_conv_w"][layer], ssm_conv_b=row(p["ssm_conv_b"]), ss_par=ss_par,
        ssm_norm_w=row(p["ssm_norm_w"]), ssm_d=p["ssm_d"][layer],
        w_branch=p["w_branch"][layer].astype(BF16), w_out=p["w_out"][layer].astype(BF16),
        w_up=p["w_up"][layer].astype(BF16), w_down=p["w_down"][layer].astype(BF16),
    )


def _trunk_layer(x, states, lw, *, past=None):
    b, l, _ = x.shape
    m = b * l
    pool_buf, dn_conv, dn_s, ssm_conv, ssm_h = states
    prompt = past is None
    tm_big = min(m, 1024)
    tm = min(m, 512)
    lc = min(l, 256)
    lp = max(lc, 2 * DN_CHUNK)

    x2 = x.reshape(m, D_MODEL)
    u2 = _inproj(x2, lw["n_mix_pre"], lw["w_in"], tm=tm_big)
    u3 = u2.reshape(b, l, NU)

    pos0 = 0 if prompt else past[2].shape[1] * past[0].shape[2]
    o_pool, pool_new = _pool(u3, pool_buf, lw["pool_w"], lw["pool_scale"], pos0=pos0)
    o_dn, dn_conv_new, dn_s_new = _deltanet(u3, dn_conv, dn_s, lw["dn_conv_w"], lw["dn_par"], lw["dn_norm_w"], lc=lc, lp=lp)
    o_ss, ss_conv_new, ss_h_new = _ssd(u3, ssm_conv, ssm_h, lw["ssm_conv_w"], lw["ssm_conv_b"], lw["ss_par"],
                                       lw["ssm_norm_w"], lw["ssm_d"], lc=lc, lp=lp)

    k_new = u3[:, :, C_SBK:C_SBK + SB_WIDTH]
    v_new = u3[:, :, C_SBV:C_SBV + SB_WIDTH]
    if prompt:
        o_sb = _sb_prompt(u3, lw["sb_bias"])
    else:
        cache_k, cache_v, page_table, layer = past
        ncol = cache_k.shape[2]
        q = u3[:, :, C_SBQ:C_SBQ + SB_WIDTH].reshape(b, l, SB_HEADS, SB_HEAD_DIM)
        q_rows = jnp.transpose(q, (0, 2, 1, 3)).reshape(b, SB_HEADS * l, SB_HEAD_DIM)
        k_cur = jnp.pad(k_new.reshape(b, l * SB_HEADS, SB_HEAD_DIM), ((0, 0), (0, ncol - l * SB_HEADS), (0, 0)))
        v_cur = jnp.pad(v_new.reshape(b, l * SB_HEADS, SB_HEAD_DIM), ((0, 0), (0, ncol - l * SB_HEADS), (0, 0)))
        bias_rows = jnp.broadcast_to(jnp.repeat(lw["sb_bias"], l)[:, None], (SB_HEADS * l, ncol)).astype(F32)
        acc = _sb_sample(q_rows, k_cur, v_cur, bias_rows, cache_k, cache_v, page_table, layer)
        o_sb = jnp.transpose(acc.reshape(b, SB_HEADS, l, SB_HEAD_DIM), (0, 2, 1, 3)).reshape(b, l, SB_WIDTH)

    merged = _merge(o_pool.reshape(m, -1), o_dn.reshape(m, -1), o_sb.reshape(m, -1), o_ss.reshape(m, -1),
                    u2, lw["w_branch"], tm=tm)
    x2 = _outproj(merged, x2, lw["w_out"], lw["n_mix_post"], tm=tm)
    x2 = _mlp(x2, lw["n_mlp_pre"], lw["w_up"], lw["w_down"], lw["n_mlp_post"], tm=tm)
    new_states = (k_new.reshape(b, l, SB_HEADS, SB_HEAD_DIM), v_new.reshape(b, l, SB_HEADS, SB_HEAD_DIM),
                  pool_new, dn_conv_new, dn_s_new, ss_conv_new, ss_h_new)
    return x2.reshape(b, l, D_MODEL), new_states


def kernel(x_prompt, x_sample, cache_sb_k, cache_sb_v, state_pool, state_dn_conv, state_dn_s, state_ssm_conv, state_ssm_h, page_table, norm_mix_pre, norm_mix_post, norm_mlp_pre, norm_mlp_post, w_in, pool_w, pool_scale, dn_conv_w, dn_a_log, dn_dt_bias, dn_norm_w, sb_bias, ssm_conv_w, ssm_conv_b, ssm_a_log, ssm_dt_bias, ssm_d, ssm_norm_w, w_branch, w_out, w_up, w_down):
    params = dict(norm_mix_pre=norm_mix_pre, norm_mix_post=norm_mix_post, norm_mlp_pre=norm_mlp_pre,
                  norm_mlp_post=norm_mlp_post, w_in=w_in, pool_w=pool_w, pool_scale=pool_scale,
                  dn_conv_w=dn_conv_w, dn_a_log=dn_a_log, dn_dt_bias=dn_dt_bias, dn_norm_w=dn_norm_w,
                  sb_bias=sb_bias, ssm_conv_w=ssm_conv_w, ssm_conv_b=ssm_conv_b, ssm_a_log=ssm_a_log,
                  ssm_dt_bias=ssm_dt_bias, ssm_d=ssm_d, ssm_norm_w=ssm_norm_w, w_branch=w_branch,
                  w_out=w_out, w_up=w_up, w_down=w_down)
    depth = w_in.shape[0]
    bp = x_prompt.shape[0]
    dt_ = x_prompt.dtype
    zero_states = (jnp.zeros((bp, POOL_BUF, POOL_WIDTH), dt_),
                   jnp.zeros((bp, CONV_WIDTH - 1, DN_CONV_DIM), dt_),
                   jnp.zeros((bp, DN_HEADS, DN_DK, DN_DV), dt_),
                   jnp.zeros((bp, CONV_WIDTH - 1, SSM_CONV_DIM), dt_),
                   jnp.zeros((bp, SSM_HEADS, SSM_HEAD_DIM, SSM_STATE), dt_))
    n_pool, page = cache_sb_k.shape[1], cache_sb_k.shape[2]
    cache_k = cache_sb_k.reshape(depth, n_pool, page * SB_HEADS, SB_HEAD_DIM)
    cache_v = cache_sb_v.reshape(depth, n_pool, page * SB_HEADS, SB_HEAD_DIM)
    y_prompt, y_sample = x_prompt, x_sample
    new_p, new_s = [], []
    for layer in range(depth):
        lw = _layer_params(layer, params)
        y_prompt, st_p = _trunk_layer(y_prompt, zero_states, lw)
        sample_states = (state_pool[layer], state_dn_conv[layer], state_dn_s[layer],
                         state_ssm_conv[layer], state_ssm_h[layer])
        y_sample, st_s = _trunk_layer(y_sample, sample_states, lw, past=(cache_k, cache_v, page_table, layer))
        new_p.append(st_p)
        new_s.append(st_s)
    outs_p = [jnp.stack(t) for t in zip(*new_p)]
    outs_s = [jnp.stack(t) for t in zip(*new_s)]
    return (y_prompt, y_sample, *outs_p, *outs_s)
```

```python
import functools
import math

import jax
import jax.numpy as jnp
from jax import lax
from jax.experimental import pallas as pl
from jax.experimental.pallas import tpu as pltpu

F32 = jnp.float32
BF16 = jnp.bfloat16

D_MODEL = 2048
N_BRANCH = 4
RMS_EPS = 1e-6
CONV_WIDTH = 4
POOL_WINDOWS = (2, 4, 8, 16)
POOL_WIDTH = D_MODEL // 4
POOL_GW = POOL_WIDTH // 4
POOL_BUF = max(POOL_WINDOWS) - 1
DN_HEADS = 4
DN_DK = 128
DN_DV = 128
DN_QK = DN_HEADS * DN_DK
DN_VW = DN_HEADS * DN_DV
DN_CONV_DIM = 2 * DN_QK + DN_VW
DN_CHUNK = 64
SB_HEADS = 4
SB_HEAD_DIM = 128
SB_WIDTH = SB_HEADS * SB_HEAD_DIM
SSM_D_INNER = D_MODEL // 2
SSM_HEAD_DIM = 64
SSM_HEADS = SSM_D_INNER // SSM_HEAD_DIM
SSM_GROUPS = 2
SSM_HPG = SSM_HEADS // SSM_GROUPS
SSM_STATE = 128
SSM_CONV_DIM = SSM_D_INNER + 2 * SSM_GROUPS * SSM_STATE
SSM_CHUNK = 64
D_FF = 4 * D_MODEL

LANES = 128
SUBLANES = 8

C_POOL = 0
C_DNQ, C_DNK, C_DNV, C_DNZ = 512, 1024, 1536, 2048
C_SBQ, C_SBK, C_SBV = 2560, 3072, 3584
C_SSZ, C_SSX, C_SSBC = 4096, 5120, 6144
C_GATE = 6656
C_SMALL = 14848
NU = 15360
_O_DNB = POOL_WIDTH + DN_CONV_DIM + DN_VW
_O_SB = _O_DNB + 2 * DN_HEADS
_O_SSDT = _O_SB + 3 * SB_WIDTH + SSM_D_INNER + SSM_CONV_DIM
_O_GATE = _O_SSDT + SSM_HEADS
_O_END = _O_GATE + N_BRANCH * D_MODEL
SM_BETA, SM_DECAY, SM_DT = 0, DN_HEADS, 2 * DN_HEADS
SM_REP = 32

VMEM_LIMIT_MB = 56


def _cparams(sem, vmem_mb=VMEM_LIMIT_MB):
    return pltpu.CompilerParams(dimension_semantics=sem, vmem_limit_bytes=vmem_mb * 1024 * 1024)


def _sigmoid(x):
    return 1.0 / (1.0 + jnp.exp(-x))


def _silu(x):
    return x * _sigmoid(x)


def _softplus(x):
    return jnp.maximum(x, 0.0) + jnp.log1p(jnp.exp(-jnp.abs(x)))


def _log_sigmoid(x):
    return jnp.minimum(x, 0.0) - jnp.log1p(jnp.exp(-jnp.abs(x)))


def _dot(a, b):
    return jnp.dot(a.astype(BF16), b.astype(BF16), preferred_element_type=F32)


def _dot_nt(a, b):
    return lax.dot_general(a.astype(BF16), b.astype(BF16), (((1,), (1,)), ((), ())), preferred_element_type=F32)


def _dot_tn(a, b):
    return lax.dot_general(a.astype(BF16), b.astype(BF16), (((0,), (0,)), ((), ())), preferred_element_type=F32)


def _split2(a):
    hi = a.astype(BF16)
    lo = (a - hi.astype(F32)).astype(BF16)
    return hi, lo


def _expand_heads(src, n_heads, width, lane0):
    k = lax.broadcasted_iota(jnp.int32, (LANES, n_heads * width), 0)
    n = lax.broadcasted_iota(jnp.int32, (LANES, n_heads * width), 1)
    sel = jnp.where((k - lane0) % SM_REP == n // width, 1.0, 0.0)
    sel = jnp.where((k >= lane0) & (k < lane0 + 3 * SM_REP), sel, 0.0).astype(BF16)
    lane = lax.broadcasted_iota(jnp.int32, src.shape, 1)
    hi = src.astype(BF16).astype(F32)
    r1 = src - hi
    mid = r1.astype(BF16).astype(F32)
    lo = r1 - mid
    pieces = jnp.where(lane < lane0 + SM_REP, hi, jnp.where(lane < lane0 + 2 * SM_REP, mid, lo))
    return jnp.dot(pieces.astype(BF16), sel, preferred_element_type=F32)


def _dot3(a, b):
    ah, al = _split2(a)
    bh, bl = _split2(b)
    d = lambda x, y: jnp.dot(x, y, preferred_element_type=F32)
    return d(ah, bh) + (d(ah, bl) + d(al, bh))


def _rms_rows(x, g):
    return x * lax.rsqrt(jnp.mean(x * x, axis=-1, keepdims=True) + RMS_EPS) * g


def _norm_rows_to(h_ref, x_ref, g_ref, tm):
    ch = min(tm, 256)

    def body(r, c):
        rs = pl.ds(pl.multiple_of(r * ch, ch), ch)
        h_ref[rs, :] = _rms_rows(x_ref[rs, :], g_ref[...]).astype(h_ref.dtype)
        return c

    lax.fori_loop(0, tm // ch, body, 0)


def _inproj_kernel(x_ref, g_ref, w_ref, o_ref, h_ref, *, tm):
    @pl.when(pl.program_id(1) == 0)
    def _():
        _norm_rows_to(h_ref, x_ref, g_ref, tm)

    o_ref[...] = jnp.dot(h_ref[...], w_ref[...], preferred_element_type=F32)


def _inproj(x2, g, w_p, *, tm, tn=1024):
    m = x2.shape[0]
    return pl.pallas_call(
        functools.partial(_inproj_kernel, tm=tm),
        grid=(m // tm, NU // tn),
        in_specs=[
            pl.BlockSpec((tm, D_MODEL), lambda i, j: (i, 0)),
            pl.BlockSpec((1, D_MODEL), lambda i, j: (0, 0)),
            pl.BlockSpec((D_MODEL, tn), lambda i, j: (0, j)),
        ],
        out_specs=pl.BlockSpec((tm, tn), lambda i, j: (i, j)),
        out_shape=jax.ShapeDtypeStruct((m, NU), F32),
        scratch_shapes=[pltpu.VMEM((tm, D_MODEL), BF16)],
        compiler_params=_cparams(("parallel", "arbitrary")),
        name="inproj",
    )(x2, g, w_p)


def _pool_kernel(u_ref, buf_ref, w_ref, sc_ref, o_ref, new_ref, ext_ref, *, lin, lp, pos0):
    ext_ref[0:16, :] = jnp.zeros((16, POOL_WIDTH), F32)
    ext_ref[1:16, :] = buf_ref[0]
    if lin < lp:
        ext_ref[16:16 + lp, :] = jnp.zeros((lp, POOL_WIDTH), F32)
    ext_ref[16:16 + lin, :] = u_ref[0]
    ch = min(lp, 256)
    for c0 in range(0, lp, ch):
        pos = pos0 + c0 + lax.broadcasted_iota(jnp.int32, (ch, 1), 0)
        for gi, w in enumerate(POOL_WINDOWS):
            cols = slice(gi * POOL_GW, (gi + 1) * POOL_GW)
            s = ext_ref[16 + c0:16 + c0 + ch, cols]
            tot = s
            for k in range(1, w):
                tot = tot + ext_ref[16 + c0 - k:16 + c0 - k + ch, cols]
            cnt = jnp.minimum(pos + 1, w).astype(F32)
            y = tot / cnt - s
            yo = _dot(y, w_ref[gi]) * sc_ref[:, cols]
            n = min(ch, lin - c0)
            o_ref[0, c0:c0 + n, cols] = yo[:n]
    new_ref[0] = ext_ref[1 + lin:16 + lin, :]


def _pool(u3, buf, w, sc, *, pos0):
    b, lin, _ = u3.shape
    lp = max(lin, SUBLANES)
    return pl.pallas_call(
        functools.partial(_pool_kernel, lin=lin, lp=lp, pos0=pos0),
        grid=(b,),
        in_specs=[
            pl.BlockSpec((1, lin, POOL_WIDTH), lambda i: (i, 0, C_POOL // POOL_WIDTH)),
            pl.BlockSpec((1, POOL_BUF, POOL_WIDTH), lambda i: (i, 0, 0)),
            pl.BlockSpec((4, POOL_GW, POOL_GW), lambda i: (0, 0, 0)),
            pl.BlockSpec((1, POOL_WIDTH), lambda i: (0, 0)),
        ],
        out_specs=[
            pl.BlockSpec((1, lin, POOL_WIDTH), lambda i: (i, 0, 0)),
            pl.BlockSpec((1, POOL_BUF, POOL_WIDTH), lambda i: (i, 0, 0)),
        ],
        out_shape=[
            jax.ShapeDtypeStruct((b, lin, POOL_WIDTH), F32),
            jax.ShapeDtypeStruct((b, POOL_BUF, POOL_WIDTH), F32),
        ],
        scratch_shapes=[pltpu.VMEM((16 + lp, POOL_WIDTH), F32)],
        compiler_params=_cparams(("parallel",)),
        name="pool",
    )(u3, buf, w, sc)


ROWS = 64


def _fill_ext(ext_ref, cbuf_ref, parts, *, lin, lp):
    l = pl.program_id(1)

    @pl.when(l == 0)
    def _():
        ext_ref[5:8, :] = cbuf_ref[0]

    @pl.when(l > 0)
    def _():
        ext_ref[5:8, :] = ext_ref[5 + lin:8 + lin, :]

    if lin < lp:
        ext_ref[8:8 + lp, :] = jnp.zeros((lp, ext_ref.shape[1]), F32)
    for ref, c0, width in parts:
        ext_ref[8:8 + lin, c0:c0 + width] = ref[0]


def _conv_silu(ext_ref, r0, n, cols, w_ref, b_ref):
    acc = None
    for i in range(CONV_WIDTH):
        part = ext_ref[5 + r0 + i:5 + r0 + i + n, cols] * w_ref[i:i + 1, cols]
        acc = part if acc is None else acc + part
    if b_ref is not None:
        acc = acc + b_ref[:, cols]
    return _silu(acc)


def _pad_rows(dst_ref, src_ref, *, lin, lp):
    if lin < lp:
        dst_ref[...] = jnp.zeros(dst_ref.shape, F32)
    dst_ref[0:lin, :] = src_ref[0]


def _row_valid(r0, n, lin):
    if r0 + n <= lin:
        return None
    return (r0 + lax.broadcasted_iota(jnp.int32, (n, 1), 0)) < lin


def _chunk_cumsum(x, chunk):
    rin = lax.broadcasted_iota(jnp.int32, x.shape, 0) % chunk
    s = 1
    while s < chunk:
        x = x + jnp.where(rin >= s, pltpu.roll(x, s, 0), 0.0)
        s *= 2
    return x


def _diag_inv2(a0, a1):
    c = a0.shape[0]
    hb = c // 2
    ng = hb // SUBLANES
    sub = lax.broadcasted_iota(jnp.int32, (SUBLANES, 2 * c), 0)
    lane = lax.broadcasted_iota(jnp.int32, (SUBLANES, 2 * c), 1)
    base = (lane // hb) * hb
    lmod = lane - base
    a01 = jnp.concatenate([a0, a1], axis=1)
    odd = ((lax.broadcasted_iota(jnp.int32, (hb, 2 * c), 1) // hb) % 2) == 1
    packed = jnp.where(odd, a01[hb:], a01[:hb])
    racc = [jnp.zeros((SUBLANES, 2 * c), F32) for _ in range(ng)]
    tg = [jnp.zeros((SUBLANES, 2 * c), F32) for _ in range(ng)]
    for j in range(hb):
        gj, rj = divmod(j, SUBLANES)
        t_j = jnp.where(lmod[0:1] == j, 1.0, 0.0) - racc[gj][rj:rj + 1, :]
        tg[gj] = jnp.where(sub == rj, t_j, tg[gj])
        if j == hb - 1:
            break
        for g in range(gj, ng):
            col = jnp.take_along_axis(packed[g * SUBLANES:(g + 1) * SUBLANES], base + j, axis=1)
            racc[g] = racc[g] + col * t_j
    dinv = jnp.concatenate(tg, axis=0)
    bd = jnp.concatenate([jnp.where(odd, 0.0, dinv), jnp.where(odd, dinv, 0.0)], axis=0)
    return [bd[:, :c], bd[:, c:]]


def _dn_kernel(q_ref, k_ref, v_ref, z_ref, sm_ref, cbuf_ref, s0_ref, cw_ref, par_ref, nw_ref,
               o_ref, cnew_ref, snew_ref,
               ext_ref, qn_ref, kn_ref, vv_ref, zz_ref, smp_ref, be_ref, gc_ref, gt_ref, oo_ref, s_ref,
               uc_ref, wc_ref, qe_ref, kt_ref, at_ref, *, lin, lp, chunk):
    l = pl.program_id(1)
    nch = lp // chunk

    @pl.when(l == 0)
    def _():
        s_ref[...] = s0_ref[0]

    _fill_ext(ext_ref, cbuf_ref, ((q_ref, 0, DN_QK), (k_ref, DN_QK, DN_QK), (v_ref, 2 * DN_QK, DN_VW)), lin=lin, lp=lp)
    _pad_rows(zz_ref, z_ref, lin=lin, lp=lp)
    _pad_rows(smp_ref, sm_ref, lin=lin, lp=lp)

    for r0 in range(0, lp, ROWS):
        valid = _row_valid(r0, ROWS, lin)
        for part, dst in enumerate((qn_ref, kn_ref, vv_ref)):
            for h in range(DN_HEADS):
                cols = slice(part * DN_QK + h * DN_DK, part * DN_QK + (h + 1) * DN_DK)
                y = _conv_silu(ext_ref, r0, ROWS, cols, cw_ref, None)
                if part < 2:
                    y = y * lax.rsqrt(jnp.sum(y * y, axis=-1, keepdims=True) + 1e-6)
                if part == 0:
                    y = y * (DN_DK ** -0.5)
                if valid is not None:
                    y = jnp.where(valid, y, 0.0)
                dst[r0:r0 + ROWS, h * DN_DK:(h + 1) * DN_DK] = y

    sm = smp_ref[...]
    beta = _sigmoid(sm)
    g = -jnp.exp(par_ref[0:1, :]) * _softplus(sm + par_ref[1:2, :])
    valid = _row_valid(0, lp, lin)
    if valid is not None:
        beta = jnp.where(valid, beta, 0.0)
        g = jnp.where(valid, g, 0.0)
    gcum = _chunk_cumsum(g, chunk)
    be_ref[...] = beta
    gc_ref[...] = gcum
    gt = gcum.T
    for c in range(nch):
        gt_ref[c] = gt[:, c * chunk:(c + 1) * chunk]

    rid = lax.broadcasted_iota(jnp.int32, (chunk, chunk), 0)
    cid = lax.broadcasted_iota(jnp.int32, (chunk, chunk), 1)
    strict_lower = cid < rid
    lower = cid <= rid
    off = (rid >= chunk // 2) & (cid < chunk // 2)
    heads = range(DN_HEADS)
    hsl = [slice(h * DN_DK, (h + 1) * DN_DK) for h in heads]

    def prep_body(c, carry):
        r0 = pl.multiple_of(c * chunk, chunk)
        rs = pl.ds(r0, chunk)
        gcb = gc_ref[rs, :]
        beb = be_ref[rs, :]
        glast_row = gc_ref[pl.ds(r0 + chunk - 1, 1), :]
        gc = [gcb[:, SM_DECAY + h:SM_DECAY + h + 1] for h in heads]
        bt = [beb[:, SM_BETA + h:SM_BETA + h + 1] for h in heads]
        gr = [gt_ref[c, SM_DECAY + h:SM_DECAY + h + 1, :] for h in heads]
        kh = [kn_ref[rs, hsl[h]] for h in heads]
        qh = [qn_ref[rs, hsl[h]] for h in heads]
        kb = [kh[h] * bt[h] for h in heads]
        eg = [jnp.exp(gc[h]) for h in heads]
        amat = [_dot_nt(kb[h], kh[h]) * jnp.exp(jnp.where(strict_lower, gc[h] - gr[h], -jnp.inf)) for h in heads]
        for h in heads:
            at_ref[c * DN_HEADS + h] = _dot_nt(qh[h], kh[h]) * jnp.exp(jnp.where(lower, gc[h] - gr[h], -jnp.inf))
            qe_ref[rs, hsl[h]] = qh[h] * eg[h]
            kt_ref[rs, hsl[h]] = kh[h] * jnp.exp(glast_row[:, SM_DECAY + h:SM_DECAY + h + 1] - gc[h])
        dinv = _diag_inv2(amat[0], amat[1]) + _diag_inv2(amat[2], amat[3])
        inner = [_dot3(jnp.where(off, amat[h], 0.0), dinv[h]) for h in heads]
        tmat = [dinv[h] - _dot3(dinv[h], inner[h]) for h in heads]
        rhs = [jnp.concatenate([vv_ref[rs, hsl[h]] * bt[h], kb[h] * eg[h]], axis=1) for h in heads]
        sol = [_dot3(tmat[h], rhs[h]) for h in heads]
        for h in heads:
            uc_ref[rs, hsl[h]] = sol[h][:, :DN_DV]
            wc_ref[rs, hsl[h]] = sol[h][:, DN_DV:]
        return carry

    lax.fori_loop(0, nch, prep_body, 0)

    def scan_body(c, carry):
        r0 = pl.multiple_of(c * chunk, chunk)
        rs = pl.ds(r0, chunk)
        glast_row = gc_ref[pl.ds(r0 + chunk - 1, 1), :]
        sh = [s_ref[h] for h in heads]
        ws = [_dot(wc_ref[rs, hsl[h]], sh[h]) for h in heads]
        qs = [_dot(qe_ref[rs, hsl[h]], sh[h]) for h in heads]
        v_new = [uc_ref[rs, hsl[h]] - ws[h] for h in heads]
        o2 = [_dot(at_ref[c * DN_HEADS + h], v_new[h]) for h in heads]
        kv = [_dot_tn(kt_ref[rs, hsl[h]], v_new[h]) for h in heads]
        for h in heads:
            s_ref[h] = sh[h] * jnp.exp(glast_row[:, SM_DECAY + h:SM_DECAY + h + 1]) + kv[h]
            oo_ref[rs, hsl[h]] = _rms_rows(qs[h] + o2[h], nw_ref[...]) * _silu(zz_ref[rs, hsl[h]])
        return carry

    lax.fori_loop(0, nch, scan_body, 0)
    o_ref[0] = oo_ref[0:lin, :]

    @pl.when(l == pl.num_programs(1) - 1)
    def _():
        cnew_ref[0] = ext_ref[5 + lin:8 + lin, :]
        snew_ref[0] = s_ref[...]


def _deltanet(u3, cbuf, s0, cw, par, nw, *, lc, lp):
    b, l, _ = u3.shape
    nl = l // lc
    blk = lambda c0: pl.BlockSpec((1, lc, 512), lambda i, j: (i, j, c0 // 512))
    return pl.pallas_call(
        functools.partial(_dn_kernel, lin=lc, lp=lp, chunk=DN_CHUNK),
        grid=(b, nl),
        in_specs=[
            blk(C_DNQ), blk(C_DNK), blk(C_DNV), blk(C_DNZ),
            pl.BlockSpec((1, lc, LANES), lambda i, j: (i, j, C_SMALL // LANES)),
            pl.BlockSpec((1, CONV_WIDTH - 1, DN_CONV_DIM), lambda i, j: (i, 0, 0)),
            pl.BlockSpec((1, DN_HEADS, DN_DK, DN_DV), lambda i, j: (i, 0, 0, 0)),
            pl.BlockSpec((CONV_WIDTH, DN_CONV_DIM), lambda i, j: (0, 0)),
            pl.BlockSpec((SUBLANES, LANES), lambda i, j: (0, 0)),
            pl.BlockSpec((1, DN_DV), lambda i, j: (0, 0)),
        ],
        out_specs=[
            pl.BlockSpec((1, lc, DN_VW), lambda i, j: (i, j, 0)),
            pl.BlockSpec((1, CONV_WIDTH - 1, DN_CONV_DIM), lambda i, j: (i, 0, 0)),
            pl.BlockSpec((1, DN_HEADS, DN_DK, DN_DV), lambda i, j: (i, 0, 0, 0)),
        ],
        out_shape=[
            jax.ShapeDtypeStruct((b, l, DN_VW), F32),
            jax.ShapeDtypeStruct((b, CONV_WIDTH - 1, DN_CONV_DIM), F32),
            jax.ShapeDtypeStruct((b, DN_HEADS, DN_DK, DN_DV), F32),
        ],
        scratch_shapes=[
            pltpu.VMEM((8 + lp, DN_CONV_DIM), F32),
            pltpu.VMEM((lp, DN_QK), F32), pltpu.VMEM((lp, DN_QK), F32), pltpu.VMEM((lp, DN_VW), F32),
            pltpu.VMEM((lp, DN_VW), F32), pltpu.VMEM((lp, LANES), F32),
            pltpu.VMEM((lp, LANES), F32), pltpu.VMEM((lp, LANES), F32),
            pltpu.VMEM((lp // DN_CHUNK, LANES, DN_CHUNK), F32),
            pltpu.VMEM((lp, DN_VW), F32),
            pltpu.VMEM((DN_HEADS, DN_DK, DN_DV), F32),
            pltpu.VMEM((lp, DN_VW), F32), pltpu.VMEM((lp, DN_QK), F32),
            pltpu.VMEM((lp, DN_QK), F32), pltpu.VMEM((lp, DN_QK), F32),
            pltpu.VMEM((lp // DN_CHUNK * DN_HEADS, DN_CHUNK, DN_CHUNK), F32),
        ],
        compiler_params=_cparams(("parallel", "arbitrary")),
        name="deltanet",
    )(u3, u3, u3, u3, u3, cbuf, s0, cw, par, nw)


def _ssd_kernel(z_ref, x_ref, bc_ref, sm_ref, cbuf_ref, h0_ref, cw_ref, cb_ref, par_ref, nw_ref, dsk_ref,
                o_ref, cnew_ref, hnew_ref,
                ext_ref, xs_ref, bm_ref, cm_ref, zz_ref, smp_ref, ce_ref, de_ref, ct_ref, yy_ref, ht_ref,
                *, lin, lp, chunk):
    assert chunk == SSM_HEAD_DIM
    l = pl.program_id(1)
    nch = lp // chunk
    gn = SSM_GROUPS * SSM_STATE

    @pl.when(l == 0)
    def _():
        for g in range(SSM_GROUPS):
            hg = h0_ref[0, g * SSM_HPG:(g + 1) * SSM_HPG]
            ht_ref[g] = hg.reshape(SSM_HPG * SSM_HEAD_DIM, SSM_STATE).T

    _fill_ext(ext_ref, cbuf_ref, ((x_ref, 0, SSM_D_INNER), (bc_ref, SSM_D_INNER, 2 * gn)), lin=lin, lp=lp)
    _pad_rows(zz_ref, z_ref, lin=lin, lp=lp)
    _pad_rows(smp_ref, sm_ref, lin=lin, lp=lp)

    for r0 in range(0, lp, ROWS):
        valid = _row_valid(r0, ROWS, lin)
        for c0 in range(0, SSM_CONV_DIM, LANES):
            y = _conv_silu(ext_ref, r0, ROWS, slice(c0, c0 + LANES), cw_ref, cb_ref)
            if valid is not None:
                y = jnp.where(valid, y, 0.0)
            if c0 < SSM_D_INNER:
                xs_ref[r0:r0 + ROWS, c0:c0 + LANES] = y
            elif c0 < SSM_D_INNER + gn:
                bm_ref[r0:r0 + ROWS, c0 - SSM_D_INNER:c0 - SSM_D_INNER + LANES] = y
            else:
                cm_ref[r0:r0 + ROWS, c0 - SSM_D_INNER - gn:c0 - SSM_D_INNER - gn + LANES] = y

    sm = smp_ref[...]
    dt = _softplus(sm + par_ref[1:2, :])
    valid = _row_valid(0, lp, lin)
    if valid is not None:
        dt = jnp.where(valid, dt, 0.0)
    cum = _chunk_cumsum(dt * (-jnp.exp(par_ref[0:1, :])), chunk)
    ct = cum.T
    for c in range(nch):
        ct_ref[c] = ct[:, c * chunk:(c + 1) * chunk]

    spread = _expand_heads(jnp.concatenate([cum, dt], axis=0), SSM_HEADS, SSM_HEAD_DIM, SM_DT)
    ce_ref[...] = spread[:lp]
    de_ref[...] = spread[lp:]

    pw = 2 * SSM_HEAD_DIM
    row2 = lax.broadcasted_iota(jnp.int32, (chunk, 2 * chunk), 0)
    lane2 = lax.broadcasted_iota(jnp.int32, (chunk, 2 * chunk), 1)
    lower2 = (lane2 % chunk) <= row2
    first = lax.broadcasted_iota(jnp.int32, (chunk, pw), 1) < SSM_HEAD_DIM
    groups = range(SSM_GROUPS)
    gw_ = SSM_HPG * SSM_HEAD_DIM

    def chunk_body(c, carry):
        r0 = pl.multiple_of(c * chunk, chunk)
        rs = pl.ds(r0, chunk)
        bg = [bm_ref[rs, g * SSM_STATE:(g + 1) * SSM_STATE] for g in groups]
        cg = [cm_ref[rs, g * SSM_STATE:(g + 1) * SSM_STATE] for g in groups]
        ce = [ce_ref[rs, g * gw_:(g + 1) * gw_] for g in groups]
        de = [de_ref[rs, g * gw_:(g + 1) * gw_] for g in groups]
        xg = [xs_ref[rs, g * gw_:(g + 1) * gw_] for g in groups]
        ht = [ht_ref[g] for g in groups]
        cb = [_dot_nt(cg[g], bg[g]) for g in groups]
        ys = [_dot(cg[g], ht[g]) * jnp.exp(ce[g]) for g in groups]
        for g in groups:
            last = ce[g][chunk - 1:chunk, :]
            ht_ref[g] = ht[g] * jnp.exp(last) + _dot_tn(bg[g], xg[g] * (jnp.exp(last - ce[g]) * de[g]))
        for g in groups:
            cb2 = jnp.concatenate([cb[g], cb[g]], axis=1)
            xd = xg[g] * de[g]
            for p in range(SSM_HPG // 2):
                ps = slice(p * pw, (p + 1) * pw)
                hd = g * SSM_HPG + 2 * p
                ctp = ct_ref[c, SM_DT + hd:SM_DT + hd + 2, :]
                crp = jnp.concatenate([ctp[0:1], ctp[1:2]], axis=1)
                lm = jnp.exp(jnp.where(lower2, ce[g][:, ps] - crp, -jnp.inf))
                xdp = xd[:, ps]
                bd = jnp.concatenate([jnp.where(first, xdp, 0.0), jnp.where(first, 0.0, xdp)], axis=0)
                cols = slice(g * gw_ + p * pw, g * gw_ + (p + 1) * pw)
                yy_ref[rs, cols] = _dot(cb2 * lm, bd) + ys[g][:, ps] + dsk_ref[:, cols] * xg[g][:, ps]
        return carry

    lax.fori_loop(0, nch, chunk_body, 0)

    gw = SSM_D_INNER // SSM_GROUPS
    for r0 in range(0, lp, ROWS):
        n = min(ROWS, lin - r0)
        if n <= 0:
            break
        for g in range(SSM_GROUPS):
            cols = slice(g * gw, (g + 1) * gw)
            t = yy_ref[r0:r0 + ROWS, cols] * _silu(zz_ref[r0:r0 + ROWS, cols])
            t = _rms_rows(t, nw_ref[:, cols])
            o_ref[0, r0:r0 + n, cols] = t[:n]

    @pl.when(l == pl.num_programs(1) - 1)
    def _():
        cnew_ref[0] = ext_ref[5 + lin:8 + lin, :]
        for g in range(SSM_GROUPS):
            hnew_ref[0, g * SSM_HPG:(g + 1) * SSM_HPG] = ht_ref[g].T.reshape(SSM_HPG, SSM_HEAD_DIM, SSM_STATE)


def _ssd(u3, cbuf, h0, cw, cb, par, nw, d, *, lc, lp):
    b, l, _ = u3.shape
    nl = l // lc
    gn2 = 2 * SSM_GROUPS * SSM_STATE
    return pl.pallas_call(
        functools.partial(_ssd_kernel, lin=lc, lp=lp, chunk=SSM_CHUNK),
        grid=(b, nl),
        in_specs=[
            pl.BlockSpec((1, lc, SSM_D_INNER), lambda i, j: (i, j, C_SSZ // SSM_D_INNER)),
            pl.BlockSpec((1, lc, SSM_D_INNER), lambda i, j: (i, j, C_SSX // SSM_D_INNER)),
            pl.BlockSpec((1, lc, gn2), lambda i, j: (i, j, C_SSBC // gn2)),
            pl.BlockSpec((1, lc, LANES), lambda i, j: (i, j, C_SMALL // LANES)),
            pl.BlockSpec((1, CONV_WIDTH - 1, SSM_CONV_DIM), lambda i, j: (i, 0, 0)),
            pl.BlockSpec((1, SSM_HEADS, SSM_HEAD_DIM, SSM_STATE), lambda i, j: (i, 0, 0, 0)),
            pl.BlockSpec((CONV_WIDTH, SSM_CONV_DIM), lambda i, j: (0, 0)),
            pl.BlockSpec((1, SSM_CONV_DIM), lambda i, j: (0, 0)),
            pl.BlockSpec((SUBLANES, LANES), lambda i, j: (0, 0)),
            pl.BlockSpec((1, SSM_D_INNER), lambda i, j: (0, 0)),
            pl.BlockSpec((1, SSM_D_INNER), lambda i, j: (0, 0)),
        ],
        out_specs=[
            pl.BlockSpec((1, lc, SSM_D_INNER), lambda i, j: (i, j, 0)),
            pl.BlockSpec((1, CONV_WIDTH - 1, SSM_CONV_DIM), lambda i, j: (i, 0, 0)),
            pl.BlockSpec((1, SSM_HEADS, SSM_HEAD_DIM, SSM_STATE), lambda i, j: (i, 0, 0, 0)),
        ],
        out_shape=[
            jax.ShapeDtypeStruct((b, l, SSM_D_INNER), F32),
            jax.ShapeDtypeStruct((b, CONV_WIDTH - 1, SSM_CONV_DIM), F32),
            jax.ShapeDtypeStruct((b, SSM_HEADS, SSM_HEAD_DIM, SSM_STATE), F32),
        ],
        scratch_shapes=[
            pltpu.VMEM((8 + lp, SSM_CONV_DIM), F32),
            pltpu.VMEM((lp, SSM_D_INNER), F32),
            pltpu.VMEM((lp, SSM_GROUPS * SSM_STATE), F32), pltpu.VMEM((lp, SSM_GROUPS * SSM_STATE), F32),
            pltpu.VMEM((lp, SSM_D_INNER), F32), pltpu.VMEM((lp, LANES), F32),
            pltpu.VMEM((lp, SSM_D_INNER), F32), pltpu.VMEM((lp, SSM_D_INNER), F32),
            pltpu.VMEM((lp // SSM_CHUNK, LANES, SSM_CHUNK), F32),
            pltpu.VMEM((lp, SSM_D_INNER), F32),
            pltpu.VMEM((SSM_GROUPS, SSM_STATE, SSM_HPG * SSM_HEAD_DIM), F32),
        ],
        compiler_params=_cparams(("parallel", "arbitrary")),
        name="ssd",
    )(u3, u3, u3, u3, cbuf, h0, cw, cb, par, nw, d)


def _strict_upper_stack(n):
    j = lax.broadcasted_iota(jnp.int32, (2 * n, n), 0) % n
    s = lax.broadcasted_iota(jnp.int32, (2 * n, n), 1)
    return jnp.where(j > s, 1.0, 0.0).astype(BF16)


def _rev_excl_cumsum(la, uu):
    hi, lo = _split2(la)
    return jnp.dot(jnp.concatenate([hi, lo], axis=1), uu, preferred_element_type=F32)


def _sbp_kernel(bias_ref, q_ref, k_ref, v_ref, o_ref, *, tq, scale):
    h = pl.program_id(1)
    qi = pl.program_id(2)
    bias = bias_ref[h]
    q = q_ref[0].astype(BF16)
    uu = _strict_upper_stack(tq)

    def sweep(blocks, carry, diagonal):
        c, acc = carry
        rows = [pl.ds(kj * tq if isinstance(kj, int) else pl.multiple_of(kj * tq, tq), tq) for kj in blocks]
        zs = [_dot_nt(q, k_ref[0, r, :]) * scale + bias for r in rows]
        lss, las = [], []
        for z in zs:
            l1 = jnp.log1p(jnp.exp(-jnp.abs(z)))
            lss.append(jnp.minimum(z, 0.0) - l1)
            las.append(-jnp.maximum(z, 0.0) - l1)
        if diagonal:
            valid = lax.broadcasted_iota(jnp.int32, (tq, tq), 1) < lax.broadcasted_iota(jnp.int32, (tq, tq), 0)
            las = [jnp.where(valid, la, 0.0) for la in las]
        survs = [_rev_excl_cumsum(la, uu) for la in las]
        for ls, la, surv, r in zip(lss, las, survs, rows):
            att = jnp.exp(ls + surv + c)
            if diagonal:
                att = jnp.where(valid, att, 0.0)
            acc = acc + _dot(att, v_ref[0, r, :])
            c = c + jnp.sum(la, axis=1, keepdims=True)
        return c, acc

    carry = (jnp.zeros((tq, 1), F32), jnp.zeros((tq, SB_HEAD_DIM), F32))
    carry = sweep([qi], carry, True)
    carry = lax.fori_loop(0, lax.shift_right_logical(qi, 1),
                          lambda t, cr: sweep([qi - 1 - 2 * t, qi - 2 - 2 * t], cr, False), carry)
    carry = lax.fori_loop(0, qi & 1, lambda t, cr: sweep([0], cr, False), carry)
    o_ref[0] = carry[1]


def _sb_prompt(u3, bias, *, tq=256):
    b, l, _ = u3.shape
    tq = min(tq, l)
    kv = lambda c0: pl.BlockSpec((1, l, SB_HEAD_DIM), lambda i, h, j: (i, 0, c0 // SB_HEAD_DIM + h))
    return pl.pallas_call(
        functools.partial(_sbp_kernel, tq=tq, scale=SB_HEAD_DIM ** -0.5),
        grid=(b, SB_HEADS, l // tq),
        in_specs=[
            pl.BlockSpec(memory_space=pltpu.SMEM),
            pl.BlockSpec((1, tq, SB_HEAD_DIM), lambda i, h, j: (i, j, C_SBQ // SB_HEAD_DIM + h)),
            kv(C_SBK), kv(C_SBV),
        ],
        out_specs=pl.BlockSpec((1, tq, SB_HEAD_DIM), lambda i, h, j: (i, j, h)),
        out_shape=jax.ShapeDtypeStruct((b, l, SB_WIDTH), F32),
        compiler_params=_cparams(("parallel", "parallel", "arbitrary")),
        name="sb_prompt",
    )(bias, u3, u3, u3)


def _sbs_kernel(pt_ref, q_ref, kc_ref, vc_ref, bias_ref, uu_ref, *rest, pp, tq, scale):
    k_refs = rest[:pp]
    v_refs = rest[pp:2 * pp]
    o_ref = rest[2 * pp]
    c_ref = rest[2 * pp + 1]
    p = pl.program_id(1)
    hq = q_ref.shape[1]
    ncol = kc_ref.shape[1]
    q = q_ref[0].astype(BF16)
    bias = bias_ref[...]
    row_head = lax.broadcasted_iota(jnp.int32, (hq, ncol), 0) // tq
    col = lax.broadcasted_iota(jnp.int32, (hq, ncol), 1)
    own = (col % SB_HEADS) == row_head

    def rev_cumsum(la):
        hi, lo = _split2(la)
        return jnp.dot(jnp.concatenate([hi, lo], axis=1), uu_ref[...], preferred_element_type=F32)

    @pl.when(p == 0)
    def _():
        z = _dot_nt(q, kc_ref[0]) * scale + bias
        t = lax.broadcasted_iota(jnp.int32, (hq, ncol), 0) % tq
        valid = own & ((col // SB_HEADS) < t)
        ls = _log_sigmoid(z)
        la = jnp.where(valid, ls - z, 0.0)
        att = jnp.where(valid, jnp.exp(ls + rev_cumsum(la)), 0.0)
        o_ref[0] = _dot(att, vc_ref[0])
        c_ref[...] = jnp.sum(la, axis=1, keepdims=True)

    z = jnp.concatenate([_dot_nt(q, k_refs[j][...]) for j in range(pp)], axis=0)
    z = z * scale + jnp.concatenate([bias] * pp, axis=0)
    valid = jnp.concatenate([own] * pp, axis=0)
    ls = _log_sigmoid(z)
    la = jnp.where(valid, ls - z, 0.0)
    surv = rev_cumsum(la)
    tot = jnp.sum(la, axis=1, keepdims=True)
    cur = c_ref[...]
    cs = []
    for j in range(pp):
        cs.append(cur)
        cur = cur + tot[j * hq:(j + 1) * hq]
    c_ref[...] = cur
    att = jnp.where(valid, jnp.exp(ls + surv + jnp.concatenate(cs, axis=0)), 0.0)
    acc = o_ref[0]
    for j in range(pp):
        acc = acc + _dot(att[j * hq:(j + 1) * hq], v_refs[j][...])
    o_ref[0] = acc


def _sb_sample(q_rows, k_cur, v_cur, bias_rows, cache_k, cache_v, page_table, layer, *, pp=8):
    b, hq, _ = q_rows.shape
    n_pages = page_table.shape[1]
    ncol = cache_k.shape[2]
    pp = math.gcd(pp, n_pages)
    tq = hq // SB_HEADS
    jj = lax.broadcasted_iota(jnp.int32, (2 * ncol, ncol), 0) % ncol
    ss = lax.broadcasted_iota(jnp.int32, (2 * ncol, ncol), 1)
    uu = jnp.where(jj > ss, 1.0, 0.0).astype(BF16)

    def page_spec(j):
        return pl.BlockSpec((None, None, ncol, SB_HEAD_DIM),
                            lambda i, p, pt: (layer, pt[i, n_pages - 1 - (p * pp + j)], 0, 0))

    grid_spec = pltpu.PrefetchScalarGridSpec(
        num_scalar_prefetch=1,
        grid=(b, n_pages // pp),
        in_specs=[
            pl.BlockSpec((1, hq, SB_HEAD_DIM), lambda i, p, pt: (i, 0, 0)),
            pl.BlockSpec((1, ncol, SB_HEAD_DIM), lambda i, p, pt: (i, 0, 0)),
            pl.BlockSpec((1, ncol, SB_HEAD_DIM), lambda i, p, pt: (i, 0, 0)),
            pl.BlockSpec((hq, ncol), lambda i, p, pt: (0, 0)),
            pl.BlockSpec((2 * ncol, ncol), lambda i, p, pt: (0, 0)),
        ] + [page_spec(j) for j in range(pp)] + [page_spec(j) for j in range(pp)],
        out_specs=pl.BlockSpec((1, hq, SB_HEAD_DIM), lambda i, p, pt: (i, 0, 0)),
        scratch_shapes=[pltpu.VMEM((hq, 1), F32)],
    )
    return pl.pallas_call(
        functools.partial(_sbs_kernel, pp=pp, tq=tq, scale=SB_HEAD_DIM ** -0.5),
        grid_spec=grid_spec,
        out_shape=jax.ShapeDtypeStruct((b, hq, SB_HEAD_DIM), F32),
        compiler_params=_cparams(("parallel", "arbitrary")),
        name="sb_sample",
    )(page_table, q_rows, k_cur, v_cur, bias_rows, uu, *([cache_k] * pp), *([cache_v] * pp))


def _merge_kernel(op_ref, od_ref, os_ref, oa_ref, ob_ref, g0_ref, g1_ref, g2_ref, g3_ref,
                  w0_ref, w1_ref, w2_ref, w3_ref, w4_ref, o_ref):
    acc = _sigmoid(g0_ref[...]) * _dot(op_ref[...], w0_ref[...])
    acc = acc + _sigmoid(g1_ref[...]) * _dot(od_ref[...], w1_ref[...])
    acc = acc + _sigmoid(g2_ref[...]) * _dot(os_ref[...], w2_ref[...])
    acc = acc + _sigmoid(g3_ref[...]) * (_dot(oa_ref[...], w3_ref[...]) + _dot(ob_ref[...], w4_ref[...]))
    o_ref[...] = acc.astype(o_ref.dtype)


def _merge(o_pool, o_dn, o_sb, o_ss, u2, w_br, *, tm, tn=512):
    m = u2.shape[0]
    row = lambda: pl.BlockSpec((tm, 512), lambda i, j: (i, 0))
    gate = lambda k: pl.BlockSpec((tm, tn), lambda i, j: (i, (C_GATE + k * D_MODEL) // tn + j))
    wrow = lambda k: pl.BlockSpec((512, tn), lambda i, j: (k, j))
    return pl.pallas_call(
        _merge_kernel,
        grid=(m // tm, D_MODEL // tn),
        in_specs=[row(), row(), row(), row(), pl.BlockSpec((tm, 512), lambda i, j: (i, 1)),
                  gate(0), gate(1), gate(2), gate(3),
                  wrow(0), wrow(1), wrow(2), wrow(3), wrow(4)],
        out_specs=pl.BlockSpec((tm, tn), lambda i, j: (i, j)),
        out_shape=jax.ShapeDtypeStruct((m, D_MODEL), BF16),
        compiler_params=_cparams(("parallel", "arbitrary")),
        name="merge",
    )(o_pool, o_dn, o_sb, o_ss, o_ss, u2, u2, u2, u2, w_br, w_br, w_br, w_br, w_br)


def _outproj_kernel(m_ref, x_ref, w_ref, g_ref, o_ref):
    mix = jnp.dot(m_ref[...], w_ref[...], preferred_element_type=F32)
    o_ref[...] = x_ref[...] + _rms_rows(mix, g_ref[...])


def _outproj(merged, x2, w, g, *, tm):
    m = x2.shape[0]
    return pl.pallas_call(
        _outproj_kernel,
        grid=(m // tm,),
        in_specs=[
            pl.BlockSpec((tm, D_MODEL), lambda i: (i, 0)),
            pl.BlockSpec((tm, D_MODEL), lambda i: (i, 0)),
            pl.BlockSpec((D_MODEL, D_MODEL), lambda i: (0, 0)),
            pl.BlockSpec((1, D_MODEL), lambda i: (0, 0)),
        ],
        out_specs=pl.BlockSpec((tm, D_MODEL), lambda i: (i, 0)),
        out_shape=jax.ShapeDtypeStruct((m, D_MODEL), F32),
        compiler_params=_cparams(("parallel",)),
        name="outproj",
    )(merged, x2, w, g)


def _mlp_kernel(x_ref, g1_ref, wu_ref, wd_ref, g2_ref, o_ref, h_ref, acc_ref, *, tm):
    f = pl.program_id(1)

    @pl.when(f == 0)
    def _():
        _norm_rows_to(h_ref, x_ref, g1_ref, tm)
        acc_ref[...] = jnp.zeros(acc_ref.shape, F32)

    a = jnp.dot(h_ref[...], wu_ref[...], preferred_element_type=F32)
    a = jnp.square(jnp.maximum(a, 0.0)).astype(BF16)
    acc_ref[...] += jnp.dot(a, wd_ref[...], preferred_element_type=F32)

    @pl.when(f == pl.num_programs(1) - 1)
    def _():
        o_ref[...] = x_ref[...] + _rms_rows(acc_ref[...], g2_ref[...])


def _mlp(x2, g1, wu, wd, g2, *, tm, tf=1024):
    m = x2.shape[0]
    return pl.pallas_call(
        functools.partial(_mlp_kernel, tm=tm),
        grid=(m // tm, D_FF // tf),
        in_specs=[
            pl.BlockSpec((tm, D_MODEL), lambda i, f: (i, 0)),
            pl.BlockSpec((1, D_MODEL), lambda i, f: (0, 0)),
            pl.BlockSpec((D_MODEL, tf), lambda i, f: (0, f)),
            pl.BlockSpec((tf, D_MODEL), lambda i, f: (f, 0)),
            pl.BlockSpec((1, D_MODEL), lambda i, f: (0, 0)),
        ],
        out_specs=pl.BlockSpec((tm, D_MODEL), lambda i, f: (i, 0)),
        out_shape=jax.ShapeDtypeStruct((m, D_MODEL), F32),
        scratch_shapes=[pltpu.VMEM((tm, D_MODEL), BF16), pltpu.VMEM((tm, D_MODEL), F32)],
        compiler_params=_cparams(("parallel", "arbitrary")),
        name="mlp",
    )(x2, g1, wu, wd, g2)


def _lane_row(vals, offset):
    return jnp.zeros((LANES,), F32).at[offset:offset + vals.shape[0]].set(vals.astype(F32))


def _layer_params(layer, p):
    w_in = p["w_in"][layer]
    w_dt = w_in[:, _O_SSDT:_O_GATE]
    gap = jnp.zeros((D_MODEL, SM_REP - SSM_HEADS), w_in.dtype)
    w_in_p = jnp.concatenate(
        [w_in[:, :_O_DNB], w_in[:, _O_SB:_O_SSDT], w_in[:, _O_GATE:_O_END],
         w_in[:, _O_DNB:_O_SB], w_dt, gap, w_dt, gap, w_dt,
         jnp.zeros((D_MODEL, NU - C_SMALL - SM_DT - 2 * SM_REP - SSM_HEADS), w_in.dtype)], axis=1).astype(BF16)
    zrow = jnp.zeros((LANES,), F32)
    rep3 = lambda v: sum(_lane_row(v, SM_DT + i * SM_REP) for i in range(3))
    dn_par = jnp.stack([_lane_row(p["dn_a_log"][layer], SM_DECAY), _lane_row(p["dn_dt_bias"][layer], SM_DECAY)] + [zrow] * 6)
    ss_par = jnp.stack([rep3(p["ssm_a_log"][layer]), rep3(p["ssm_dt_bias"][layer])] + [zrow] * 6)
    row = lambda a: a[layer].reshape(1, -1)
    return dict(
        w_in=w_in_p, n_mix_pre=row(p["norm_mix_pre"]), n_mix_post=row(p["norm_mix_post"]),
        n_mlp_pre=row(p["norm_mlp_pre"]), n_mlp_post=row(p["norm_mlp_post"]),
        pool_w=p["pool_w"][layer].astype(BF16), pool_scale=row(p["pool_scale"]),
        dn_conv_w=p["dn_conv_w"][layer], dn_par=dn_par, dn_norm_w=row(p["dn_norm_w"]),
        sb_bias=p["sb_bias"][layer],
        ssm_conv_w=p["ssm_conv_w"][layer], ssm_conv_b=row(p["ssm_conv_b"]), ss_par=ss_par,
        ssm_norm_w=row(p["ssm_norm_w"]), ssm_d=jnp.repeat(p["ssm_d"][layer].astype(F32), SSM_HEAD_DIM).reshape(1, -1),
        w_branch=p["w_branch"][layer].astype(BF16), w_out=p["w_out"][layer].astype(BF16),
        w_up=p["w_up"][layer].astype(BF16), w_down=p["w_down"][layer].astype(BF16),
    )


def _trunk_layer(x, states, lw, *, past=None):
    b, l, _ = x.shape
    m = b * l
    pool_buf, dn_conv, dn_s, ssm_conv, ssm_h = states
    prompt = past is None
    tm_big = min(m, 1024)
    tm = min(m, 512)
    lc = min(l, 256)
    lp = max(lc, 2 * DN_CHUNK)

    x2 = x.reshape(m, D_MODEL)
    u2 = _inproj(x2, lw["n_mix_pre"], lw["w_in"], tm=tm_big)
    u3 = u2.reshape(b, l, NU)

    pos0 = 0 if prompt else past[2].shape[1] * past[0].shape[2]
    o_pool, pool_new = _pool(u3, pool_buf, lw["pool_w"], lw["pool_scale"], pos0=pos0)
    o_dn, dn_conv_new, dn_s_new = _deltanet(u3, dn_conv, dn_s, lw["dn_conv_w"], lw["dn_par"], lw["dn_norm_w"], lc=lc, lp=lp)
    o_ss, ss_conv_new, ss_h_new = _ssd(u3, ssm_conv, ssm_h, lw["ssm_conv_w"], lw["ssm_conv_b"], lw["ss_par"],
                                       lw["ssm_norm_w"], lw["ssm_d"], lc=lc, lp=lp)

    k_new = u3[:, :, C_SBK:C_SBK + SB_WIDTH]
    v_new = u3[:, :, C_SBV:C_SBV + SB_WIDTH]
    if prompt:
        o_sb = _sb_prompt(u3, lw["sb_bias"])
    else:
        cache_k, cache_v, page_table, layer = past
        ncol = cache_k.shape[2]
        q = u3[:, :, C_SBQ:C_SBQ + SB_WIDTH].reshape(b, l, SB_HEADS, SB_HEAD_DIM)
        q_rows = jnp.transpose(q, (0, 2, 1, 3)).reshape(b, SB_HEADS * l, SB_HEAD_DIM)
        k_cur = jnp.pad(k_new.reshape(b, l * SB_HEADS, SB_HEAD_DIM), ((0, 0), (0, ncol - l * SB_HEADS), (0, 0)))
        v_cur = jnp.pad(v_new.reshape(b, l * SB_HEADS, SB_HEAD_DIM), ((0, 0), (0, ncol - l * SB_HEADS), (0, 0)))
        bias_rows = jnp.broadcast_to(jnp.repeat(lw["sb_bias"], l)[:, None], (SB_HEADS * l, ncol)).astype(F32)
        acc = _sb_sample(q_rows, k_cur, v_cur, bias_rows, cache_k, cache_v, page_table, layer)
        o_sb = jnp.transpose(acc.reshape(b, SB_HEADS, l, SB_HEAD_DIM), (0, 2, 1, 3)).reshape(b, l, SB_WIDTH)

    merged = _merge(o_pool.reshape(m, -1), o_dn.reshape(m, -1), o_sb.reshape(m, -1), o_ss.reshape(m, -1),
                    u2, lw["w_branch"], tm=tm)
    x2 = _outproj(merged, x2, lw["w_out"], lw["n_mix_post"], tm=tm)
    x2 = _mlp(x2, lw["n_mlp_pre"], lw["w_up"], lw["w_down"], lw["n_mlp_post"], tm=tm)
    new_states = (k_new.reshape(b, l, SB_HEADS, SB_HEAD_DIM), v_new.reshape(b, l, SB_HEADS, SB_HEAD_DIM),
                  pool_new, dn_conv_new, dn_s_new, ss_conv_new, ss_h_new)
    return x2.reshape(b, l, D_MODEL), new_states


def kernel(x_prompt, x_sample, cache_sb_k, cache_sb_v, state_pool, state_dn_conv, state_dn_s, state_ssm_conv, state_ssm_h, page_table, norm_mix_pre, norm_mix_post, norm_mlp_pre, norm_mlp_post, w_in, pool_w, pool_scale, dn_conv_w, dn_a_log, dn_dt_bias, dn_norm_w, sb_bias, ssm_conv_w, ssm_conv_b, ssm_a_log, ssm_dt_bias, ssm_d, ssm_norm_w, w_branch, w_out, w_up, w_down):
    params = dict(norm_mix_pre=norm_mix_pre, norm_mix_post=norm_mix_post, norm_mlp_pre=norm_mlp_pre,
                  norm_mlp_post=norm_mlp_post, w_in=w_in, pool_w=pool_w, pool_scale=pool_scale,
                  dn_conv_w=dn_conv_w, dn_a_log=dn_a_log, dn_dt_bias=dn_dt_bias, dn_norm_w=dn_norm_w,
                  sb_bias=sb_bias, ssm_conv_w=ssm_conv_w, ssm_conv_b=ssm_conv_b, ssm_a_log=ssm_a_log,
                  ssm_dt_bias=ssm_dt_bias, ssm_d=ssm_d, ssm_norm_w=ssm_norm_w, w_branch=w_branch,
                  w_out=w_out, w_up=w_up, w_down=w_down)
    depth = w_in.shape[0]
    bp = x_prompt.shape[0]
    dt_ = x_prompt.dtype
    zero_states = (jnp.zeros((bp, POOL_BUF, POOL_WIDTH), dt_),
                   jnp.zeros((bp, CONV_WIDTH - 1, DN_CONV_DIM), dt_),
                   jnp.zeros((bp, DN_HEADS, DN_DK, DN_DV), dt_),
                   jnp.zeros((bp, CONV_WIDTH - 1, SSM_CONV_DIM), dt_),
                   jnp.zeros((bp, SSM_HEADS, SSM_HEAD_DIM, SSM_STATE), dt_))
    n_pool, page = cache_sb_k.shape[1], cache_sb_k.shape[2]
    cache_k = cache_sb_k.reshape(depth, n_pool, page * SB_HEADS, SB_HEAD_DIM)
    cache_v = cache_sb_v.reshape(depth, n_pool, page * SB_HEADS, SB_HEAD_DIM)
    y_prompt, y_sample = x_prompt, x_sample
    new_p, new_s = [], []
    for layer in range(depth):
        lw = _layer_params(layer, params)
        y_prompt, st_p = _trunk_layer(y_prompt, zero_states, lw)
        sample_states = (state_pool[layer], state_dn_conv[layer], state_dn_s[layer],
                         state_ssm_conv[layer], state_ssm_h[layer])
        y_sample, st_s = _trunk_layer(y_sample, sample_states, lw, past=(cache_k, cache_v, page_table, layer))
        new_p.append(st_p)
        new_s.append(st_s)
    outs_p = [jnp.stack(t) for t in zip(*new_p)]
    outs_s = [jnp.stack(t) for t in zip(*new_s)]
    return (y_prompt, y_sample, *outs_p, *outs_s)
```

```python
import functools
import math

import jax
import jax.numpy as jnp
from jax import lax
from jax.experimental import pallas as pl
from jax.experimental.pallas import tpu as pltpu

F32 = jnp.float32
BF16 = jnp.bfloat16

D_MODEL = 2048
N_BRANCH = 4
RMS_EPS = 1e-6
CONV_WIDTH = 4
POOL_WINDOWS = (2, 4, 8, 16)
POOL_WIDTH = D_MODEL // 4
POOL_GW = POOL_WIDTH // 4
POOL_BUF = max(POOL_WINDOWS) - 1
DN_HEADS = 4
DN_DK = 128
DN_DV = 128
DN_QK = DN_HEADS * DN_DK
DN_VW = DN_HEADS * DN_DV
DN_CONV_DIM = 2 * DN_QK + DN_VW
DN_CHUNK = 64
SB_HEADS = 4
SB_HEAD_DIM = 128
SB_WIDTH = SB_HEADS * SB_HEAD_DIM
SSM_D_INNER = D_MODEL // 2
SSM_HEAD_DIM = 64
SSM_HEADS = SSM_D_INNER // SSM_HEAD_DIM
SSM_GROUPS = 2
SSM_HPG = SSM_HEADS // SSM_GROUPS
SSM_STATE = 128
SSM_CONV_DIM = SSM_D_INNER + 2 * SSM_GROUPS * SSM_STATE
SSM_CHUNK = 64
D_FF = 4 * D_MODEL

LANES = 128
SUBLANES = 8

C_POOL = 0
C_DNQ, C_DNK, C_DNV, C_DNZ = 512, 1024, 1536, 2048
C_SBQ, C_SBK, C_SBV = 2560, 3072, 3584
C_SSZ, C_SSX, C_SSBC = 4096, 5120, 6144
C_SMALL = 6656
NU = 7168
_O_DNB = POOL_WIDTH + DN_CONV_DIM + DN_VW
_O_SB = _O_DNB + 2 * DN_HEADS
_O_SSDT = _O_SB + 3 * SB_WIDTH + SSM_D_INNER + SSM_CONV_DIM
_O_GATE = _O_SSDT + SSM_HEADS
_O_END = _O_GATE + N_BRANCH * D_MODEL
SM_BETA, SM_DECAY, SM_DT = 0, DN_HEADS, 2 * DN_HEADS
SM_GROUP = 2 * DN_HEADS + SSM_HEADS
SM_REP = 32

VMEM_LIMIT_MB = 56


def _cparams(sem, vmem_mb=VMEM_LIMIT_MB):
    return pltpu.CompilerParams(dimension_semantics=sem, vmem_limit_bytes=vmem_mb * 1024 * 1024)


def _sigmoid(x):
    return 1.0 / (1.0 + jnp.exp(-x))


def _silu(x):
    hx = 0.5 * x
    return hx + hx * jnp.tanh(hx)


def _softplus(x):
    return jnp.maximum(x, 0.0) + jnp.log1p(jnp.exp(-jnp.abs(x)))


def _log_sigmoid(x):
    return jnp.minimum(x, 0.0) - jnp.log1p(jnp.exp(-jnp.abs(x)))


def _dot(a, b):
    return jnp.dot(a.astype(BF16), b.astype(BF16), preferred_element_type=F32)


def _dot_nt(a, b):
    return lax.dot_general(a.astype(BF16), b.astype(BF16), (((1,), (1,)), ((), ())), preferred_element_type=F32)


def _dot_tn(a, b):
    return lax.dot_general(a.astype(BF16), b.astype(BF16), (((0,), (0,)), ((), ())), preferred_element_type=F32)


def _split2(a):
    hi = a.astype(BF16)
    lo = (a - hi.astype(F32)).astype(BF16)
    return hi, lo


def _expand_heads(src, n_heads, width, lane0):
    k = lax.broadcasted_iota(jnp.int32, (LANES, n_heads * width), 0)
    n = lax.broadcasted_iota(jnp.int32, (LANES, n_heads * width), 1)
    sel = jnp.where((k % SM_REP - lane0 == n // width) & (k < 3 * SM_REP), 1.0, 0.0).astype(BF16)
    lane = lax.broadcasted_iota(jnp.int32, src.shape, 1)
    hi = src.astype(BF16).astype(F32)
    r1 = src - hi
    mid = r1.astype(BF16).astype(F32)
    lo = r1 - mid
    pieces = jnp.where(lane < SM_REP, hi, jnp.where(lane < 2 * SM_REP, mid, lo))
    return jnp.dot(pieces.astype(BF16), sel, preferred_element_type=F32)


def _dot3(a, b):
    ah, al = _split2(a)
    bh, bl = _split2(b)
    d = lambda x, y: jnp.dot(x, y, preferred_element_type=F32)
    return d(ah, bh) + (d(ah, bl) + d(al, bh))


def _rms_rows(x, g):
    return x * lax.rsqrt(jnp.mean(x * x, axis=-1, keepdims=True) + RMS_EPS) * g


def _norm_rows_to(h_ref, x_ref, g_ref, tm):
    ch = min(tm, 256)

    def body(r, c):
        rs = pl.ds(pl.multiple_of(r * ch, ch), ch)
        h_ref[rs, :] = _rms_rows(x_ref[rs, :], g_ref[...]).astype(h_ref.dtype)
        return c

    lax.fori_loop(0, tm // ch, body, 0)


def _inproj_kernel(x_ref, g_ref, w_ref, o_ref, h_ref, *, tm, gate):
    @pl.when(pl.program_id(1) == 0)
    def _():
        _norm_rows_to(h_ref, x_ref, g_ref, tm)

    acc = jnp.dot(h_ref[...], w_ref[...], preferred_element_type=F32)
    o_ref[...] = (_sigmoid(acc) if gate else acc).astype(o_ref.dtype)


def _inproj(x2, g, w_all, layer, *, tm, tn=1024, gate=False):
    m = x2.shape[0]
    n = w_all.shape[2]
    return pl.pallas_call(
        functools.partial(_inproj_kernel, tm=tm, gate=gate),
        grid=(m // tm, n // tn),
        in_specs=[
            pl.BlockSpec((tm, D_MODEL), lambda i, j: (i, 0)),
            pl.BlockSpec((1, D_MODEL), lambda i, j: (0, 0)),
            pl.BlockSpec((None, D_MODEL, tn), lambda i, j: (layer, 0, j)),
        ],
        out_specs=pl.BlockSpec((tm, tn), lambda i, j: (i, j)),
        out_shape=jax.ShapeDtypeStruct((m, n), BF16 if gate else F32),
        scratch_shapes=[pltpu.VMEM((tm, D_MODEL), BF16)],
        compiler_params=_cparams(("parallel", "arbitrary")),
        name="inproj_gate" if gate else "inproj",
    )(x2, g, w_all)


def _pool_kernel(u_ref, buf_ref, w_ref, sc_ref, o_ref, new_ref, ext_ref, *, lin, lp, pos0):
    ext_ref[0:16, :] = jnp.zeros((16, POOL_WIDTH), F32)
    ext_ref[1:16, :] = buf_ref[0]
    if lin < lp:
        ext_ref[16:16 + lp, :] = jnp.zeros((lp, POOL_WIDTH), F32)
    ext_ref[16:16 + lin, :] = u_ref[0]
    ch = min(lp, 256)
    for c0 in range(0, lp, ch):
        pos = pos0 + c0 + lax.broadcasted_iota(jnp.int32, (ch, 1), 0)
        for gi, w in enumerate(POOL_WINDOWS):
            cols = slice(gi * POOL_GW, (gi + 1) * POOL_GW)
            s = ext_ref[16 + c0:16 + c0 + ch, cols]
            tot = s
            for k in range(1, w):
                tot = tot + ext_ref[16 + c0 - k:16 + c0 - k + ch, cols]
            cnt = jnp.minimum(pos + 1, w).astype(F32)
            y = tot / cnt - s
            yo = _dot(y, w_ref[gi]) * sc_ref[:, cols]
            n = min(ch, lin - c0)
            o_ref[0, c0:c0 + n, cols] = yo[:n]
    new_ref[0] = ext_ref[1 + lin:16 + lin, :]


def _pool(u3, buf, w, sc, *, pos0):
    b, lin, _ = u3.shape
    lp = max(lin, SUBLANES)
    return pl.pallas_call(
        functools.partial(_pool_kernel, lin=lin, lp=lp, pos0=pos0),
        grid=(b,),
        in_specs=[
            pl.BlockSpec((1, lin, POOL_WIDTH), lambda i: (i, 0, C_POOL // POOL_WIDTH)),
            pl.BlockSpec((1, POOL_BUF, POOL_WIDTH), lambda i: (i, 0, 0)),
            pl.BlockSpec((4, POOL_GW, POOL_GW), lambda i: (0, 0, 0)),
            pl.BlockSpec((1, POOL_WIDTH), lambda i: (0, 0)),
        ],
        out_specs=[
            pl.BlockSpec((1, lin, POOL_WIDTH), lambda i: (i, 0, 0)),
            pl.BlockSpec((1, POOL_BUF, POOL_WIDTH), lambda i: (i, 0, 0)),
        ],
        out_shape=[
            jax.ShapeDtypeStruct((b, lin, POOL_WIDTH), F32),
            jax.ShapeDtypeStruct((b, POOL_BUF, POOL_WIDTH), F32),
        ],
        scratch_shapes=[pltpu.VMEM((16 + lp, POOL_WIDTH), F32)],
        compiler_params=_cparams(("parallel",)),
        name="pool",
    )(u3, buf, w, sc)


ROWS = 64


def _fill_ext(ext_ref, cbuf_ref, parts, *, lin, lp):
    l = pl.program_id(1)

    @pl.when(l == 0)
    def _():
        ext_ref[5:8, :] = cbuf_ref[0]

    @pl.when(l > 0)
    def _():
        ext_ref[5:8, :] = ext_ref[5 + lin:8 + lin, :]

    if lin < lp:
        ext_ref[8:8 + lp, :] = jnp.zeros((lp, ext_ref.shape[1]), F32)
    for ref, c0, width in parts:
        ext_ref[8:8 + lin, c0:c0 + width] = ref[0]


def _conv_silu(ext_ref, r0, n, cols, w_ref, b_ref):
    acc = None
    for i in range(CONV_WIDTH):
        part = ext_ref[5 + r0 + i:5 + r0 + i + n, cols] * w_ref[i:i + 1, cols]
        acc = part if acc is None else acc + part
    if b_ref is not None:
        acc = acc + b_ref[:, cols]
    return _silu(acc)


def _pad_rows(dst_ref, src_ref, *, lin, lp):
    if lin < lp:
        dst_ref[...] = jnp.zeros(dst_ref.shape, F32)
    dst_ref[0:lin, :] = src_ref[0]


def _row_valid(r0, n, lin):
    if r0 + n <= lin:
        return None
    return (r0 + lax.broadcasted_iota(jnp.int32, (n, 1), 0)) < lin


def _chunk_cumsum(x, chunk):
    rin = lax.broadcasted_iota(jnp.int32, x.shape, 0) % chunk
    s = 1
    while s < chunk:
        x = x + jnp.where(rin >= s, pltpu.roll(x, s, 0), 0.0)
        s *= 2
    return x


def _diag_inv2(a0, a1):
    c = a0.shape[0]
    hb = c // 2
    ng = hb // SUBLANES
    sub = lax.broadcasted_iota(jnp.int32, (SUBLANES, 2 * c), 0)
    lane = lax.broadcasted_iota(jnp.int32, (SUBLANES, 2 * c), 1)
    base = (lane // hb) * hb
    lmod = lane - base
    a01 = jnp.concatenate([a0, a1], axis=1)
    odd = ((lax.broadcasted_iota(jnp.int32, (hb, 2 * c), 1) // hb) % 2) == 1
    packed = jnp.where(odd, a01[hb:], a01[:hb])
    racc = [jnp.zeros((SUBLANES, 2 * c), F32) for _ in range(ng)]
    tg = [jnp.zeros((SUBLANES, 2 * c), F32) for _ in range(ng)]
    for j in range(hb):
        gj, rj = divmod(j, SUBLANES)
        t_j = jnp.where(lmod[0:1] == j, 1.0, 0.0) - racc[gj][rj:rj + 1, :]
        tg[gj] = jnp.where(sub == rj, t_j, tg[gj])
        if j == hb - 1:
            break
        for g in range(gj, ng):
            col = jnp.take_along_axis(packed[g * SUBLANES:(g + 1) * SUBLANES], base + j, axis=1)
            racc[g] = racc[g] + col * t_j
    dinv = jnp.concatenate(tg, axis=0)
    bd = jnp.concatenate([jnp.where(odd, 0.0, dinv), jnp.where(odd, dinv, 0.0)], axis=0)
    return [bd[:, :c], bd[:, c:]]


def _dn_kernel(q_ref, k_ref, v_ref, z_ref, sm_ref, cbuf_ref, s0_ref, cw_ref, par_ref, nw_ref,
               o_ref, cnew_ref, snew_ref,
               ext_ref, qn_ref, kn_ref, kb_ref, zz_ref, smp_ref, be_ref, ge_ref, gt_ref, oo_ref, s_ref,
               uc_ref, wc_ref, qe_ref, kt_ref, at_ref, rhs_ref, *, lin, lp, chunk):
    l = pl.program_id(1)
    nch = lp // chunk

    @pl.when(l == 0)
    def _():
        s_ref[...] = s0_ref[0]

    _fill_ext(ext_ref, cbuf_ref, ((q_ref, 0, DN_QK), (k_ref, DN_QK, DN_QK), (v_ref, 2 * DN_QK, DN_VW)), lin=lin, lp=lp)
    _pad_rows(zz_ref, z_ref, lin=lin, lp=lp)
    _pad_rows(smp_ref, sm_ref, lin=lin, lp=lp)

    sm = smp_ref[...]
    beta = _sigmoid(sm)
    g = -jnp.exp(par_ref[0:1, :]) * _softplus(sm + par_ref[1:2, :])
    valid = _row_valid(0, lp, lin)
    if valid is not None:
        beta = jnp.where(valid, beta, 0.0)
        g = jnp.where(valid, g, 0.0)
    gcum = _chunk_cumsum(g, chunk)
    be_ref[...] = _expand_heads(beta, DN_HEADS, DN_DK, SM_BETA)
    ge_ref[...] = _expand_heads(gcum, DN_HEADS, DN_DK, SM_DECAY)
    gt = gcum.T
    for c in range(nch):
        gt_ref[c] = gt[:, c * chunk:(c + 1) * chunk]

    assert ROWS == chunk
    for r0 in range(0, lp, ROWS):
        valid = _row_valid(r0, ROWS, lin)
        rows = slice(r0, r0 + ROWS)
        for h in range(DN_HEADS):
            hs = slice(h * DN_DK, (h + 1) * DN_DK)
            conv = lambda part: _conv_silu(ext_ref, r0, ROWS, slice(part * DN_QK + h * DN_DK, part * DN_QK + (h + 1) * DN_DK), cw_ref, None)
            l2n = lambda y: y * lax.rsqrt(jnp.sum(y * y, axis=-1, keepdims=True) + 1e-6)
            q = l2n(conv(0)) * (DN_DK ** -0.5)
            k = l2n(conv(1))
            v = conv(2)
            if valid is not None:
                q, k, v = (jnp.where(valid, t, 0.0) for t in (q, k, v))
            bt = be_ref[rows, hs]
            ge = ge_ref[rows, hs]
            eg = jnp.exp(ge)
            kb = k * bt
            qn_ref[rows, hs] = q
            kn_ref[rows, hs] = k
            kb_ref[rows, hs] = kb
            qe_ref[rows, hs] = q * eg
            kt_ref[rows, hs] = k * jnp.exp(ge[ROWS - 1:ROWS, :] - ge)
            rhs_ref[h, rows, 0:DN_DV] = v * bt
            rhs_ref[h, rows, DN_DV:2 * DN_DV] = kb * eg

    rid = lax.broadcasted_iota(jnp.int32, (chunk, chunk), 0)
    cid = lax.broadcasted_iota(jnp.int32, (chunk, chunk), 1)
    strict_lower = cid < rid
    lower = cid <= rid
    off = (rid >= chunk // 2) & (cid < chunk // 2)
    heads = range(DN_HEADS)
    hsl = [slice(h * DN_DK, (h + 1) * DN_DK) for h in heads]

    def prep_body(c2, carry):
        items = [(i, h) for i in range(2) for h in heads]
        cidx = [2 * c2 + i for i in range(2)]
        rs = [pl.ds(pl.multiple_of(c * chunk, chunk), chunk) for c in cidx]
        kh = [kn_ref[rs[c], hsl[h]] for c, h in items]
        decay = [jnp.exp(jnp.where(lower, ge_ref[rs[c], h * DN_DK:h * DN_DK + chunk]
                                   - gt_ref[cidx[c], SM_DECAY + h:SM_DECAY + h + 1, :], -jnp.inf)) for c, h in items]
        amat = [_dot_nt(kb_ref[rs[c], hsl[h]], kh[n]) * jnp.where(strict_lower, decay[n], 0.0)
                for n, (c, h) in enumerate(items)]
        for n, (c, h) in enumerate(items):
            at_ref[cidx[c] * DN_HEADS + h] = _dot_nt(qn_ref[rs[c], hsl[h]], kh[n]) * decay[n]
        dinv = sum((_diag_inv2(amat[n], amat[n + 1]) for n in range(0, len(items), 2)), [])
        inner = [_dot3(jnp.where(off, a, 0.0), d) for a, d in zip(amat, dinv)]
        tmat = [d - _dot3(d, i) for d, i in zip(dinv, inner)]
        sol = [_dot3(t, rhs_ref[h, rs[c], :]) for t, (c, h) in zip(tmat, items)]
        for s, (c, h) in zip(sol, items):
            uc_ref[rs[c], hsl[h]] = s[:, :DN_DV]
            wc_ref[rs[c], hsl[h]] = s[:, DN_DV:]
        return carry

    assert nch % 2 == 0
    lax.fori_loop(0, nch // 2, prep_body, 0)

    def scan_body(c, carry):
        r0 = pl.multiple_of(c * chunk, chunk)
        rs = pl.ds(r0, chunk)
        glast = ge_ref[pl.ds(r0 + chunk - 1, 1), :]
        sh = [s_ref[h] for h in heads]
        ws = [_dot(wc_ref[rs, hsl[h]], sh[h]) for h in heads]
        qs = [_dot(qe_ref[rs, hsl[h]], sh[h]) for h in heads]
        v_new = [uc_ref[rs, hsl[h]] - ws[h] for h in heads]
        o2 = [_dot(at_ref[c * DN_HEADS + h], v_new[h]) for h in heads]
        kv = [_dot_tn(kt_ref[rs, hsl[h]], v_new[h]) for h in heads]
        for h in heads:
            s_ref[h] = sh[h] * jnp.exp(glast[:, hsl[h]]) + kv[h]
            oo_ref[rs, hsl[h]] = _rms_rows(qs[h] + o2[h], nw_ref[...]) * _silu(zz_ref[rs, hsl[h]])
        return carry

    lax.fori_loop(0, nch, scan_body, 0)
    o_ref[0] = oo_ref[0:lin, :]

    @pl.when(l == pl.num_programs(1) - 1)
    def _():
        cnew_ref[0] = ext_ref[5 + lin:8 + lin, :]
        snew_ref[0] = s_ref[...]


def _deltanet(u3, cbuf, s0, cw, par, nw, *, lc, lp):
    b, l, _ = u3.shape
    nl = l // lc
    blk = lambda c0: pl.BlockSpec((1, lc, 512), lambda i, j: (i, j, c0 // 512))
    return pl.pallas_call(
        functools.partial(_dn_kernel, lin=lc, lp=lp, chunk=DN_CHUNK),
        grid=(b, nl),
        in_specs=[
            blk(C_DNQ), blk(C_DNK), blk(C_DNV), blk(C_DNZ),
            pl.BlockSpec((1, lc, LANES), lambda i, j: (i, j, C_SMALL // LANES)),
            pl.BlockSpec((1, CONV_WIDTH - 1, DN_CONV_DIM), lambda i, j: (i, 0, 0)),
            pl.BlockSpec((1, DN_HEADS, DN_DK, DN_DV), lambda i, j: (i, 0, 0, 0)),
            pl.BlockSpec((CONV_WIDTH, DN_CONV_DIM), lambda i, j: (0, 0)),
            pl.BlockSpec((SUBLANES, LANES), lambda i, j: (0, 0)),
            pl.BlockSpec((1, DN_DV), lambda i, j: (0, 0)),
        ],
        out_specs=[
            pl.BlockSpec((1, lc, DN_VW), lambda i, j: (i, j, 0)),
            pl.BlockSpec((1, CONV_WIDTH - 1, DN_CONV_DIM), lambda i, j: (i, 0, 0)),
            pl.BlockSpec((1, DN_HEADS, DN_DK, DN_DV), lambda i, j: (i, 0, 0, 0)),
        ],
        out_shape=[
            jax.ShapeDtypeStruct((b, l, DN_VW), F32),
            jax.ShapeDtypeStruct((b, CONV_WIDTH - 1, DN_CONV_DIM), F32),
            jax.ShapeDtypeStruct((b, DN_HEADS, DN_DK, DN_DV), F32),
        ],
        scratch_shapes=[
            pltpu.VMEM((8 + lp, DN_CONV_DIM), F32),
            pltpu.VMEM((lp, DN_QK), F32), pltpu.VMEM((lp, DN_QK), F32), pltpu.VMEM((lp, DN_VW), F32),
            pltpu.VMEM((lp, DN_VW), F32), pltpu.VMEM((lp, LANES), F32),
            pltpu.VMEM((lp, DN_QK), F32), pltpu.VMEM((lp, DN_QK), F32),
            pltpu.VMEM((lp // DN_CHUNK, LANES, DN_CHUNK), F32),
            pltpu.VMEM((lp, DN_VW), F32),
            pltpu.VMEM((DN_HEADS, DN_DK, DN_DV), F32),
            pltpu.VMEM((lp, DN_VW), F32), pltpu.VMEM((lp, DN_QK), F32),
            pltpu.VMEM((lp, DN_QK), F32), pltpu.VMEM((lp, DN_QK), F32),
            pltpu.VMEM((lp // DN_CHUNK * DN_HEADS, DN_CHUNK, DN_CHUNK), F32),
            pltpu.VMEM((DN_HEADS, lp, 2 * DN_DV), F32),
        ],
        compiler_params=_cparams(("parallel", "arbitrary")),
        name="deltanet",
    )(u3, u3, u3, u3, u3, cbuf, s0, cw, par, nw)


def _ssd_kernel(z_ref, x_ref, bc_ref, sm_ref, cbuf_ref, h0_ref, cw_ref, cb_ref, par_ref, nw_ref, dsk_ref,
                o_ref, cnew_ref, hnew_ref,
                ext_ref, xs_ref, bm_ref, cm_ref, zz_ref, smp_ref, ce_ref, de_ref, ct_ref, yy_ref, ht_ref,
                *, lin, lp, chunk):
    assert chunk == SSM_HEAD_DIM
    l = pl.program_id(1)
    nch = lp // chunk
    gn = SSM_GROUPS * SSM_STATE

    @pl.when(l == 0)
    def _():
        for g in range(SSM_GROUPS):
            hg = h0_ref[0, g * SSM_HPG:(g + 1) * SSM_HPG]
            ht_ref[g] = hg.reshape(SSM_HPG * SSM_HEAD_DIM, SSM_STATE).T

    _fill_ext(ext_ref, cbuf_ref, ((x_ref, 0, SSM_D_INNER), (bc_ref, SSM_D_INNER, 2 * gn)), lin=lin, lp=lp)
    _pad_rows(zz_ref, z_ref, lin=lin, lp=lp)
    _pad_rows(smp_ref, sm_ref, lin=lin, lp=lp)

    for r0 in range(0, lp, ROWS):
        valid = _row_valid(r0, ROWS, lin)
        for c0 in range(0, SSM_CONV_DIM, LANES):
            y = _conv_silu(ext_ref, r0, ROWS, slice(c0, c0 + LANES), cw_ref, cb_ref)
            if valid is not None:
                y = jnp.where(valid, y, 0.0)
            if c0 < SSM_D_INNER:
                xs_ref[r0:r0 + ROWS, c0:c0 + LANES] = y
            elif c0 < SSM_D_INNER + gn:
                bm_ref[r0:r0 + ROWS, c0 - SSM_D_INNER:c0 - SSM_D_INNER + LANES] = y
            else:
                cm_ref[r0:r0 + ROWS, c0 - SSM_D_INNER - gn:c0 - SSM_D_INNER - gn + LANES] = y

    sm = smp_ref[...]
    dt = _softplus(sm + par_ref[1:2, :])
    valid = _row_valid(0, lp, lin)
    if valid is not None:
        dt = jnp.where(valid, dt, 0.0)
    cum = _chunk_cumsum(dt * (-jnp.exp(par_ref[0:1, :])), chunk)
    ct = cum.T
    for c in range(nch):
        ct_ref[c] = ct[:, c * chunk:(c + 1) * chunk]

    spread = _expand_heads(jnp.concatenate([cum, dt], axis=0), SSM_HEADS, SSM_HEAD_DIM, SM_DT)
    ce_ref[...] = spread[:lp]
    de_ref[...] = spread[lp:]

    pw = 2 * SSM_HEAD_DIM
    row2 = lax.broadcasted_iota(jnp.int32, (chunk, 2 * chunk), 0)
    lane2 = lax.broadcasted_iota(jnp.int32, (chunk, 2 * chunk), 1)
    lower2 = (lane2 % chunk) <= row2
    first = lax.broadcasted_iota(jnp.int32, (chunk, pw), 1) < SSM_HEAD_DIM
    groups = range(SSM_GROUPS)
    gw_ = SSM_HPG * SSM_HEAD_DIM

    def chunk_body(c, carry):
        r0 = pl.multiple_of(c * chunk, chunk)
        rs = pl.ds(r0, chunk)
        bg = [bm_ref[rs, g * SSM_STATE:(g + 1) * SSM_STATE] for g in groups]
        cg = [cm_ref[rs, g * SSM_STATE:(g + 1) * SSM_STATE] for g in groups]
        ce = [ce_ref[rs, g * gw_:(g + 1) * gw_] for g in groups]
        de = [de_ref[rs, g * gw_:(g + 1) * gw_] for g in groups]
        xg = [xs_ref[rs, g * gw_:(g + 1) * gw_] for g in groups]
        ht = [ht_ref[g] for g in groups]
        cb = [_dot_nt(cg[g], bg[g]) for g in groups]
        ys = [_dot(cg[g], ht[g]) * jnp.exp(ce[g]) for g in groups]
        for g in groups:
            last = ce[g][chunk - 1:chunk, :]
            ht_ref[g] = ht[g] * jnp.exp(last) + _dot_tn(bg[g], xg[g] * (jnp.exp(last - ce[g]) * de[g]))
        for g in groups:
            cb2 = jnp.concatenate([cb[g], cb[g]], axis=1)
            xd = xg[g] * de[g]
            for p in range(SSM_HPG // 2):
                ps = slice(p * pw, (p + 1) * pw)
                hd = g * SSM_HPG + 2 * p
                ctp = ct_ref[c, SM_DT + hd:SM_DT + hd + 2, :]
                crp = jnp.concatenate([ctp[0:1], ctp[1:2]], axis=1)
                lm = jnp.exp(jnp.where(lower2, ce[g][:, ps] - crp, -jnp.inf))
                xdp = xd[:, ps]
                bd = jnp.concatenate([jnp.where(first, xdp, 0.0), jnp.where(first, 0.0, xdp)], axis=0)
                cols = slice(g * gw_ + p * pw, g * gw_ + (p + 1) * pw)
                yy_ref[rs, cols] = _dot(cb2 * lm, bd) + ys[g][:, ps] + dsk_ref[:, cols] * xg[g][:, ps]
        return carry

    lax.fori_loop(0, nch, chunk_body, 0)

    gw = SSM_D_INNER // SSM_GROUPS
    for r0 in range(0, lp, ROWS):
        n = min(ROWS, lin - r0)
        if n <= 0:
            break
        for g in range(SSM_GROUPS):
            cols = slice(g * gw, (g + 1) * gw)
            t = yy_ref[r0:r0 + ROWS, cols] * _silu(zz_ref[r0:r0 + ROWS, cols])
            t = _rms_rows(t, nw_ref[:, cols])
            o_ref[0, r0:r0 + n, cols] = t[:n]

    @pl.when(l == pl.num_programs(1) - 1)
    def _():
        cnew_ref[0] = ext_ref[5 + lin:8 + lin, :]
        for g in range(SSM_GROUPS):
            hnew_ref[0, g * SSM_HPG:(g + 1) * SSM_HPG] = ht_ref[g].T.reshape(SSM_HPG, SSM_HEAD_DIM, SSM_STATE)


def _ssd(u3, cbuf, h0, cw, cb, par, nw, d, *, lc, lp):
    b, l, _ = u3.shape
    nl = l // lc
    gn2 = 2 * SSM_GROUPS * SSM_STATE
    return pl.pallas_call(
        functools.partial(_ssd_kernel, lin=lc, lp=lp, chunk=SSM_CHUNK),
        grid=(b, nl),
        in_specs=[
            pl.BlockSpec((1, lc, SSM_D_INNER), lambda i, j: (i, j, C_SSZ // SSM_D_INNER)),
            pl.BlockSpec((1, lc, SSM_D_INNER), lambda i, j: (i, j, C_SSX // SSM_D_INNER)),
            pl.BlockSpec((1, lc, gn2), lambda i, j: (i, j, C_SSBC // gn2)),
            pl.BlockSpec((1, lc, LANES), lambda i, j: (i, j, C_SMALL // LANES)),
            pl.BlockSpec((1, CONV_WIDTH - 1, SSM_CONV_DIM), lambda i, j: (i, 0, 0)),
            pl.BlockSpec((1, SSM_HEADS, SSM_HEAD_DIM, SSM_STATE), lambda i, j: (i, 0, 0, 0)),
            pl.BlockSpec((CONV_WIDTH, SSM_CONV_DIM), lambda i, j: (0, 0)),
            pl.BlockSpec((1, SSM_CONV_DIM), lambda i, j: (0, 0)),
            pl.BlockSpec((SUBLANES, LANES), lambda i, j: (0, 0)),
            pl.BlockSpec((1, SSM_D_INNER), lambda i, j: (0, 0)),
            pl.BlockSpec((1, SSM_D_INNER), lambda i, j: (0, 0)),
        ],
        out_specs=[
            pl.BlockSpec((1, lc, SSM_D_INNER), lambda i, j: (i, j, 0)),
            pl.BlockSpec((1, CONV_WIDTH - 1, SSM_CONV_DIM), lambda i, j: (i, 0, 0)),
            pl.BlockSpec((1, SSM_HEADS, SSM_HEAD_DIM, SSM_STATE), lambda i, j: (i, 0, 0, 0)),
        ],
        out_shape=[
            jax.ShapeDtypeStruct((b, l, SSM_D_INNER), F32),
            jax.ShapeDtypeStruct((b, CONV_WIDTH - 1, SSM_CONV_DIM), F32),
            jax.ShapeDtypeStruct((b, SSM_HEADS, SSM_HEAD_DIM, SSM_STATE), F32),
        ],
        scratch_shapes=[
            pltpu.VMEM((8 + lp, SSM_CONV_DIM), F32),
            pltpu.VMEM((lp, SSM_D_INNER), F32),
            pltpu.VMEM((lp, SSM_GROUPS * SSM_STATE), F32), pltpu.VMEM((lp, SSM_GROUPS * SSM_STATE), F32),
            pltpu.VMEM((lp, SSM_D_INNER), F32), pltpu.VMEM((lp, LANES), F32),
            pltpu.VMEM((lp, SSM_D_INNER), F32), pltpu.VMEM((lp, SSM_D_INNER), F32),
            pltpu.VMEM((lp // SSM_CHUNK, LANES, SSM_CHUNK), F32),
            pltpu.VMEM((lp, SSM_D_INNER), F32),
            pltpu.VMEM((SSM_GROUPS, SSM_STATE, SSM_HPG * SSM_HEAD_DIM), F32),
        ],
        compiler_params=_cparams(("parallel", "arbitrary")),
        name="ssd",
    )(u3, u3, u3, u3, cbuf, h0, cw, cb, par, nw, d)


def _strict_upper_stack(n):
    j = lax.broadcasted_iota(jnp.int32, (2 * n, n), 0) % n
    s = lax.broadcasted_iota(jnp.int32, (2 * n, n), 1)
    return jnp.where(j > s, 1.0, 0.0).astype(BF16)


def _rev_excl_cumsum(la, uu):
    hi, lo = _split2(la)
    return jnp.dot(jnp.concatenate([hi, lo], axis=1), uu, preferred_element_type=F32)


def _sbp_kernel(bias_ref, q_ref, k_ref, v_ref, o_ref, *, tq, scale):
    h = pl.program_id(1)
    qi = pl.program_id(2)
    bias = bias_ref[h]
    q = q_ref[0].astype(BF16)
    uu = _strict_upper_stack(tq)

    def sweep(blocks, carry, diagonal):
        c, acc = carry
        rows = [pl.ds(kj * tq if isinstance(kj, int) else pl.multiple_of(kj * tq, tq), tq) for kj in blocks]
        zs = [_dot_nt(q, k_ref[0, r, :]) * scale + bias for r in rows]
        lss, las = [], []
        for z in zs:
            l1 = jnp.log1p(jnp.exp(-jnp.abs(z)))
            lss.append(jnp.minimum(z, 0.0) - l1)
            las.append(-jnp.maximum(z, 0.0) - l1)
        if diagonal:
            valid = lax.broadcasted_iota(jnp.int32, (tq, tq), 1) < lax.broadcasted_iota(jnp.int32, (tq, tq), 0)
            las = [jnp.where(valid, la, 0.0) for la in las]
        survs = [_rev_excl_cumsum(la, uu) for la in las]
        for ls, la, surv, r in zip(lss, las, survs, rows):
            att = jnp.exp(ls + surv + c)
            if diagonal:
                att = jnp.where(valid, att, 0.0)
            acc = acc + _dot(att, v_ref[0, r, :])
            c = c + jnp.sum(la, axis=1, keepdims=True)
        return c, acc

    carry = (jnp.zeros((tq, 1), F32), jnp.zeros((tq, SB_HEAD_DIM), F32))
    carry = sweep([qi], carry, True)
    carry = lax.fori_loop(0, lax.shift_right_logical(qi, 1),
                          lambda t, cr: sweep([qi - 1 - 2 * t, qi - 2 - 2 * t], cr, False), carry)
    carry = lax.fori_loop(0, qi & 1, lambda t, cr: sweep([0], cr, False), carry)
    o_ref[0] = carry[1]


def _sb_prompt(u3, bias, *, tq=256):
    b, l, _ = u3.shape
    tq = min(tq, l)
    kv = lambda c0: pl.BlockSpec((1, l, SB_HEAD_DIM), lambda i, h, j: (i, 0, c0 // SB_HEAD_DIM + h))
    return pl.pallas_call(
        functools.partial(_sbp_kernel, tq=tq, scale=SB_HEAD_DIM ** -0.5),
        grid=(b, SB_HEADS, l // tq),
        in_specs=[
            pl.BlockSpec(memory_space=pltpu.SMEM),
            pl.BlockSpec((1, tq, SB_HEAD_DIM), lambda i, h, j: (i, j, C_SBQ // SB_HEAD_DIM + h)),
            kv(C_SBK), kv(C_SBV),
        ],
        out_specs=pl.BlockSpec((1, tq, SB_HEAD_DIM), lambda i, h, j: (i, j, h)),
        out_shape=jax.ShapeDtypeStruct((b, l, SB_WIDTH), F32),
        compiler_params=_cparams(("parallel", "parallel", "arbitrary")),
        name="sb_prompt",
    )(bias, u3, u3, u3)


def _sbs_kernel(pt_ref, q_ref, kc_ref, vc_ref, bias_ref, uu_ref, *rest, pp, tq, scale):
    k_refs = rest[:pp]
    v_refs = rest[pp:2 * pp]
    o_ref = rest[2 * pp]
    c_ref = rest[2 * pp + 1]
    p = pl.program_id(1)
    hq = q_ref.shape[1]
    ncol = kc_ref.shape[1]
    q = q_ref[0].astype(BF16)
    bias = bias_ref[...]
    row_head = lax.broadcasted_iota(jnp.int32, (hq, ncol), 0) // tq
    col = lax.broadcasted_iota(jnp.int32, (hq, ncol), 1)
    own = (col % SB_HEADS) == row_head

    def rev_cumsum(la):
        hi, lo = _split2(la)
        return jnp.dot(jnp.concatenate([hi, lo], axis=1), uu_ref[...], preferred_element_type=F32)

    @pl.when(p == 0)
    def _():
        z = _dot_nt(q, kc_ref[0]) * scale + bias
        t = lax.broadcasted_iota(jnp.int32, (hq, ncol), 0) % tq
        valid = own & ((col // SB_HEADS) < t)
        ls = _log_sigmoid(z)
        la = jnp.where(valid, ls - z, 0.0)
        att = jnp.where(valid, jnp.exp(ls + rev_cumsum(la)), 0.0)
        o_ref[0] = _dot(att, vc_ref[0])
        c_ref[...] = jnp.sum(la, axis=1, keepdims=True)

    z = jnp.concatenate([_dot_nt(q, k_refs[j][...]) for j in range(pp)], axis=0)
    z = z * scale + jnp.concatenate([bias] * pp, axis=0)
    valid = jnp.concatenate([own] * pp, axis=0)
    ls = _log_sigmoid(z)
    la = jnp.where(valid, ls - z, 0.0)
    surv = rev_cumsum(la)
    tot = jnp.sum(la, axis=1, keepdims=True)
    cur = c_ref[...]
    cs = []
    for j in range(pp):
        cs.append(cur)
        cur = cur + tot[j * hq:(j + 1) * hq]
    c_ref[...] = cur
    att = jnp.where(valid, jnp.exp(ls + surv + jnp.concatenate(cs, axis=0)), 0.0)
    acc = o_ref[0]
    for j in range(pp):
        acc = acc + _dot(att[j * hq:(j + 1) * hq], v_refs[j][...])
    o_ref[0] = acc


def _sb_sample(q_rows, k_cur, v_cur, bias_rows, cache_k, cache_v, page_table, layer, *, pp=8):
    b, hq, _ = q_rows.shape
    n_pages = page_table.shape[1]
    ncol = cache_k.shape[2]
    pp = math.gcd(pp, n_pages)
    tq = hq // SB_HEADS
    jj = lax.broadcasted_iota(jnp.int32, (2 * ncol, ncol), 0) % ncol
    ss = lax.broadcasted_iota(jnp.int32, (2 * ncol, ncol), 1)
    uu = jnp.where(jj > ss, 1.0, 0.0).astype(BF16)

    def page_spec(j):
        return pl.BlockSpec((None, None, ncol, SB_HEAD_DIM),
                            lambda i, p, pt: (layer, pt[i, n_pages - 1 - (p * pp + j)], 0, 0))

    grid_spec = pltpu.PrefetchScalarGridSpec(
        num_scalar_prefetch=1,
        grid=(b, n_pages // pp),
        in_specs=[
            pl.BlockSpec((1, hq, SB_HEAD_DIM), lambda i, p, pt: (i, 0, 0)),
            pl.BlockSpec((1, ncol, SB_HEAD_DIM), lambda i, p, pt: (i, 0, 0)),
            pl.BlockSpec((1, ncol, SB_HEAD_DIM), lambda i, p, pt: (i, 0, 0)),
            pl.BlockSpec((hq, ncol), lambda i, p, pt: (0, 0)),
            pl.BlockSpec((2 * ncol, ncol), lambda i, p, pt: (0, 0)),
        ] + [page_spec(j) for j in range(pp)] + [page_spec(j) for j in range(pp)],
        out_specs=pl.BlockSpec((1, hq, SB_HEAD_DIM), lambda i, p, pt: (i, 0, 0)),
        scratch_shapes=[pltpu.VMEM((hq, 1), F32)],
    )
    return pl.pallas_call(
        functools.partial(_sbs_kernel, pp=pp, tq=tq, scale=SB_HEAD_DIM ** -0.5),
        grid_spec=grid_spec,
        out_shape=jax.ShapeDtypeStruct((b, hq, SB_HEAD_DIM), F32),
        compiler_params=_cparams(("parallel", "arbitrary")),
        name="sb_sample",
    )(page_table, q_rows, k_cur, v_cur, bias_rows, uu, *([cache_k] * pp), *([cache_v] * pp))


def _merge_kernel(op_ref, od_ref, os_ref, oa_ref, ob_ref, g0_ref, g1_ref, g2_ref, g3_ref,
                  w0_ref, w1_ref, w2_ref, w3_ref, w4_ref, o_ref):
    gate = lambda g_ref: g_ref[...].astype(F32)
    acc = gate(g0_ref) * _dot(op_ref[...], w0_ref[...])
    acc = acc + gate(g1_ref) * _dot(od_ref[...], w1_ref[...])
    acc = acc + gate(g2_ref) * _dot(os_ref[...], w2_ref[...])
    acc = acc + gate(g3_ref) * (_dot(oa_ref[...], w3_ref[...]) + _dot(ob_ref[...], w4_ref[...]))
    o_ref[...] = acc.astype(o_ref.dtype)


def _merge(o_pool, o_dn, o_sb, o_ss, gates, w_br_all, layer, *, tm, tn=512):
    m = gates.shape[0]
    row = lambda: pl.BlockSpec((tm, 512), lambda i, j: (i, 0))
    gate = lambda k: pl.BlockSpec((tm, tn), lambda i, j: (i, k * D_MODEL // tn + j))
    wrow = lambda k: pl.BlockSpec((None, 512, tn), lambda i, j: (layer, k, j))
    return pl.pallas_call(
        _merge_kernel,
        grid=(m // tm, D_MODEL // tn),
        in_specs=[row(), row(), row(), row(), pl.BlockSpec((tm, 512), lambda i, j: (i, 1)),
                  gate(0), gate(1), gate(2), gate(3),
                  wrow(0), wrow(1), wrow(2), wrow(3), wrow(4)],
        out_specs=pl.BlockSpec((tm, tn), lambda i, j: (i, j)),
        out_shape=jax.ShapeDtypeStruct((m, D_MODEL), BF16),
        compiler_params=_cparams(("parallel", "arbitrary")),
        name="merge",
    )(o_pool, o_dn, o_sb, o_ss, o_ss, gates, gates, gates, gates, w_br_all, w_br_all, w_br_all, w_br_all, w_br_all)


def _outproj_kernel(m_ref, x_ref, w_ref, g_ref, o_ref):
    mix = jnp.dot(m_ref[...], w_ref[...], preferred_element_type=F32)
    o_ref[...] = x_ref[...] + _rms_rows(mix, g_ref[...])


def _outproj(merged, x2, w_all, layer, g, *, tm):
    m = x2.shape[0]
    return pl.pallas_call(
        _outproj_kernel,
        grid=(m // tm,),
        in_specs=[
            pl.BlockSpec((tm, D_MODEL), lambda i: (i, 0)),
            pl.BlockSpec((tm, D_MODEL), lambda i: (i, 0)),
            pl.BlockSpec((None, D_MODEL, D_MODEL), lambda i: (layer, 0, 0)),
            pl.BlockSpec((1, D_MODEL), lambda i: (0, 0)),
        ],
        out_specs=pl.BlockSpec((tm, D_MODEL), lambda i: (i, 0)),
        out_shape=jax.ShapeDtypeStruct((m, D_MODEL), F32),
        compiler_params=_cparams(("parallel",)),
        name="outproj",
    )(merged, x2, w_all, g)


def _mlp_kernel(x_ref, g1_ref, wu_ref, wd_ref, g2_ref, o_ref, h_ref, acc_ref, *, tm):
    f = pl.program_id(1)

    @pl.when(f == 0)
    def _():
        _norm_rows_to(h_ref, x_ref, g1_ref, tm)
        acc_ref[...] = jnp.zeros(acc_ref.shape, F32)

    a = jnp.dot(h_ref[...], wu_ref[...], preferred_element_type=F32)
    a = jnp.square(jnp.maximum(a, 0.0)).astype(BF16)
    acc_ref[...] += jnp.dot(a, wd_ref[...], preferred_element_type=F32)

    @pl.when(f == pl.num_programs(1) - 1)
    def _():
        o_ref[...] = x_ref[...] + _rms_rows(acc_ref[...], g2_ref[...])


def _mlp(x2, g1, wu_all, wd_all, layer, g2, *, tm, tf=1024):
    m = x2.shape[0]
    return pl.pallas_call(
        functools.partial(_mlp_kernel, tm=tm),
        grid=(m // tm, D_FF // tf),
        in_specs=[
            pl.BlockSpec((tm, D_MODEL), lambda i, f: (i, 0)),
            pl.BlockSpec((1, D_MODEL), lambda i, f: (0, 0)),
            pl.BlockSpec((None, D_MODEL, tf), lambda i, f: (layer, 0, f)),
            pl.BlockSpec((None, tf, D_MODEL), lambda i, f: (layer, f, 0)),
            pl.BlockSpec((1, D_MODEL), lambda i, f: (0, 0)),
        ],
        out_specs=pl.BlockSpec((tm, D_MODEL), lambda i, f: (i, 0)),
        out_shape=jax.ShapeDtypeStruct((m, D_MODEL), F32),
        scratch_shapes=[pltpu.VMEM((tm, D_MODEL), BF16), pltpu.VMEM((tm, D_MODEL), F32)],
        compiler_params=_cparams(("parallel", "arbitrary")),
        name="mlp",
    )(x2, g1, wu_all, wd_all, g2)


def _lane_rows(vals, offset):
    out = jnp.zeros((vals.shape[0], LANES), F32)
    for k in range(3):
        o = offset + k * SM_REP
        out = out.at[:, o:o + vals.shape[1]].set(vals.astype(F32))
    return out


def _prepare_params(p):
    w_in = p["w_in"]
    depth = w_in.shape[0]
    small = w_in[:, :, _O_DNB:_O_SB], w_in[:, :, _O_SSDT:_O_GATE]
    gap = jnp.zeros((depth, D_MODEL, SM_REP - SM_GROUP), w_in.dtype)
    w_u = jnp.concatenate(
        [w_in[:, :, :_O_DNB], w_in[:, :, _O_SB:_O_SSDT], *small, gap, *small, gap, *small,
         jnp.zeros((depth, D_MODEL, NU - C_SMALL - 2 * SM_REP - SM_GROUP), w_in.dtype)], axis=2).astype(BF16)
    w_g = w_in[:, :, _O_GATE:_O_END].astype(BF16)
    zrow = jnp.zeros((depth, LANES), F32)
    par = lambda a_log, dt_bias, off: jnp.stack(
        [_lane_rows(a_log, off), _lane_rows(dt_bias, off)] + [zrow] * (SUBLANES - 2), axis=1)
    return dict(
        w_u=w_u, w_g=w_g,
        n_mix_pre=p["norm_mix_pre"], n_mix_post=p["norm_mix_post"],
        n_mlp_pre=p["norm_mlp_pre"], n_mlp_post=p["norm_mlp_post"],
        pool_w=p["pool_w"].astype(BF16), pool_scale=p["pool_scale"],
        dn_conv_w=p["dn_conv_w"], dn_par=par(p["dn_a_log"], p["dn_dt_bias"], SM_DECAY), dn_norm_w=p["dn_norm_w"],
        sb_bias=p["sb_bias"],
        ssm_conv_w=p["ssm_conv_w"], ssm_conv_b=p["ssm_conv_b"],
        ss_par=par(p["ssm_a_log"], p["ssm_dt_bias"], SM_DT), ssm_norm_w=p["ssm_norm_w"],
        ssm_d=jnp.repeat(p["ssm_d"].astype(F32), SSM_HEAD_DIM, axis=1),
        w_branch=p["w_branch"].astype(BF16), w_out=p["w_out"].astype(BF16),
        w_up=p["w_up"].astype(BF16), w_down=p["w_down"].astype(BF16),
    )


def _trunk_layer(x, states, pw, layer, *, past=None):
    b, l, _ = x.shape
    m = b * l
    pool_buf, dn_conv, dn_s, ssm_conv, ssm_h = states
    prompt = past is None
    tm_big = min(m, 1024)
    tm = min(m, 512)
    lc = min(l, 256)
    lp = max(lc, 2 * DN_CHUNK)
    row = lambda name: pw[name][layer].reshape(1, -1)

    x2 = x.reshape(m, D_MODEL)
    u2 = _inproj(x2, row("n_mix_pre"), pw["w_u"], layer, tm=tm_big)
    gates = _inproj(x2, row("n_mix_pre"), pw["w_g"], layer, tm=tm_big, gate=True)
    u3 = u2.reshape(b, l, NU)

    pos0 = 0 if prompt else past[2].shape[1] * (past[0].shape[2] // SB_HEADS)
    o_pool, pool_new = _pool(u3, pool_buf, pw["pool_w"][layer], row("pool_scale"), pos0=pos0)
    o_dn, dn_conv_new, dn_s_new = _deltanet(u3, dn_conv, dn_s, pw["dn_conv_w"][layer], pw["dn_par"][layer],
                                            row("dn_norm_w"), lc=lc, lp=lp)
    o_ss, ss_conv_new, ss_h_new = _ssd(u3, ssm_conv, ssm_h, pw["ssm_conv_w"][layer], row("ssm_conv_b"),
                                       pw["ss_par"][layer], row("ssm_norm_w"), row("ssm_d"), lc=lc, lp=lp)

    k_new = u3[:, :, C_SBK:C_SBK + SB_WIDTH]
    v_new = u3[:, :, C_SBV:C_SBV + SB_WIDTH]
    sb_bias = pw["sb_bias"][layer]
    if prompt:
        o_sb = _sb_prompt(u3, sb_bias)
    else:
        cache_k, cache_v, page_table = past
        ncol = cache_k.shape[2]
        q = u3[:, :, C_SBQ:C_SBQ + SB_WIDTH].reshape(b, l, SB_HEADS, SB_HEAD_DIM)
        q_rows = jnp.transpose(q, (0, 2, 1, 3)).reshape(b, SB_HEADS * l, SB_HEAD_DIM)
        k_cur = jnp.pad(k_new.reshape(b, l * SB_HEADS, SB_HEAD_DIM), ((0, 0), (0, ncol - l * SB_HEADS), (0, 0)))
        v_cur = jnp.pad(v_new.reshape(b, l * SB_HEADS, SB_HEAD_DIM), ((0, 0), (0, ncol - l * SB_HEADS), (0, 0)))
        bias_rows = jnp.broadcast_to(jnp.repeat(sb_bias, l)[:, None], (SB_HEADS * l, ncol)).astype(F32)
        acc = _sb_sample(q_rows, k_cur, v_cur, bias_rows, cache_k, cache_v, page_table, layer)
        o_sb = jnp.transpose(acc.reshape(b, SB_HEADS, l, SB_HEAD_DIM), (0, 2, 1, 3)).reshape(b, l, SB_WIDTH)

    merged = _merge(o_pool.reshape(m, -1), o_dn.reshape(m, -1), o_sb.reshape(m, -1), o_ss.reshape(m, -1),
                    gates, pw["w_branch"], layer, tm=tm)
    x2 = _outproj(merged, x2, pw["w_out"], layer, row("n_mix_post"), tm=tm)
    x2 = _mlp(x2, row("n_mlp_pre"), pw["w_up"], pw["w_down"], layer, row("n_mlp_post"), tm=tm)
    new_states = (k_new.reshape(b, l, SB_HEADS, SB_HEAD_DIM), v_new.reshape(b, l, SB_HEADS, SB_HEAD_DIM),
                  pool_new, dn_conv_new, dn_s_new, ss_conv_new, ss_h_new)
    return x2.reshape(b, l, D_MODEL), new_states


def kernel(x_prompt, x_sample, cache_sb_k, cache_sb_v, state_pool, state_dn_conv, state_dn_s, state_ssm_conv, state_ssm_h, page_table, norm_mix_pre, norm_mix_post, norm_mlp_pre, norm_mlp_post, w_in, pool_w, pool_scale, dn_conv_w, dn_a_log, dn_dt_bias, dn_norm_w, sb_bias, ssm_conv_w, ssm_conv_b, ssm_a_log, ssm_dt_bias, ssm_d, ssm_norm_w, w_branch, w_out, w_up, w_down):
    pw = _prepare_params(dict(
        norm_mix_pre=norm_mix_pre, norm_mix_post=norm_mix_post, norm_mlp_pre=norm_mlp_pre,
        norm_mlp_post=norm_mlp_post, w_in=w_in, pool_w=pool_w, pool_scale=pool_scale,
        dn_conv_w=dn_conv_w, dn_a_log=dn_a_log, dn_dt_bias=dn_dt_bias, dn_norm_w=dn_norm_w,
        sb_bias=sb_bias, ssm_conv_w=ssm_conv_w, ssm_conv_b=ssm_conv_b, ssm_a_log=ssm_a_log,
        ssm_dt_bias=ssm_dt_bias, ssm_d=ssm_d, ssm_norm_w=ssm_norm_w, w_branch=w_branch,
        w_out=w_out, w_up=w_up, w_down=w_down))
    depth = w_in.shape[0]
    bp = x_prompt.shape[0]
    dt_ = x_prompt.dtype
    zero_states = (jnp.zeros((bp, POOL_BUF, POOL_WIDTH), dt_),
                   jnp.zeros((bp, CONV_WIDTH - 1, DN_CONV_DIM), dt_),
                   jnp.zeros((bp, DN_HEADS, DN_DK, DN_DV), dt_),
                   jnp.zeros((bp, CONV_WIDTH - 1, SSM_CONV_DIM), dt_),
                   jnp.zeros((bp, SSM_HEADS, SSM_HEAD_DIM, SSM_STATE), dt_))
    n_pool, page = cache_sb_k.shape[1], cache_sb_k.shape[2]
    cache_k = cache_sb_k.reshape(depth, n_pool, page * SB_HEADS, SB_HEAD_DIM)
    cache_v = cache_sb_v.reshape(depth, n_pool, page * SB_HEADS, SB_HEAD_DIM)
    y_prompt, y_sample = x_prompt, x_sample
    new_p, new_s = [], []
    for layer in range(depth):
        y_prompt, st_p = _trunk_layer(y_prompt, zero_states, pw, layer)
        sample_states = (state_pool[layer], state_dn_conv[layer], state_dn_s[layer],
                         state_ssm_conv[layer], state_ssm_h[layer])
        y_sample, st_s = _trunk_layer(y_sample, sample_states, pw, layer, past=(cache_k, cache_v, page_table))
        new_p.append(st_p)
        new_s.append(st_s)
    outs_p = [jnp.stack(t) for t in zip(*new_p)]
    outs_s = [jnp.stack(t) for t in zip(*new_s)]
    return (y_prompt, y_sample, *outs_p, *outs_s)
```

```python
import functools
import math

import jax
import jax.numpy as jnp
from jax import lax
from jax.experimental import pallas as pl
from jax.experimental.pallas import tpu as pltpu

F32 = jnp.float32
BF16 = jnp.bfloat16

D_MODEL = 2048
N_BRANCH = 4
RMS_EPS = 1e-6
CONV_WIDTH = 4
POOL_WINDOWS = (2, 4, 8, 16)
POOL_WIDTH = D_MODEL // 4
POOL_GW = POOL_WIDTH // 4
POOL_BUF = max(POOL_WINDOWS) - 1
DN_HEADS = 4
DN_DK = 128
DN_DV = 128
DN_QK = DN_HEADS * DN_DK
DN_VW = DN_HEADS * DN_DV
DN_CONV_DIM = 2 * DN_QK + DN_VW
DN_CHUNK = 64
SB_HEADS = 4
SB_HEAD_DIM = 128
SB_WIDTH = SB_HEADS * SB_HEAD_DIM
SSM_D_INNER = D_MODEL // 2
SSM_HEAD_DIM = 64
SSM_HEADS = SSM_D_INNER // SSM_HEAD_DIM
SSM_GROUPS = 2
SSM_HPG = SSM_HEADS // SSM_GROUPS
SSM_STATE = 128
SSM_CONV_DIM = SSM_D_INNER + 2 * SSM_GROUPS * SSM_STATE
SSM_CHUNK = 64
D_FF = 4 * D_MODEL

LANES = 128
SUBLANES = 8

C_POOL = 0
C_DNQ, C_DNK, C_DNV, C_DNZ = 512, 1024, 1536, 2048
C_SBQ, C_SBK, C_SBV = 2560, 3072, 3584
C_SSZ, C_SSX, C_SSBC = 4096, 5120, 6144
C_SMALL = 6656
NU = 7168
_O_DNB = POOL_WIDTH + DN_CONV_DIM + DN_VW
_O_SB = _O_DNB + 2 * DN_HEADS
_O_SSDT = _O_SB + 3 * SB_WIDTH + SSM_D_INNER + SSM_CONV_DIM
_O_GATE = _O_SSDT + SSM_HEADS
_O_END = _O_GATE + N_BRANCH * D_MODEL
SM_BETA, SM_DECAY, SM_DT = 0, DN_HEADS, 2 * DN_HEADS
SM_GROUP = 2 * DN_HEADS + SSM_HEADS
SM_REP = 32

VMEM_LIMIT_MB = 56


def _cparams(sem, vmem_mb=VMEM_LIMIT_MB):
    return pltpu.CompilerParams(dimension_semantics=sem, vmem_limit_bytes=vmem_mb * 1024 * 1024)


def _sigmoid(x):
    return 1.0 / (1.0 + jnp.exp(-x))


def _silu(x):
    hx = 0.5 * x
    return hx + hx * jnp.tanh(hx)


def _softplus(x):
    return jnp.maximum(x, 0.0) + jnp.log1p(jnp.exp(-jnp.abs(x)))


def _log_sigmoid(x):
    return jnp.minimum(x, 0.0) - jnp.log1p(jnp.exp(-jnp.abs(x)))


def _dot(a, b):
    return jnp.dot(a.astype(BF16), b.astype(BF16), preferred_element_type=F32)


def _dot_nt(a, b):
    return lax.dot_general(a.astype(BF16), b.astype(BF16), (((1,), (1,)), ((), ())), preferred_element_type=F32)


def _dot_tn(a, b):
    return lax.dot_general(a.astype(BF16), b.astype(BF16), (((0,), (0,)), ((), ())), preferred_element_type=F32)


def _split2(a):
    hi = a.astype(BF16)
    lo = (a - hi.astype(F32)).astype(BF16)
    return hi, lo


def _expand_heads(src, n_heads, width, lane0):
    k = lax.broadcasted_iota(jnp.int32, (LANES, n_heads * width), 0)
    n = lax.broadcasted_iota(jnp.int32, (LANES, n_heads * width), 1)
    sel = jnp.where((k % SM_REP - lane0 == n // width) & (k < 3 * SM_REP), 1.0, 0.0).astype(BF16)
    lane = lax.broadcasted_iota(jnp.int32, src.shape, 1)
    hi = src.astype(BF16).astype(F32)
    r1 = src - hi
    mid = r1.astype(BF16).astype(F32)
    lo = r1 - mid
    pieces = jnp.where(lane < SM_REP, hi, jnp.where(lane < 2 * SM_REP, mid, lo))
    return jnp.dot(pieces.astype(BF16), sel, preferred_element_type=F32)


def _dot3(a, b):
    ah, al = _split2(a)
    bh, bl = _split2(b)
    d = lambda x, y: jnp.dot(x, y, preferred_element_type=F32)
    return d(ah, bh) + (d(ah, bl) + d(al, bh))


def _rms_rows(x, g):
    return x * lax.rsqrt(jnp.mean(x * x, axis=-1, keepdims=True) + RMS_EPS) * g


def _norm_rows_to(h_ref, x_ref, g_ref, tm):
    ch = min(tm, 256)

    def body(r, c):
        rs = pl.ds(pl.multiple_of(r * ch, ch), ch)
        h_ref[rs, :] = _rms_rows(x_ref[rs, :], g_ref[...]).astype(h_ref.dtype)
        return c

    lax.fori_loop(0, tm // ch, body, 0)


def _inproj_kernel(x_ref, g_ref, w_ref, o_ref, h_ref, *, tm, gate):
    @pl.when(pl.program_id(1) == 0)
    def _():
        _norm_rows_to(h_ref, x_ref, g_ref, tm)

    acc = jnp.dot(h_ref[...], w_ref[...], preferred_element_type=F32)
    o_ref[...] = (_sigmoid(acc) if gate else acc).astype(o_ref.dtype)


def _inproj(x2, g, w_all, layer, *, tm, tn=1024, gate=False):
    m = x2.shape[0]
    n = w_all.shape[2]
    return pl.pallas_call(
        functools.partial(_inproj_kernel, tm=tm, gate=gate),
        grid=(m // tm, n // tn),
        in_specs=[
            pl.BlockSpec((tm, D_MODEL), lambda i, j: (i, 0)),
            pl.BlockSpec((1, D_MODEL), lambda i, j: (0, 0)),
            pl.BlockSpec((None, D_MODEL, tn), lambda i, j: (layer, 0, j)),
        ],
        out_specs=pl.BlockSpec((tm, tn), lambda i, j: (i, j)),
        out_shape=jax.ShapeDtypeStruct((m, n), BF16 if gate else F32),
        scratch_shapes=[pltpu.VMEM((tm, D_MODEL), BF16)],
        compiler_params=_cparams(("parallel", "arbitrary")),
        name="inproj_gate" if gate else "inproj",
    )(x2, g, w_all)


def _pool_kernel(u_ref, buf_ref, w_ref, sc_ref, o_ref, new_ref, ext_ref, *, lin, lp, pos0):
    ext_ref[0:16, :] = jnp.zeros((16, POOL_WIDTH), F32)
    ext_ref[1:16, :] = buf_ref[0]
    if lin < lp:
        ext_ref[16:16 + lp, :] = jnp.zeros((lp, POOL_WIDTH), F32)
    ext_ref[16:16 + lin, :] = u_ref[0]
    ch = min(lp, 256)
    for c0 in range(0, lp, ch):
        pos = pos0 + c0 + lax.broadcasted_iota(jnp.int32, (ch, 1), 0)
        for gi, w in enumerate(POOL_WINDOWS):
            cols = slice(gi * POOL_GW, (gi + 1) * POOL_GW)
            s = ext_ref[16 + c0:16 + c0 + ch, cols]
            tot = s
            for k in range(1, w):
                tot = tot + ext_ref[16 + c0 - k:16 + c0 - k + ch, cols]
            cnt = jnp.minimum(pos + 1, w).astype(F32)
            y = tot / cnt - s
            yo = _dot(y, w_ref[gi]) * sc_ref[:, cols]
            n = min(ch, lin - c0)
            o_ref[0, c0:c0 + n, cols] = yo[:n]
    new_ref[0] = ext_ref[1 + lin:16 + lin, :]


def _pool(u3, buf, w, sc, *, pos0):
    b, lin, _ = u3.shape
    lp = max(lin, SUBLANES)
    return pl.pallas_call(
        functools.partial(_pool_kernel, lin=lin, lp=lp, pos0=pos0),
        grid=(b,),
        in_specs=[
            pl.BlockSpec((1, lin, POOL_WIDTH), lambda i: (i, 0, C_POOL // POOL_WIDTH)),
            pl.BlockSpec((1, POOL_BUF, POOL_WIDTH), lambda i: (i, 0, 0)),
            pl.BlockSpec((4, POOL_GW, POOL_GW), lambda i: (0, 0, 0)),
            pl.BlockSpec((1, POOL_WIDTH), lambda i: (0, 0)),
        ],
        out_specs=[
            pl.BlockSpec((1, lin, POOL_WIDTH), lambda i: (i, 0, 0)),
            pl.BlockSpec((1, POOL_BUF, POOL_WIDTH), lambda i: (i, 0, 0)),
        ],
        out_shape=[
            jax.ShapeDtypeStruct((b, lin, POOL_WIDTH), F32),
            jax.ShapeDtypeStruct((b, POOL_BUF, POOL_WIDTH), F32),
        ],
        scratch_shapes=[pltpu.VMEM((16 + lp, POOL_WIDTH), F32)],
        compiler_params=_cparams(("parallel",)),
        name="pool",
    )(u3, buf, w, sc)


ROWS = 64


def _fill_ext(ext_ref, cbuf_ref, parts, *, lin, lp):
    l = pl.program_id(1)

    @pl.when(l == 0)
    def _():
        ext_ref[5:8, :] = cbuf_ref[0]

    @pl.when(l > 0)
    def _():
        ext_ref[5:8, :] = ext_ref[5 + lin:8 + lin, :]

    if lin < lp:
        ext_ref[8:8 + lp, :] = jnp.zeros((lp, ext_ref.shape[1]), F32)
    for ref, c0, width in parts:
        ext_ref[8:8 + lin, c0:c0 + width] = ref[0]


def _conv_silu(ext_ref, r0, n, cols, w_ref, b_ref):
    acc = None
    for i in range(CONV_WIDTH):
        part = ext_ref[5 + r0 + i:5 + r0 + i + n, cols] * w_ref[i:i + 1, cols]
        acc = part if acc is None else acc + part
    if b_ref is not None:
        acc = acc + b_ref[:, cols]
    return _silu(acc)


def _pad_rows(dst_ref, src_ref, *, lin, lp):
    if lin < lp:
        dst_ref[...] = jnp.zeros(dst_ref.shape, F32)
    dst_ref[0:lin, :] = src_ref[0]


def _row_valid(r0, n, lin):
    if r0 + n <= lin:
        return None
    return (r0 + lax.broadcasted_iota(jnp.int32, (n, 1), 0)) < lin


def _chunk_cumsum(x, chunk):
    rin = lax.broadcasted_iota(jnp.int32, x.shape, 0) % chunk
    s = 1
    while s < chunk:
        x = x + jnp.where(rin >= s, pltpu.roll(x, s, 0), 0.0)
        s *= 2
    return x


def _diag_inv2(a0, a1):
    c = a0.shape[0]
    hb = c // 2
    ng = hb // SUBLANES
    sub = lax.broadcasted_iota(jnp.int32, (SUBLANES, 2 * c), 0)
    lane = lax.broadcasted_iota(jnp.int32, (SUBLANES, 2 * c), 1)
    base = (lane // hb) * hb
    lmod = lane - base
    a01 = jnp.concatenate([a0, a1], axis=1)
    odd = ((lax.broadcasted_iota(jnp.int32, (hb, 2 * c), 1) // hb) % 2) == 1
    packed = jnp.where(odd, a01[hb:], a01[:hb])
    racc = [jnp.zeros((SUBLANES, 2 * c), F32) for _ in range(ng)]
    tg = [jnp.zeros((SUBLANES, 2 * c), F32) for _ in range(ng)]
    for j in range(hb):
        gj, rj = divmod(j, SUBLANES)
        t_j = jnp.where(lmod[0:1] == j, 1.0, 0.0) - racc[gj][rj:rj + 1, :]
        tg[gj] = jnp.where(sub == rj, t_j, tg[gj])
        if j == hb - 1:
            break
        for g in range(gj, ng):
            col = jnp.take_along_axis(packed[g * SUBLANES:(g + 1) * SUBLANES], base + j, axis=1)
            racc[g] = racc[g] + col * t_j
    dinv = jnp.concatenate(tg, axis=0)
    bd = jnp.concatenate([jnp.where(odd, 0.0, dinv), jnp.where(odd, dinv, 0.0)], axis=0)
    return [bd[:, :c], bd[:, c:]]


def _dn_kernel(q_ref, k_ref, v_ref, z_ref, sm_ref, cbuf_ref, s0_ref, cw_ref, par_ref, nw_ref,
               o_ref, cnew_ref, snew_ref,
               ext_ref, qn_ref, kn_ref, kb_ref, zz_ref, smp_ref, be_ref, ge_ref, gt_ref, oo_ref, s_ref,
               uc_ref, wc_ref, qe_ref, kt_ref, at_ref, rhs_ref, *, lin, lp, chunk):
    l = pl.program_id(1)
    nch = lp // chunk

    @pl.when(l == 0)
    def _():
        s_ref[...] = s0_ref[0]

    _fill_ext(ext_ref, cbuf_ref, ((q_ref, 0, DN_QK), (k_ref, DN_QK, DN_QK), (v_ref, 2 * DN_QK, DN_VW)), lin=lin, lp=lp)
    _pad_rows(zz_ref, z_ref, lin=lin, lp=lp)
    _pad_rows(smp_ref, sm_ref, lin=lin, lp=lp)

    sm = smp_ref[...]
    beta = _sigmoid(sm)
    g = -jnp.exp(par_ref[0:1, :]) * _softplus(sm + par_ref[1:2, :])
    valid = _row_valid(0, lp, lin)
    if valid is not None:
        beta = jnp.where(valid, beta, 0.0)
        g = jnp.where(valid, g, 0.0)
    gcum = _chunk_cumsum(g, chunk)
    be_ref[...] = _expand_heads(beta, DN_HEADS, DN_DK, SM_BETA)
    ge_ref[...] = _expand_heads(gcum, DN_HEADS, DN_DK, SM_DECAY)
    gt = gcum.T
    for c in range(nch):
        gt_ref[c] = gt[:, c * chunk:(c + 1) * chunk]

    assert ROWS == chunk
    for r0 in range(0, lp, ROWS):
        valid = _row_valid(r0, ROWS, lin)
        rows = slice(r0, r0 + ROWS)
        for h in range(DN_HEADS):
            hs = slice(h * DN_DK, (h + 1) * DN_DK)
            conv = lambda part: _conv_silu(ext_ref, r0, ROWS, slice(part * DN_QK + h * DN_DK, part * DN_QK + (h + 1) * DN_DK), cw_ref, None)
            l2n = lambda y: y * lax.rsqrt(jnp.sum(y * y, axis=-1, keepdims=True) + 1e-6)
            q = l2n(conv(0)) * (DN_DK ** -0.5)
            k = l2n(conv(1))
            v = conv(2)
            if valid is not None:
                q, k, v = (jnp.where(valid, t, 0.0) for t in (q, k, v))
            bt = be_ref[rows, hs]
            ge = ge_ref[rows, hs]
            eg = jnp.exp(ge)
            kb = k * bt
            qn_ref[rows, hs] = q
            kn_ref[rows, hs] = k
            kb_ref[rows, hs] = kb
            qe_ref[rows, hs] = q * eg
            kt_ref[rows, hs] = k * jnp.exp(ge[ROWS - 1:ROWS, :] - ge)
            rhs_ref[h, rows, 0:DN_DV] = v * bt
            rhs_ref[h, rows, DN_DV:2 * DN_DV] = kb * eg

    rid = lax.broadcasted_iota(jnp.int32, (chunk, chunk), 0)
    cid = lax.broadcasted_iota(jnp.int32, (chunk, chunk), 1)
    strict_lower = cid < rid
    lower = cid <= rid
    off = (rid >= chunk // 2) & (cid < chunk // 2)
    heads = range(DN_HEADS)
    hsl = [slice(h * DN_DK, (h + 1) * DN_DK) for h in heads]

    def prep_body(c2, carry):
        items = [(i, h) for i in range(2) for h in heads]
        cidx = [2 * c2 + i for i in range(2)]
        rs = [pl.ds(pl.multiple_of(c * chunk, chunk), chunk) for c in cidx]
        kh = [kn_ref[rs[c], hsl[h]] for c, h in items]
        decay = [jnp.exp(jnp.where(lower, ge_ref[rs[c], h * DN_DK:h * DN_DK + chunk]
                                   - gt_ref[cidx[c], SM_DECAY + h:SM_DECAY + h + 1, :], -jnp.inf)) for c, h in items]
        amat = [_dot_nt(kb_ref[rs[c], hsl[h]], kh[n]) * jnp.where(strict_lower, decay[n], 0.0)
                for n, (c, h) in enumerate(items)]
        for n, (c, h) in enumerate(items):
            at_ref[cidx[c] * DN_HEADS + h] = _dot_nt(qn_ref[rs[c], hsl[h]], kh[n]) * decay[n]
        dinv = sum((_diag_inv2(amat[n], amat[n + 1]) for n in range(0, len(items), 2)), [])
        inner = [_dot3(jnp.where(off, a, 0.0), d) for a, d in zip(amat, dinv)]
        tmat = [d - _dot3(d, i) for d, i in zip(dinv, inner)]
        sol = [_dot3(t, rhs_ref[h, rs[c], :]) for t, (c, h) in zip(tmat, items)]
        for s, (c, h) in zip(sol, items):
            uc_ref[rs[c], hsl[h]] = s[:, :DN_DV]
            wc_ref[rs[c], hsl[h]] = s[:, DN_DV:]
        return carry

    assert nch % 2 == 0
    lax.fori_loop(0, nch // 2, prep_body, 0)

    def scan_body(c, carry):
        r0 = pl.multiple_of(c * chunk, chunk)
        rs = pl.ds(r0, chunk)
        glast = ge_ref[pl.ds(r0 + chunk - 1, 1), :]
        sh = [s_ref[h] for h in heads]
        ws = [_dot(wc_ref[rs, hsl[h]], sh[h]) for h in heads]
        qs = [_dot(qe_ref[rs, hsl[h]], sh[h]) for h in heads]
        v_new = [uc_ref[rs, hsl[h]] - ws[h] for h in heads]
        o2 = [_dot(at_ref[c * DN_HEADS + h], v_new[h]) for h in heads]
        kv = [_dot_tn(kt_ref[rs, hsl[h]], v_new[h]) for h in heads]
        for h in heads:
            s_ref[h] = sh[h] * jnp.exp(glast[:, hsl[h]]) + kv[h]
            oo_ref[rs, hsl[h]] = _rms_rows(qs[h] + o2[h], nw_ref[...]) * _silu(zz_ref[rs, hsl[h]])
        return carry

    lax.fori_loop(0, nch, scan_body, 0)
    o_ref[0] = oo_ref[0:lin, :]

    @pl.when(l == pl.num_programs(1) - 1)
    def _():
        cnew_ref[0] = ext_ref[5 + lin:8 + lin, :]
        snew_ref[0] = s_ref[...]


def _deltanet(u3, cbuf, s0, cw, par, nw, *, lc, lp):
    b, l, _ = u3.shape
    nl = l // lc
    blk = lambda c0: pl.BlockSpec((1, lc, 512), lambda i, j: (i, j, c0 // 512))
    return pl.pallas_call(
        functools.partial(_dn_kernel, lin=lc, lp=lp, chunk=DN_CHUNK),
        grid=(b, nl),
        in_specs=[
            blk(C_DNQ), blk(C_DNK), blk(C_DNV), blk(C_DNZ),
            pl.BlockSpec((1, lc, LANES), lambda i, j: (i, j, C_SMALL // LANES)),
            pl.BlockSpec((1, CONV_WIDTH - 1, DN_CONV_DIM), lambda i, j: (i, 0, 0)),
            pl.BlockSpec((1, DN_HEADS, DN_DK, DN_DV), lambda i, j: (i, 0, 0, 0)),
            pl.BlockSpec((CONV_WIDTH, DN_CONV_DIM), lambda i, j: (0, 0)),
            pl.BlockSpec((SUBLANES, LANES), lambda i, j: (0, 0)),
            pl.BlockSpec((1, DN_DV), lambda i, j: (0, 0)),
        ],
        out_specs=[
            pl.BlockSpec((1, lc, DN_VW), lambda i, j: (i, j, 0)),
            pl.BlockSpec((1, CONV_WIDTH - 1, DN_CONV_DIM), lambda i, j: (i, 0, 0)),
            pl.BlockSpec((1, DN_HEADS, DN_DK, DN_DV), lambda i, j: (i, 0, 0, 0)),
        ],
        out_shape=[
            jax.ShapeDtypeStruct((b, l, DN_VW), F32),
            jax.ShapeDtypeStruct((b, CONV_WIDTH - 1, DN_CONV_DIM), F32),
            jax.ShapeDtypeStruct((b, DN_HEADS, DN_DK, DN_DV), F32),
        ],
        scratch_shapes=[
            pltpu.VMEM((8 + lp, DN_CONV_DIM), F32),
            pltpu.VMEM((lp, DN_QK), F32), pltpu.VMEM((lp, DN_QK), F32), pltpu.VMEM((lp, DN_VW), F32),
            pltpu.VMEM((lp, DN_VW), F32), pltpu.VMEM((lp, LANES), F32),
            pltpu.VMEM((lp, DN_QK), F32), pltpu.VMEM((lp, DN_QK), F32),
            pltpu.VMEM((lp // DN_CHUNK, LANES, DN_CHUNK), F32),
            pltpu.VMEM((lp, DN_VW), F32),
            pltpu.VMEM((DN_HEADS, DN_DK, DN_DV), F32),
            pltpu.VMEM((lp, DN_VW), F32), pltpu.VMEM((lp, DN_QK), F32),
            pltpu.VMEM((lp, DN_QK), F32), pltpu.VMEM((lp, DN_QK), F32),
            pltpu.VMEM((lp // DN_CHUNK * DN_HEADS, DN_CHUNK, DN_CHUNK), F32),
            pltpu.VMEM((DN_HEADS, lp, 2 * DN_DV), F32),
        ],
        compiler_params=_cparams(("parallel", "arbitrary")),
        name="deltanet",
    )(u3, u3, u3, u3, u3, cbuf, s0, cw, par, nw)


def _ssd_kernel(z_ref, x_ref, bc_ref, sm_ref, cbuf_ref, h0_ref, cw_ref, cb_ref, par_ref, nw_ref, dsk_ref,
                o_ref, cnew_ref, hnew_ref,
                ext_ref, xs_ref, bm_ref, cm_ref, zz_ref, smp_ref, ce_ref, de_ref, ct_ref, yy_ref, ht_ref,
                *, lin, lp, chunk):
    assert chunk == SSM_HEAD_DIM
    l = pl.program_id(1)
    nch = lp // chunk
    gn = SSM_GROUPS * SSM_STATE

    @pl.when(l == 0)
    def _():
        for g in range(SSM_GROUPS):
            hg = h0_ref[0, g * SSM_HPG:(g + 1) * SSM_HPG]
            ht_ref[g] = hg.reshape(SSM_HPG * SSM_HEAD_DIM, SSM_STATE).T

    _fill_ext(ext_ref, cbuf_ref, ((x_ref, 0, SSM_D_INNER), (bc_ref, SSM_D_INNER, 2 * gn)), lin=lin, lp=lp)
    _pad_rows(zz_ref, z_ref, lin=lin, lp=lp)
    _pad_rows(smp_ref, sm_ref, lin=lin, lp=lp)

    for r0 in range(0, lp, ROWS):
        valid = _row_valid(r0, ROWS, lin)
        for c0 in range(0, SSM_CONV_DIM, LANES):
            y = _conv_silu(ext_ref, r0, ROWS, slice(c0, c0 + LANES), cw_ref, cb_ref)
            if valid is not None:
                y = jnp.where(valid, y, 0.0)
            if c0 < SSM_D_INNER:
                xs_ref[r0:r0 + ROWS, c0:c0 + LANES] = y
            elif c0 < SSM_D_INNER + gn:
                bm_ref[r0:r0 + ROWS, c0 - SSM_D_INNER:c0 - SSM_D_INNER + LANES] = y
            else:
                cm_ref[r0:r0 + ROWS, c0 - SSM_D_INNER - gn:c0 - SSM_D_INNER - gn + LANES] = y

    sm = smp_ref[...]
    dt = _softplus(sm + par_ref[1:2, :])
    valid = _row_valid(0, lp, lin)
    if valid is not None:
        dt = jnp.where(valid, dt, 0.0)
    cum = _chunk_cumsum(dt * (-jnp.exp(par_ref[0:1, :])), chunk)
    ct = cum.T
    for c in range(nch):
        ct_ref[c] = ct[:, c * chunk:(c + 1) * chunk]

    spread = _expand_heads(jnp.concatenate([cum, dt], axis=0), SSM_HEADS, SSM_HEAD_DIM, SM_DT)
    ce_ref[...] = spread[:lp]
    de_ref[...] = spread[lp:]

    pw = 2 * SSM_HEAD_DIM
    row2 = lax.broadcasted_iota(jnp.int32, (chunk, 2 * chunk), 0)
    lane2 = lax.broadcasted_iota(jnp.int32, (chunk, 2 * chunk), 1)
    lower2 = (lane2 % chunk) <= row2
    first = lax.broadcasted_iota(jnp.int32, (chunk, pw), 1) < SSM_HEAD_DIM
    groups = range(SSM_GROUPS)
    gw_ = SSM_HPG * SSM_HEAD_DIM

    def chunk_body(c, carry):
        r0 = pl.multiple_of(c * chunk, chunk)
        rs = pl.ds(r0, chunk)
        bg = [bm_ref[rs, g * SSM_STATE:(g + 1) * SSM_STATE] for g in groups]
        cg = [cm_ref[rs, g * SSM_STATE:(g + 1) * SSM_STATE] for g in groups]
        ce = [ce_ref[rs, g * gw_:(g + 1) * gw_] for g in groups]
        de = [de_ref[rs, g * gw_:(g + 1) * gw_] for g in groups]
        xg = [xs_ref[rs, g * gw_:(g + 1) * gw_] for g in groups]
        ht = [ht_ref[g] for g in groups]
        cb = [_dot_nt(cg[g], bg[g]) for g in groups]
        ys = [_dot(cg[g], ht[g]) * jnp.exp(ce[g]) for g in groups]
        for g in groups:
            last = ce[g][chunk - 1:chunk, :]
            ht_ref[g] = ht[g] * jnp.exp(last) + _dot_tn(bg[g], xg[g] * (jnp.exp(last - ce[g]) * de[g]))
        for g in groups:
            cb2 = jnp.concatenate([cb[g], cb[g]], axis=1)
            xd = xg[g] * de[g]
            for p in range(SSM_HPG // 2):
                ps = slice(p * pw, (p + 1) * pw)
                hd = g * SSM_HPG + 2 * p
                ctp = ct_ref[c, SM_DT + hd:SM_DT + hd + 2, :]
                crp = jnp.concatenate([ctp[0:1], ctp[1:2]], axis=1)
                lm = jnp.exp(jnp.where(lower2, ce[g][:, ps] - crp, -jnp.inf))
                xdp = xd[:, ps]
                bd = jnp.concatenate([jnp.where(first, xdp, 0.0), jnp.where(first, 0.0, xdp)], axis=0)
                cols = slice(g * gw_ + p * pw, g * gw_ + (p + 1) * pw)
                yy_ref[rs, cols] = _dot(cb2 * lm, bd) + ys[g][:, ps] + dsk_ref[:, cols] * xg[g][:, ps]
        return carry

    lax.fori_loop(0, nch, chunk_body, 0)

    gw = SSM_D_INNER // SSM_GROUPS
    for r0 in range(0, lp, ROWS):
        n = min(ROWS, lin - r0)
        if n <= 0:
            break
        for g in range(SSM_GROUPS):
            cols = slice(g * gw, (g + 1) * gw)
            t = yy_ref[r0:r0 + ROWS, cols] * _silu(zz_ref[r0:r0 + ROWS, cols])
            t = _rms_rows(t, nw_ref[:, cols])
            o_ref[0, r0:r0 + n, cols] = t[:n]

    @pl.when(l == pl.num_programs(1) - 1)
    def _():
        cnew_ref[0] = ext_ref[5 + lin:8 + lin, :]
        for g in range(SSM_GROUPS):
            hnew_ref[0, g * SSM_HPG:(g + 1) * SSM_HPG] = ht_ref[g].T.reshape(SSM_HPG, SSM_HEAD_DIM, SSM_STATE)


def _ssd(u3, cbuf, h0, cw, cb, par, nw, d, *, lc, lp):
    b, l, _ = u3.shape
    nl = l // lc
    gn2 = 2 * SSM_GROUPS * SSM_STATE
    return pl.pallas_call(
        functools.partial(_ssd_kernel, lin=lc, lp=lp, chunk=SSM_CHUNK),
        grid=(b, nl),
        in_specs=[
            pl.BlockSpec((1, lc, SSM_D_INNER), lambda i, j: (i, j, C_SSZ // SSM_D_INNER)),
            pl.BlockSpec((1, lc, SSM_D_INNER), lambda i, j: (i, j, C_SSX // SSM_D_INNER)),
            pl.BlockSpec((1, lc, gn2), lambda i, j: (i, j, C_SSBC // gn2)),
            pl.BlockSpec((1, lc, LANES), lambda i, j: (i, j, C_SMALL // LANES)),
            pl.BlockSpec((1, CONV_WIDTH - 1, SSM_CONV_DIM), lambda i, j: (i, 0, 0)),
            pl.BlockSpec((1, SSM_HEADS, SSM_HEAD_DIM, SSM_STATE), lambda i, j: (i, 0, 0, 0)),
            pl.BlockSpec((CONV_WIDTH, SSM_CONV_DIM), lambda i, j: (0, 0)),
            pl.BlockSpec((1, SSM_CONV_DIM), lambda i, j: (0, 0)),
            pl.BlockSpec((SUBLANES, LANES), lambda i, j: (0, 0)),
            pl.BlockSpec((1, SSM_D_INNER), lambda i, j: (0, 0)),
            pl.BlockSpec((1, SSM_D_INNER), lambda i, j: (0, 0)),
        ],
        out_specs=[
            pl.BlockSpec((1, lc, SSM_D_INNER), lambda i, j: (i, j, 0)),
            pl.BlockSpec((1, CONV_WIDTH - 1, SSM_CONV_DIM), lambda i, j: (i, 0, 0)),
            pl.BlockSpec((1, SSM_HEADS, SSM_HEAD_DIM, SSM_STATE), lambda i, j: (i, 0, 0, 0)),
        ],
        out_shape=[
            jax.ShapeDtypeStruct((b, l, SSM_D_INNER), F32),
            jax.ShapeDtypeStruct((b, CONV_WIDTH - 1, SSM_CONV_DIM), F32),
            jax.ShapeDtypeStruct((b, SSM_HEADS, SSM_HEAD_DIM, SSM_STATE), F32),
        ],
        scratch_shapes=[
            pltpu.VMEM((8 + lp, SSM_CONV_DIM), F32),
            pltpu.VMEM((lp, SSM_D_INNER), F32),
            pltpu.VMEM((lp, SSM_GROUPS * SSM_STATE), F32), pltpu.VMEM((lp, SSM_GROUPS * SSM_STATE), F32),
            pltpu.VMEM((lp, SSM_D_INNER), F32), pltpu.VMEM((lp, LANES), F32),
            pltpu.VMEM((lp, SSM_D_INNER), F32), pltpu.VMEM((lp, SSM_D_INNER), F32),
            pltpu.VMEM((lp // SSM_CHUNK, LANES, SSM_CHUNK), F32),
            pltpu.VMEM((lp, SSM_D_INNER), F32),
            pltpu.VMEM((SSM_GROUPS, SSM_STATE, SSM_HPG * SSM_HEAD_DIM), F32),
        ],
        compiler_params=_cparams(("parallel", "arbitrary")),
        name="ssd",
    )(u3, u3, u3, u3, cbuf, h0, cw, cb, par, nw, d)


def _strict_upper_stack(n):
    j = lax.broadcasted_iota(jnp.int32, (2 * n, n), 0) % n
    s = lax.broadcasted_iota(jnp.int32, (2 * n, n), 1)
    return jnp.where(j > s, 1.0, 0.0).astype(BF16)


def _rev_excl_cumsum(la, uu):
    hi, lo = _split2(la)
    return jnp.dot(jnp.concatenate([hi, lo], axis=1), uu, preferred_element_type=F32)


def _sbp_kernel(bias_ref, q_ref, k_ref, v_ref, o_ref, *, tq, scale):
    h = pl.program_id(1)
    qi = pl.program_id(2)
    bias = bias_ref[h]
    q = q_ref[0].astype(BF16)
    uu = _strict_upper_stack(tq)

    def sweep(blocks, carry, diagonal):
        c, acc = carry
        rows = [pl.ds(kj * tq if isinstance(kj, int) else pl.multiple_of(kj * tq, tq), tq) for kj in blocks]
        zs = [_dot_nt(q, k_ref[0, r, :]) * scale + bias for r in rows]
        lss, las = [], []
        for z in zs:
            l1 = jnp.log1p(jnp.exp(-jnp.abs(z)))
            lss.append(jnp.minimum(z, 0.0) - l1)
            las.append(-jnp.maximum(z, 0.0) - l1)
        if diagonal:
            valid = lax.broadcasted_iota(jnp.int32, (tq, tq), 1) < lax.broadcasted_iota(jnp.int32, (tq, tq), 0)
            las = [jnp.where(valid, la, 0.0) for la in las]
        survs = [_rev_excl_cumsum(la, uu) for la in las]
        for ls, la, surv, r in zip(lss, las, survs, rows):
            att = jnp.exp(ls + surv + c)
            if diagonal:
                att = jnp.where(valid, att, 0.0)
            acc = acc + _dot(att, v_ref[0, r, :])
            c = c + jnp.sum(la, axis=1, keepdims=True)
        return c, acc

    carry = (jnp.zeros((tq, 1), F32), jnp.zeros((tq, SB_HEAD_DIM), F32))
    carry = sweep([qi], carry, True)
    carry = lax.fori_loop(0, lax.shift_right_logical(qi, 1),
                          lambda t, cr: sweep([qi - 1 - 2 * t, qi - 2 - 2 * t], cr, False), carry)
    carry = lax.fori_loop(0, qi & 1, lambda t, cr: sweep([0], cr, False), carry)
    o_ref[0] = carry[1]


def _sb_prompt(u3, bias, *, tq=256):
    b, l, _ = u3.shape
    tq = min(tq, l)
    kv = lambda c0: pl.BlockSpec((1, l, SB_HEAD_DIM), lambda i, h, j: (i, 0, c0 // SB_HEAD_DIM + h))
    return pl.pallas_call(
        functools.partial(_sbp_kernel, tq=tq, scale=SB_HEAD_DIM ** -0.5),
        grid=(b, SB_HEADS, l // tq),
        in_specs=[
            pl.BlockSpec(memory_space=pltpu.SMEM),
            pl.BlockSpec((1, tq, SB_HEAD_DIM), lambda i, h, j: (i, j, C_SBQ // SB_HEAD_DIM + h)),
            kv(C_SBK), kv(C_SBV),
        ],
        out_specs=pl.BlockSpec((1, tq, SB_HEAD_DIM), lambda i, h, j: (i, j, h)),
        out_shape=jax.ShapeDtypeStruct((b, l, SB_WIDTH), F32),
        compiler_params=_cparams(("parallel", "parallel", "arbitrary")),
        name="sb_prompt",
    )(bias, u3, u3, u3)


def _sbs_kernel(pt_ref, q_ref, kc_ref, vc_ref, bias_ref, uu_ref, *rest, pp, tq, scale):
    k_refs = rest[:pp]
    v_refs = rest[pp:2 * pp]
    o_ref = rest[2 * pp]
    c_ref = rest[2 * pp + 1]
    p = pl.program_id(1)
    hq = q_ref.shape[1]
    ncol = kc_ref.shape[1]
    q = q_ref[0].astype(BF16)
    bias = bias_ref[...]
    row_head = lax.broadcasted_iota(jnp.int32, (hq, ncol), 0) // tq
    col = lax.broadcasted_iota(jnp.int32, (hq, ncol), 1)
    own = (col % SB_HEADS) == row_head

    def rev_cumsum(la):
        hi, lo = _split2(la)
        return jnp.dot(jnp.concatenate([hi, lo], axis=1), uu_ref[...], preferred_element_type=F32)

    @pl.when(p == 0)
    def _():
        z = _dot_nt(q, kc_ref[0]) * scale + bias
        t = lax.broadcasted_iota(jnp.int32, (hq, ncol), 0) % tq
        valid = own & ((col // SB_HEADS) < t)
        ls = _log_sigmoid(z)
        la = jnp.where(valid, ls - z, 0.0)
        att = jnp.where(valid, jnp.exp(ls + rev_cumsum(la)), 0.0)
        o_ref[0] = _dot(att, vc_ref[0])
        c_ref[...] = jnp.sum(la, axis=1, keepdims=True)

    z = jnp.concatenate([_dot_nt(q, k_refs[j][...]) for j in range(pp)], axis=0)
    z = z * scale + jnp.concatenate([bias] * pp, axis=0)
    valid = jnp.concatenate([own] * pp, axis=0)
    ls = _log_sigmoid(z)
    la = jnp.where(valid, ls - z, 0.0)
    surv = rev_cumsum(la)
    tot = jnp.sum(la, axis=1, keepdims=True)
    cur = c_ref[...]
    cs = []
    for j in range(pp):
        cs.append(cur)
        cur = cur + tot[j * hq:(j + 1) * hq]
    c_ref[...] = cur
    att = jnp.where(valid, jnp.exp(ls + surv + jnp.concatenate(cs, axis=0)), 0.0)
    acc = o_ref[0]
    for j in range(pp):
        acc = acc + _dot(att[j * hq:(j + 1) * hq], v_refs[j][...])
    o_ref[0] = acc


def _sb_sample(q_rows, k_cur, v_cur, bias_rows, cache_k, cache_v, page_table, layer, *, pp=16):
    b, hq, _ = q_rows.shape
    n_pages = page_table.shape[1]
    ncol = cache_k.shape[2]
    pp = math.gcd(pp, n_pages)
    tq = hq // SB_HEADS
    jj = lax.broadcasted_iota(jnp.int32, (2 * ncol, ncol), 0) % ncol
    ss = lax.broadcasted_iota(jnp.int32, (2 * ncol, ncol), 1)
    uu = jnp.where(jj > ss, 1.0, 0.0).astype(BF16)

    def page_spec(j):
        return pl.BlockSpec((None, None, ncol, SB_HEAD_DIM),
                            lambda i, p, pt: (layer, pt[i, n_pages - 1 - (p * pp + j)], 0, 0))

    grid_spec = pltpu.PrefetchScalarGridSpec(
        num_scalar_prefetch=1,
        grid=(b, n_pages // pp),
        in_specs=[
            pl.BlockSpec((1, hq, SB_HEAD_DIM), lambda i, p, pt: (i, 0, 0)),
            pl.BlockSpec((1, ncol, SB_HEAD_DIM), lambda i, p, pt: (i, 0, 0)),
            pl.BlockSpec((1, ncol, SB_HEAD_DIM), lambda i, p, pt: (i, 0, 0)),
            pl.BlockSpec((hq, ncol), lambda i, p, pt: (0, 0)),
            pl.BlockSpec((2 * ncol, ncol), lambda i, p, pt: (0, 0)),
        ] + [page_spec(j) for j in range(pp)] + [page_spec(j) for j in range(pp)],
        out_specs=pl.BlockSpec((1, hq, SB_HEAD_DIM), lambda i, p, pt: (i, 0, 0)),
        scratch_shapes=[pltpu.VMEM((hq, 1), F32)],
    )
    return pl.pallas_call(
        functools.partial(_sbs_kernel, pp=pp, tq=tq, scale=SB_HEAD_DIM ** -0.5),
        grid_spec=grid_spec,
        out_shape=jax.ShapeDtypeStruct((b, hq, SB_HEAD_DIM), F32),
        compiler_params=_cparams(("parallel", "arbitrary")),
        name="sb_sample",
    )(page_table, q_rows, k_cur, v_cur, bias_rows, uu, *([cache_k] * pp), *([cache_v] * pp))


def _mixout_kernel(op_ref, od_ref, os_ref, oa_ref, gt_ref, wb_ref, wo_ref, x_ref, g_ref, o_ref):
    gate = lambda k: gt_ref[:, k * D_MODEL:(k + 1) * D_MODEL].astype(F32)
    br = lambda o_ref_, r0, n: _dot(o_ref_[...], wb_ref[r0:r0 + n, :])
    acc = gate(0) * br(op_ref, 0, POOL_WIDTH)
    acc = acc + gate(1) * br(od_ref, POOL_WIDTH, DN_VW)
    acc = acc + gate(2) * br(os_ref, POOL_WIDTH + DN_VW, SB_WIDTH)
    acc = acc + gate(3) * br(oa_ref, POOL_WIDTH + DN_VW + SB_WIDTH, SSM_D_INNER)
    mix = jnp.dot(acc.astype(BF16), wo_ref[...], preferred_element_type=F32)
    o_ref[...] = x_ref[...] + _rms_rows(mix, g_ref[...])


def _mixout(o_pool, o_dn, o_sb, o_ss, gates, w_br_all, w_out_all, layer, x2, g, *, tm):
    m = x2.shape[0]
    rows = lambda w: pl.BlockSpec((tm, w), lambda i: (i, 0))
    resident = lambda r: pl.BlockSpec((None, r, D_MODEL), lambda i: (layer, 0, 0), pipeline_mode=pl.Buffered(1))
    return pl.pallas_call(
        _mixout_kernel,
        grid=(m // tm,),
        in_specs=[rows(POOL_WIDTH), rows(DN_VW), rows(SB_WIDTH), rows(SSM_D_INNER), rows(N_BRANCH * D_MODEL),
                  resident(w_br_all.shape[1]), resident(D_MODEL), rows(D_MODEL),
                  pl.BlockSpec((1, D_MODEL), lambda i: (0, 0))],
        out_specs=rows(D_MODEL),
        out_shape=jax.ShapeDtypeStruct((m, D_MODEL), F32),
        compiler_params=_cparams(("parallel",)),
        name="mixout",
    )(o_pool, o_dn, o_sb, o_ss, gates, w_br_all, w_out_all, x2, g)


def _mlp_kernel(x_ref, g1_ref, wu_ref, wd_ref, g2_ref, o_ref, h_ref, acc_ref, *, tm):
    f = pl.program_id(1)

    @pl.when(f == 0)
    def _():
        _norm_rows_to(h_ref, x_ref, g1_ref, tm)
        acc_ref[...] = jnp.zeros(acc_ref.shape, F32)

    a = jnp.dot(h_ref[...], wu_ref[...], preferred_element_type=F32)
    a = jnp.square(jnp.maximum(a, 0.0)).astype(BF16)
    acc_ref[...] += jnp.dot(a, wd_ref[...], preferred_element_type=F32)

    @pl.when(f == pl.num_programs(1) - 1)
    def _():
        o_ref[...] = x_ref[...] + _rms_rows(acc_ref[...], g2_ref[...])


def _mlp(x2, g1, wu_all, wd_all, layer, g2, *, tm, tf=1024):
    m = x2.shape[0]
    return pl.pallas_call(
        functools.partial(_mlp_kernel, tm=tm),
        grid=(m // tm, D_FF // tf),
        in_specs=[
            pl.BlockSpec((tm, D_MODEL), lambda i, f: (i, 0)),
            pl.BlockSpec((1, D_MODEL), lambda i, f: (0, 0)),
            pl.BlockSpec((None, D_MODEL, tf), lambda i, f: (layer, 0, f)),
            pl.BlockSpec((None, tf, D_MODEL), lambda i, f: (layer, f, 0)),
            pl.BlockSpec((1, D_MODEL), lambda i, f: (0, 0)),
        ],
        out_specs=pl.BlockSpec((tm, D_MODEL), lambda i, f: (i, 0)),
        out_shape=jax.ShapeDtypeStruct((m, D_MODEL), F32),
        scratch_shapes=[pltpu.VMEM((tm, D_MODEL), BF16), pltpu.VMEM((tm, D_MODEL), F32)],
        compiler_params=_cparams(("parallel", "arbitrary")),
        name="mlp",
    )(x2, g1, wu_all, wd_all, g2)


def _lane_rows(vals, offset):
    out = jnp.zeros((vals.shape[0], LANES), F32)
    for k in range(3):
        o = offset + k * SM_REP
        out = out.at[:, o:o + vals.shape[1]].set(vals.astype(F32))
    return out


def _prepare_params(p):
    w_in = p["w_in"]
    depth = w_in.shape[0]
    small = w_in[:, :, _O_DNB:_O_SB], w_in[:, :, _O_SSDT:_O_GATE]
    gap = jnp.zeros((depth, D_MODEL, SM_REP - SM_GROUP), w_in.dtype)
    w_u = jnp.concatenate(
        [w_in[:, :, :_O_DNB], w_in[:, :, _O_SB:_O_SSDT], *small, gap, *small, gap, *small,
         jnp.zeros((depth, D_MODEL, NU - C_SMALL - 2 * SM_REP - SM_GROUP), w_in.dtype)], axis=2).astype(BF16)
    w_g = w_in[:, :, _O_GATE:_O_END].astype(BF16)
    zrow = jnp.zeros((depth, LANES), F32)
    par = lambda a_log, dt_bias, off: jnp.stack(
        [_lane_rows(a_log, off), _lane_rows(dt_bias, off)] + [zrow] * (SUBLANES - 2), axis=1)
    return dict(
        w_u=w_u, w_g=w_g,
        n_mix_pre=p["norm_mix_pre"], n_mix_post=p["norm_mix_post"],
        n_mlp_pre=p["norm_mlp_pre"], n_mlp_post=p["norm_mlp_post"],
        pool_w=p["pool_w"].astype(BF16), pool_scale=p["pool_scale"],
        dn_conv_w=p["dn_conv_w"], dn_par=par(p["dn_a_log"], p["dn_dt_bias"], SM_DECAY), dn_norm_w=p["dn_norm_w"],
        sb_bias=p["sb_bias"],
        ssm_conv_w=p["ssm_conv_w"], ssm_conv_b=p["ssm_conv_b"],
        ss_par=par(p["ssm_a_log"], p["ssm_dt_bias"], SM_DT), ssm_norm_w=p["ssm_norm_w"],
        ssm_d=jnp.repeat(p["ssm_d"].astype(F32), SSM_HEAD_DIM, axis=1),
        w_branch=p["w_branch"].astype(BF16), w_out=p["w_out"].astype(BF16),
        w_up=p["w_up"].astype(BF16), w_down=p["w_down"].astype(BF16),
    )


def _trunk_layer(x, states, pw, layer, *, past=None):
    b, l, _ = x.shape
    m = b * l
    pool_buf, dn_conv, dn_s, ssm_conv, ssm_h = states
    prompt = past is None
    tm_big = min(m, 1024)
    tm = min(m, 512)
    lc = min(l, 256)
    lp = max(lc, 2 * DN_CHUNK)
    row = lambda name: pw[name][layer].reshape(1, -1)

    x2 = x.reshape(m, D_MODEL)
    u2 = _inproj(x2, row("n_mix_pre"), pw["w_u"], layer, tm=tm_big)
    gates = _inproj(x2, row("n_mix_pre"), pw["w_g"], layer, tm=tm_big, gate=True)
    u3 = u2.reshape(b, l, NU)

    pos0 = 0 if prompt else past[2].shape[1] * (past[0].shape[2] // SB_HEADS)
    o_pool, pool_new = _pool(u3, pool_buf, pw["pool_w"][layer], row("pool_scale"), pos0=pos0)
    o_dn, dn_conv_new, dn_s_new = _deltanet(u3, dn_conv, dn_s, pw["dn_conv_w"][layer], pw["dn_par"][layer],
                                            row("dn_norm_w"), lc=lc, lp=lp)
    o_ss, ss_conv_new, ss_h_new = _ssd(u3, ssm_conv, ssm_h, pw["ssm_conv_w"][layer], row("ssm_conv_b"),
                                       pw["ss_par"][layer], row("ssm_norm_w"), row("ssm_d"), lc=lc, lp=lp)

    k_new = u3[:, :, C_SBK:C_SBK + SB_WIDTH]
    v_new = u3[:, :, C_SBV:C_SBV + SB_WIDTH]
    sb_bias = pw["sb_bias"][layer]
    if prompt:
        o_sb = _sb_prompt(u3, sb_bias)
    else:
        cache_k, cache_v, page_table = past
        ncol = cache_k.shape[2]
        q = u3[:, :, C_SBQ:C_SBQ + SB_WIDTH].reshape(b, l, SB_HEADS, SB_HEAD_DIM)
        q_rows = jnp.transpose(q, (0, 2, 1, 3)).reshape(b, SB_HEADS * l, SB_HEAD_DIM)
        k_cur = jnp.pad(k_new.reshape(b, l * SB_HEADS, SB_HEAD_DIM), ((0, 0), (0, ncol - l * SB_HEADS), (0, 0)))
        v_cur = jnp.pad(v_new.reshape(b, l * SB_HEADS, SB_HEAD_DIM), ((0, 0), (0, ncol - l * SB_HEADS), (0, 0)))
        bias_rows = jnp.broadcast_to(jnp.repeat(sb_bias, l)[:, None], (SB_HEADS * l, ncol)).astype(F32)
        acc = _sb_sample(q_rows, k_cur, v_cur, bias_rows, cache_k, cache_v, page_table, layer)
        o_sb = jnp.transpose(acc.reshape(b, SB_HEADS, l, SB_HEAD_DIM), (0, 2, 1, 3)).reshape(b, l, SB_WIDTH)

    x2 = _mixout(o_pool.reshape(m, -1), o_dn.reshape(m, -1), o_sb.reshape(m, -1), o_ss.reshape(m, -1),
                 gates, pw["w_branch"], pw["w_out"], layer, x2, row("n_mix_post"), tm=min(m, 256))
    x2 = _mlp(x2, row("n_mlp_pre"), pw["w_up"], pw["w_down"], layer, row("n_mlp_post"), tm=tm)
    new_states = (k_new.reshape(b, l, SB_HEADS, SB_HEAD_DIM), v_new.reshape(b, l, SB_HEADS, SB_HEAD_DIM),
                  pool_new, dn_conv_new, dn_s_new, ss_conv_new, ss_h_new)
    return x2.reshape(b, l, D_MODEL), new_states


def kernel(x_prompt, x_sample, cache_sb_k, cache_sb_v, state_pool, state_dn_conv, state_dn_s, state_ssm_conv, state_ssm_h, page_table, norm_mix_pre, norm_mix_post, norm_mlp_pre, norm_mlp_post, w_in, pool_w, pool_scale, dn_conv_w, dn_a_log, dn_dt_bias, dn_norm_w, sb_bias, ssm_conv_w, ssm_conv_b, ssm_a_log, ssm_dt_bias, ssm_d, ssm_norm_w, w_branch, w_out, w_up, w_down):
    pw = _prepare_params(dict(
        norm_mix_pre=norm_mix_pre, norm_mix_post=norm_mix_post, norm_mlp_pre=norm_mlp_pre,
        norm_mlp_post=norm_mlp_post, w_in=w_in, pool_w=pool_w, pool_scale=pool_scale,
        dn_conv_w=dn_conv_w, dn_a_log=dn_a_log, dn_dt_bias=dn_dt_bias, dn_norm_w=dn_norm_w,
        sb_bias=sb_bias, ssm_conv_w=ssm_conv_w, ssm_conv_b=ssm_conv_b, ssm_a_log=ssm_a_log,
        ssm_dt_bias=ssm_dt_bias, ssm_d=ssm_d, ssm_norm_w=ssm_norm_w, w_branch=w_branch,
        w_out=w_out, w_up=w_up, w_down=w_down))
    depth = w_in.shape[0]
    bp = x_prompt.shape[0]
    dt_ = x_prompt.dtype
    zero_states = (jnp.zeros((bp, POOL_BUF, POOL_WIDTH), dt_),
                   jnp.zeros((bp, CONV_WIDTH - 1, DN_CONV_DIM), dt_),
                   jnp.zeros((bp, DN_HEADS, DN_DK, DN_DV), dt_),
                   jnp.zeros((bp, CONV_WIDTH - 1, SSM_CONV_DIM), dt_),
                   jnp.zeros((bp, SSM_HEADS, SSM_HEAD_DIM, SSM_STATE), dt_))
    n_pool, page = cache_sb_k.shape[1], cache_sb_k.shape[2]
    cache_k = cache_sb_k.reshape(depth, n_pool, page * SB_HEADS, SB_HEAD_DIM)
    cache_v = cache_sb_v.reshape(depth, n_pool, page * SB_HEADS, SB_HEAD_DIM)
    y_prompt, y_sample = x_prompt, x_sample
    new_p, new_s = [], []
    for layer in range(depth):
        y_prompt, st_p = _trunk_layer(y_prompt, zero_states, pw, layer)
        sample_states = (state_pool[layer], state_dn_conv[layer], state_dn_s[layer],
                         state_ssm_conv[layer], state_ssm_h[layer])
        y_sample, st_s = _trunk_layer(y_sample, sample_states, pw, layer, past=(cache_k, cache_v, page_table))
        new_p.append(st_p)
        new_s.append(st_s)
    outs_p = [jnp.stack(t) for t in zip(*new_p)]
    outs_s = [jnp.stack(t) for t in zip(*new_s)]
    return (y_prompt, y_sample, *outs_p, *outs_s)
```

```python
import functools
import math

import jax
import jax.numpy as jnp
from jax import lax
from jax.experimental import pallas as pl
from jax.experimental.pallas import tpu as pltpu

F32 = jnp.float32
BF16 = jnp.bfloat16

D_MODEL = 2048
N_BRANCH = 4
RMS_EPS = 1e-6
CONV_WIDTH = 4
POOL_WINDOWS = (2, 4, 8, 16)
POOL_WIDTH = D_MODEL // 4
POOL_GW = POOL_WIDTH // 4
POOL_BUF = max(POOL_WINDOWS) - 1
DN_HEADS = 4
DN_DK = 128
DN_DV = 128
DN_QK = DN_HEADS * DN_DK
DN_VW = DN_HEADS * DN_DV
DN_CONV_DIM = 2 * DN_QK + DN_VW
DN_CHUNK = 64
SB_HEADS = 4
SB_HEAD_DIM = 128
SB_WIDTH = SB_HEADS * SB_HEAD_DIM
SSM_D_INNER = D_MODEL // 2
SSM_HEAD_DIM = 64
SSM_HEADS = SSM_D_INNER // SSM_HEAD_DIM
SSM_GROUPS = 2
SSM_HPG = SSM_HEADS // SSM_GROUPS
SSM_STATE = 128
SSM_CONV_DIM = SSM_D_INNER + 2 * SSM_GROUPS * SSM_STATE
SSM_CHUNK = 64
D_FF = 4 * D_MODEL

LANES = 128
SUBLANES = 8

C_POOL = 0
C_DNQ, C_DNK, C_DNV, C_DNZ = 512, 1024, 1536, 2048
C_SBQ, C_SBK, C_SBV = 2560, 3072, 3584
C_SSZ, C_SSX, C_SSBC = 4096, 5120, 6144
C_SMALL = 6656
NU = 7168
_O_DNB = POOL_WIDTH + DN_CONV_DIM + DN_VW
_O_SB = _O_DNB + 2 * DN_HEADS
_O_SSDT = _O_SB + 3 * SB_WIDTH + SSM_D_INNER + SSM_CONV_DIM
_O_GATE = _O_SSDT + SSM_HEADS
_O_END = _O_GATE + N_BRANCH * D_MODEL
SM_BETA, SM_DECAY, SM_DT = 0, DN_HEADS, 2 * DN_HEADS
SM_GROUP = 2 * DN_HEADS + SSM_HEADS
SM_REP = 32

VMEM_LIMIT_MB = 56


def _cparams(sem, vmem_mb=VMEM_LIMIT_MB):
    return pltpu.CompilerParams(dimension_semantics=sem, vmem_limit_bytes=vmem_mb * 1024 * 1024)


def _sigmoid(x):
    return 1.0 / (1.0 + jnp.exp(-x))


def _silu(x):
    hx = 0.5 * x
    return hx + hx * jnp.tanh(hx)


def _softplus(x):
    return jnp.maximum(x, 0.0) + jnp.log1p(jnp.exp(-jnp.abs(x)))


def _log_sigmoid(x):
    return jnp.minimum(x, 0.0) - jnp.log1p(jnp.exp(-jnp.abs(x)))


def _dot(a, b):
    return jnp.dot(a.astype(BF16), b.astype(BF16), preferred_element_type=F32)


def _dot_nt(a, b):
    return lax.dot_general(a.astype(BF16), b.astype(BF16), (((1,), (1,)), ((), ())), preferred_element_type=F32)


def _dot_tn(a, b):
    return lax.dot_general(a.astype(BF16), b.astype(BF16), (((0,), (0,)), ((), ())), preferred_element_type=F32)


def _split2(a):
    hi = a.astype(BF16)
    lo = (a - hi.astype(F32)).astype(BF16)
    return hi, lo


def _expand_heads(src, n_heads, width, lane0):
    k = lax.broadcasted_iota(jnp.int32, (LANES, n_heads * width), 0)
    n = lax.broadcasted_iota(jnp.int32, (LANES, n_heads * width), 1)
    sel = jnp.where((k % SM_REP - lane0 == n // width) & (k < 3 * SM_REP), 1.0, 0.0).astype(BF16)
    lane = lax.broadcasted_iota(jnp.int32, src.shape, 1)
    hi = src.astype(BF16).astype(F32)
    r1 = src - hi
    mid = r1.astype(BF16).astype(F32)
    lo = r1 - mid
    pieces = jnp.where(lane < SM_REP, hi, jnp.where(lane < 2 * SM_REP, mid, lo))
    return jnp.dot(pieces.astype(BF16), sel, preferred_element_type=F32)


def _dot3(a, b):
    ah, al = _split2(a)
    bh, bl = _split2(b)
    d = lambda x, y: jnp.dot(x, y, preferred_element_type=F32)
    return d(ah, bh) + (d(ah, bl) + d(al, bh))


def _rms_rows(x, g):
    return x * lax.rsqrt(jnp.mean(x * x, axis=-1, keepdims=True) + RMS_EPS) * g


def _norm_rows_to(h_ref, x_ref, g_ref, tm):
    ch = min(tm, 256)

    def body(r, c):
        rs = pl.ds(pl.multiple_of(r * ch, ch), ch)
        h_ref[rs, :] = _rms_rows(x_ref[rs, :], g_ref[...]).astype(h_ref.dtype)
        return c

    lax.fori_loop(0, tm // ch, body, 0)


def _inproj_kernel(x_ref, g_ref, w_ref, o_ref, h_ref, *, tm, gate):
    @pl.when(pl.program_id(1) == 0)
    def _():
        _norm_rows_to(h_ref, x_ref, g_ref, tm)

    acc = jnp.dot(h_ref[...], w_ref[...], preferred_element_type=F32)
    o_ref[...] = (_sigmoid(acc) if gate else acc).astype(o_ref.dtype)


def _inproj(x2, g, w_all, layer, *, tm, tn=1024, gate=False):
    m = x2.shape[0]
    n = w_all.shape[2]
    return pl.pallas_call(
        functools.partial(_inproj_kernel, tm=tm, gate=gate),
        grid=(m // tm, n // tn),
        in_specs=[
            pl.BlockSpec((tm, D_MODEL), lambda i, j: (i, 0)),
            pl.BlockSpec((1, D_MODEL), lambda i, j: (0, 0)),
            pl.BlockSpec((None, D_MODEL, tn), lambda i, j: (layer, 0, j)),
        ],
        out_specs=pl.BlockSpec((tm, tn), lambda i, j: (i, j)),
        out_shape=jax.ShapeDtypeStruct((m, n), BF16 if gate else F32),
        scratch_shapes=[pltpu.VMEM((tm, D_MODEL), BF16)],
        compiler_params=_cparams(("parallel", "arbitrary")),
        name="inproj_gate" if gate else "inproj",
    )(x2, g, w_all)


def _pool_kernel(u_ref, buf_ref, w_ref, sc_ref, o_ref, new_ref, ext_ref, *, lin, lp, pos0):
    ext_ref[0:16, :] = jnp.zeros((16, POOL_WIDTH), F32)
    ext_ref[1:16, :] = buf_ref[0]
    if lin < lp:
        ext_ref[16:16 + lp, :] = jnp.zeros((lp, POOL_WIDTH), F32)
    ext_ref[16:16 + lin, :] = u_ref[0]
    ch = min(lp, 256)
    for c0 in range(0, lp, ch):
        pos = pos0 + c0 + lax.broadcasted_iota(jnp.int32, (ch, 1), 0)
        for gi, w in enumerate(POOL_WINDOWS):
            cols = slice(gi * POOL_GW, (gi + 1) * POOL_GW)
            s = ext_ref[16 + c0:16 + c0 + ch, cols]
            tot = s
            for k in range(1, w):
                tot = tot + ext_ref[16 + c0 - k:16 + c0 - k + ch, cols]
            cnt = jnp.minimum(pos + 1, w).astype(F32)
            y = tot / cnt - s
            yo = _dot(y, w_ref[gi]) * sc_ref[:, cols]
            n = min(ch, lin - c0)
            o_ref[0, c0:c0 + n, cols] = yo[:n]
    new_ref[0] = ext_ref[1 + lin:16 + lin, :]


def _pool(u3, buf, w, sc, *, pos0):
    b, lin, _ = u3.shape
    lp = max(lin, SUBLANES)
    return pl.pallas_call(
        functools.partial(_pool_kernel, lin=lin, lp=lp, pos0=pos0),
        grid=(b,),
        in_specs=[
            pl.BlockSpec((1, lin, POOL_WIDTH), lambda i: (i, 0, C_POOL // POOL_WIDTH)),
            pl.BlockSpec((1, POOL_BUF, POOL_WIDTH), lambda i: (i, 0, 0)),
            pl.BlockSpec((4, POOL_GW, POOL_GW), lambda i: (0, 0, 0)),
            pl.BlockSpec((1, POOL_WIDTH), lambda i: (0, 0)),
        ],
        out_specs=[
            pl.BlockSpec((1, lin, POOL_WIDTH), lambda i: (i, 0, 0)),
            pl.BlockSpec((1, POOL_BUF, POOL_WIDTH), lambda i: (i, 0, 0)),
        ],
        out_shape=[
            jax.ShapeDtypeStruct((b, lin, POOL_WIDTH), F32),
            jax.ShapeDtypeStruct((b, POOL_BUF, POOL_WIDTH), F32),
        ],
        scratch_shapes=[pltpu.VMEM((16 + lp, POOL_WIDTH), F32)],
        compiler_params=_cparams(("parallel",)),
        name="pool",
    )(u3, buf, w, sc)


ROWS = 64


def _fill_ext(ext_ref, cbuf_ref, parts, *, lin, lp):
    l = pl.program_id(1)

    @pl.when(l == 0)
    def _():
        ext_ref[5:8, :] = cbuf_ref[0]

    @pl.when(l > 0)
    def _():
        ext_ref[5:8, :] = ext_ref[5 + lin:8 + lin, :]

    if lin < lp:
        ext_ref[8:8 + lp, :] = jnp.zeros((lp, ext_ref.shape[1]), F32)
    for ref, c0, width in parts:
        ext_ref[8:8 + lin, c0:c0 + width] = ref[0]


def _conv_silu(ext_ref, r0, n, cols, w_ref, b_ref):
    acc = None
    for i in range(CONV_WIDTH):
        part = ext_ref[5 + r0 + i:5 + r0 + i + n, cols] * w_ref[i:i + 1, cols]
        acc = part if acc is None else acc + part
    if b_ref is not None:
        acc = acc + b_ref[:, cols]
    return _silu(acc)


def _pad_rows(dst_ref, src_ref, *, lin, lp):
    if lin < lp:
        dst_ref[...] = jnp.zeros(dst_ref.shape, F32)
    dst_ref[0:lin, :] = src_ref[0]


def _row_valid(r0, n, lin):
    if r0 + n <= lin:
        return None
    return (r0 + lax.broadcasted_iota(jnp.int32, (n, 1), 0)) < lin


def _chunk_cumsum(x, chunk):
    rin = lax.broadcasted_iota(jnp.int32, x.shape, 0) % chunk
    s = 1
    while s < chunk:
        x = x + jnp.where(rin >= s, pltpu.roll(x, s, 0), 0.0)
        s *= 2
    return x


def _diag_inv2(a0, a1):
    c = a0.shape[0]
    hb = c // 2
    ng = hb // SUBLANES
    sub = lax.broadcasted_iota(jnp.int32, (SUBLANES, 2 * c), 0)
    lane = lax.broadcasted_iota(jnp.int32, (SUBLANES, 2 * c), 1)
    base = (lane // hb) * hb
    lmod = lane - base
    a01 = jnp.concatenate([a0, a1], axis=1)
    odd = ((lax.broadcasted_iota(jnp.int32, (hb, 2 * c), 1) // hb) % 2) == 1
    packed = jnp.where(odd, a01[hb:], a01[:hb])
    racc = [jnp.zeros((SUBLANES, 2 * c), F32) for _ in range(ng)]
    tg = [jnp.zeros((SUBLANES, 2 * c), F32) for _ in range(ng)]
    for j in range(hb):
        gj, rj = divmod(j, SUBLANES)
        t_j = jnp.where(lmod[0:1] == j, 1.0, 0.0) - racc[gj][rj:rj + 1, :]
        tg[gj] = jnp.where(sub == rj, t_j, tg[gj])
        if j == hb - 1:
            break
        for g in range(gj, ng):
            col = jnp.take_along_axis(packed[g * SUBLANES:(g + 1) * SUBLANES], base + j, axis=1)
            racc[g] = racc[g] + col * t_j
    dinv = jnp.concatenate(tg, axis=0)
    bd = jnp.concatenate([jnp.where(odd, 0.0, dinv), jnp.where(odd, dinv, 0.0)], axis=0)
    return [bd[:, :c], bd[:, c:]]


def _dn_kernel(q_ref, k_ref, v_ref, z_ref, sm_ref, cbuf_ref, s0_ref, cw_ref, par_ref, nw_ref,
               o_ref, cnew_ref, snew_ref,
               ext_ref, qn_ref, kn_ref, kb_ref, zz_ref, smp_ref, be_ref, ge_ref, gt_ref, oo_ref, s_ref,
               uc_ref, wc_ref, qe_ref, kt_ref, at_ref, rhs_ref, *, lin, lp, chunk):
    l = pl.program_id(1)
    nch = lp // chunk

    @pl.when(l == 0)
    def _():
        s_ref[...] = s0_ref[0]

    _fill_ext(ext_ref, cbuf_ref, ((q_ref, 0, DN_QK), (k_ref, DN_QK, DN_QK), (v_ref, 2 * DN_QK, DN_VW)), lin=lin, lp=lp)
    _pad_rows(zz_ref, z_ref, lin=lin, lp=lp)
    _pad_rows(smp_ref, sm_ref, lin=lin, lp=lp)

    sm = smp_ref[...]
    beta = _sigmoid(sm)
    g = -jnp.exp(par_ref[0:1, :]) * _softplus(sm + par_ref[1:2, :])
    valid = _row_valid(0, lp, lin)
    if valid is not None:
        beta = jnp.where(valid, beta, 0.0)
        g = jnp.where(valid, g, 0.0)
    gcum = _chunk_cumsum(g, chunk)
    be_ref[...] = _expand_heads(beta, DN_HEADS, DN_DK, SM_BETA)
    ge_ref[...] = _expand_heads(gcum, DN_HEADS, DN_DK, SM_DECAY)
    gt = gcum.T
    for c in range(nch):
        gt_ref[c] = gt[:, c * chunk:(c + 1) * chunk]

    assert ROWS == chunk
    for r0 in range(0, lp, ROWS):
        valid = _row_valid(r0, ROWS, lin)
        rows = slice(r0, r0 + ROWS)
        for h in range(DN_HEADS):
            hs = slice(h * DN_DK, (h + 1) * DN_DK)
            conv = lambda part: _conv_silu(ext_ref, r0, ROWS, slice(part * DN_QK + h * DN_DK, part * DN_QK + (h + 1) * DN_DK), cw_ref, None)
            l2n = lambda y: y * lax.rsqrt(jnp.sum(y * y, axis=-1, keepdims=True) + 1e-6)
            q = l2n(conv(0)) * (DN_DK ** -0.5)
            k = l2n(conv(1))
            v = conv(2)
            if valid is not None:
                q, k, v = (jnp.where(valid, t, 0.0) for t in (q, k, v))
            bt = be_ref[rows, hs]
            ge = ge_ref[rows, hs]
            eg = jnp.exp(ge)
            kb = k * bt
            qn_ref[rows, hs] = q
            kn_ref[rows, hs] = k
            kb_ref[rows, hs] = kb
            qe_ref[rows, hs] = q * eg
            kt_ref[rows, hs] = k * jnp.exp(ge[ROWS - 1:ROWS, :] - ge)
            rhs_ref[h, rows, 0:DN_DV] = v * bt
            rhs_ref[h, rows, DN_DV:2 * DN_DV] = kb * eg

    rid = lax.broadcasted_iota(jnp.int32, (chunk, chunk), 0)
    cid = lax.broadcasted_iota(jnp.int32, (chunk, chunk), 1)
    strict_lower = cid < rid
    lower = cid <= rid
    off = (rid >= chunk // 2) & (cid < chunk // 2)
    heads = range(DN_HEADS)
    hsl = [slice(h * DN_DK, (h + 1) * DN_DK) for h in heads]

    def prep_body(c2, carry):
        items = [(i, h) for i in range(2) for h in heads]
        cidx = [2 * c2 + i for i in range(2)]
        rs = [pl.ds(pl.multiple_of(c * chunk, chunk), chunk) for c in cidx]
        kh = [kn_ref[rs[c], hsl[h]] for c, h in items]
        decay = [jnp.exp(jnp.where(lower, ge_ref[rs[c], h * DN_DK:h * DN_DK + chunk]
                                   - gt_ref[cidx[c], SM_DECAY + h:SM_DECAY + h + 1, :], -jnp.inf)) for c, h in items]
        amat = [_dot_nt(kb_ref[rs[c], hsl[h]], kh[n]) * jnp.where(strict_lower, decay[n], 0.0)
                for n, (c, h) in enumerate(items)]
        for n, (c, h) in enumerate(items):
            at_ref[cidx[c] * DN_HEADS + h] = _dot_nt(qn_ref[rs[c], hsl[h]], kh[n]) * decay[n]
        dinv = sum((_diag_inv2(amat[n], amat[n + 1]) for n in range(0, len(items), 2)), [])
        inner = [_dot3(jnp.where(off, a, 0.0), d) for a, d in zip(amat, dinv)]
        tmat = [d - _dot3(d, i) for d, i in zip(dinv, inner)]
        sol = [_dot3(t, rhs_ref[h, rs[c], :]) for t, (c, h) in zip(tmat, items)]
        for s, (c, h) in zip(sol, items):
            uc_ref[rs[c], hsl[h]] = s[:, :DN_DV]
            wc_ref[rs[c], hsl[h]] = s[:, DN_DV:]
        return carry

    assert nch % 2 == 0
    lax.fori_loop(0, nch // 2, prep_body, 0)

    def scan_body(c, carry):
        r0 = pl.multiple_of(c * chunk, chunk)
        rs = pl.ds(r0, chunk)
        glast = ge_ref[pl.ds(r0 + chunk - 1, 1), :]
        sh = [s_ref[h] for h in heads]
        ws = [_dot(wc_ref[rs, hsl[h]], sh[h]) for h in heads]
        qs = [_dot(qe_ref[rs, hsl[h]], sh[h]) for h in heads]
        v_new = [uc_ref[rs, hsl[h]] - ws[h] for h in heads]
        o2 = [_dot(at_ref[c * DN_HEADS + h], v_new[h]) for h in heads]
        kv = [_dot_tn(kt_ref[rs, hsl[h]], v_new[h]) for h in heads]
        for h in heads:
            s_ref[h] = sh[h] * jnp.exp(glast[:, hsl[h]]) + kv[h]
            oo_ref[rs, hsl[h]] = _rms_rows(qs[h] + o2[h], nw_ref[...]) * _silu(zz_ref[rs, hsl[h]])
        return carry

    lax.fori_loop(0, nch, scan_body, 0)
    o_ref[0] = oo_ref[0:lin, :]

    @pl.when(l == pl.num_programs(1) - 1)
    def _():
        cnew_ref[0] = ext_ref[5 + lin:8 + lin, :]
        snew_ref[0] = s_ref[...]


def _deltanet(u3, cbuf, s0, cw, par, nw, *, lc, lp):
    b, l, _ = u3.shape
    nl = l // lc
    blk = lambda c0: pl.BlockSpec((1, lc, 512), lambda i, j: (i, j, c0 // 512))
    return pl.pallas_call(
        functools.partial(_dn_kernel, lin=lc, lp=lp, chunk=DN_CHUNK),
        grid=(b, nl),
        in_specs=[
            blk(C_DNQ), blk(C_DNK), blk(C_DNV), blk(C_DNZ),
            pl.BlockSpec((1, lc, LANES), lambda i, j: (i, j, C_SMALL // LANES)),
            pl.BlockSpec((1, CONV_WIDTH - 1, DN_CONV_DIM), lambda i, j: (i, 0, 0)),
            pl.BlockSpec((1, DN_HEADS, DN_DK, DN_DV), lambda i, j: (i, 0, 0, 0)),
            pl.BlockSpec((CONV_WIDTH, DN_CONV_DIM), lambda i, j: (0, 0)),
            pl.BlockSpec((SUBLANES, LANES), lambda i, j: (0, 0)),
            pl.BlockSpec((1, DN_DV), lambda i, j: (0, 0)),
        ],
        out_specs=[
            pl.BlockSpec((1, lc, DN_VW), lambda i, j: (i, j, 0)),
            pl.BlockSpec((1, CONV_WIDTH - 1, DN_CONV_DIM), lambda i, j: (i, 0, 0)),
            pl.BlockSpec((1, DN_HEADS, DN_DK, DN_DV), lambda i, j: (i, 0, 0, 0)),
        ],
        out_shape=[
            jax.ShapeDtypeStruct((b, l, DN_VW), F32),
            jax.ShapeDtypeStruct((b, CONV_WIDTH - 1, DN_CONV_DIM), F32),
            jax.ShapeDtypeStruct((b, DN_HEADS, DN_DK, DN_DV), F32),
        ],
        scratch_shapes=[
            pltpu.VMEM((8 + lp, DN_CONV_DIM), F32),
            pltpu.VMEM((lp, DN_QK), F32), pltpu.VMEM((lp, DN_QK), F32), pltpu.VMEM((lp, DN_VW), F32),
            pltpu.VMEM((lp, DN_VW), F32), pltpu.VMEM((lp, LANES), F32),
            pltpu.VMEM((lp, DN_QK), F32), pltpu.VMEM((lp, DN_QK), F32),
            pltpu.VMEM((lp // DN_CHUNK, LANES, DN_CHUNK), F32),
            pltpu.VMEM((lp, DN_VW), F32),
            pltpu.VMEM((DN_HEADS, DN_DK, DN_DV), F32),
            pltpu.VMEM((lp, DN_VW), F32), pltpu.VMEM((lp, DN_QK), F32),
            pltpu.VMEM((lp, DN_QK), F32), pltpu.VMEM((lp, DN_QK), F32),
            pltpu.VMEM((lp // DN_CHUNK * DN_HEADS, DN_CHUNK, DN_CHUNK), F32),
            pltpu.VMEM((DN_HEADS, lp, 2 * DN_DV), F32),
        ],
        compiler_params=_cparams(("parallel", "arbitrary")),
        name="deltanet",
    )(u3, u3, u3, u3, u3, cbuf, s0, cw, par, nw)


def _ssd_kernel(z_ref, x_ref, bc_ref, sm_ref, cbuf_ref, h0_ref, cw_ref, cb_ref, par_ref, nw_ref, dsk_ref,
                o_ref, cnew_ref, hnew_ref,
                ext_ref, xs_ref, bm_ref, cm_ref, zz_ref, smp_ref, ce_ref, de_ref, ct_ref, yy_ref, ht_ref,
                *, lin, lp, chunk):
    assert chunk == SSM_HEAD_DIM
    l = pl.program_id(1)
    nch = lp // chunk
    gn = SSM_GROUPS * SSM_STATE

    @pl.when(l == 0)
    def _():
        for g in range(SSM_GROUPS):
            hg = h0_ref[0, g * SSM_HPG:(g + 1) * SSM_HPG]
            ht_ref[g] = hg.reshape(SSM_HPG * SSM_HEAD_DIM, SSM_STATE).T

    _fill_ext(ext_ref, cbuf_ref, ((x_ref, 0, SSM_D_INNER), (bc_ref, SSM_D_INNER, 2 * gn)), lin=lin, lp=lp)
    _pad_rows(zz_ref, z_ref, lin=lin, lp=lp)
    _pad_rows(smp_ref, sm_ref, lin=lin, lp=lp)

    for r0 in range(0, lp, ROWS):
        valid = _row_valid(r0, ROWS, lin)
        for c0 in range(0, SSM_CONV_DIM, LANES):
            y = _conv_silu(ext_ref, r0, ROWS, slice(c0, c0 + LANES), cw_ref, cb_ref)
            if valid is not None:
                y = jnp.where(valid, y, 0.0)
            if c0 < SSM_D_INNER:
                xs_ref[r0:r0 + ROWS, c0:c0 + LANES] = y
            elif c0 < SSM_D_INNER + gn:
                bm_ref[r0:r0 + ROWS, c0 - SSM_D_INNER:c0 - SSM_D_INNER + LANES] = y
            else:
                cm_ref[r0:r0 + ROWS, c0 - SSM_D_INNER - gn:c0 - SSM_D_INNER - gn + LANES] = y

    sm = smp_ref[...]
    dt = _softplus(sm + par_ref[1:2, :])
    valid = _row_valid(0, lp, lin)
    if valid is not None:
        dt = jnp.where(valid, dt, 0.0)
    cum = _chunk_cumsum(dt * (-jnp.exp(par_ref[0:1, :])), chunk)
    ct = cum.T
    for c in range(nch):
        ct_ref[c] = ct[:, c * chunk:(c + 1) * chunk]

    spread = _expand_heads(jnp.concatenate([cum, dt], axis=0), SSM_HEADS, SSM_HEAD_DIM, SM_DT)
    ce_ref[...] = spread[:lp]
    de_ref[...] = spread[lp:]

    pw = 2 * SSM_HEAD_DIM
    row2 = lax.broadcasted_iota(jnp.int32, (chunk, 2 * chunk), 0)
    lane2 = lax.broadcasted_iota(jnp.int32, (chunk, 2 * chunk), 1)
    lower2 = (lane2 % chunk) <= row2
    first = lax.broadcasted_iota(jnp.int32, (chunk, pw), 1) < SSM_HEAD_DIM
    groups = range(SSM_GROUPS)
    gw_ = SSM_HPG * SSM_HEAD_DIM

    def chunk_body(c, carry):
        r0 = pl.multiple_of(c * chunk, chunk)
        rs = pl.ds(r0, chunk)
        bg = [bm_ref[rs, g * SSM_STATE:(g + 1) * SSM_STATE] for g in groups]
        cg = [cm_ref[rs, g * SSM_STATE:(g + 1) * SSM_STATE] for g in groups]
        ce = [ce_ref[rs, g * gw_:(g + 1) * gw_] for g in groups]
        de = [de_ref[rs, g * gw_:(g + 1) * gw_] for g in groups]
        xg = [xs_ref[rs, g * gw_:(g + 1) * gw_] for g in groups]
        ht = [ht_ref[g] for g in groups]
        cb = [_dot_nt(cg[g], bg[g]) for g in groups]
        ys = [_dot(cg[g], ht[g]) * jnp.exp(ce[g]) for g in groups]
        for g in groups:
            last = ce[g][chunk - 1:chunk, :]
            ht_ref[g] = ht[g] * jnp.exp(last) + _dot_tn(bg[g], xg[g] * (jnp.exp(last - ce[g]) * de[g]))
        for g in groups:
            cb2 = jnp.concatenate([cb[g], cb[g]], axis=1)
            xd = xg[g] * de[g]
            for p in range(SSM_HPG // 2):
                ps = slice(p * pw, (p + 1) * pw)
                hd = g * SSM_HPG + 2 * p
                ctp = ct_ref[c, SM_DT + hd:SM_DT + hd + 2, :]
                crp = jnp.concatenate([ctp[0:1], ctp[1:2]], axis=1)
                lm = jnp.exp(jnp.where(lower2, ce[g][:, ps] - crp, -jnp.inf))
                xdp = xd[:, ps]
                bd = jnp.concatenate([jnp.where(first, xdp, 0.0), jnp.where(first, 0.0, xdp)], axis=0)
                cols = slice(g * gw_ + p * pw, g * gw_ + (p + 1) * pw)
                yy_ref[rs, cols] = _dot(cb2 * lm, bd) + ys[g][:, ps] + dsk_ref[:, cols] * xg[g][:, ps]
        return carry

    lax.fori_loop(0, nch, chunk_body, 0)

    gw = SSM_D_INNER // SSM_GROUPS
    for r0 in range(0, lp, ROWS):
        n = min(ROWS, lin - r0)
        if n <= 0:
            break
        for g in range(SSM_GROUPS):
            cols = slice(g * gw, (g + 1) * gw)
            t = yy_ref[r0:r0 + ROWS, cols] * _silu(zz_ref[r0:r0 + ROWS, cols])
            t = _rms_rows(t, nw_ref[:, cols])
            o_ref[0, r0:r0 + n, cols] = t[:n]

    @pl.when(l == pl.num_programs(1) - 1)
    def _():
        cnew_ref[0] = ext_ref[5 + lin:8 + lin, :]
        for g in range(SSM_GROUPS):
            hnew_ref[0, g * SSM_HPG:(g + 1) * SSM_HPG] = ht_ref[g].T.reshape(SSM_HPG, SSM_HEAD_DIM, SSM_STATE)


def _ssd(u3, cbuf, h0, cw, cb, par, nw, d, *, lc, lp):
    b, l, _ = u3.shape
    nl = l // lc
    gn2 = 2 * SSM_GROUPS * SSM_STATE
    return pl.pallas_call(
        functools.partial(_ssd_kernel, lin=lc, lp=lp, chunk=SSM_CHUNK),
        grid=(b, nl),
        in_specs=[
            pl.BlockSpec((1, lc, SSM_D_INNER), lambda i, j: (i, j, C_SSZ // SSM_D_INNER)),
            pl.BlockSpec((1, lc, SSM_D_INNER), lambda i, j: (i, j, C_SSX // SSM_D_INNER)),
            pl.BlockSpec((1, lc, gn2), lambda i, j: (i, j, C_SSBC // gn2)),
            pl.BlockSpec((1, lc, LANES), lambda i, j: (i, j, C_SMALL // LANES)),
            pl.BlockSpec((1, CONV_WIDTH - 1, SSM_CONV_DIM), lambda i, j: (i, 0, 0)),
            pl.BlockSpec((1, SSM_HEADS, SSM_HEAD_DIM, SSM_STATE), lambda i, j: (i, 0, 0, 0)),
            pl.BlockSpec((CONV_WIDTH, SSM_CONV_DIM), lambda i, j: (0, 0)),
            pl.BlockSpec((1, SSM_CONV_DIM), lambda i, j: (0, 0)),
            pl.BlockSpec((SUBLANES, LANES), lambda i, j: (0, 0)),
            pl.BlockSpec((1, SSM_D_INNER), lambda i, j: (0, 0)),
            pl.BlockSpec((1, SSM_D_INNER), lambda i, j: (0, 0)),
        ],
        out_specs=[
            pl.BlockSpec((1, lc, SSM_D_INNER), lambda i, j: (i, j, 0)),
            pl.BlockSpec((1, CONV_WIDTH - 1, SSM_CONV_DIM), lambda i, j: (i, 0, 0)),
            pl.BlockSpec((1, SSM_HEADS, SSM_HEAD_DIM, SSM_STATE), lambda i, j: (i, 0, 0, 0)),
        ],
        out_shape=[
            jax.ShapeDtypeStruct((b, l, SSM_D_INNER), F32),
            jax.ShapeDtypeStruct((b, CONV_WIDTH - 1, SSM_CONV_DIM), F32),
            jax.ShapeDtypeStruct((b, SSM_HEADS, SSM_HEAD_DIM, SSM_STATE), F32),
        ],
        scratch_shapes=[
            pltpu.VMEM((8 + lp, SSM_CONV_DIM), F32),
            pltpu.VMEM((lp, SSM_D_INNER), F32),
            pltpu.VMEM((lp, SSM_GROUPS * SSM_STATE), F32), pltpu.VMEM((lp, SSM_GROUPS * SSM_STATE), F32),
            pltpu.VMEM((lp, SSM_D_INNER), F32), pltpu.VMEM((lp, LANES), F32),
            pltpu.VMEM((lp, SSM_D_INNER), F32), pltpu.VMEM((lp, SSM_D_INNER), F32),
            pltpu.VMEM((lp // SSM_CHUNK, LANES, SSM_CHUNK), F32),
            pltpu.VMEM((lp, SSM_D_INNER), F32),
            pltpu.VMEM((SSM_GROUPS, SSM_STATE, SSM_HPG * SSM_HEAD_DIM), F32),
        ],
        compiler_params=_cparams(("parallel", "arbitrary")),
        name="ssd",
    )(u3, u3, u3, u3, cbuf, h0, cw, cb, par, nw, d)


def _strict_upper_stack(n):
    j = lax.broadcasted_iota(jnp.int32, (2 * n, n), 0) % n
    s = lax.broadcasted_iota(jnp.int32, (2 * n, n), 1)
    return jnp.where(j > s, 1.0, 0.0).astype(BF16)


def _rev_excl_cumsum(la, uu):
    hi, lo = _split2(la)
    return jnp.dot(jnp.concatenate([hi, lo], axis=1), uu, preferred_element_type=F32)


def _sbp_kernel(bias_ref, q_ref, k_ref, v_ref, o_ref, *, tq, scale):
    h = pl.program_id(1)
    qi = pl.program_id(2)
    bias = bias_ref[h]
    q = q_ref[0].astype(BF16)
    uu = _strict_upper_stack(tq)

    def sweep(blocks, carry, diagonal):
        c, acc = carry
        rows = [pl.ds(kj * tq if isinstance(kj, int) else pl.multiple_of(kj * tq, tq), tq) for kj in blocks]
        zs = [_dot_nt(q, k_ref[0, r, :]) * scale + bias for r in rows]
        lss, las = [], []
        for z in zs:
            l1 = jnp.log1p(jnp.exp(-jnp.abs(z)))
            lss.append(jnp.minimum(z, 0.0) - l1)
            las.append(-jnp.maximum(z, 0.0) - l1)
        if diagonal:
            valid = lax.broadcasted_iota(jnp.int32, (tq, tq), 1) < lax.broadcasted_iota(jnp.int32, (tq, tq), 0)
            las = [jnp.where(valid, la, 0.0) for la in las]
        survs = [_rev_excl_cumsum(la, uu) for la in las]
        for ls, la, surv, r in zip(lss, las, survs, rows):
            att = jnp.exp(ls + surv + c)
            if diagonal:
                att = jnp.where(valid, att, 0.0)
            acc = acc + _dot(att, v_ref[0, r, :])
            c = c + jnp.sum(la, axis=1, keepdims=True)
        return c, acc

    carry = (jnp.zeros((tq, 1), F32), jnp.zeros((tq, SB_HEAD_DIM), F32))
    carry = sweep([qi], carry, True)
    carry = lax.fori_loop(0, lax.shift_right_logical(qi, 1),
                          lambda t, cr: sweep([qi - 1 - 2 * t, qi - 2 - 2 * t], cr, False), carry)
    carry = lax.fori_loop(0, qi & 1, lambda t, cr: sweep([0], cr, False), carry)
    o_ref[0] = carry[1]


def _sb_prompt(u3, bias, *, tq=256):
    b, l, _ = u3.shape
    tq = min(tq, l)
    kv = lambda c0: pl.BlockSpec((1, l, SB_HEAD_DIM), lambda i, h, j: (i, 0, c0 // SB_HEAD_DIM + h))
    return pl.pallas_call(
        functools.partial(_sbp_kernel, tq=tq, scale=SB_HEAD_DIM ** -0.5),
        grid=(b, SB_HEADS, l // tq),
        in_specs=[
            pl.BlockSpec(memory_space=pltpu.SMEM),
            pl.BlockSpec((1, tq, SB_HEAD_DIM), lambda i, h, j: (i, j, C_SBQ // SB_HEAD_DIM + h)),
            kv(C_SBK), kv(C_SBV),
        ],
        out_specs=pl.BlockSpec((1, tq, SB_HEAD_DIM), lambda i, h, j: (i, j, h)),
        out_shape=jax.ShapeDtypeStruct((b, l, SB_WIDTH), F32),
        compiler_params=_cparams(("parallel", "parallel", "arbitrary")),
        name="sb_prompt",
    )(bias, u3, u3, u3)


def _kvrows_kernel(k_ref, v_ref, ko_ref, vo_ref, *, tl):
    for h in range(SB_HEADS):
        rows = pl.ds(h, tl, stride=SB_HEADS)
        ko_ref[0, rows, :] = k_ref[0, :, h * SB_HEAD_DIM:(h + 1) * SB_HEAD_DIM]
        vo_ref[0, rows, :] = v_ref[0, :, h * SB_HEAD_DIM:(h + 1) * SB_HEAD_DIM]


def _kv_rows(u3, *, tl=512):
    b, l, _ = u3.shape
    tl = min(tl, l)
    src = lambda c0: pl.BlockSpec((1, tl, SB_WIDTH), lambda i, j: (i, j, c0 // SB_WIDTH))
    dst = pl.BlockSpec((1, tl * SB_HEADS, SB_HEAD_DIM), lambda i, j: (i, j, 0))
    shape = jax.ShapeDtypeStruct((b, l * SB_HEADS, SB_HEAD_DIM), F32)
    return pl.pallas_call(
        functools.partial(_kvrows_kernel, tl=tl),
        grid=(b, l // tl),
        in_specs=[src(C_SBK), src(C_SBV)],
        out_specs=[dst, dst],
        out_shape=[shape, shape],
        compiler_params=_cparams(("parallel", "parallel")),
        name="kv_rows",
    )(u3, u3)


def _sbs_kernel(pt_ref, q_ref, kc_ref, vc_ref, bias_ref, uu_ref, *rest, pp, tq, scale):
    k_refs = rest[:pp]
    v_refs = rest[pp:2 * pp]
    o_ref = rest[2 * pp]
    c_ref = rest[2 * pp + 1]
    p = pl.program_id(1)
    hq = q_ref.shape[1]
    ncol = kc_ref.shape[1]
    q = q_ref[0].astype(BF16)
    bias = bias_ref[...]
    row_head = lax.broadcasted_iota(jnp.int32, (hq, ncol), 0) // tq
    col = lax.broadcasted_iota(jnp.int32, (hq, ncol), 1)
    own = (col % SB_HEADS) == row_head

    def rev_cumsum(la):
        hi, lo = _split2(la)
        return jnp.dot(jnp.concatenate([hi, lo], axis=1), uu_ref[...], preferred_element_type=F32)

    @pl.when(p == 0)
    def _():
        z = _dot_nt(q, kc_ref[0]) * scale + bias
        t = lax.broadcasted_iota(jnp.int32, (hq, ncol), 0) % tq
        valid = own & ((col // SB_HEADS) < t)
        ls = _log_sigmoid(z)
        la = jnp.where(valid, ls - z, 0.0)
        att = jnp.where(valid, jnp.exp(ls + rev_cumsum(la)), 0.0)
        o_ref[0] = _dot(att, vc_ref[0])
        c_ref[...] = jnp.sum(la, axis=1, keepdims=True)

    z = jnp.concatenate([_dot_nt(q, k_refs[j][...]) for j in range(pp)], axis=0)
    z = z * scale + jnp.concatenate([bias] * pp, axis=0)
    valid = jnp.concatenate([own] * pp, axis=0)
    ls = _log_sigmoid(z)
    la = jnp.where(valid, ls - z, 0.0)
    surv = rev_cumsum(la)
    tot = jnp.sum(la, axis=1, keepdims=True)
    cur = c_ref[...]
    cs = []
    for j in range(pp):
        cs.append(cur)
        cur = cur + tot[j * hq:(j + 1) * hq]
    c_ref[...] = cur
    att = jnp.where(valid, jnp.exp(ls + surv + jnp.concatenate(cs, axis=0)), 0.0)
    acc = o_ref[0]
    for j in range(pp):
        acc = acc + _dot(att[j * hq:(j + 1) * hq], v_refs[j][...])
    o_ref[0] = acc


def _sb_sample(q_rows, k_cur, v_cur, bias_rows, cache_k, cache_v, page_table, layer, *, pp=16):
    b, hq, _ = q_rows.shape
    n_pages = page_table.shape[1]
    ncol = cache_k.shape[2]
    pp = math.gcd(pp, n_pages)
    tq = hq // SB_HEADS
    jj = lax.broadcasted_iota(jnp.int32, (2 * ncol, ncol), 0) % ncol
    ss = lax.broadcasted_iota(jnp.int32, (2 * ncol, ncol), 1)
    uu = jnp.where(jj > ss, 1.0, 0.0).astype(BF16)

    def page_spec(j):
        return pl.BlockSpec((None, None, ncol, SB_HEAD_DIM),
                            lambda i, p, pt: (layer, pt[i, n_pages - 1 - (p * pp + j)], 0, 0))

    grid_spec = pltpu.PrefetchScalarGridSpec(
        num_scalar_prefetch=1,
        grid=(b, n_pages // pp),
        in_specs=[
            pl.BlockSpec((1, hq, SB_HEAD_DIM), lambda i, p, pt: (i, 0, 0)),
            pl.BlockSpec((1, ncol, SB_HEAD_DIM), lambda i, p, pt: (i, 0, 0)),
            pl.BlockSpec((1, ncol, SB_HEAD_DIM), lambda i, p, pt: (i, 0, 0)),
            pl.BlockSpec((hq, ncol), lambda i, p, pt: (0, 0)),
            pl.BlockSpec((2 * ncol, ncol), lambda i, p, pt: (0, 0)),
        ] + [page_spec(j) for j in range(pp)] + [page_spec(j) for j in range(pp)],
        out_specs=pl.BlockSpec((1, hq, SB_HEAD_DIM), lambda i, p, pt: (i, 0, 0)),
        scratch_shapes=[pltpu.VMEM((hq, 1), F32)],
    )
    return pl.pallas_call(
        functools.partial(_sbs_kernel, pp=pp, tq=tq, scale=SB_HEAD_DIM ** -0.5),
        grid_spec=grid_spec,
        out_shape=jax.ShapeDtypeStruct((b, hq, SB_HEAD_DIM), F32),
        compiler_params=_cparams(("parallel", "arbitrary")),
        name="sb_sample",
    )(page_table, q_rows, k_cur, v_cur, bias_rows, uu, *([cache_k] * pp), *([cache_v] * pp))


def _mixout_kernel(op_ref, od_ref, os_ref, oa_ref, gt_ref, wb_ref, wo_ref, x_ref, g_ref, o_ref):
    gate = lambda k: gt_ref[:, k * D_MODEL:(k + 1) * D_MODEL].astype(F32)
    br = lambda o_ref_, r0, n: _dot(o_ref_[...], wb_ref[r0:r0 + n, :])
    acc = gate(0) * br(op_ref, 0, POOL_WIDTH)
    acc = acc + gate(1) * br(od_ref, POOL_WIDTH, DN_VW)
    acc = acc + gate(2) * br(os_ref, POOL_WIDTH + DN_VW, SB_WIDTH)
    acc = acc + gate(3) * br(oa_ref, POOL_WIDTH + DN_VW + SB_WIDTH, SSM_D_INNER)
    mix = jnp.dot(acc.astype(BF16), wo_ref[...], preferred_element_type=F32)
    o_ref[...] = x_ref[...] + _rms_rows(mix, g_ref[...])


def _mixout(o_pool, o_dn, o_sb, o_ss, gates, w_br_all, w_out_all, layer, x2, g, *, tm):
    m = x2.shape[0]
    rows = lambda w: pl.BlockSpec((tm, w), lambda i: (i, 0))
    resident = lambda r: pl.BlockSpec((None, r, D_MODEL), lambda i: (layer, 0, 0), pipeline_mode=pl.Buffered(1))
    return pl.pallas_call(
        _mixout_kernel,
        grid=(m // tm,),
        in_specs=[rows(POOL_WIDTH), rows(DN_VW), rows(SB_WIDTH), rows(SSM_D_INNER), rows(N_BRANCH * D_MODEL),
                  resident(w_br_all.shape[1]), resident(D_MODEL), rows(D_MODEL),
                  pl.BlockSpec((1, D_MODEL), lambda i: (0, 0))],
        out_specs=rows(D_MODEL),
        out_shape=jax.ShapeDtypeStruct((m, D_MODEL), F32),
        compiler_params=_cparams(("parallel",)),
        name="mixout",
    )(o_pool, o_dn, o_sb, o_ss, gates, w_br_all, w_out_all, x2, g)


def _mlp_kernel(x_ref, g1_ref, wu_ref, wd_ref, g2_ref, o_ref, h_ref, acc_ref, *, tm):
    f = pl.program_id(1)

    @pl.when(f == 0)
    def _():
        _norm_rows_to(h_ref, x_ref, g1_ref, tm)
        acc_ref[...] = jnp.zeros(acc_ref.shape, F32)

    a = jnp.dot(h_ref[...], wu_ref[...], preferred_element_type=F32)
    a = jnp.square(jnp.maximum(a, 0.0)).astype(BF16)
    acc_ref[...] += jnp.dot(a, wd_ref[...], preferred_element_type=F32)

    @pl.when(f == pl.num_programs(1) - 1)
    def _():
        o_ref[...] = x_ref[...] + _rms_rows(acc_ref[...], g2_ref[...])


def _mlp(x2, g1, wu_all, wd_all, layer, g2, *, tm, tf=1024):
    m = x2.shape[0]
    return pl.pallas_call(
        functools.partial(_mlp_kernel, tm=tm),
        grid=(m // tm, D_FF // tf),
        in_specs=[
            pl.BlockSpec((tm, D_MODEL), lambda i, f: (i, 0)),
            pl.BlockSpec((1, D_MODEL), lambda i, f: (0, 0)),
            pl.BlockSpec((None, D_MODEL, tf), lambda i, f: (layer, 0, f)),
            pl.BlockSpec((None, tf, D_MODEL), lambda i, f: (layer, f, 0)),
            pl.BlockSpec((1, D_MODEL), lambda i, f: (0, 0)),
        ],
        out_specs=pl.BlockSpec((tm, D_MODEL), lambda i, f: (i, 0)),
        out_shape=jax.ShapeDtypeStruct((m, D_MODEL), F32),
        scratch_shapes=[pltpu.VMEM((tm, D_MODEL), BF16), pltpu.VMEM((tm, D_MODEL), F32)],
        compiler_params=_cparams(("parallel", "arbitrary")),
        name="mlp",
    )(x2, g1, wu_all, wd_all, g2)


def _lane_rows(vals, offset):
    out = jnp.zeros((vals.shape[0], LANES), F32)
    for k in range(3):
        o = offset + k * SM_REP
        out = out.at[:, o:o + vals.shape[1]].set(vals.astype(F32))
    return out


def _win_kernel(w_ref, wu_ref, wg_ref):
    def shifted(c0, width):
        a0 = c0 // LANES * LANES
        a1 = min(-(-(c0 + width) // LANES) * LANES, w_ref.shape[1])
        return w_ref[:, a0:a1][:, c0 - a0:c0 - a0 + width].astype(BF16)

    wu_ref[:, 0:_O_DNB] = w_ref[:, 0:_O_DNB].astype(BF16)
    wu_ref[:, _O_DNB:C_SMALL] = shifted(_O_SB, _O_SSDT - _O_SB)
    wg_ref[...] = shifted(_O_GATE, _O_END - _O_GATE)
    k = lax.broadcasted_iota(jnp.int32, (LANES, LANES), 0)
    n = lax.broadcasted_iota(jnp.int32, (LANES, LANES), 1)
    place = (n % SM_REP == k) & (n < 3 * SM_REP)
    p_dn = jnp.where(place & (k < SM_DT), 1.0, 0.0).astype(BF16)
    p_dt = jnp.where(place & (k >= SM_DT) & (k < SM_GROUP), 1.0, 0.0).astype(BF16)
    assert _O_DNB % LANES == 0 and _O_SSDT - C_SMALL == SM_DT
    small = (jnp.dot(w_ref[:, _O_DNB:_O_DNB + LANES].astype(BF16), p_dn, preferred_element_type=F32)
             + jnp.dot(w_ref[:, C_SMALL:C_SMALL + LANES].astype(BF16), p_dt, preferred_element_type=F32))
    wu_ref[:, C_SMALL:C_SMALL + LANES] = small.astype(BF16)
    wu_ref[:, C_SMALL + LANES:NU] = jnp.zeros((wu_ref.shape[0], NU - C_SMALL - LANES), BF16)


def _regroup_w_in(w_in, *, tr=64):
    depth, d, n = w_in.shape
    return pl.pallas_call(
        _win_kernel,
        grid=(depth, d // tr),
        in_specs=[pl.BlockSpec((None, tr, n), lambda l, i: (l, i, 0))],
        out_specs=[pl.BlockSpec((None, tr, NU), lambda l, i: (l, i, 0)),
                   pl.BlockSpec((None, tr, N_BRANCH * D_MODEL), lambda l, i: (l, i, 0))],
        out_shape=[jax.ShapeDtypeStruct((depth, d, NU), BF16),
                   jax.ShapeDtypeStruct((depth, d, N_BRANCH * D_MODEL), BF16)],
        compiler_params=_cparams(("parallel", "parallel")),
        name="regroup_w_in",
    )(w_in)


def _prepare_params(p):
    w_in = p["w_in"]
    depth = w_in.shape[0]
    w_u, w_g = _regroup_w_in(w_in)
    zrow = jnp.zeros((depth, LANES), F32)
    par = lambda a_log, dt_bias, off: jnp.stack(
        [_lane_rows(a_log, off), _lane_rows(dt_bias, off)] + [zrow] * (SUBLANES - 2), axis=1)
    return dict(
        w_u=w_u, w_g=w_g,
        n_mix_pre=p["norm_mix_pre"], n_mix_post=p["norm_mix_post"],
        n_mlp_pre=p["norm_mlp_pre"], n_mlp_post=p["norm_mlp_post"],
        pool_w=p["pool_w"].astype(BF16), pool_scale=p["pool_scale"],
        dn_conv_w=p["dn_conv_w"], dn_par=par(p["dn_a_log"], p["dn_dt_bias"], SM_DECAY), dn_norm_w=p["dn_norm_w"],
        sb_bias=p["sb_bias"],
        ssm_conv_w=p["ssm_conv_w"], ssm_conv_b=p["ssm_conv_b"],
        ss_par=par(p["ssm_a_log"], p["ssm_dt_bias"], SM_DT), ssm_norm_w=p["ssm_norm_w"],
        ssm_d=jnp.repeat(p["ssm_d"].astype(F32), SSM_HEAD_DIM, axis=1),
        w_branch=p["w_branch"].astype(BF16), w_out=p["w_out"].astype(BF16),
        w_up=p["w_up"].astype(BF16), w_down=p["w_down"].astype(BF16),
    )


def _trunk_layer(x, states, pw, layer, *, past=None):
    b, l, _ = x.shape
    m = b * l
    pool_buf, dn_conv, dn_s, ssm_conv, ssm_h = states
    prompt = past is None
    tm_big = min(m, 1024)
    tm = min(m, 512)
    lc = min(l, 256)
    lp = max(lc, 2 * DN_CHUNK)
    row = lambda name: pw[name][layer].reshape(1, -1)

    x2 = x.reshape(m, D_MODEL)
    u2 = _inproj(x2, row("n_mix_pre"), pw["w_u"], layer, tm=tm_big)
    gates = _inproj(x2, row("n_mix_pre"), pw["w_g"], layer, tm=tm_big, gate=True)
    u3 = u2.reshape(b, l, NU)

    pos0 = 0 if prompt else past[2].shape[1] * (past[0].shape[2] // SB_HEADS)
    o_pool, pool_new = _pool(u3, pool_buf, pw["pool_w"][layer], row("pool_scale"), pos0=pos0)
    o_dn, dn_conv_new, dn_s_new = _deltanet(u3, dn_conv, dn_s, pw["dn_conv_w"][layer], pw["dn_par"][layer],
                                            row("dn_norm_w"), lc=lc, lp=lp)
    o_ss, ss_conv_new, ss_h_new = _ssd(u3, ssm_conv, ssm_h, pw["ssm_conv_w"][layer], row("ssm_conv_b"),
                                       pw["ss_par"][layer], row("ssm_norm_w"), row("ssm_d"), lc=lc, lp=lp)

    k_rows, v_rows = _kv_rows(u3)
    sb_bias = pw["sb_bias"][layer]
    if prompt:
        o_sb = _sb_prompt(u3, sb_bias)
    else:
        cache_k, cache_v, page_table = past
        ncol = cache_k.shape[2]
        q = u3[:, :, C_SBQ:C_SBQ + SB_WIDTH].reshape(b, l, SB_HEADS, SB_HEAD_DIM)
        q_rows = jnp.transpose(q, (0, 2, 1, 3)).reshape(b, SB_HEADS * l, SB_HEAD_DIM)
        k_cur = jnp.pad(k_rows, ((0, 0), (0, ncol - l * SB_HEADS), (0, 0)))
        v_cur = jnp.pad(v_rows, ((0, 0), (0, ncol - l * SB_HEADS), (0, 0)))
        bias_rows = jnp.broadcast_to(jnp.repeat(sb_bias, l)[:, None], (SB_HEADS * l, ncol)).astype(F32)
        acc = _sb_sample(q_rows, k_cur, v_cur, bias_rows, cache_k, cache_v, page_table, layer)
        o_sb = jnp.transpose(acc.reshape(b, SB_HEADS, l, SB_HEAD_DIM), (0, 2, 1, 3)).reshape(b, l, SB_WIDTH)

    x2 = _mixout(o_pool.reshape(m, -1), o_dn.reshape(m, -1), o_sb.reshape(m, -1), o_ss.reshape(m, -1),
                 gates, pw["w_branch"], pw["w_out"], layer, x2, row("n_mix_post"), tm=min(m, 256))
    x2 = _mlp(x2, row("n_mlp_pre"), pw["w_up"], pw["w_down"], layer, row("n_mlp_post"), tm=tm)
    new_states = (k_rows.reshape(b, l, SB_HEADS, SB_HEAD_DIM), v_rows.reshape(b, l, SB_HEADS, SB_HEAD_DIM),
                  pool_new, dn_conv_new, dn_s_new, ss_conv_new, ss_h_new)
    return x2.reshape(b, l, D_MODEL), new_states


def kernel(x_prompt, x_sample, cache_sb_k, cache_sb_v, state_pool, state_dn_conv, state_dn_s, state_ssm_conv, state_ssm_h, page_table, norm_mix_pre, norm_mix_post, norm_mlp_pre, norm_mlp_post, w_in, pool_w, pool_scale, dn_conv_w, dn_a_log, dn_dt_bias, dn_norm_w, sb_bias, ssm_conv_w, ssm_conv_b, ssm_a_log, ssm_dt_bias, ssm_d, ssm_norm_w, w_branch, w_out, w_up, w_down):
    pw = _prepare_params(dict(
        norm_mix_pre=norm_mix_pre, norm_mix_post=norm_mix_post, norm_mlp_pre=norm_mlp_pre,
        norm_mlp_post=norm_mlp_post, w_in=w_in, pool_w=pool_w, pool_scale=pool_scale,
        dn_conv_w=dn_conv_w, dn_a_log=dn_a_log, dn_dt_bias=dn_dt_bias, dn_norm_w=dn_norm_w,
        sb_bias=sb_bias, ssm_conv_w=ssm_conv_w, ssm_conv_b=ssm_conv_b, ssm_a_log=ssm_a_log,
        ssm_dt_bias=ssm_dt_bias, ssm_d=ssm_d, ssm_norm_w=ssm_norm_w, w_branch=w_branch,
        w_out=w_out, w_up=w_up, w_down=w_down))
    depth = w_in.shape[0]
    bp = x_prompt.shape[0]
    dt_ = x_prompt.dtype
    zero_states = (jnp.zeros((bp, POOL_BUF, POOL_WIDTH), dt_),
                   jnp.zeros((bp, CONV_WIDTH - 1, DN_CONV_DIM), dt_),
                   jnp.zeros((bp, DN_HEADS, DN_DK, DN_DV), dt_),
                   jnp.zeros((bp, CONV_WIDTH - 1, SSM_CONV_DIM), dt_),
                   jnp.zeros((bp, SSM_HEADS, SSM_HEAD_DIM, SSM_STATE), dt_))
    n_pool, page = cache_sb_k.shape[1], cache_sb_k.shape[2]
    cache_k = cache_sb_k.reshape(depth, n_pool, page * SB_HEADS, SB_HEAD_DIM)
    cache_v = cache_sb_v.reshape(depth, n_pool, page * SB_HEADS, SB_HEAD_DIM)
    y_prompt, y_sample = x_prompt, x_sample
    new_p, new_s = [], []
    for layer in range(depth):
        y_prompt, st_p = _trunk_layer(y_prompt, zero_states, pw, layer)
        sample_states = (state_pool[layer], state_dn_conv[layer], state_dn_s[layer],
                         state_ssm_conv[layer], state_ssm_h[layer])
        y_sample, st_s = _trunk_layer(y_sample, sample_states, pw, layer, past=(cache_k, cache_v, page_table))
        new_p.append(st_p)
        new_s.append(st_s)
    outs_p = [jnp.stack(t) for t in zip(*new_p)]
    outs_s = [jnp.stack(t) for t in zip(*new_s)]
    return (y_prompt, y_sample, *outs_p, *outs_s)
```

```python
import functools
import math

import jax
import jax.numpy as jnp
from jax import lax
from jax.experimental import pallas as pl
from jax.experimental.pallas import tpu as pltpu

F32 = jnp.float32
BF16 = jnp.bfloat16

D_MODEL = 2048
N_BRANCH = 4
RMS_EPS = 1e-6
CONV_WIDTH = 4
POOL_WINDOWS = (2, 4, 8, 16)
POOL_WIDTH = D_MODEL // 4
POOL_GW = POOL_WIDTH // 4
POOL_BUF = max(POOL_WINDOWS) - 1
DN_HEADS = 4
DN_DK = 128
DN_DV = 128
DN_QK = DN_HEADS * DN_DK
DN_VW = DN_HEADS * DN_DV
DN_CONV_DIM = 2 * DN_QK + DN_VW
DN_CHUNK = 64
SB_HEADS = 4
SB_HEAD_DIM = 128
SB_WIDTH = SB_HEADS * SB_HEAD_DIM
SSM_D_INNER = D_MODEL // 2
SSM_HEAD_DIM = 64
SSM_HEADS = SSM_D_INNER // SSM_HEAD_DIM
SSM_GROUPS = 2
SSM_HPG = SSM_HEADS // SSM_GROUPS
SSM_STATE = 128
SSM_CONV_DIM = SSM_D_INNER + 2 * SSM_GROUPS * SSM_STATE
SSM_CHUNK = 64
D_FF = 4 * D_MODEL

LANES = 128
SUBLANES = 8

C_POOL = 0
C_DNQ, C_DNK, C_DNV, C_DNZ = 512, 1024, 1536, 2048
C_SBQ, C_SBK, C_SBV = 2560, 3072, 3584
C_SSZ, C_SSX, C_SSBC = 4096, 5120, 6144
C_SMALL = 6656
NU = 7168
_O_DNB = POOL_WIDTH + DN_CONV_DIM + DN_VW
_O_SB = _O_DNB + 2 * DN_HEADS
_O_SSDT = _O_SB + 3 * SB_WIDTH + SSM_D_INNER + SSM_CONV_DIM
_O_GATE = _O_SSDT + SSM_HEADS
_O_END = _O_GATE + N_BRANCH * D_MODEL
SM_BETA, SM_DECAY, SM_DT = 0, DN_HEADS, 2 * DN_HEADS
SM_GROUP = 2 * DN_HEADS + SSM_HEADS
SM_REP = 32

VMEM_LIMIT_MB = 56


def _cparams(sem, vmem_mb=VMEM_LIMIT_MB):
    return pltpu.CompilerParams(dimension_semantics=sem, vmem_limit_bytes=vmem_mb * 1024 * 1024)


def _sigmoid(x):
    return 1.0 / (1.0 + jnp.exp(-x))


def _silu(x):
    hx = 0.5 * x
    return hx + hx * jnp.tanh(hx)


def _softplus(x):
    return jnp.maximum(x, 0.0) + jnp.log1p(jnp.exp(-jnp.abs(x)))


def _log_sigmoid(x):
    return jnp.minimum(x, 0.0) - jnp.log1p(jnp.exp(-jnp.abs(x)))


def _dot(a, b):
    return jnp.dot(a.astype(BF16), b.astype(BF16), preferred_element_type=F32)


def _dot_nt(a, b):
    return lax.dot_general(a.astype(BF16), b.astype(BF16), (((1,), (1,)), ((), ())), preferred_element_type=F32)


def _dot_tn(a, b):
    return lax.dot_general(a.astype(BF16), b.astype(BF16), (((0,), (0,)), ((), ())), preferred_element_type=F32)


def _split2(a):
    hi = a.astype(BF16)
    lo = (a - hi.astype(F32)).astype(BF16)
    return hi, lo


def _expand_heads(src, n_heads, width, lane0):
    k = lax.broadcasted_iota(jnp.int32, (LANES, n_heads * width), 0)
    n = lax.broadcasted_iota(jnp.int32, (LANES, n_heads * width), 1)
    sel = jnp.where((k % SM_REP - lane0 == n // width) & (k < 3 * SM_REP), 1.0, 0.0).astype(BF16)
    lane = lax.broadcasted_iota(jnp.int32, src.shape, 1)
    hi = src.astype(BF16).astype(F32)
    r1 = src - hi
    mid = r1.astype(BF16).astype(F32)
    lo = r1 - mid
    pieces = jnp.where(lane < SM_REP, hi, jnp.where(lane < 2 * SM_REP, mid, lo))
    return jnp.dot(pieces.astype(BF16), sel, preferred_element_type=F32)


def _dot3(a, b):
    ah, al = _split2(a)
    bh, bl = _split2(b)
    d = lambda x, y: jnp.dot(x, y, preferred_element_type=F32)
    return d(ah, bh) + (d(ah, bl) + d(al, bh))


def _rms_rows(x, g):
    return x * lax.rsqrt(jnp.mean(x * x, axis=-1, keepdims=True) + RMS_EPS) * g


def _norm_rows_to(h_ref, x_ref, g_ref, tm):
    ch = min(tm, 256)

    def body(r, c):
        rs = pl.ds(pl.multiple_of(r * ch, ch), ch)
        h_ref[rs, :] = _rms_rows(x_ref[rs, :], g_ref[...]).astype(h_ref.dtype)
        return c

    lax.fori_loop(0, tm // ch, body, 0)


def _inproj_kernel(x_ref, g_ref, w_ref, o_ref, h_ref, *, tm, gate):
    @pl.when(pl.program_id(1) == 0)
    def _():
        _norm_rows_to(h_ref, x_ref, g_ref, tm)

    acc = _dot_nt(h_ref[...], w_ref[...])
    o_ref[...] = (_sigmoid(acc) if gate else acc).astype(o_ref.dtype)


def _inproj(x2, g, w_all, layer, *, tm, tn=1024, gate=False):
    m = x2.shape[0]
    n = w_all.shape[1]
    return pl.pallas_call(
        functools.partial(_inproj_kernel, tm=tm, gate=gate),
        grid=(m // tm, n // tn),
        in_specs=[
            pl.BlockSpec((tm, D_MODEL), lambda i, j: (i, 0)),
            pl.BlockSpec((1, D_MODEL), lambda i, j: (0, 0)),
            pl.BlockSpec((None, tn, D_MODEL), lambda i, j: (layer, j, 0)),
        ],
        out_specs=pl.BlockSpec((tm, tn), lambda i, j: (i, j)),
        out_shape=jax.ShapeDtypeStruct((m, n), BF16 if gate else F32),
        scratch_shapes=[pltpu.VMEM((tm, D_MODEL), BF16)],
        compiler_params=_cparams(("parallel", "arbitrary")),
        name="inproj_gate" if gate else "inproj",
    )(x2, g, w_all)


def _pool_kernel(u_ref, buf_ref, w_ref, sc_ref, o_ref, new_ref, ext_ref, *, lin, lp, pos0):
    ext_ref[0:16, :] = jnp.zeros((16, POOL_WIDTH), F32)
    ext_ref[1:16, :] = buf_ref[0]
    if lin < lp:
        ext_ref[16:16 + lp, :] = jnp.zeros((lp, POOL_WIDTH), F32)
    ext_ref[16:16 + lin, :] = u_ref[0]
    ch = min(lp, 256)
    for c0 in range(0, lp, ch):
        pos = pos0 + c0 + lax.broadcasted_iota(jnp.int32, (ch, 1), 0)
        for gi, w in enumerate(POOL_WINDOWS):
            cols = slice(gi * POOL_GW, (gi + 1) * POOL_GW)
            s = ext_ref[16 + c0:16 + c0 + ch, cols]
            tot = s
            for k in range(1, w):
                tot = tot + ext_ref[16 + c0 - k:16 + c0 - k + ch, cols]
            cnt = jnp.minimum(pos + 1, w).astype(F32)
            y = tot / cnt - s
            yo = _dot(y, w_ref[gi]) * sc_ref[:, cols]
            n = min(ch, lin - c0)
            o_ref[0, c0:c0 + n, cols] = yo[:n]
    new_ref[0] = ext_ref[1 + lin:16 + lin, :]


def _pool(u3, buf, w, sc, *, pos0):
    b, lin, _ = u3.shape
    lp = max(lin, SUBLANES)
    return pl.pallas_call(
        functools.partial(_pool_kernel, lin=lin, lp=lp, pos0=pos0),
        grid=(b,),
        in_specs=[
            pl.BlockSpec((1, lin, POOL_WIDTH), lambda i: (i, 0, C_POOL // POOL_WIDTH)),
            pl.BlockSpec((1, POOL_BUF, POOL_WIDTH), lambda i: (i, 0, 0)),
            pl.BlockSpec((4, POOL_GW, POOL_GW), lambda i: (0, 0, 0)),
            pl.BlockSpec((1, POOL_WIDTH), lambda i: (0, 0)),
        ],
        out_specs=[
            pl.BlockSpec((1, lin, POOL_WIDTH), lambda i: (i, 0, 0)),
            pl.BlockSpec((1, POOL_BUF, POOL_WIDTH), lambda i: (i, 0, 0)),
        ],
        out_shape=[
            jax.ShapeDtypeStruct((b, lin, POOL_WIDTH), F32),
            jax.ShapeDtypeStruct((b, POOL_BUF, POOL_WIDTH), F32),
        ],
        scratch_shapes=[pltpu.VMEM((16 + lp, POOL_WIDTH), F32)],
        compiler_params=_cparams(("parallel",)),
        name="pool",
    )(u3, buf, w, sc)


ROWS = 64


def _fill_ext(ext_ref, cbuf_ref, parts, *, lin, lp):
    l = pl.program_id(1)

    @pl.when(l == 0)
    def _():
        ext_ref[5:8, :] = cbuf_ref[0]

    @pl.when(l > 0)
    def _():
        ext_ref[5:8, :] = ext_ref[5 + lin:8 + lin, :]

    if lin < lp:
        ext_ref[8:8 + lp, :] = jnp.zeros((lp, ext_ref.shape[1]), F32)
    for ref, c0, width in parts:
        ext_ref[8:8 + lin, c0:c0 + width] = ref[0]


def _conv_silu(ext_ref, r0, n, cols, w_ref, b_ref):
    acc = None
    for i in range(CONV_WIDTH):
        part = ext_ref[5 + r0 + i:5 + r0 + i + n, cols] * w_ref[i:i + 1, cols]
        acc = part if acc is None else acc + part
    if b_ref is not None:
        acc = acc + b_ref[:, cols]
    return _silu(acc)


def _pad_rows(dst_ref, src_ref, *, lin, lp):
    if lin < lp:
        dst_ref[...] = jnp.zeros(dst_ref.shape, F32)
    dst_ref[0:lin, :] = src_ref[0]


def _row_valid(r0, n, lin):
    if r0 + n <= lin:
        return None
    return (r0 + lax.broadcasted_iota(jnp.int32, (n, 1), 0)) < lin


def _chunk_cumsum(x, chunk):
    rin = lax.broadcasted_iota(jnp.int32, x.shape, 0) % chunk
    s = 1
    while s < chunk:
        x = x + jnp.where(rin >= s, pltpu.roll(x, s, 0), 0.0)
        s *= 2
    return x


def _diag_inv2(a0, a1):
    c = a0.shape[0]
    hb = c // 2
    ng = hb // SUBLANES
    sub = lax.broadcasted_iota(jnp.int32, (SUBLANES, 2 * c), 0)
    lane = lax.broadcasted_iota(jnp.int32, (SUBLANES, 2 * c), 1)
    base = (lane // hb) * hb
    lmod = lane - base
    a01 = jnp.concatenate([a0, a1], axis=1)
    odd = ((lax.broadcasted_iota(jnp.int32, (hb, 2 * c), 1) // hb) % 2) == 1
    packed = jnp.where(odd, a01[hb:], a01[:hb])
    racc = [jnp.zeros((SUBLANES, 2 * c), F32) for _ in range(ng)]
    tg = [jnp.zeros((SUBLANES, 2 * c), F32) for _ in range(ng)]
    for j in range(hb):
        gj, rj = divmod(j, SUBLANES)
        t_j = jnp.where(lmod[0:1] == j, 1.0, 0.0) - racc[gj][rj:rj + 1, :]
        tg[gj] = jnp.where(sub == rj, t_j, tg[gj])
        if j == hb - 1:
            break
        for g in range(gj, ng):
            col = jnp.take_along_axis(packed[g * SUBLANES:(g + 1) * SUBLANES], base + j, axis=1)
            racc[g] = racc[g] + col * t_j
    dinv = jnp.concatenate(tg, axis=0)
    bd = jnp.concatenate([jnp.where(odd, 0.0, dinv), jnp.where(odd, dinv, 0.0)], axis=0)
    return [bd[:, :c], bd[:, c:]]


def _dn_kernel(q_ref, k_ref, v_ref, z_ref, sm_ref, cbuf_ref, s0_ref, cw_ref, par_ref, nw_ref,
               o_ref, cnew_ref, snew_ref,
               ext_ref, qn_ref, kn_ref, kb_ref, zz_ref, smp_ref, be_ref, ge_ref, gt_ref, oo_ref, s_ref,
               uc_ref, wc_ref, qe_ref, kt_ref, at_ref, rhs_ref, *, lin, lp, chunk):
    l = pl.program_id(1)
    nch = lp // chunk

    @pl.when(l == 0)
    def _():
        s_ref[...] = s0_ref[0]

    _fill_ext(ext_ref, cbuf_ref, ((q_ref, 0, DN_QK), (k_ref, DN_QK, DN_QK), (v_ref, 2 * DN_QK, DN_VW)), lin=lin, lp=lp)
    _pad_rows(zz_ref, z_ref, lin=lin, lp=lp)
    _pad_rows(smp_ref, sm_ref, lin=lin, lp=lp)

    sm = smp_ref[...]
    beta = _sigmoid(sm)
    g = -jnp.exp(par_ref[0:1, :]) * _softplus(sm + par_ref[1:2, :])
    valid = _row_valid(0, lp, lin)
    if valid is not None:
        beta = jnp.where(valid, beta, 0.0)
        g = jnp.where(valid, g, 0.0)
    gcum = _chunk_cumsum(g, chunk)
    be_ref[...] = _expand_heads(beta, DN_HEADS, DN_DK, SM_BETA)
    ge_ref[...] = _expand_heads(gcum, DN_HEADS, DN_DK, SM_DECAY)
    gt = gcum.T
    for c in range(nch):
        gt_ref[c] = gt[:, c * chunk:(c + 1) * chunk]

    assert ROWS == chunk
    for r0 in range(0, lp, ROWS):
        valid = _row_valid(r0, ROWS, lin)
        rows = slice(r0, r0 + ROWS)
        for h in range(DN_HEADS):
            hs = slice(h * DN_DK, (h + 1) * DN_DK)
            conv = lambda part: _conv_silu(ext_ref, r0, ROWS, slice(part * DN_QK + h * DN_DK, part * DN_QK + (h + 1) * DN_DK), cw_ref, None)
            l2n = lambda y: y * lax.rsqrt(jnp.sum(y * y, axis=-1, keepdims=True) + 1e-6)
            q = l2n(conv(0)) * (DN_DK ** -0.5)
            k = l2n(conv(1))
            v = conv(2)
            if valid is not None:
                q, k, v = (jnp.where(valid, t, 0.0) for t in (q, k, v))
            bt = be_ref[rows, hs]
            ge = ge_ref[rows, hs]
            eg = jnp.exp(ge)
            kb = k * bt
            qn_ref[rows, hs] = q
            kn_ref[rows, hs] = k
            kb_ref[rows, hs] = kb
            qe_ref[rows, hs] = q * eg
            kt_ref[rows, hs] = k * jnp.exp(ge[ROWS - 1:ROWS, :] - ge)
            rhs_ref[h, rows, 0:DN_DV] = v * bt
            rhs_ref[h, rows, DN_DV:2 * DN_DV] = kb * eg

    rid = lax.broadcasted_iota(jnp.int32, (chunk, chunk), 0)
    cid = lax.broadcasted_iota(jnp.int32, (chunk, chunk), 1)
    strict_lower = cid < rid
    lower = cid <= rid
    off = (rid >= chunk // 2) & (cid < chunk // 2)
    heads = range(DN_HEADS)
    hsl = [slice(h * DN_DK, (h + 1) * DN_DK) for h in heads]

    def prep_body(c2, carry):
        items = [(i, h) for i in range(2) for h in heads]
        cidx = [2 * c2 + i for i in range(2)]
        rs = [pl.ds(pl.multiple_of(c * chunk, chunk), chunk) for c in cidx]
        kh = [kn_ref[rs[c], hsl[h]] for c, h in items]
        decay = [jnp.exp(jnp.where(lower, ge_ref[rs[c], h * DN_DK:h * DN_DK + chunk]
                                   - gt_ref[cidx[c], SM_DECAY + h:SM_DECAY + h + 1, :], -jnp.inf)) for c, h in items]
        amat = [_dot_nt(kb_ref[rs[c], hsl[h]], kh[n]) * jnp.where(strict_lower, decay[n], 0.0)
                for n, (c, h) in enumerate(items)]
        for n, (c, h) in enumerate(items):
            at_ref[cidx[c] * DN_HEADS + h] = _dot_nt(qn_ref[rs[c], hsl[h]], kh[n]) * decay[n]
        dinv = sum((_diag_inv2(amat[n], amat[n + 1]) for n in range(0, len(items), 2)), [])
        inner = [_dot3(jnp.where(off, a, 0.0), d) for a, d in zip(amat, dinv)]
        tmat = [d - _dot3(d, i) for d, i in zip(dinv, inner)]
        sol = [_dot3(t, rhs_ref[h, rs[c], :]) for t, (c, h) in zip(tmat, items)]
        for s, (c, h) in zip(sol, items):
            uc_ref[rs[c], hsl[h]] = s[:, :DN_DV]
            wc_ref[rs[c], hsl[h]] = s[:, DN_DV:]
        return carry

    assert nch % 2 == 0
    lax.fori_loop(0, nch // 2, prep_body, 0)

    def scan_body(c, carry):
        r0 = pl.multiple_of(c * chunk, chunk)
        rs = pl.ds(r0, chunk)
        glast = ge_ref[pl.ds(r0 + chunk - 1, 1), :]
        sh = [s_ref[h] for h in heads]
        ws = [_dot(wc_ref[rs, hsl[h]], sh[h]) for h in heads]
        qs = [_dot(qe_ref[rs, hsl[h]], sh[h]) for h in heads]
        v_new = [uc_ref[rs, hsl[h]] - ws[h] for h in heads]
        o2 = [_dot(at_ref[c * DN_HEADS + h], v_new[h]) for h in heads]
        kv = [_dot_tn(kt_ref[rs, hsl[h]], v_new[h]) for h in heads]
        for h in heads:
            s_ref[h] = sh[h] * jnp.exp(glast[:, hsl[h]]) + kv[h]
            oo_ref[rs, hsl[h]] = _rms_rows(qs[h] + o2[h], nw_ref[...]) * _silu(zz_ref[rs, hsl[h]])
        return carry

    lax.fori_loop(0, nch, scan_body, 0)
    o_ref[0] = oo_ref[0:lin, :]

    @pl.when(l == pl.num_programs(1) - 1)
    def _():
        cnew_ref[0] = ext_ref[5 + lin:8 + lin, :]
        snew_ref[0] = s_ref[...]


def _deltanet(u3, cbuf, s0, cw, par, nw, *, lc, lp):
    b, l, _ = u3.shape
    nl = l // lc
    blk = lambda c0: pl.BlockSpec((1, lc, 512), lambda i, j: (i, j, c0 // 512))
    return pl.pallas_call(
        functools.partial(_dn_kernel, lin=lc, lp=lp, chunk=DN_CHUNK),
        grid=(b, nl),
        in_specs=[
            blk(C_DNQ), blk(C_DNK), blk(C_DNV), blk(C_DNZ),
            pl.BlockSpec((1, lc, LANES), lambda i, j: (i, j, C_SMALL // LANES)),
            pl.BlockSpec((1, CONV_WIDTH - 1, DN_CONV_DIM), lambda i, j: (i, 0, 0)),
            pl.BlockSpec((1, DN_HEADS, DN_DK, DN_DV), lambda i, j: (i, 0, 0, 0)),
            pl.BlockSpec((CONV_WIDTH, DN_CONV_DIM), lambda i, j: (0, 0)),
            pl.BlockSpec((SUBLANES, LANES), lambda i, j: (0, 0)),
            pl.BlockSpec((1, DN_DV), lambda i, j: (0, 0)),
        ],
        out_specs=[
            pl.BlockSpec((1, lc, DN_VW), lambda i, j: (i, j, 0)),
            pl.BlockSpec((1, CONV_WIDTH - 1, DN_CONV_DIM), lambda i, j: (i, 0, 0)),
            pl.BlockSpec((1, DN_HEADS, DN_DK, DN_DV), lambda i, j: (i, 0, 0, 0)),
        ],
        out_shape=[
            jax.ShapeDtypeStruct((b, l, DN_VW), F32),
            jax.ShapeDtypeStruct((b, CONV_WIDTH - 1, DN_CONV_DIM), F32),
            jax.ShapeDtypeStruct((b, DN_HEADS, DN_DK, DN_DV), F32),
        ],
        scratch_shapes=[
            pltpu.VMEM((8 + lp, DN_CONV_DIM), F32),
            pltpu.VMEM((lp, DN_QK), F32), pltpu.VMEM((lp, DN_QK), F32), pltpu.VMEM((lp, DN_VW), F32),
            pltpu.VMEM((lp, DN_VW), F32), pltpu.VMEM((lp, LANES), F32),
            pltpu.VMEM((lp, DN_QK), F32), pltpu.VMEM((lp, DN_QK), F32),
            pltpu.VMEM((lp // DN_CHUNK, LANES, DN_CHUNK), F32),
            pltpu.VMEM((lp, DN_VW), F32),
            pltpu.VMEM((DN_HEADS, DN_DK, DN_DV), F32),
            pltpu.VMEM((lp, DN_VW), F32), pltpu.VMEM((lp, DN_QK), F32),
            pltpu.VMEM((lp, DN_QK), F32), pltpu.VMEM((lp, DN_QK), F32),
            pltpu.VMEM((lp // DN_CHUNK * DN_HEADS, DN_CHUNK, DN_CHUNK), F32),
            pltpu.VMEM((DN_HEADS, lp, 2 * DN_DV), F32),
        ],
        compiler_params=_cparams(("parallel", "arbitrary")),
        name="deltanet",
    )(u3, u3, u3, u3, u3, cbuf, s0, cw, par, nw)


def _ssd_kernel(z_ref, x_ref, bc_ref, sm_ref, cbuf_ref, h0_ref, cw_ref, cb_ref, par_ref, nw_ref, dsk_ref,
                o_ref, cnew_ref, hnew_ref,
                ext_ref, xs_ref, bm_ref, cm_ref, zz_ref, smp_ref, ce_ref, de_ref, ct_ref, yy_ref, ht_ref,
                *, lin, lp, chunk):
    assert chunk == SSM_HEAD_DIM
    l = pl.program_id(1)
    nch = lp // chunk
    gn = SSM_GROUPS * SSM_STATE

    @pl.when(l == 0)
    def _():
        for g in range(SSM_GROUPS):
            hg = h0_ref[0, g * SSM_HPG:(g + 1) * SSM_HPG]
            ht_ref[g] = hg.reshape(SSM_HPG * SSM_HEAD_DIM, SSM_STATE).T

    _fill_ext(ext_ref, cbuf_ref, ((x_ref, 0, SSM_D_INNER), (bc_ref, SSM_D_INNER, 2 * gn)), lin=lin, lp=lp)
    _pad_rows(zz_ref, z_ref, lin=lin, lp=lp)
    _pad_rows(smp_ref, sm_ref, lin=lin, lp=lp)

    for r0 in range(0, lp, ROWS):
        valid = _row_valid(r0, ROWS, lin)
        for c0 in range(0, SSM_CONV_DIM, LANES):
            y = _conv_silu(ext_ref, r0, ROWS, slice(c0, c0 + LANES), cw_ref, cb_ref)
            if valid is not None:
                y = jnp.where(valid, y, 0.0)
            if c0 < SSM_D_INNER:
                xs_ref[r0:r0 + ROWS, c0:c0 + LANES] = y
            elif c0 < SSM_D_INNER + gn:
                bm_ref[r0:r0 + ROWS, c0 - SSM_D_INNER:c0 - SSM_D_INNER + LANES] = y
            else:
                cm_ref[r0:r0 + ROWS, c0 - SSM_D_INNER - gn:c0 - SSM_D_INNER - gn + LANES] = y

    sm = smp_ref[...]
    dt = _softplus(sm + par_ref[1:2, :])
    valid = _row_valid(0, lp, lin)
    if valid is not None:
        dt = jnp.where(valid, dt, 0.0)
    cum = _chunk_cumsum(dt * (-jnp.exp(par_ref[0:1, :])), chunk)
    ct = cum.T
    for c in range(nch):
        ct_ref[c] = ct[:, c * chunk:(c + 1) * chunk]

    spread = _expand_heads(jnp.concatenate([cum, dt], axis=0), SSM_HEADS, SSM_HEAD_DIM, SM_DT)
    ce_ref[...] = spread[:lp]
    de_ref[...] = spread[lp:]

    pw = 2 * SSM_HEAD_DIM
    row2 = lax.broadcasted_iota(jnp.int32, (chunk, 2 * chunk), 0)
    lane2 = lax.broadcasted_iota(jnp.int32, (chunk, 2 * chunk), 1)
    lower2 = (lane2 % chunk) <= row2
    first = lax.broadcasted_iota(jnp.int32, (chunk, pw), 1) < SSM_HEAD_DIM
    groups = range(SSM_GROUPS)
    gw_ = SSM_HPG * SSM_HEAD_DIM

    def chunk_body(c, carry):
        r0 = pl.multiple_of(c * chunk, chunk)
        rs = pl.ds(r0, chunk)
        bg = [bm_ref[rs, g * SSM_STATE:(g + 1) * SSM_STATE] for g in groups]
        cg = [cm_ref[rs, g * SSM_STATE:(g + 1) * SSM_STATE] for g in groups]
        ce = [ce_ref[rs, g * gw_:(g + 1) * gw_] for g in groups]
        de = [de_ref[rs, g * gw_:(g + 1) * gw_] for g in groups]
        xg = [xs_ref[rs, g * gw_:(g + 1) * gw_] for g in groups]
        ht = [ht_ref[g] for g in groups]
        cb = [_dot_nt(cg[g], bg[g]) for g in groups]
        ys = [_dot(cg[g], ht[g]) * jnp.exp(ce[g]) for g in groups]
        for g in groups:
            last = ce[g][chunk - 1:chunk, :]
            ht_ref[g] = ht[g] * jnp.exp(last) + _dot_tn(bg[g], xg[g] * (jnp.exp(last - ce[g]) * de[g]))
        for g in groups:
            cb2 = jnp.concatenate([cb[g], cb[g]], axis=1)
            xd = xg[g] * de[g]
            for p in range(SSM_HPG // 2):
                ps = slice(p * pw, (p + 1) * pw)
                hd = g * SSM_HPG + 2 * p
                ctp = ct_ref[c, SM_DT + hd:SM_DT + hd + 2, :]
                crp = jnp.concatenate([ctp[0:1], ctp[1:2]], axis=1)
                lm = jnp.exp(jnp.where(lower2, ce[g][:, ps] - crp, -jnp.inf))
                xdp = xd[:, ps]
                bd = jnp.concatenate([jnp.where(first, xdp, 0.0), jnp.where(first, 0.0, xdp)], axis=0)
                cols = slice(g * gw_ + p * pw, g * gw_ + (p + 1) * pw)
                yy_ref[rs, cols] = _dot(cb2 * lm, bd) + ys[g][:, ps] + dsk_ref[:, cols] * xg[g][:, ps]
        return carry

    lax.fori_loop(0, nch, chunk_body, 0)

    gw = SSM_D_INNER // SSM_GROUPS
    for r0 in range(0, lp, ROWS):
        n = min(ROWS, lin - r0)
        if n <= 0:
            break
        for g in range(SSM_GROUPS):
            cols = slice(g * gw, (g + 1) * gw)
            t = yy_ref[r0:r0 + ROWS, cols] * _silu(zz_ref[r0:r0 + ROWS, cols])
            t = _rms_rows(t, nw_ref[:, cols])
            o_ref[0, r0:r0 + n, cols] = t[:n]

    @pl.when(l == pl.num_programs(1) - 1)
    def _():
        cnew_ref[0] = ext_ref[5 + lin:8 + lin, :]
        for g in range(SSM_GROUPS):
            hnew_ref[0, g * SSM_HPG:(g + 1) * SSM_HPG] = ht_ref[g].T.reshape(SSM_HPG, SSM_HEAD_DIM, SSM_STATE)


def _ssd(u3, cbuf, h0, cw, cb, par, nw, d, *, lc, lp):
    b, l, _ = u3.shape
    nl = l // lc
    gn2 = 2 * SSM_GROUPS * SSM_STATE
    return pl.pallas_call(
        functools.partial(_ssd_kernel, lin=lc, lp=lp, chunk=SSM_CHUNK),
        grid=(b, nl),
        in_specs=[
            pl.BlockSpec((1, lc, SSM_D_INNER), lambda i, j: (i, j, C_SSZ // SSM_D_INNER)),
            pl.BlockSpec((1, lc, SSM_D_INNER), lambda i, j: (i, j, C_SSX // SSM_D_INNER)),
            pl.BlockSpec((1, lc, gn2), lambda i, j: (i, j, C_SSBC // gn2)),
            pl.BlockSpec((1, lc, LANES), lambda i, j: (i, j, C_SMALL // LANES)),
            pl.BlockSpec((1, CONV_WIDTH - 1, SSM_CONV_DIM), lambda i, j: (i, 0, 0)),
            pl.BlockSpec((1, SSM_HEADS, SSM_HEAD_DIM, SSM_STATE), lambda i, j: (i, 0, 0, 0)),
            pl.BlockSpec((CONV_WIDTH, SSM_CONV_DIM), lambda i, j: (0, 0)),
            pl.BlockSpec((1, SSM_CONV_DIM), lambda i, j: (0, 0)),
            pl.BlockSpec((SUBLANES, LANES), lambda i, j: (0, 0)),
            pl.BlockSpec((1, SSM_D_INNER), lambda i, j: (0, 0)),
            pl.BlockSpec((1, SSM_D_INNER), lambda i, j: (0, 0)),
        ],
        out_specs=[
            pl.BlockSpec((1, lc, SSM_D_INNER), lambda i, j: (i, j, 0)),
            pl.BlockSpec((1, CONV_WIDTH - 1, SSM_CONV_DIM), lambda i, j: (i, 0, 0)),
            pl.BlockSpec((1, SSM_HEADS, SSM_HEAD_DIM, SSM_STATE), lambda i, j: (i, 0, 0, 0)),
        ],
        out_shape=[
            jax.ShapeDtypeStruct((b, l, SSM_D_INNER), F32),
            jax.ShapeDtypeStruct((b, CONV_WIDTH - 1, SSM_CONV_DIM), F32),
            jax.ShapeDtypeStruct((b, SSM_HEADS, SSM_HEAD_DIM, SSM_STATE), F32),
        ],
        scratch_shapes=[
            pltpu.VMEM((8 + lp, SSM_CONV_DIM), F32),
            pltpu.VMEM((lp, SSM_D_INNER), F32),
            pltpu.VMEM((lp, SSM_GROUPS * SSM_STATE), F32), pltpu.VMEM((lp, SSM_GROUPS * SSM_STATE), F32),
            pltpu.VMEM((lp, SSM_D_INNER), F32), pltpu.VMEM((lp, LANES), F32),
            pltpu.VMEM((lp, SSM_D_INNER), F32), pltpu.VMEM((lp, SSM_D_INNER), F32),
            pltpu.VMEM((lp // SSM_CHUNK, LANES, SSM_CHUNK), F32),
            pltpu.VMEM((lp, SSM_D_INNER), F32),
            pltpu.VMEM((SSM_GROUPS, SSM_STATE, SSM_HPG * SSM_HEAD_DIM), F32),
        ],
        compiler_params=_cparams(("parallel", "arbitrary")),
        name="ssd",
    )(u3, u3, u3, u3, cbuf, h0, cw, cb, par, nw, d)


def _strict_upper_stack(n):
    j = lax.broadcasted_iota(jnp.int32, (2 * n, n), 0) % n
    s = lax.broadcasted_iota(jnp.int32, (2 * n, n), 1)
    return jnp.where(j > s, 1.0, 0.0).astype(BF16)


def _rev_excl_cumsum(la, uu):
    hi, lo = _split2(la)
    return jnp.dot(jnp.concatenate([hi, lo], axis=1), uu, preferred_element_type=F32)


def _sbp_kernel(bias_ref, q_ref, k_ref, v_ref, o_ref, *, tq, scale):
    h = pl.program_id(1)
    qi = pl.program_id(2)
    bias = bias_ref[h]
    q = q_ref[0].astype(BF16)
    uu = _strict_upper_stack(tq)

    def sweep(blocks, carry, diagonal):
        c, acc = carry
        rows = [pl.ds(kj * tq if isinstance(kj, int) else pl.multiple_of(kj * tq, tq), tq) for kj in blocks]
        zs = [_dot_nt(q, k_ref[0, r, :]) * scale + bias for r in rows]
        lss, las = [], []
        for z in zs:
            l1 = jnp.log1p(jnp.exp(-jnp.abs(z)))
            lss.append(jnp.minimum(z, 0.0) - l1)
            las.append(-jnp.maximum(z, 0.0) - l1)
        if diagonal:
            valid = lax.broadcasted_iota(jnp.int32, (tq, tq), 1) < lax.broadcasted_iota(jnp.int32, (tq, tq), 0)
            las = [jnp.where(valid, la, 0.0) for la in las]
        survs = [_rev_excl_cumsum(la, uu) for la in las]
        for ls, la, surv, r in zip(lss, las, survs, rows):
            att = jnp.exp(ls + surv + c)
            if diagonal:
                att = jnp.where(valid, att, 0.0)
            acc = acc + _dot(att, v_ref[0, r, :])
            c = c + jnp.sum(la, axis=1, keepdims=True)
        return c, acc

    carry = (jnp.zeros((tq, 1), F32), jnp.zeros((tq, SB_HEAD_DIM), F32))
    carry = sweep([qi], carry, True)
    carry = lax.fori_loop(0, lax.shift_right_logical(qi, 1),
                          lambda t, cr: sweep([qi - 1 - 2 * t, qi - 2 - 2 * t], cr, False), carry)
    carry = lax.fori_loop(0, qi & 1, lambda t, cr: sweep([0], cr, False), carry)
    o_ref[0] = carry[1]


def _sb_prompt(u3, bias, *, tq=256):
    b, l, _ = u3.shape
    tq = min(tq, l)
    kv = lambda c0: pl.BlockSpec((1, l, SB_HEAD_DIM), lambda i, h, j: (i, 0, c0 // SB_HEAD_DIM + h))
    return pl.pallas_call(
        functools.partial(_sbp_kernel, tq=tq, scale=SB_HEAD_DIM ** -0.5),
        grid=(b, SB_HEADS, l // tq),
        in_specs=[
            pl.BlockSpec(memory_space=pltpu.SMEM),
            pl.BlockSpec((1, tq, SB_HEAD_DIM), lambda i, h, j: (i, j, C_SBQ // SB_HEAD_DIM + h)),
            kv(C_SBK), kv(C_SBV),
        ],
        out_specs=pl.BlockSpec((1, tq, SB_HEAD_DIM), lambda i, h, j: (i, j, h)),
        out_shape=jax.ShapeDtypeStruct((b, l, SB_WIDTH), F32),
        compiler_params=_cparams(("parallel", "parallel", "arbitrary")),
        name="sb_prompt",
    )(bias, u3, u3, u3)


def _kvrows_kernel(k_ref, v_ref, ko_ref, vo_ref, *, tl):
    for h in range(SB_HEADS):
        rows = pl.ds(h, tl, stride=SB_HEADS)
        ko_ref[0, rows, :] = k_ref[0, :, h * SB_HEAD_DIM:(h + 1) * SB_HEAD_DIM]
        vo_ref[0, rows, :] = v_ref[0, :, h * SB_HEAD_DIM:(h + 1) * SB_HEAD_DIM]


def _kv_rows(u3, *, tl=512):
    b, l, _ = u3.shape
    tl = min(tl, l)
    src = lambda c0: pl.BlockSpec((1, tl, SB_WIDTH), lambda i, j: (i, j, c0 // SB_WIDTH))
    dst = pl.BlockSpec((1, tl * SB_HEADS, SB_HEAD_DIM), lambda i, j: (i, j, 0))
    shape = jax.ShapeDtypeStruct((b, l * SB_HEADS, SB_HEAD_DIM), F32)
    return pl.pallas_call(
        functools.partial(_kvrows_kernel, tl=tl),
        grid=(b, l // tl),
        in_specs=[src(C_SBK), src(C_SBV)],
        out_specs=[dst, dst],
        out_shape=[shape, shape],
        compiler_params=_cparams(("parallel", "parallel")),
        name="kv_rows",
    )(u3, u3)


def _sbs_kernel(pt_ref, q_ref, kc_ref, vc_ref, bias_ref, uu_ref, *rest, pp, tq, scale):
    k_refs = rest[:pp]
    v_refs = rest[pp:2 * pp]
    o_ref = rest[2 * pp]
    c_ref = rest[2 * pp + 1]
    p = pl.program_id(1)
    hq = q_ref.shape[1]
    ncol = kc_ref.shape[1]
    q = q_ref[0].astype(BF16)
    bias = bias_ref[...]
    row_head = lax.broadcasted_iota(jnp.int32, (hq, ncol), 0) // tq
    col = lax.broadcasted_iota(jnp.int32, (hq, ncol), 1)
    own = (col % SB_HEADS) == row_head

    def rev_cumsum(la):
        hi, lo = _split2(la)
        return jnp.dot(jnp.concatenate([hi, lo], axis=1), uu_ref[...], preferred_element_type=F32)

    @pl.when(p == 0)
    def _():
        z = _dot_nt(q, kc_ref[0]) * scale + bias
        t = lax.broadcasted_iota(jnp.int32, (hq, ncol), 0) % tq
        valid = own & ((col // SB_HEADS) < t)
        ls = _log_sigmoid(z)
        la = jnp.where(valid, ls - z, 0.0)
        att = jnp.where(valid, jnp.exp(ls + rev_cumsum(la)), 0.0)
        o_ref[0] = _dot(att, vc_ref[0])
        c_ref[...] = jnp.sum(la, axis=1, keepdims=True)

    z = jnp.concatenate([_dot_nt(q, k_refs[j][...]) for j in range(pp)], axis=0)
    z = z * scale + jnp.concatenate([bias] * pp, axis=0)
    valid = jnp.concatenate([own] * pp, axis=0)
    ls = _log_sigmoid(z)
    la = jnp.where(valid, ls - z, 0.0)
    surv = rev_cumsum(la)
    tot = jnp.sum(la, axis=1, keepdims=True)
    cur = c_ref[...]
    cs = []
    for j in range(pp):
        cs.append(cur)
        cur = cur + tot[j * hq:(j + 1) * hq]
    c_ref[...] = cur
    att = jnp.where(valid, jnp.exp(ls + surv + jnp.concatenate(cs, axis=0)), 0.0)
    acc = o_ref[0]
    for j in range(pp):
        acc = acc + _dot(att[j * hq:(j + 1) * hq], v_refs[j][...])
    o_ref[0] = acc


def _sb_sample(q_rows, k_cur, v_cur, bias_rows, cache_k, cache_v, page_table, layer, *, pp=16):
    b, hq, _ = q_rows.shape
    n_pages = page_table.shape[1]
    ncol = cache_k.shape[2]
    pp = math.gcd(pp, n_pages)
    tq = hq // SB_HEADS
    jj = lax.broadcasted_iota(jnp.int32, (2 * ncol, ncol), 0) % ncol
    ss = lax.broadcasted_iota(jnp.int32, (2 * ncol, ncol), 1)
    uu = jnp.where(jj > ss, 1.0, 0.0).astype(BF16)

    def page_spec(j):
        return pl.BlockSpec((None, None, ncol, SB_HEAD_DIM),
                            lambda i, p, pt: (layer, pt[i, n_pages - 1 - (p * pp + j)], 0, 0))

    grid_spec = pltpu.PrefetchScalarGridSpec(
        num_scalar_prefetch=1,
        grid=(b, n_pages // pp),
        in_specs=[
            pl.BlockSpec((1, hq, SB_HEAD_DIM), lambda i, p, pt: (i, 0, 0)),
            pl.BlockSpec((1, ncol, SB_HEAD_DIM), lambda i, p, pt: (i, 0, 0)),
            pl.BlockSpec((1, ncol, SB_HEAD_DIM), lambda i, p, pt: (i, 0, 0)),
            pl.BlockSpec((hq, ncol), lambda i, p, pt: (0, 0)),
            pl.BlockSpec((2 * ncol, ncol), lambda i, p, pt: (0, 0)),
        ] + [page_spec(j) for j in range(pp)] + [page_spec(j) for j in range(pp)],
        out_specs=pl.BlockSpec((1, hq, SB_HEAD_DIM), lambda i, p, pt: (i, 0, 0)),
        scratch_shapes=[pltpu.VMEM((hq, 1), F32)],
    )
    return pl.pallas_call(
        functools.partial(_sbs_kernel, pp=pp, tq=tq, scale=SB_HEAD_DIM ** -0.5),
        grid_spec=grid_spec,
        out_shape=jax.ShapeDtypeStruct((b, hq, SB_HEAD_DIM), F32),
        compiler_params=_cparams(("parallel", "arbitrary")),
        name="sb_sample",
    )(page_table, q_rows, k_cur, v_cur, bias_rows, uu, *([cache_k] * pp), *([cache_v] * pp))


def _mixout_kernel(op_ref, od_ref, os_ref, oa_ref, gt_ref, wb_ref, wo_ref, x_ref, g_ref, o_ref):
    gate = lambda k: gt_ref[:, k * D_MODEL:(k + 1) * D_MODEL].astype(F32)
    br = lambda o_ref_, r0, n: _dot(o_ref_[...], wb_ref[r0:r0 + n, :])
    acc = gate(0) * br(op_ref, 0, POOL_WIDTH)
    acc = acc + gate(1) * br(od_ref, POOL_WIDTH, DN_VW)
    acc = acc + gate(2) * br(os_ref, POOL_WIDTH + DN_VW, SB_WIDTH)
    acc = acc + gate(3) * br(oa_ref, POOL_WIDTH + DN_VW + SB_WIDTH, SSM_D_INNER)
    mix = jnp.dot(acc.astype(BF16), wo_ref[...], preferred_element_type=F32)
    o_ref[...] = x_ref[...] + _rms_rows(mix, g_ref[...])


def _mixout(o_pool, o_dn, o_sb, o_ss, gates, w_br_all, w_out_all, layer, x2, g, *, tm):
    m = x2.shape[0]
    rows = lambda w: pl.BlockSpec((tm, w), lambda i: (i, 0))
    resident = lambda r: pl.BlockSpec((None, r, D_MODEL), lambda i: (layer, 0, 0), pipeline_mode=pl.Buffered(1))
    return pl.pallas_call(
        _mixout_kernel,
        grid=(m // tm,),
        in_specs=[rows(POOL_WIDTH), rows(DN_VW), rows(SB_WIDTH), rows(SSM_D_INNER), rows(N_BRANCH * D_MODEL),
                  resident(w_br_all.shape[1]), resident(D_MODEL), rows(D_MODEL),
                  pl.BlockSpec((1, D_MODEL), lambda i: (0, 0))],
        out_specs=rows(D_MODEL),
        out_shape=jax.ShapeDtypeStruct((m, D_MODEL), F32),
        compiler_params=_cparams(("parallel",)),
        name="mixout",
    )(o_pool, o_dn, o_sb, o_ss, gates, w_br_all, w_out_all, x2, g)


def _mlp_kernel(x_ref, g1_ref, wu_ref, wd_ref, g2_ref, o_ref, h_ref, acc_ref, *, tm):
    f = pl.program_id(1)

    @pl.when(f == 0)
    def _():
        _norm_rows_to(h_ref, x_ref, g1_ref, tm)
        acc_ref[...] = jnp.zeros(acc_ref.shape, F32)

    a = jnp.dot(h_ref[...], wu_ref[...], preferred_element_type=F32)
    a = jnp.square(jnp.maximum(a, 0.0)).astype(BF16)
    acc_ref[...] += jnp.dot(a, wd_ref[...], preferred_element_type=F32)

    @pl.when(f == pl.num_programs(1) - 1)
    def _():
        o_ref[...] = x_ref[...] + _rms_rows(acc_ref[...], g2_ref[...])


def _mlp(x2, g1, wu_all, wd_all, layer, g2, *, tm, tf=1024):
    m = x2.shape[0]
    return pl.pallas_call(
        functools.partial(_mlp_kernel, tm=tm),
        grid=(m // tm, D_FF // tf),
        in_specs=[
            pl.BlockSpec((tm, D_MODEL), lambda i, f: (i, 0)),
            pl.BlockSpec((1, D_MODEL), lambda i, f: (0, 0)),
            pl.BlockSpec((None, D_MODEL, tf), lambda i, f: (layer, 0, f)),
            pl.BlockSpec((None, tf, D_MODEL), lambda i, f: (layer, f, 0)),
            pl.BlockSpec((1, D_MODEL), lambda i, f: (0, 0)),
        ],
        out_specs=pl.BlockSpec((tm, D_MODEL), lambda i, f: (i, 0)),
        out_shape=jax.ShapeDtypeStruct((m, D_MODEL), F32),
        scratch_shapes=[pltpu.VMEM((tm, D_MODEL), BF16), pltpu.VMEM((tm, D_MODEL), F32)],
        compiler_params=_cparams(("parallel", "arbitrary")),
        name="mlp",
    )(x2, g1, wu_all, wd_all, g2)


def _lane_rows(vals, offset):
    out = jnp.zeros((vals.shape[0], LANES), F32)
    for k in range(3):
        o = offset + k * SM_REP
        out = out.at[:, o:o + vals.shape[1]].set(vals.astype(F32))
    return out


def _prepare_params(p):
    depth = p["w_in"].shape[0]
    wt = jnp.transpose(p["w_in"], (0, 2, 1))
    small = wt[:, _O_DNB:_O_SB], wt[:, _O_SSDT:_O_GATE]
    gap = jnp.zeros((depth, SM_REP - SM_GROUP, D_MODEL), wt.dtype)
    w_u = jnp.concatenate(
        [wt[:, :_O_DNB], wt[:, _O_SB:_O_SSDT], *small, gap, *small, gap, *small,
         jnp.zeros((depth, NU - C_SMALL - 2 * SM_REP - SM_GROUP, D_MODEL), wt.dtype)], axis=1).astype(BF16)
    w_g = wt[:, _O_GATE:_O_END].astype(BF16)
    zrow = jnp.zeros((depth, LANES), F32)
    par = lambda a_log, dt_bias, off: jnp.stack(
        [_lane_rows(a_log, off), _lane_rows(dt_bias, off)] + [zrow] * (SUBLANES - 2), axis=1)
    return dict(
        w_u=w_u, w_g=w_g,
        n_mix_pre=p["norm_mix_pre"], n_mix_post=p["norm_mix_post"],
        n_mlp_pre=p["norm_mlp_pre"], n_mlp_post=p["norm_mlp_post"],
        pool_w=p["pool_w"].astype(BF16), pool_scale=p["pool_scale"],
        dn_conv_w=p["dn_conv_w"], dn_par=par(p["dn_a_log"], p["dn_dt_bias"], SM_DECAY), dn_norm_w=p["dn_norm_w"],
        sb_bias=p["sb_bias"],
        ssm_conv_w=p["ssm_conv_w"], ssm_conv_b=p["ssm_conv_b"],
        ss_par=par(p["ssm_a_log"], p["ssm_dt_bias"], SM_DT), ssm_norm_w=p["ssm_norm_w"],
        ssm_d=jnp.repeat(p["ssm_d"].astype(F32), SSM_HEAD_DIM, axis=1),
        w_branch=p["w_branch"].astype(BF16), w_out=p["w_out"].astype(BF16),
        w_up=p["w_up"].astype(BF16), w_down=p["w_down"].astype(BF16),
    )


def _trunk_layer(x, states, pw, layer, *, past=None):
    b, l, _ = x.shape
    m = b * l
    pool_buf, dn_conv, dn_s, ssm_conv, ssm_h = states
    prompt = past is None
    tm_big = min(m, 1024)
    tm = min(m, 512)
    lc = min(l, 256)
    lp = max(lc, 2 * DN_CHUNK)
    row = lambda name: pw[name][layer].reshape(1, -1)

    x2 = x.reshape(m, D_MODEL)
    u2 = _inproj(x2, row("n_mix_pre"), pw["w_u"], layer, tm=tm_big)
    gates = _inproj(x2, row("n_mix_pre"), pw["w_g"], layer, tm=tm_big, gate=True)
    u3 = u2.reshape(b, l, NU)

    pos0 = 0 if prompt else past[2].shape[1] * (past[0].shape[2] // SB_HEADS)
    o_pool, pool_new = _pool(u3, pool_buf, pw["pool_w"][layer], row("pool_scale"), pos0=pos0)
    o_dn, dn_conv_new, dn_s_new = _deltanet(u3, dn_conv, dn_s, pw["dn_conv_w"][layer], pw["dn_par"][layer],
                                            row("dn_norm_w"), lc=lc, lp=lp)
    o_ss, ss_conv_new, ss_h_new = _ssd(u3, ssm_conv, ssm_h, pw["ssm_conv_w"][layer], row("ssm_conv_b"),
                                       pw["ss_par"][layer], row("ssm_norm_w"), row("ssm_d"), lc=lc, lp=lp)

    k_rows, v_rows = _kv_rows(u3)
    sb_bias = pw["sb_bias"][layer]
    if prompt:
        o_sb = _sb_prompt(u3, sb_bias)
    else:
        cache_k, cache_v, page_table = past
        ncol = cache_k.shape[2]
        q = u3[:, :, C_SBQ:C_SBQ + SB_WIDTH].reshape(b, l, SB_HEADS, SB_HEAD_DIM)
        q_rows = jnp.transpose(q, (0, 2, 1, 3)).reshape(b, SB_HEADS * l, SB_HEAD_DIM)
        k_cur = jnp.pad(k_rows, ((0, 0), (0, ncol - l * SB_HEADS), (0, 0)))
        v_cur = jnp.pad(v_rows, ((0, 0), (0, ncol - l * SB_HEADS), (0, 0)))
        bias_rows = jnp.broadcast_to(jnp.repeat(sb_bias, l)[:, None], (SB_HEADS * l, ncol)).astype(F32)
        acc = _sb_sample(q_rows, k_cur, v_cur, bias_rows, cache_k, cache_v, page_table, layer)
        o_sb = jnp.transpose(acc.reshape(b, SB_HEADS, l, SB_HEAD_DIM), (0, 2, 1, 3)).reshape(b, l, SB_WIDTH)

    x2 = _mixout(o_pool.reshape(m, -1), o_dn.reshape(m, -1), o_sb.reshape(m, -1), o_ss.reshape(m, -1),
                 gates, pw["w_branch"], pw["w_out"], layer, x2, row("n_mix_post"), tm=min(m, 256))
    x2 = _mlp(x2, row("n_mlp_pre"), pw["w_up"], pw["w_down"], layer, row("n_mlp_post"), tm=tm)
    new_states = (k_rows.reshape(b, l, SB_HEADS, SB_HEAD_DIM), v_rows.reshape(b, l, SB_HEADS, SB_HEAD_DIM),
                  pool_new, dn_conv_new, dn_s_new, ss_conv_new, ss_h_new)
    return x2.reshape(b, l, D_MODEL), new_states


def kernel(x_prompt, x_sample, cache_sb_k, cache_sb_v, state_pool, state_dn_conv, state_dn_s, state_ssm_conv, state_ssm_h, page_table, norm_mix_pre, norm_mix_post, norm_mlp_pre, norm_mlp_post, w_in, pool_w, pool_scale, dn_conv_w, dn_a_log, dn_dt_bias, dn_norm_w, sb_bias, ssm_conv_w, ssm_conv_b, ssm_a_log, ssm_dt_bias, ssm_d, ssm_norm_w, w_branch, w_out, w_up, w_down):
    pw = _prepare_params(dict(
        norm_mix_pre=norm_mix_pre, norm_mix_post=norm_mix_post, norm_mlp_pre=norm_mlp_pre,
        norm_mlp_post=norm_mlp_post, w_in=w_in, pool_w=pool_w, pool_scale=pool_scale,
        dn_conv_w=dn_conv_w, dn_a_log=dn_a_log, dn_dt_bias=dn_dt_bias, dn_norm_w=dn_norm_w,
        sb_bias=sb_bias, ssm_conv_w=ssm_conv_w, ssm_conv_b=ssm_conv_b, ssm_a_log=ssm_a_log,
        ssm_dt_bias=ssm_dt_bias, ssm_d=ssm_d, ssm_norm_w=ssm_norm_w, w_branch=w_branch,
        w_out=w_out, w_up=w_up, w_down=w_down))
    depth = w_in.shape[0]
    bp = x_prompt.shape[0]
    dt_ = x_prompt.dtype
    zero_states = (jnp.zeros((bp, POOL_BUF, POOL_WIDTH), dt_),
                   jnp.zeros((bp, CONV_WIDTH - 1, DN_CONV_DIM), dt_),
                   jnp.zeros((bp, DN_HEADS, DN_DK, DN_DV), dt_),
                   jnp.zeros((bp, CONV_WIDTH - 1, SSM_CONV_DIM), dt_),
                   jnp.zeros((bp, SSM_HEADS, SSM_HEAD_DIM, SSM_STATE), dt_))
    n_pool, page = cache_sb_k.shape[1], cache_sb_k.shape[2]
    cache_k = cache_sb_k.reshape(depth, n_pool, page * SB_HEADS, SB_HEAD_DIM)
    cache_v = cache_sb_v.reshape(depth, n_pool, page * SB_HEADS, SB_HEAD_DIM)
    y_prompt, y_sample = x_prompt, x_sample
    new_p, new_s = [], []
    for layer in range(depth):
        y_prompt, st_p = _trunk_layer(y_prompt, zero_states, pw, layer)
        sample_states = (state_pool[layer], state_dn_conv[layer], state_dn_s[layer],
                         state_ssm_conv[layer], state_ssm_h[layer])
        y_sample, st_s = _trunk_layer(y_sample, sample_states, pw, layer, past=(cache_k, cache_v, page_table))
        new_p.append(st_p)
        new_s.append(st_s)
    outs_p = [jnp.stack(t) for t in zip(*new_p)]
    outs_s = [jnp.stack(t) for t in zip(*new_s)]
    return (y_prompt, y_sample, *outs_p, *outs_s)
```

```python
import functools
import math

import jax
import jax.numpy as jnp
from jax import lax
from jax.experimental import pallas as pl
from jax.experimental.pallas import tpu as pltpu

F32 = jnp.float32
BF16 = jnp.bfloat16

D_MODEL = 2048
N_BRANCH = 4
RMS_EPS = 1e-6
CONV_WIDTH = 4
POOL_WINDOWS = (2, 4, 8, 16)
POOL_WIDTH = D_MODEL // 4
POOL_GW = POOL_WIDTH // 4
POOL_BUF = max(POOL_WINDOWS) - 1
DN_HEADS = 4
DN_DK = 128
DN_DV = 128
DN_QK = DN_HEADS * DN_DK
DN_VW = DN_HEADS * DN_DV
DN_CONV_DIM = 2 * DN_QK + DN_VW
DN_CHUNK = 64
SB_HEADS = 4
SB_HEAD_DIM = 128
SB_WIDTH = SB_HEADS * SB_HEAD_DIM
SSM_D_INNER = D_MODEL // 2
SSM_HEAD_DIM = 64
SSM_HEADS = SSM_D_INNER // SSM_HEAD_DIM
SSM_GROUPS = 2
SSM_HPG = SSM_HEADS // SSM_GROUPS
SSM_STATE = 128
SSM_CONV_DIM = SSM_D_INNER + 2 * SSM_GROUPS * SSM_STATE
SSM_CHUNK = 64
D_FF = 4 * D_MODEL

LANES = 128
SUBLANES = 8

C_POOL = 0
C_DNQ, C_DNK, C_DNV, C_DNZ = 512, 1024, 1536, 2048
C_SBQ, C_SBK, C_SBV = 2560, 3072, 3584
C_SSZ, C_SSX, C_SSBC = 4096, 5120, 6144
C_SMALL = 6656
NU = 7168
_O_DNB = POOL_WIDTH + DN_CONV_DIM + DN_VW
_O_SB = _O_DNB + 2 * DN_HEADS
_O_SSDT = _O_SB + 3 * SB_WIDTH + SSM_D_INNER + SSM_CONV_DIM
_O_GATE = _O_SSDT + SSM_HEADS
_O_END = _O_GATE + N_BRANCH * D_MODEL
SM_BETA, SM_DECAY, SM_DT = 0, DN_HEADS, 2 * DN_HEADS
SM_GROUP = 2 * DN_HEADS + SSM_HEADS
SM_REP = 32

VMEM_LIMIT_MB = 56


def _cparams(sem, vmem_mb=VMEM_LIMIT_MB):
    return pltpu.CompilerParams(dimension_semantics=sem, vmem_limit_bytes=vmem_mb * 1024 * 1024)


def _sigmoid(x):
    return 1.0 / (1.0 + jnp.exp(-x))


def _silu(x):
    hx = 0.5 * x
    return hx + hx * jnp.tanh(hx)


def _softplus(x):
    return jnp.maximum(x, 0.0) + jnp.log1p(jnp.exp(-jnp.abs(x)))


def _log_sigmoid(x):
    return jnp.minimum(x, 0.0) - jnp.log1p(jnp.exp(-jnp.abs(x)))


def _dot(a, b):
    return jnp.dot(a.astype(BF16), b.astype(BF16), preferred_element_type=F32)


def _dot_nt(a, b):
    return lax.dot_general(a.astype(BF16), b.astype(BF16), (((1,), (1,)), ((), ())), preferred_element_type=F32)


def _dot_tn(a, b):
    return lax.dot_general(a.astype(BF16), b.astype(BF16), (((0,), (0,)), ((), ())), preferred_element_type=F32)


def _split2(a):
    hi = a.astype(BF16)
    lo = (a - hi.astype(F32)).astype(BF16)
    return hi, lo


def _expand_heads(src, n_heads, width, lane0):
    k = lax.broadcasted_iota(jnp.int32, (LANES, n_heads * width), 0)
    n = lax.broadcasted_iota(jnp.int32, (LANES, n_heads * width), 1)
    sel = jnp.where((k % SM_REP - lane0 == n // width) & (k < 3 * SM_REP), 1.0, 0.0).astype(BF16)
    lane = lax.broadcasted_iota(jnp.int32, src.shape, 1)
    hi = src.astype(BF16).astype(F32)
    r1 = src - hi
    mid = r1.astype(BF16).astype(F32)
    lo = r1 - mid
    pieces = jnp.where(lane < SM_REP, hi, jnp.where(lane < 2 * SM_REP, mid, lo))
    return jnp.dot(pieces.astype(BF16), sel, preferred_element_type=F32)


def _dot3(a, b):
    ah, al = _split2(a)
    bh, bl = _split2(b)
    d = lambda x, y: jnp.dot(x, y, preferred_element_type=F32)
    return d(ah, bh) + (d(ah, bl) + d(al, bh))


def _rms_rows(x, g):
    return x * lax.rsqrt(jnp.mean(x * x, axis=-1, keepdims=True) + RMS_EPS) * g


def _norm_rows_to(h_ref, x_ref, g_ref, tm):
    ch = min(tm, 256)

    def body(r, c):
        rs = pl.ds(pl.multiple_of(r * ch, ch), ch)
        h_ref[rs, :] = _rms_rows(x_ref[rs, :], g_ref[...]).astype(h_ref.dtype)
        return c

    lax.fori_loop(0, tm // ch, body, 0)


def _inproj_kernel(x_ref, g_ref, w_ref, o_ref, h_ref, *, tm, gate):
    @pl.when(pl.program_id(1) == 0)
    def _():
        _norm_rows_to(h_ref, x_ref, g_ref, tm)

    acc = _dot_nt(h_ref[...], w_ref[...])
    o_ref[...] = (_sigmoid(acc) if gate else acc).astype(o_ref.dtype)


def _inproj(x2, g, w_all, layer, *, tm, tn=1024, gate=False):
    m = x2.shape[0]
    n = w_all.shape[1]
    return pl.pallas_call(
        functools.partial(_inproj_kernel, tm=tm, gate=gate),
        grid=(m // tm, n // tn),
        in_specs=[
            pl.BlockSpec((tm, D_MODEL), lambda i, j: (i, 0)),
            pl.BlockSpec((1, D_MODEL), lambda i, j: (0, 0)),
            pl.BlockSpec((None, tn, D_MODEL), lambda i, j: (layer, j, 0)),
        ],
        out_specs=pl.BlockSpec((tm, tn), lambda i, j: (i, j)),
        out_shape=jax.ShapeDtypeStruct((m, n), BF16 if gate else F32),
        scratch_shapes=[pltpu.VMEM((tm, D_MODEL), BF16)],
        compiler_params=_cparams(("parallel", "arbitrary")),
        name="inproj_gate" if gate else "inproj",
    )(x2, g, w_all)


def _pool_kernel(u_ref, buf_ref, w_ref, sc_ref, o_ref, new_ref, ext_ref, *, lin, lp, pos0):
    ext_ref[0:16, :] = jnp.zeros((16, POOL_WIDTH), F32)
    ext_ref[1:16, :] = buf_ref[0]
    if lin < lp:
        ext_ref[16:16 + lp, :] = jnp.zeros((lp, POOL_WIDTH), F32)
    ext_ref[16:16 + lin, :] = u_ref[0]
    ch = min(lp, 256)
    for c0 in range(0, lp, ch):
        pos = pos0 + c0 + lax.broadcasted_iota(jnp.int32, (ch, 1), 0)
        for gi, w in enumerate(POOL_WINDOWS):
            cols = slice(gi * POOL_GW, (gi + 1) * POOL_GW)
            s = ext_ref[16 + c0:16 + c0 + ch, cols]
            tot = s
            for k in range(1, w):
                tot = tot + ext_ref[16 + c0 - k:16 + c0 - k + ch, cols]
            cnt = jnp.minimum(pos + 1, w).astype(F32)
            y = tot / cnt - s
            yo = _dot(y, w_ref[gi]) * sc_ref[:, cols]
            n = min(ch, lin - c0)
            o_ref[0, c0:c0 + n, cols] = yo[:n]
    new_ref[0] = ext_ref[1 + lin:16 + lin, :]


def _pool(u3, buf, w, sc, *, pos0):
    b, lin, _ = u3.shape
    lp = max(lin, SUBLANES)
    return pl.pallas_call(
        functools.partial(_pool_kernel, lin=lin, lp=lp, pos0=pos0),
        grid=(b,),
        in_specs=[
            pl.BlockSpec((1, lin, POOL_WIDTH), lambda i: (i, 0, C_POOL // POOL_WIDTH)),
            pl.BlockSpec((1, POOL_BUF, POOL_WIDTH), lambda i: (i, 0, 0)),
            pl.BlockSpec((4, POOL_GW, POOL_GW), lambda i: (0, 0, 0)),
            pl.BlockSpec((1, POOL_WIDTH), lambda i: (0, 0)),
        ],
        out_specs=[
            pl.BlockSpec((1, lin, POOL_WIDTH), lambda i: (i, 0, 0)),
            pl.BlockSpec((1, POOL_BUF, POOL_WIDTH), lambda i: (i, 0, 0)),
        ],
        out_shape=[
            jax.ShapeDtypeStruct((b, lin, POOL_WIDTH), F32),
            jax.ShapeDtypeStruct((b, POOL_BUF, POOL_WIDTH), F32),
        ],
        scratch_shapes=[pltpu.VMEM((16 + lp, POOL_WIDTH), F32)],
        compiler_params=_cparams(("parallel",)),
        name="pool",
    )(u3, buf, w, sc)


ROWS = 64


def _fill_ext(ext_ref, cbuf_ref, parts, *, lin, lp):
    l = pl.program_id(1)

    @pl.when(l == 0)
    def _():
        ext_ref[5:8, :] = cbuf_ref[0]

    @pl.when(l > 0)
    def _():
        ext_ref[5:8, :] = ext_ref[5 + lin:8 + lin, :]

    if lin < lp:
        ext_ref[8:8 + lp, :] = jnp.zeros((lp, ext_ref.shape[1]), F32)
    for ref, c0, width in parts:
        ext_ref[8:8 + lin, c0:c0 + width] = ref[0]


def _conv_silu(ext_ref, r0, n, cols, w_ref, b_ref):
    acc = None
    for i in range(CONV_WIDTH):
        part = ext_ref[5 + r0 + i:5 + r0 + i + n, cols] * w_ref[i:i + 1, cols]
        acc = part if acc is None else acc + part
    if b_ref is not None:
        acc = acc + b_ref[:, cols]
    return _silu(acc)


def _pad_rows(dst_ref, src_ref, *, lin, lp):
    if lin < lp:
        dst_ref[...] = jnp.zeros(dst_ref.shape, F32)
    dst_ref[0:lin, :] = src_ref[0]


def _row_valid(r0, n, lin):
    if r0 + n <= lin:
        return None
    return (r0 + lax.broadcasted_iota(jnp.int32, (n, 1), 0)) < lin


def _chunk_cumsum(x, chunk):
    rin = lax.broadcasted_iota(jnp.int32, x.shape, 0) % chunk
    s = 1
    while s < chunk:
        x = x + jnp.where(rin >= s, pltpu.roll(x, s, 0), 0.0)
        s *= 2
    return x


def _diag_inv2(a0, a1):
    c = a0.shape[0]
    hb = c // 2
    ng = hb // SUBLANES
    sub = lax.broadcasted_iota(jnp.int32, (SUBLANES, 2 * c), 0)
    lane = lax.broadcasted_iota(jnp.int32, (SUBLANES, 2 * c), 1)
    base = (lane // hb) * hb
    lmod = lane - base
    a01 = jnp.concatenate([a0, a1], axis=1)
    odd = ((lax.broadcasted_iota(jnp.int32, (hb, 2 * c), 1) // hb) % 2) == 1
    packed = jnp.where(odd, a01[hb:], a01[:hb])
    racc = [jnp.zeros((SUBLANES, 2 * c), F32) for _ in range(ng)]
    tg = [jnp.zeros((SUBLANES, 2 * c), F32) for _ in range(ng)]
    for j in range(hb):
        gj, rj = divmod(j, SUBLANES)
        t_j = jnp.where(lmod[0:1] == j, 1.0, 0.0) - racc[gj][rj:rj + 1, :]
        tg[gj] = jnp.where(sub == rj, t_j, tg[gj])
        if j == hb - 1:
            break
        for g in range(gj, ng):
            col = jnp.take_along_axis(packed[g * SUBLANES:(g + 1) * SUBLANES], base + j, axis=1)
            racc[g] = racc[g] + col * t_j
    dinv = jnp.concatenate(tg, axis=0)
    bd = jnp.concatenate([jnp.where(odd, 0.0, dinv), jnp.where(odd, dinv, 0.0)], axis=0)
    return [bd[:, :c], bd[:, c:]]


def _dn_kernel(q_ref, k_ref, v_ref, z_ref, sm_ref, cbuf_ref, s0_ref, cw_ref, par_ref, nw_ref,
               o_ref, cnew_ref, snew_ref,
               ext_ref, qn_ref, kn_ref, kb_ref, zz_ref, smp_ref, be_ref, ge_ref, gt_ref, oo_ref, s_ref,
               uc_ref, wc_ref, qe_ref, kt_ref, at_ref, rhs_ref, *, lin, lp, chunk):
    l = pl.program_id(1)
    nch = lp // chunk

    @pl.when(l == 0)
    def _():
        s_ref[...] = s0_ref[0]

    _fill_ext(ext_ref, cbuf_ref, ((q_ref, 0, DN_QK), (k_ref, DN_QK, DN_QK), (v_ref, 2 * DN_QK, DN_VW)), lin=lin, lp=lp)
    _pad_rows(zz_ref, z_ref, lin=lin, lp=lp)
    _pad_rows(smp_ref, sm_ref, lin=lin, lp=lp)

    sm = smp_ref[...]
    beta = _sigmoid(sm)
    g = -jnp.exp(par_ref[0:1, :]) * _softplus(sm + par_ref[1:2, :])
    valid = _row_valid(0, lp, lin)
    if valid is not None:
        beta = jnp.where(valid, beta, 0.0)
        g = jnp.where(valid, g, 0.0)
    gcum = _chunk_cumsum(g, chunk)
    be_ref[...] = _expand_heads(beta, DN_HEADS, DN_DK, SM_BETA)
    ge_ref[...] = _expand_heads(gcum, DN_HEADS, DN_DK, SM_DECAY)
    gt = gcum.T
    for c in range(nch):
        gt_ref[c] = gt[:, c * chunk:(c + 1) * chunk]

    assert ROWS == chunk
    for r0 in range(0, lp, ROWS):
        valid = _row_valid(r0, ROWS, lin)
        rows = slice(r0, r0 + ROWS)
        for h in range(DN_HEADS):
            hs = slice(h * DN_DK, (h + 1) * DN_DK)
            conv = lambda part: _conv_silu(ext_ref, r0, ROWS, slice(part * DN_QK + h * DN_DK, part * DN_QK + (h + 1) * DN_DK), cw_ref, None)
            l2n = lambda y: y * lax.rsqrt(jnp.sum(y * y, axis=-1, keepdims=True) + 1e-6)
            q = l2n(conv(0)) * (DN_DK ** -0.5)
            k = l2n(conv(1))
            v = conv(2)
            if valid is not None:
                q, k, v = (jnp.where(valid, t, 0.0) for t in (q, k, v))
            bt = be_ref[rows, hs]
            ge = ge_ref[rows, hs]
            eg = jnp.exp(ge)
            kb = k * bt
            qn_ref[rows, hs] = q
            kn_ref[rows, hs] = k
            kb_ref[rows, hs] = kb
            qe_ref[rows, hs] = q * eg
            kt_ref[rows, hs] = k * jnp.exp(ge[ROWS - 1:ROWS, :] - ge)
            rhs_ref[h, rows, 0:DN_DV] = v * bt
            rhs_ref[h, rows, DN_DV:2 * DN_DV] = kb * eg

    rid = lax.broadcasted_iota(jnp.int32, (chunk, chunk), 0)
    cid = lax.broadcasted_iota(jnp.int32, (chunk, chunk), 1)
    strict_lower = cid < rid
    lower = cid <= rid
    off = (rid >= chunk // 2) & (cid < chunk // 2)
    heads = range(DN_HEADS)
    hsl = [slice(h * DN_DK, (h + 1) * DN_DK) for h in heads]

    def prep_body(c2, carry):
        items = [(i, h) for i in range(2) for h in heads]
        cidx = [2 * c2 + i for i in range(2)]
        rs = [pl.ds(pl.multiple_of(c * chunk, chunk), chunk) for c in cidx]
        kh = [kn_ref[rs[c], hsl[h]] for c, h in items]
        decay = [jnp.exp(jnp.where(lower, ge_ref[rs[c], h * DN_DK:h * DN_DK + chunk]
                                   - gt_ref[cidx[c], SM_DECAY + h:SM_DECAY + h + 1, :], -jnp.inf)) for c, h in items]
        amat = [_dot_nt(kb_ref[rs[c], hsl[h]], kh[n]) * jnp.where(strict_lower, decay[n], 0.0)
                for n, (c, h) in enumerate(items)]
        for n, (c, h) in enumerate(items):
            at_ref[cidx[c] * DN_HEADS + h] = _dot_nt(qn_ref[rs[c], hsl[h]], kh[n]) * decay[n]
        dinv = sum((_diag_inv2(amat[n], amat[n + 1]) for n in range(0, len(items), 2)), [])
        inner = [_dot3(jnp.where(off, a, 0.0), d) for a, d in zip(amat, dinv)]
        tmat = [d - _dot3(d, i) for d, i in zip(dinv, inner)]
        sol = [_dot3(t, rhs_ref[h, rs[c], :]) for t, (c, h) in zip(tmat, items)]
        for s, (c, h) in zip(sol, items):
            uc_ref[rs[c], hsl[h]] = s[:, :DN_DV]
            wc_ref[rs[c], hsl[h]] = s[:, DN_DV:]
        return carry

    assert nch % 2 == 0
    lax.fori_loop(0, nch // 2, prep_body, 0)

    def scan_body(c, carry):
        r0 = pl.multiple_of(c * chunk, chunk)
        rs = pl.ds(r0, chunk)
        glast = ge_ref[pl.ds(r0 + chunk - 1, 1), :]
        sh = [s_ref[h] for h in heads]
        ws = [_dot(wc_ref[rs, hsl[h]], sh[h]) for h in heads]
        qs = [_dot(qe_ref[rs, hsl[h]], sh[h]) for h in heads]
        v_new = [uc_ref[rs, hsl[h]] - ws[h] for h in heads]
        o2 = [_dot(at_ref[c * DN_HEADS + h], v_new[h]) for h in heads]
        kv = [_dot_tn(kt_ref[rs, hsl[h]], v_new[h]) for h in heads]
        for h in heads:
            s_ref[h] = sh[h] * jnp.exp(glast[:, hsl[h]]) + kv[h]
            oo_ref[rs, hsl[h]] = _rms_rows(qs[h] + o2[h], nw_ref[...]) * _silu(zz_ref[rs, hsl[h]])
        return carry

    lax.fori_loop(0, nch, scan_body, 0)
    o_ref[0] = oo_ref[0:lin, :]

    @pl.when(l == pl.num_programs(1) - 1)
    def _():
        cnew_ref[0] = ext_ref[5 + lin:8 + lin, :]
        snew_ref[0] = s_ref[...]


def _deltanet(u3, cbuf, s0, cw, par, nw, *, lc, lp):
    b, l, _ = u3.shape
    nl = l // lc
    blk = lambda c0: pl.BlockSpec((1, lc, 512), lambda i, j: (i, j, c0 // 512))
    return pl.pallas_call(
        functools.partial(_dn_kernel, lin=lc, lp=lp, chunk=DN_CHUNK),
        grid=(b, nl),
        in_specs=[
            blk(C_DNQ), blk(C_DNK), blk(C_DNV), blk(C_DNZ),
            pl.BlockSpec((1, lc, LANES), lambda i, j: (i, j, C_SMALL // LANES)),
            pl.BlockSpec((1, CONV_WIDTH - 1, DN_CONV_DIM), lambda i, j: (i, 0, 0)),
            pl.BlockSpec((1, DN_HEADS, DN_DK, DN_DV), lambda i, j: (i, 0, 0, 0)),
            pl.BlockSpec((CONV_WIDTH, DN_CONV_DIM), lambda i, j: (0, 0)),
            pl.BlockSpec((SUBLANES, LANES), lambda i, j: (0, 0)),
            pl.BlockSpec((1, DN_DV), lambda i, j: (0, 0)),
        ],
        out_specs=[
            pl.BlockSpec((1, lc, DN_VW), lambda i, j: (i, j, 0)),
            pl.BlockSpec((1, CONV_WIDTH - 1, DN_CONV_DIM), lambda i, j: (i, 0, 0)),
            pl.BlockSpec((1, DN_HEADS, DN_DK, DN_DV), lambda i, j: (i, 0, 0, 0)),
        ],
        out_shape=[
            jax.ShapeDtypeStruct((b, l, DN_VW), F32),
            jax.ShapeDtypeStruct((b, CONV_WIDTH - 1, DN_CONV_DIM), F32),
            jax.ShapeDtypeStruct((b, DN_HEADS, DN_DK, DN_DV), F32),
        ],
        scratch_shapes=[
            pltpu.VMEM((8 + lp, DN_CONV_DIM), F32),
            pltpu.VMEM((lp, DN_QK), F32), pltpu.VMEM((lp, DN_QK), F32), pltpu.VMEM((lp, DN_VW), F32),
            pltpu.VMEM((lp, DN_VW), F32), pltpu.VMEM((lp, LANES), F32),
            pltpu.VMEM((lp, DN_QK), F32), pltpu.VMEM((lp, DN_QK), F32),
            pltpu.VMEM((lp // DN_CHUNK, LANES, DN_CHUNK), F32),
            pltpu.VMEM((lp, DN_VW), F32),
            pltpu.VMEM((DN_HEADS, DN_DK, DN_DV), F32),
            pltpu.VMEM((lp, DN_VW), F32), pltpu.VMEM((lp, DN_QK), F32),
            pltpu.VMEM((lp, DN_QK), F32), pltpu.VMEM((lp, DN_QK), F32),
            pltpu.VMEM((lp // DN_CHUNK * DN_HEADS, DN_CHUNK, DN_CHUNK), F32),
            pltpu.VMEM((DN_HEADS, lp, 2 * DN_DV), F32),
        ],
        compiler_params=_cparams(("parallel", "arbitrary")),
        name="deltanet",
    )(u3, u3, u3, u3, u3, cbuf, s0, cw, par, nw)


def _ssd_kernel(z_ref, x_ref, bc_ref, sm_ref, cbuf_ref, h0_ref, cw_ref, cb_ref, par_ref, nw_ref, dsk_ref,
                o_ref, cnew_ref, hnew_ref,
                ext_ref, xs_ref, bm_ref, cm_ref, zz_ref, smp_ref, ce_ref, de_ref, ct_ref, yy_ref, ht_ref,
                *, lin, lp, chunk):
    assert chunk == SSM_HEAD_DIM
    l = pl.program_id(1)
    nch = lp // chunk
    gn = SSM_GROUPS * SSM_STATE

    @pl.when(l == 0)
    def _():
        for g in range(SSM_GROUPS):
            hg = h0_ref[0, g * SSM_HPG:(g + 1) * SSM_HPG]
            ht_ref[g] = hg.reshape(SSM_HPG * SSM_HEAD_DIM, SSM_STATE).T

    _fill_ext(ext_ref, cbuf_ref, ((x_ref, 0, SSM_D_INNER), (bc_ref, SSM_D_INNER, 2 * gn)), lin=lin, lp=lp)
    _pad_rows(zz_ref, z_ref, lin=lin, lp=lp)
    _pad_rows(smp_ref, sm_ref, lin=lin, lp=lp)

    for r0 in range(0, lp, ROWS):
        valid = _row_valid(r0, ROWS, lin)
        for c0 in range(0, SSM_CONV_DIM, LANES):
            y = _conv_silu(ext_ref, r0, ROWS, slice(c0, c0 + LANES), cw_ref, cb_ref)
            if valid is not None:
                y = jnp.where(valid, y, 0.0)
            if c0 < SSM_D_INNER:
                xs_ref[r0:r0 + ROWS, c0:c0 + LANES] = y
            elif c0 < SSM_D_INNER + gn:
                bm_ref[r0:r0 + ROWS, c0 - SSM_D_INNER:c0 - SSM_D_INNER + LANES] = y
            else:
                cm_ref[r0:r0 + ROWS, c0 - SSM_D_INNER - gn:c0 - SSM_D_INNER - gn + LANES] = y

    sm = smp_ref[...]
    dt = _softplus(sm + par_ref[1:2, :])
    valid = _row_valid(0, lp, lin)
    if valid is not None:
        dt = jnp.where(valid, dt, 0.0)
    cum = _chunk_cumsum(dt * (-jnp.exp(par_ref[0:1, :])), chunk)
    ct = cum.T
    for c in range(nch):
        ct_ref[c] = ct[:, c * chunk:(c + 1) * chunk]

    spread = _expand_heads(jnp.concatenate([cum, dt], axis=0), SSM_HEADS, SSM_HEAD_DIM, SM_DT)
    ce_ref[...] = spread[:lp]
    de_ref[...] = spread[lp:]

    pw = 2 * SSM_HEAD_DIM
    row2 = lax.broadcasted_iota(jnp.int32, (chunk, 2 * chunk), 0)
    lane2 = lax.broadcasted_iota(jnp.int32, (chunk, 2 * chunk), 1)
    lower2 = (lane2 % chunk) <= row2
    first = lax.broadcasted_iota(jnp.int32, (chunk, pw), 1) < SSM_HEAD_DIM
    groups = range(SSM_GROUPS)
    gw_ = SSM_HPG * SSM_HEAD_DIM

    def chunk_body(c, carry):
        r0 = pl.multiple_of(c * chunk, chunk)
        rs = pl.ds(r0, chunk)
        bg = [bm_ref[rs, g * SSM_STATE:(g + 1) * SSM_STATE] for g in groups]
        cg = [cm_ref[rs, g * SSM_STATE:(g + 1) * SSM_STATE] for g in groups]
        ce = [ce_ref[rs, g * gw_:(g + 1) * gw_] for g in groups]
        de = [de_ref[rs, g * gw_:(g + 1) * gw_] for g in groups]
        xg = [xs_ref[rs, g * gw_:(g + 1) * gw_] for g in groups]
        ht = [ht_ref[g] for g in groups]
        cb = [_dot_nt(cg[g], bg[g]) for g in groups]
        ys = [_dot(cg[g], ht[g]) * jnp.exp(ce[g]) for g in groups]
        for g in groups:
            last = ce[g][chunk - 1:chunk, :]
            ht_ref[g] = ht[g] * jnp.exp(last) + _dot_tn(bg[g], xg[g] * (jnp.exp(last - ce[g]) * de[g]))
        for g in groups:
            cb2 = jnp.concatenate([cb[g], cb[g]], axis=1)
            xd = xg[g] * de[g]
            for p in range(SSM_HPG // 2):
                ps = slice(p * pw, (p + 1) * pw)
                hd = g * SSM_HPG + 2 * p
                ctp = ct_ref[c, SM_DT + hd:SM_DT + hd + 2, :]
                crp = jnp.concatenate([ctp[0:1], ctp[1:2]], axis=1)
                lm = jnp.exp(jnp.where(lower2, ce[g][:, ps] - crp, -jnp.inf))
                xdp = xd[:, ps]
                bd = jnp.concatenate([jnp.where(first, xdp, 0.0), jnp.where(first, 0.0, xdp)], axis=0)
                cols = slice(g * gw_ + p * pw, g * gw_ + (p + 1) * pw)
                yy_ref[rs, cols] = _dot(cb2 * lm, bd) + ys[g][:, ps] + dsk_ref[:, cols] * xg[g][:, ps]
        return carry

    lax.fori_loop(0, nch, chunk_body, 0)

    gw = SSM_D_INNER // SSM_GROUPS
    for r0 in range(0, lp, ROWS):
        n = min(ROWS, lin - r0)
        if n <= 0:
            break
        for g in range(SSM_GROUPS):
            cols = slice(g * gw, (g + 1) * gw)
            t = yy_ref[r0:r0 + ROWS, cols] * _silu(zz_ref[r0:r0 + ROWS, cols])
            t = _rms_rows(t, nw_ref[:, cols])
            o_ref[0, r0:r0 + n, cols] = t[:n]

    @pl.when(l == pl.num_programs(1) - 1)
    def _():
        cnew_ref[0] = ext_ref[5 + lin:8 + lin, :]
        for g in range(SSM_GROUPS):
            hnew_ref[0, g * SSM_HPG:(g + 1) * SSM_HPG] = ht_ref[g].T.reshape(SSM_HPG, SSM_HEAD_DIM, SSM_STATE)


def _ssd(u3, cbuf, h0, cw, cb, par, nw, d, *, lc, lp):
    b, l, _ = u3.shape
    nl = l // lc
    gn2 = 2 * SSM_GROUPS * SSM_STATE
    return pl.pallas_call(
        functools.partial(_ssd_kernel, lin=lc, lp=lp, chunk=SSM_CHUNK),
        grid=(b, nl),
        in_specs=[
            pl.BlockSpec((1, lc, SSM_D_INNER), lambda i, j: (i, j, C_SSZ // SSM_D_INNER)),
            pl.BlockSpec((1, lc, SSM_D_INNER), lambda i, j: (i, j, C_SSX // SSM_D_INNER)),
            pl.BlockSpec((1, lc, gn2), lambda i, j: (i, j, C_SSBC // gn2)),
            pl.BlockSpec((1, lc, LANES), lambda i, j: (i, j, C_SMALL // LANES)),
            pl.BlockSpec((1, CONV_WIDTH - 1, SSM_CONV_DIM), lambda i, j: (i, 0, 0)),
            pl.BlockSpec((1, SSM_HEADS, SSM_HEAD_DIM, SSM_STATE), lambda i, j: (i, 0, 0, 0)),
            pl.BlockSpec((CONV_WIDTH, SSM_CONV_DIM), lambda i, j: (0, 0)),
            pl.BlockSpec((1, SSM_CONV_DIM), lambda i, j: (0, 0)),
            pl.BlockSpec((SUBLANES, LANES), lambda i, j: (0, 0)),
            pl.BlockSpec((1, SSM_D_INNER), lambda i, j: (0, 0)),
            pl.BlockSpec((1, SSM_D_INNER), lambda i, j: (0, 0)),
        ],
        out_specs=[
            pl.BlockSpec((1, lc, SSM_D_INNER), lambda i, j: (i, j, 0)),
            pl.BlockSpec((1, CONV_WIDTH - 1, SSM_CONV_DIM), lambda i, j: (i, 0, 0)),
            pl.BlockSpec((1, SSM_HEADS, SSM_HEAD_DIM, SSM_STATE), lambda i, j: (i, 0, 0, 0)),
        ],
        out_shape=[
            jax.ShapeDtypeStruct((b, l, SSM_D_INNER), F32),
            jax.ShapeDtypeStruct((b, CONV_WIDTH - 1, SSM_CONV_DIM), F32),
            jax.ShapeDtypeStruct((b, SSM_HEADS, SSM_HEAD_DIM, SSM_STATE), F32),
        ],
        scratch_shapes=[
            pltpu.VMEM((8 + lp, SSM_CONV_DIM), F32),
            pltpu.VMEM((lp, SSM_D_INNER), F32),
            pltpu.VMEM((lp, SSM_GROUPS * SSM_STATE), F32), pltpu.VMEM((lp, SSM_GROUPS * SSM_STATE), F32),
            pltpu.VMEM((lp, SSM_D_INNER), F32), pltpu.VMEM((lp, LANES), F32),
            pltpu.VMEM((lp, SSM_D_INNER), F32), pltpu.VMEM((lp, SSM_D_INNER), F32),
            pltpu.VMEM((lp // SSM_CHUNK, LANES, SSM_CHUNK), F32),
            pltpu.VMEM((lp, SSM_D_INNER), F32),
            pltpu.VMEM((SSM_GROUPS, SSM_STATE, SSM_HPG * SSM_HEAD_DIM), F32),
        ],
        compiler_params=_cparams(("parallel", "arbitrary")),
        name="ssd",
    )(u3, u3, u3, u3, cbuf, h0, cw, cb, par, nw, d)


def _strict_upper_stack(n):
    j = lax.broadcasted_iota(jnp.int32, (2 * n, n), 0) % n
    s = lax.broadcasted_iota(jnp.int32, (2 * n, n), 1)
    return jnp.where(j > s, 1.0, 0.0).astype(BF16)


def _rev_excl_cumsum(la, uu):
    hi, lo = _split2(la)
    return jnp.dot(jnp.concatenate([hi, lo], axis=1), uu, preferred_element_type=F32)


def _sbp_kernel(bias_ref, q_ref, k_ref, v_ref, o_ref, *, tq, scale):
    h = pl.program_id(1)
    qi = pl.program_id(2)
    bias = bias_ref[h]
    q = (q_ref[0] * scale).astype(BF16)
    uu = _strict_upper_stack(tq)

    def sweep(blocks, carry, diagonal):
        c, acc = carry
        rows = [pl.ds(kj * tq if isinstance(kj, int) else pl.multiple_of(kj * tq, tq), tq) for kj in blocks]
        zs = [_dot_nt(q, k_ref[0, r, :]) + bias for r in rows]
        lss, las = [], []
        for z in zs:
            l1 = jnp.log(1.0 + jnp.exp(-jnp.abs(z)))
            lss.append(jnp.minimum(z, 0.0) - l1)
            las.append(-jnp.maximum(z, 0.0) - l1)
        if diagonal:
            valid = lax.broadcasted_iota(jnp.int32, (tq, tq), 1) < lax.broadcasted_iota(jnp.int32, (tq, tq), 0)
            las = [jnp.where(valid, la, 0.0) for la in las]
        survs = [_rev_excl_cumsum(la, uu) for la in las]
        for ls, la, surv, r in zip(lss, las, survs, rows):
            att = jnp.exp(ls + surv + c)
            if diagonal:
                att = jnp.where(valid, att, 0.0)
            acc = acc + _dot(att, v_ref[0, r, :])
            c = c + jnp.sum(la, axis=1, keepdims=True)
        return c, acc

    carry = (jnp.zeros((tq, 1), F32), jnp.zeros((tq, SB_HEAD_DIM), F32))
    carry = sweep([qi], carry, True)
    n4 = lax.shift_right_logical(qi, 2)
    carry = lax.fori_loop(0, n4, lambda t, cr: sweep([qi - 1 - 4 * t - i for i in range(4)], cr, False), carry)
    nxt = qi - 1 - 4 * n4
    carry = lax.fori_loop(0, lax.shift_right_logical(qi, 1) & 1, lambda t, cr: sweep([nxt, nxt - 1], cr, False), carry)
    carry = lax.fori_loop(0, qi & 1, lambda t, cr: sweep([0], cr, False), carry)
    o_ref[0] = carry[1]


def _sb_prompt(u3, bias, *, tq=256):
    b, l, _ = u3.shape
    tq = min(tq, l)
    kv = lambda c0: pl.BlockSpec((1, l, SB_HEAD_DIM), lambda i, h, j: (i, 0, c0 // SB_HEAD_DIM + h))
    return pl.pallas_call(
        functools.partial(_sbp_kernel, tq=tq, scale=SB_HEAD_DIM ** -0.5),
        grid=(b, SB_HEADS, l // tq),
        in_specs=[
            pl.BlockSpec(memory_space=pltpu.SMEM),
            pl.BlockSpec((1, tq, SB_HEAD_DIM), lambda i, h, j: (i, j, C_SBQ // SB_HEAD_DIM + h)),
            kv(C_SBK), kv(C_SBV),
        ],
        out_specs=pl.BlockSpec((1, tq, SB_HEAD_DIM), lambda i, h, j: (i, j, h)),
        out_shape=jax.ShapeDtypeStruct((b, l, SB_WIDTH), F32),
        compiler_params=_cparams(("parallel", "parallel", "arbitrary")),
        name="sb_prompt",
    )(bias, u3, u3, u3)


def _kvrows_kernel(k_ref, v_ref, ko_ref, vo_ref, *, tl):
    for h in range(SB_HEADS):
        rows = pl.ds(h, tl, stride=SB_HEADS)
        ko_ref[0, rows, :] = k_ref[0, :, h * SB_HEAD_DIM:(h + 1) * SB_HEAD_DIM]
        vo_ref[0, rows, :] = v_ref[0, :, h * SB_HEAD_DIM:(h + 1) * SB_HEAD_DIM]


def _kv_rows(u3, *, tl=512):
    b, l, _ = u3.shape
    tl = min(tl, l)
    src = lambda c0: pl.BlockSpec((1, tl, SB_WIDTH), lambda i, j: (i, j, c0 // SB_WIDTH))
    dst = pl.BlockSpec((1, tl * SB_HEADS, SB_HEAD_DIM), lambda i, j: (i, j, 0))
    shape = jax.ShapeDtypeStruct((b, l * SB_HEADS, SB_HEAD_DIM), F32)
    return pl.pallas_call(
        functools.partial(_kvrows_kernel, tl=tl),
        grid=(b, l // tl),
        in_specs=[src(C_SBK), src(C_SBV)],
        out_specs=[dst, dst],
        out_shape=[shape, shape],
        compiler_params=_cparams(("parallel", "parallel")),
        name="kv_rows",
    )(u3, u3)


def _sbs_kernel(pt_ref, q_ref, kc_ref, vc_ref, bias_ref, uu_ref, *rest, pp, tq, scale):
    k_refs = rest[:pp]
    v_refs = rest[pp:2 * pp]
    o_ref = rest[2 * pp]
    c_ref = rest[2 * pp + 1]
    p = pl.program_id(1)
    hq = q_ref.shape[1]
    ncol = kc_ref.shape[1]
    q = q_ref[0].astype(BF16)
    bias = bias_ref[...]
    row_head = lax.broadcasted_iota(jnp.int32, (hq, ncol), 0) // tq
    col = lax.broadcasted_iota(jnp.int32, (hq, ncol), 1)
    own = (col % SB_HEADS) == row_head

    def rev_cumsum(la):
        hi, lo = _split2(la)
        return jnp.dot(jnp.concatenate([hi, lo], axis=1), uu_ref[...], preferred_element_type=F32)

    @pl.when(p == 0)
    def _():
        z = _dot_nt(q, kc_ref[0]) * scale + bias
        t = lax.broadcasted_iota(jnp.int32, (hq, ncol), 0) % tq
        valid = own & ((col // SB_HEADS) < t)
        ls = _log_sigmoid(z)
        la = jnp.where(valid, ls - z, 0.0)
        att = jnp.where(valid, jnp.exp(ls + rev_cumsum(la)), 0.0)
        o_ref[0] = _dot(att, vc_ref[0])
        c_ref[...] = jnp.sum(la, axis=1, keepdims=True)

    z = jnp.concatenate([_dot_nt(q, k_refs[j][...]) for j in range(pp)], axis=0)
    z = z * scale + jnp.concatenate([bias] * pp, axis=0)
    valid = jnp.concatenate([own] * pp, axis=0)
    ls = _log_sigmoid(z)
    la = jnp.where(valid, ls - z, 0.0)
    surv = rev_cumsum(la)
    tot = jnp.sum(la, axis=1, keepdims=True)
    cur = c_ref[...]
    cs = []
    for j in range(pp):
        cs.append(cur)
        cur = cur + tot[j * hq:(j + 1) * hq]
    c_ref[...] = cur
    att = jnp.where(valid, jnp.exp(ls + surv + jnp.concatenate(cs, axis=0)), 0.0)
    acc = o_ref[0]
    for j in range(pp):
        acc = acc + _dot(att[j * hq:(j + 1) * hq], v_refs[j][...])
    o_ref[0] = acc


def _sb_sample(q_rows, k_cur, v_cur, bias_rows, cache_k, cache_v, page_table, layer, *, pp=16):
    b, hq, _ = q_rows.shape
    n_pages = page_table.shape[1]
    ncol = cache_k.shape[2]
    pp = math.gcd(pp, n_pages)
    tq = hq // SB_HEADS
    jj = lax.broadcasted_iota(jnp.int32, (2 * ncol, ncol), 0) % ncol
    ss = lax.broadcasted_iota(jnp.int32, (2 * ncol, ncol), 1)
    uu = jnp.where(jj > ss, 1.0, 0.0).astype(BF16)

    def page_spec(j):
        return pl.BlockSpec((None, None, ncol, SB_HEAD_DIM),
                            lambda i, p, pt: (layer, pt[i, n_pages - 1 - (p * pp + j)], 0, 0))

    grid_spec = pltpu.PrefetchScalarGridSpec(
        num_scalar_prefetch=1,
        grid=(b, n_pages // pp),
        in_specs=[
            pl.BlockSpec((1, hq, SB_HEAD_DIM), lambda i, p, pt: (i, 0, 0)),
            pl.BlockSpec((1, ncol, SB_HEAD_DIM), lambda i, p, pt: (i, 0, 0)),
            pl.BlockSpec((1, ncol, SB_HEAD_DIM), lambda i, p, pt: (i, 0, 0)),
            pl.BlockSpec((hq, ncol), lambda i, p, pt: (0, 0)),
            pl.BlockSpec((2 * ncol, ncol), lambda i, p, pt: (0, 0)),
        ] + [page_spec(j) for j in range(pp)] + [page_spec(j) for j in range(pp)],
        out_specs=pl.BlockSpec((1, hq, SB_HEAD_DIM), lambda i, p, pt: (i, 0, 0)),
        scratch_shapes=[pltpu.VMEM((hq, 1), F32)],
    )
    return pl.pallas_call(
        functools.partial(_sbs_kernel, pp=pp, tq=tq, scale=SB_HEAD_DIM ** -0.5),
        grid_spec=grid_spec,
        out_shape=jax.ShapeDtypeStruct((b, hq, SB_HEAD_DIM), F32),
        compiler_params=_cparams(("parallel", "arbitrary")),
        name="sb_sample",
    )(page_table, q_rows, k_cur, v_cur, bias_rows, uu, *([cache_k] * pp), *([cache_v] * pp))


def _mixout_kernel(op_ref, od_ref, os_ref, oa_ref, gt_ref, wb_ref, wo_ref, x_ref, g_ref, o_ref):
    gate = lambda k: gt_ref[:, k * D_MODEL:(k + 1) * D_MODEL].astype(F32)
    br = lambda o_ref_, r0, n: _dot(o_ref_[...], wb_ref[r0:r0 + n, :])
    acc = gate(0) * br(op_ref, 0, POOL_WIDTH)
    acc = acc + gate(1) * br(od_ref, POOL_WIDTH, DN_VW)
    acc = acc + gate(2) * br(os_ref, POOL_WIDTH + DN_VW, SB_WIDTH)
    acc = acc + gate(3) * br(oa_ref, POOL_WIDTH + DN_VW + SB_WIDTH, SSM_D_INNER)
    mix = jnp.dot(acc.astype(BF16), wo_ref[...], preferred_element_type=F32)
    o_ref[...] = x_ref[...] + _rms_rows(mix, g_ref[...])


def _mixout(o_pool, o_dn, o_sb, o_ss, gates, w_br_all, w_out_all, layer, x2, g, *, tm):
    m = x2.shape[0]
    rows = lambda w: pl.BlockSpec((tm, w), lambda i: (i, 0))
    resident = lambda r: pl.BlockSpec((None, r, D_MODEL), lambda i: (layer, 0, 0), pipeline_mode=pl.Buffered(1))
    return pl.pallas_call(
        _mixout_kernel,
        grid=(m // tm,),
        in_specs=[rows(POOL_WIDTH), rows(DN_VW), rows(SB_WIDTH), rows(SSM_D_INNER), rows(N_BRANCH * D_MODEL),
                  resident(w_br_all.shape[1]), resident(D_MODEL), rows(D_MODEL),
                  pl.BlockSpec((1, D_MODEL), lambda i: (0, 0))],
        out_specs=rows(D_MODEL),
        out_shape=jax.ShapeDtypeStruct((m, D_MODEL), F32),
        compiler_params=_cparams(("parallel",)),
        name="mixout",
    )(o_pool, o_dn, o_sb, o_ss, gates, w_br_all, w_out_all, x2, g)


def _mlp_kernel(x_ref, g1_ref, wu_ref, wd_ref, g2_ref, o_ref, h_ref, acc_ref, *, tm):
    f = pl.program_id(1)

    @pl.when(f == 0)
    def _():
        _norm_rows_to(h_ref, x_ref, g1_ref, tm)
        acc_ref[...] = jnp.zeros(acc_ref.shape, F32)

    a = jnp.dot(h_ref[...], wu_ref[...], preferred_element_type=F32)
    a = jnp.square(jnp.maximum(a, 0.0)).astype(BF16)
    acc_ref[...] += jnp.dot(a, wd_ref[...], preferred_element_type=F32)

    @pl.when(f == pl.num_programs(1) - 1)
    def _():
        o_ref[...] = x_ref[...] + _rms_rows(acc_ref[...], g2_ref[...])


def _mlp(x2, g1, wu_all, wd_all, layer, g2, *, tm, tf=1024):
    m = x2.shape[0]
    return pl.pallas_call(
        functools.partial(_mlp_kernel, tm=tm),
        grid=(m // tm, D_FF // tf),
        in_specs=[
            pl.BlockSpec((tm, D_MODEL), lambda i, f: (i, 0)),
            pl.BlockSpec((1, D_MODEL), lambda i, f: (0, 0)),
            pl.BlockSpec((None, D_MODEL, tf), lambda i, f: (layer, 0, f)),
            pl.BlockSpec((None, tf, D_MODEL), lambda i, f: (layer, f, 0)),
            pl.BlockSpec((1, D_MODEL), lambda i, f: (0, 0)),
        ],
        out_specs=pl.BlockSpec((tm, D_MODEL), lambda i, f: (i, 0)),
        out_shape=jax.ShapeDtypeStruct((m, D_MODEL), F32),
        scratch_shapes=[pltpu.VMEM((tm, D_MODEL), BF16), pltpu.VMEM((tm, D_MODEL), F32)],
        compiler_params=_cparams(("parallel", "arbitrary")),
        name="mlp",
    )(x2, g1, wu_all, wd_all, g2)


def _lane_rows(vals, offset):
    out = jnp.zeros((vals.shape[0], LANES), F32)
    for k in range(3):
        o = offset + k * SM_REP
        out = out.at[:, o:o + vals.shape[1]].set(vals.astype(F32))
    return out


def _prepare_params(p):
    depth = p["w_in"].shape[0]
    wt = jnp.transpose(p["w_in"], (0, 2, 1))
    small = wt[:, _O_DNB:_O_SB], wt[:, _O_SSDT:_O_GATE]
    gap = jnp.zeros((depth, SM_REP - SM_GROUP, D_MODEL), wt.dtype)
    w_u = jnp.concatenate(
        [wt[:, :_O_DNB], wt[:, _O_SB:_O_SSDT], *small, gap, *small, gap, *small,
         jnp.zeros((depth, NU - C_SMALL - 2 * SM_REP - SM_GROUP, D_MODEL), wt.dtype)], axis=1).astype(BF16)
    w_g = wt[:, _O_GATE:_O_END].astype(BF16)
    zrow = jnp.zeros((depth, LANES), F32)
    par = lambda a_log, dt_bias, off: jnp.stack(
        [_lane_rows(a_log, off), _lane_rows(dt_bias, off)] + [zrow] * (SUBLANES - 2), axis=1)
    return dict(
        w_u=w_u, w_g=w_g,
        n_mix_pre=p["norm_mix_pre"], n_mix_post=p["norm_mix_post"],
        n_mlp_pre=p["norm_mlp_pre"], n_mlp_post=p["norm_mlp_post"],
        pool_w=p["pool_w"].astype(BF16), pool_scale=p["pool_scale"],
        dn_conv_w=p["dn_conv_w"], dn_par=par(p["dn_a_log"], p["dn_dt_bias"], SM_DECAY), dn_norm_w=p["dn_norm_w"],
        sb_bias=p["sb_bias"],
        ssm_conv_w=p["ssm_conv_w"], ssm_conv_b=p["ssm_conv_b"],
        ss_par=par(p["ssm_a_log"], p["ssm_dt_bias"], SM_DT), ssm_norm_w=p["ssm_norm_w"],
        ssm_d=jnp.repeat(p["ssm_d"].astype(F32), SSM_HEAD_DIM, axis=1),
        w_branch=p["w_branch"].astype(BF16), w_out=p["w_out"].astype(BF16),
        w_up=p["w_up"].astype(BF16), w_down=p["w_down"].astype(BF16),
    )


def _trunk_layer(x, states, pw, layer, *, past=None):
    b, l, _ = x.shape
    m = b * l
    pool_buf, dn_conv, dn_s, ssm_conv, ssm_h = states
    prompt = past is None
    tm_big = min(m, 1024)
    tm = min(m, 512)
    lc = min(l, 256)
    lp = max(lc, 2 * DN_CHUNK)
    row = lambda name: pw[name][layer].reshape(1, -1)

    x2 = x.reshape(m, D_MODEL)
    u2 = _inproj(x2, row("n_mix_pre"), pw["w_u"], layer, tm=tm_big)
    gates = _inproj(x2, row("n_mix_pre"), pw["w_g"], layer, tm=tm_big, gate=True)
    u3 = u2.reshape(b, l, NU)

    pos0 = 0 if prompt else past[2].shape[1] * (past[0].shape[2] // SB_HEADS)
    o_pool, pool_new = _pool(u3, pool_buf, pw["pool_w"][layer], row("pool_scale"), pos0=pos0)
    o_dn, dn_conv_new, dn_s_new = _deltanet(u3, dn_conv, dn_s, pw["dn_conv_w"][layer], pw["dn_par"][layer],
                                            row("dn_norm_w"), lc=lc, lp=lp)
    o_ss, ss_conv_new, ss_h_new = _ssd(u3, ssm_conv, ssm_h, pw["ssm_conv_w"][layer], row("ssm_conv_b"),
                                       pw["ss_par"][layer], row("ssm_norm_w"), row("ssm_d"), lc=lc, lp=lp)

    k_rows, v_rows = _kv_rows(u3)
    sb_bias = pw["sb_bias"][layer]
    if prompt:
        o_sb = _sb_prompt(u3, sb_bias)
    else:
        cache_k, cache_v, page_table = past
        ncol = cache_k.shape[2]
        q = u3[:, :, C_SBQ:C_SBQ + SB_WIDTH].reshape(b, l, SB_HEADS, SB_HEAD_DIM)
        q_rows = jnp.transpose(q, (0, 2, 1, 3)).reshape(b, SB_HEADS * l, SB_HEAD_DIM)
        k_cur = jnp.pad(k_rows, ((0, 0), (0, ncol - l * SB_HEADS), (0, 0)))
        v_cur = jnp.pad(v_rows, ((0, 0), (0, ncol - l * SB_HEADS), (0, 0)))
        bias_rows = jnp.broadcast_to(jnp.repeat(sb_bias, l)[:, None], (SB_HEADS * l, ncol)).astype(F32)
        acc = _sb_sample(q_rows, k_cur, v_cur, bias_rows, cache_k, cache_v, page_table, layer)
        o_sb = jnp.transpose(acc.reshape(b, SB_HEADS, l, SB_HEAD_DIM), (0, 2, 1, 3)).reshape(b, l, SB_WIDTH)

    x2 = _mixout(o_pool.reshape(m, -1), o_dn.reshape(m, -1), o_sb.reshape(m, -1), o_ss.reshape(m, -1),
                 gates, pw["w_branch"], pw["w_out"], layer, x2, row("n_mix_post"), tm=min(m, 256))
    x2 = _mlp(x2, row("n_mlp_pre"), pw["w_up"], pw["w_down"], layer, row("n_mlp_post"), tm=tm)
    new_states = (k_rows.reshape(b, l, SB_HEADS, SB_HEAD_DIM), v_rows.reshape(b, l, SB_HEADS, SB_HEAD_DIM),
                  pool_new, dn_conv_new, dn_s_new, ss_conv_new, ss_h_new)
    return x2.reshape(b, l, D_MODEL), new_states


def kernel(x_prompt, x_sample, cache_sb_k, cache_sb_v, state_pool, state_dn_conv, state_dn_s, state_ssm_conv, state_ssm_h, page_table, norm_mix_pre, norm_mix_post, norm_mlp_pre, norm_mlp_post, w_in, pool_w, pool_scale, dn_conv_w, dn_a_log, dn_dt_bias, dn_norm_w, sb_bias, ssm_conv_w, ssm_conv_b, ssm_a_log, ssm_dt_bias, ssm_d, ssm_norm_w, w_branch, w_out, w_up, w_down):
    pw = _prepare_params(dict(
        norm_mix_pre=norm_mix_pre, norm_mix_post=norm_mix_post, norm_mlp_pre=norm_mlp_pre,
        norm_mlp_post=norm_mlp_post, w_in=w_in, pool_w=pool_w, pool_scale=pool_scale,
        dn_conv_w=dn_conv_w, dn_a_log=dn_a_log, dn_dt_bias=dn_dt_bias, dn_norm_w=dn_norm_w,
        sb_bias=sb_bias, ssm_conv_w=ssm_conv_w, ssm_conv_b=ssm_conv_b, ssm_a_log=ssm_a_log,
        ssm_dt_bias=ssm_dt_bias, ssm_d=ssm_d, ssm_norm_w=ssm_norm_w, w_branch=w_branch,
        w_out=w_out, w_up=w_up, w_down=w_down))
    depth = w_in.shape[0]
    bp = x_prompt.shape[0]
    dt_ = x_prompt.dtype
    zero_states = (jnp.zeros((bp, POOL_BUF, POOL_WIDTH), dt_),
                   jnp.zeros((bp, CONV_WIDTH - 1, DN_CONV_DIM), dt_),
                   jnp.zeros((bp, DN_HEADS, DN_DK, DN_DV), dt_),
                   jnp.zeros((bp, CONV_WIDTH - 1, SSM_CONV_DIM), dt_),
                   jnp.zeros((bp, SSM_HEADS, SSM_HEAD_DIM, SSM_STATE), dt_))
    n_pool, page = cache_sb_k.shape[1], cache_sb_k.shape[2]
    cache_k = cache_sb_k.reshape(depth, n_pool, page * SB_HEADS, SB_HEAD_DIM)
    cache_v = cache_sb_v.reshape(depth, n_pool, page * SB_HEADS, SB_HEAD_DIM)
    y_prompt, y_sample = x_prompt, x_sample
    new_p, new_s = [], []
    for layer in range(depth):
        y_prompt, st_p = _trunk_layer(y_prompt, zero_states, pw, layer)
        sample_states = (state_pool[layer], state_dn_conv[layer], state_dn_s[layer],
                         state_ssm_conv[layer], state_ssm_h[layer])
        y_sample, st_s = _trunk_layer(y_sample, sample_states, pw, layer, past=(cache_k, cache_v, page_table))
        new_p.append(st_p)
        new_s.append(st_s)
    outs_p = [jnp.stack(t) for t in zip(*new_p)]
    outs_s = [jnp.stack(t) for t in zip(*new_s)]
    return (y_prompt, y_sample, *outs_p, *outs_s)
```

```python
import functools
import math

import jax
import jax.numpy as jnp
from jax import lax
from jax.experimental import pallas as pl
from jax.experimental.pallas import tpu as pltpu

F32 = jnp.float32
BF16 = jnp.bfloat16

D_MODEL = 2048
N_BRANCH = 4
RMS_EPS = 1e-6
CONV_WIDTH = 4
POOL_WINDOWS = (2, 4, 8, 16)
POOL_WIDTH = D_MODEL // 4
POOL_GW = POOL_WIDTH // 4
POOL_BUF = max(POOL_WINDOWS) - 1
DN_HEADS = 4
DN_DK = 128
DN_DV = 128
DN_QK = DN_HEADS * DN_DK
DN_VW = DN_HEADS * DN_DV
DN_CONV_DIM = 2 * DN_QK + DN_VW
DN_CHUNK = 64
SB_HEADS = 4
SB_HEAD_DIM = 128
SB_WIDTH = SB_HEADS * SB_HEAD_DIM
SSM_D_INNER = D_MODEL // 2
SSM_HEAD_DIM = 64
SSM_HEADS = SSM_D_INNER // SSM_HEAD_DIM
SSM_GROUPS = 2
SSM_HPG = SSM_HEADS // SSM_GROUPS
SSM_STATE = 128
SSM_CONV_DIM = SSM_D_INNER + 2 * SSM_GROUPS * SSM_STATE
SSM_CHUNK = 64
D_FF = 4 * D_MODEL

LANES = 128
SUBLANES = 8

C_POOL = 0
C_DNQ, C_DNK, C_DNV, C_DNZ = 512, 1024, 1536, 2048
C_SBQ, C_SBK, C_SBV = 2560, 3072, 3584
C_SSZ, C_SSX, C_SSBC = 4096, 5120, 6144
C_SMALL = 6656
NU = 7168
_O_DNB = POOL_WIDTH + DN_CONV_DIM + DN_VW
_O_SB = _O_DNB + 2 * DN_HEADS
_O_SSDT = _O_SB + 3 * SB_WIDTH + SSM_D_INNER + SSM_CONV_DIM
_O_GATE = _O_SSDT + SSM_HEADS
_O_END = _O_GATE + N_BRANCH * D_MODEL
SM_BETA, SM_DECAY, SM_DT = 0, DN_HEADS, 2 * DN_HEADS
SM_GROUP = 2 * DN_HEADS + SSM_HEADS
SM_REP = 32

VMEM_LIMIT_MB = 56


def _cparams(sem, vmem_mb=VMEM_LIMIT_MB):
    return pltpu.CompilerParams(dimension_semantics=sem, vmem_limit_bytes=vmem_mb * 1024 * 1024)


def _sigmoid(x):
    return 1.0 / (1.0 + jnp.exp(-x))


def _silu(x):
    hx = 0.5 * x
    return hx + hx * jnp.tanh(hx)


def _softplus(x):
    return jnp.maximum(x, 0.0) + jnp.log1p(jnp.exp(-jnp.abs(x)))


def _log_sigmoid(x):
    return jnp.minimum(x, 0.0) - jnp.log1p(jnp.exp(-jnp.abs(x)))


def _dot(a, b):
    return jnp.dot(a.astype(BF16), b.astype(BF16), preferred_element_type=F32)


def _dot_nt(a, b):
    return lax.dot_general(a.astype(BF16), b.astype(BF16), (((1,), (1,)), ((), ())), preferred_element_type=F32)


def _dot_tn(a, b):
    return lax.dot_general(a.astype(BF16), b.astype(BF16), (((0,), (0,)), ((), ())), preferred_element_type=F32)


def _split2(a):
    hi = a.astype(BF16)
    lo = (a - hi.astype(F32)).astype(BF16)
    return hi, lo


def _expand_heads(src, n_heads, width, lane0):
    k = lax.broadcasted_iota(jnp.int32, (LANES, n_heads * width), 0)
    n = lax.broadcasted_iota(jnp.int32, (LANES, n_heads * width), 1)
    sel = jnp.where((k % SM_REP - lane0 == n // width) & (k < 3 * SM_REP), 1.0, 0.0).astype(BF16)
    lane = lax.broadcasted_iota(jnp.int32, src.shape, 1)
    hi = src.astype(BF16).astype(F32)
    r1 = src - hi
    mid = r1.astype(BF16).astype(F32)
    lo = r1 - mid
    pieces = jnp.where(lane < SM_REP, hi, jnp.where(lane < 2 * SM_REP, mid, lo))
    return jnp.dot(pieces.astype(BF16), sel, preferred_element_type=F32)


def _dot3(a, b):
    ah, al = _split2(a)
    bh, bl = _split2(b)
    d = lambda x, y: jnp.dot(x, y, preferred_element_type=F32)
    return d(ah, bh) + (d(ah, bl) + d(al, bh))


def _rms_rows(x, g):
    return x * lax.rsqrt(jnp.mean(x * x, axis=-1, keepdims=True) + RMS_EPS) * g


def _norm_rows_to(h_ref, x_ref, g_ref, tm):
    ch = min(tm, 256)

    def body(r, c):
        rs = pl.ds(pl.multiple_of(r * ch, ch), ch)
        h_ref[rs, :] = _rms_rows(x_ref[rs, :], g_ref[...]).astype(h_ref.dtype)
        return c

    lax.fori_loop(0, tm // ch, body, 0)


def _inproj_kernel(x_ref, g_ref, w_ref, o_ref, h_ref, *, tm, gate):
    @pl.when(pl.program_id(1) == 0)
    def _():
        _norm_rows_to(h_ref, x_ref, g_ref, tm)

    acc = _dot_nt(h_ref[...], w_ref[...])
    o_ref[...] = (_sigmoid(acc) if gate else acc).astype(o_ref.dtype)


def _inproj(x2, g, w_all, layer, *, tm, tn=1024, gate=False):
    m = x2.shape[0]
    n = w_all.shape[1]
    return pl.pallas_call(
        functools.partial(_inproj_kernel, tm=tm, gate=gate),
        grid=(m // tm, n // tn),
        in_specs=[
            pl.BlockSpec((tm, D_MODEL), lambda i, j: (i, 0)),
            pl.BlockSpec((1, D_MODEL), lambda i, j: (0, 0)),
            pl.BlockSpec((None, tn, D_MODEL), lambda i, j: (layer, j, 0)),
        ],
        out_specs=pl.BlockSpec((tm, tn), lambda i, j: (i, j)),
        out_shape=jax.ShapeDtypeStruct((m, n), BF16 if gate else F32),
        scratch_shapes=[pltpu.VMEM((tm, D_MODEL), BF16)],
        compiler_params=_cparams(("parallel", "arbitrary")),
        name="inproj_gate" if gate else "inproj",
    )(x2, g, w_all)


def _pool_kernel(u_ref, buf_ref, w_ref, sc_ref, o_ref, new_ref, ext_ref, *, lin, lp, pos0):
    ext_ref[0:16, :] = jnp.zeros((16, POOL_WIDTH), F32)
    ext_ref[1:16, :] = buf_ref[0]
    if lin < lp:
        ext_ref[16:16 + lp, :] = jnp.zeros((lp, POOL_WIDTH), F32)
    ext_ref[16:16 + lin, :] = u_ref[0]
    ch = min(lp, 256)
    for c0 in range(0, lp, ch):
        pos = pos0 + c0 + lax.broadcasted_iota(jnp.int32, (ch, 1), 0)
        for gi, w in enumerate(POOL_WINDOWS):
            cols = slice(gi * POOL_GW, (gi + 1) * POOL_GW)
            s = ext_ref[16 + c0:16 + c0 + ch, cols]
            tot = s
            for k in range(1, w):
                tot = tot + ext_ref[16 + c0 - k:16 + c0 - k + ch, cols]
            cnt = jnp.minimum(pos + 1, w).astype(F32)
            y = tot / cnt - s
            yo = _dot(y, w_ref[gi]) * sc_ref[:, cols]
            n = min(ch, lin - c0)
            o_ref[0, c0:c0 + n, cols] = yo[:n]
    new_ref[0] = ext_ref[1 + lin:16 + lin, :]


def _pool(u3, buf, w, sc, *, pos0):
    b, lin, _ = u3.shape
    lp = max(lin, SUBLANES)
    return pl.pallas_call(
        functools.partial(_pool_kernel, lin=lin, lp=lp, pos0=pos0),
        grid=(b,),
        in_specs=[
            pl.BlockSpec((1, lin, POOL_WIDTH), lambda i: (i, 0, C_POOL // POOL_WIDTH)),
            pl.BlockSpec((1, POOL_BUF, POOL_WIDTH), lambda i: (i, 0, 0)),
            pl.BlockSpec((4, POOL_GW, POOL_GW), lambda i: (0, 0, 0)),
            pl.BlockSpec((1, POOL_WIDTH), lambda i: (0, 0)),
        ],
        out_specs=[
            pl.BlockSpec((1, lin, POOL_WIDTH), lambda i: (i, 0, 0)),
            pl.BlockSpec((1, POOL_BUF, POOL_WIDTH), lambda i: (i, 0, 0)),
        ],
        out_shape=[
            jax.ShapeDtypeStruct((b, lin, POOL_WIDTH), F32),
            jax.ShapeDtypeStruct((b, POOL_BUF, POOL_WIDTH), F32),
        ],
        scratch_shapes=[pltpu.VMEM((16 + lp, POOL_WIDTH), F32)],
        compiler_params=_cparams(("parallel",)),
        name="pool",
    )(u3, buf, w, sc)


ROWS = 64


def _fill_ext(ext_ref, cbuf_ref, parts, *, lin, lp):
    l = pl.program_id(1)

    @pl.when(l == 0)
    def _():
        ext_ref[5:8, :] = cbuf_ref[0]

    @pl.when(l > 0)
    def _():
        ext_ref[5:8, :] = ext_ref[5 + lin:8 + lin, :]

    if lin < lp:
        ext_ref[8:8 + lp, :] = jnp.zeros((lp, ext_ref.shape[1]), F32)
    for ref, c0, width in parts:
        ext_ref[8:8 + lin, c0:c0 + width] = ref[0]


def _conv_silu(ext_ref, r0, n, cols, w_ref, b_ref):
    acc = None
    for i in range(CONV_WIDTH):
        part = ext_ref[5 + r0 + i:5 + r0 + i + n, cols] * w_ref[i:i + 1, cols]
        acc = part if acc is None else acc + part
    if b_ref is not None:
        acc = acc + b_ref[:, cols]
    return _silu(acc)


def _pad_rows(dst_ref, src_ref, *, lin, lp):
    if lin < lp:
        dst_ref[...] = jnp.zeros(dst_ref.shape, F32)
    dst_ref[0:lin, :] = src_ref[0]


def _row_valid(r0, n, lin):
    if r0 + n <= lin:
        return None
    return (r0 + lax.broadcasted_iota(jnp.int32, (n, 1), 0)) < lin


def _chunk_cumsum(x, chunk):
    rin = lax.broadcasted_iota(jnp.int32, x.shape, 0) % chunk
    s = 1
    while s < chunk:
        x = x + jnp.where(rin >= s, pltpu.roll(x, s, 0), 0.0)
        s *= 2
    return x


def _diag_inv2(a0, a1):
    c = a0.shape[0]
    hb = c // 2
    ng = hb // SUBLANES
    sub = lax.broadcasted_iota(jnp.int32, (SUBLANES, 2 * c), 0)
    lane = lax.broadcasted_iota(jnp.int32, (SUBLANES, 2 * c), 1)
    base = (lane // hb) * hb
    lmod = lane - base
    a01 = jnp.concatenate([a0, a1], axis=1)
    odd = ((lax.broadcasted_iota(jnp.int32, (hb, 2 * c), 1) // hb) % 2) == 1
    packed = jnp.where(odd, a01[hb:], a01[:hb])
    racc = [jnp.zeros((SUBLANES, 2 * c), F32) for _ in range(ng)]
    tg = [jnp.zeros((SUBLANES, 2 * c), F32) for _ in range(ng)]
    for j in range(hb):
        gj, rj = divmod(j, SUBLANES)
        t_j = jnp.where(lmod[0:1] == j, 1.0, 0.0) - racc[gj][rj:rj + 1, :]
        tg[gj] = jnp.where(sub == rj, t_j, tg[gj])
        if j == hb - 1:
            break
        for g in range(gj, ng):
            col = jnp.take_along_axis(packed[g * SUBLANES:(g + 1) * SUBLANES], base + j, axis=1)
            racc[g] = racc[g] + col * t_j
    dinv = jnp.concatenate(tg, axis=0)
    bd = jnp.concatenate([jnp.where(odd, 0.0, dinv), jnp.where(odd, dinv, 0.0)], axis=0)
    return [bd[:, :c], bd[:, c:]]


def _dn_kernel(q_ref, k_ref, v_ref, z_ref, sm_ref, cbuf_ref, s0_ref, cw_ref, par_ref, nw_ref,
               o_ref, cnew_ref, snew_ref,
               ext_ref, qn_ref, kn_ref, kb_ref, zz_ref, smp_ref, be_ref, ge_ref, gt_ref, oo_ref, s_ref,
               uc_ref, wc_ref, qe_ref, kt_ref, at_ref, rhs_ref, *, lin, lp, chunk):
    l = pl.program_id(1)
    nch = lp // chunk

    @pl.when(l == 0)
    def _():
        s_ref[...] = s0_ref[0]

    _fill_ext(ext_ref, cbuf_ref, ((q_ref, 0, DN_QK), (k_ref, DN_QK, DN_QK), (v_ref, 2 * DN_QK, DN_VW)), lin=lin, lp=lp)
    _pad_rows(zz_ref, z_ref, lin=lin, lp=lp)
    _pad_rows(smp_ref, sm_ref, lin=lin, lp=lp)

    sm = smp_ref[...]
    beta = _sigmoid(sm)
    g = -jnp.exp(par_ref[0:1, :]) * _softplus(sm + par_ref[1:2, :])
    valid = _row_valid(0, lp, lin)
    if valid is not None:
        beta = jnp.where(valid, beta, 0.0)
        g = jnp.where(valid, g, 0.0)
    gcum = _chunk_cumsum(g, chunk)
    be_ref[...] = _expand_heads(beta, DN_HEADS, DN_DK, SM_BETA)
    ge_ref[...] = _expand_heads(gcum, DN_HEADS, DN_DK, SM_DECAY)
    gt = gcum.T
    for c in range(nch):
        gt_ref[c] = gt[:, c * chunk:(c + 1) * chunk]

    assert ROWS == chunk
    for r0 in range(0, lp, ROWS):
        valid = _row_valid(r0, ROWS, lin)
        rows = slice(r0, r0 + ROWS)
        for h in range(DN_HEADS):
            hs = slice(h * DN_DK, (h + 1) * DN_DK)
            conv = lambda part: _conv_silu(ext_ref, r0, ROWS, slice(part * DN_QK + h * DN_DK, part * DN_QK + (h + 1) * DN_DK), cw_ref, None)
            l2n = lambda y: y * lax.rsqrt(jnp.sum(y * y, axis=-1, keepdims=True) + 1e-6)
            q = l2n(conv(0)) * (DN_DK ** -0.5)
            k = l2n(conv(1))
            v = conv(2)
            if valid is not None:
                q, k, v = (jnp.where(valid, t, 0.0) for t in (q, k, v))
            bt = be_ref[rows, hs]
            ge = ge_ref[rows, hs]
            eg = jnp.exp(ge)
            kb = k * bt
            qn_ref[rows, hs] = q
            kn_ref[rows, hs] = k
            kb_ref[rows, hs] = kb
            qe_ref[rows, hs] = q * eg
            kt_ref[rows, hs] = k * jnp.exp(ge[ROWS - 1:ROWS, :] - ge)
            rhs_ref[h, rows, 0:DN_DV] = v * bt
            rhs_ref[h, rows, DN_DV:2 * DN_DV] = kb * eg

    rid = lax.broadcasted_iota(jnp.int32, (chunk, chunk), 0)
    cid = lax.broadcasted_iota(jnp.int32, (chunk, chunk), 1)
    strict_lower = cid < rid
    lower = cid <= rid
    off = (rid >= chunk // 2) & (cid < chunk // 2)
    heads = range(DN_HEADS)
    hsl = [slice(h * DN_DK, (h + 1) * DN_DK) for h in heads]

    per = 4 if nch % 4 == 0 else 2

    def prep_body(c2, carry):
        items = [(i, h) for i in range(per) for h in heads]
        cidx = [per * c2 + i for i in range(per)]
        rs = [pl.ds(pl.multiple_of(c * chunk, chunk), chunk) for c in cidx]
        kh = [kn_ref[rs[c], hsl[h]] for c, h in items]
        decay = [jnp.exp(jnp.where(lower, ge_ref[rs[c], h * DN_DK:h * DN_DK + chunk]
                                   - gt_ref[cidx[c], SM_DECAY + h:SM_DECAY + h + 1, :], -jnp.inf)) for c, h in items]
        amat = [_dot_nt(kb_ref[rs[c], hsl[h]], kh[n]) * jnp.where(strict_lower, decay[n], 0.0)
                for n, (c, h) in enumerate(items)]
        for n, (c, h) in enumerate(items):
            at_ref[cidx[c] * DN_HEADS + h] = _dot_nt(qn_ref[rs[c], hsl[h]], kh[n]) * decay[n]
        dinv = sum((_diag_inv2(amat[n], amat[n + 1]) for n in range(0, len(items), 2)), [])
        inner = [_dot3(jnp.where(off, a, 0.0), d) for a, d in zip(amat, dinv)]
        tmat = [d - _dot3(d, i) for d, i in zip(dinv, inner)]
        sol = [_dot3(t, rhs_ref[h, rs[c], :]) for t, (c, h) in zip(tmat, items)]
        for s, (c, h) in zip(sol, items):
            uc_ref[rs[c], hsl[h]] = s[:, :DN_DV]
            wc_ref[rs[c], hsl[h]] = s[:, DN_DV:]
        return carry

    assert nch % per == 0
    lax.fori_loop(0, nch // per, prep_body, 0)

    def scan_body(c, carry):
        r0 = pl.multiple_of(c * chunk, chunk)
        rs = pl.ds(r0, chunk)
        glast = ge_ref[pl.ds(r0 + chunk - 1, 1), :]
        sh = [s_ref[h] for h in heads]
        ws = [_dot(wc_ref[rs, hsl[h]], sh[h]) for h in heads]
        qs = [_dot(qe_ref[rs, hsl[h]], sh[h]) for h in heads]
        v_new = [uc_ref[rs, hsl[h]] - ws[h] for h in heads]
        o2 = [_dot(at_ref[c * DN_HEADS + h], v_new[h]) for h in heads]
        kv = [_dot_tn(kt_ref[rs, hsl[h]], v_new[h]) for h in heads]
        for h in heads:
            s_ref[h] = sh[h] * jnp.exp(glast[:, hsl[h]]) + kv[h]
            oo_ref[rs, hsl[h]] = _rms_rows(qs[h] + o2[h], nw_ref[...]) * _silu(zz_ref[rs, hsl[h]])
        return carry

    lax.fori_loop(0, nch, scan_body, 0)
    o_ref[0] = oo_ref[0:lin, :]

    @pl.when(l == pl.num_programs(1) - 1)
    def _():
        cnew_ref[0] = ext_ref[5 + lin:8 + lin, :]
        snew_ref[0] = s_ref[...]


def _deltanet(u3, cbuf, s0, cw, par, nw, *, lc, lp):
    b, l, _ = u3.shape
    nl = l // lc
    blk = lambda c0: pl.BlockSpec((1, lc, 512), lambda i, j: (i, j, c0 // 512))
    return pl.pallas_call(
        functools.partial(_dn_kernel, lin=lc, lp=lp, chunk=DN_CHUNK),
        grid=(b, nl),
        in_specs=[
            blk(C_DNQ), blk(C_DNK), blk(C_DNV), blk(C_DNZ),
            pl.BlockSpec((1, lc, LANES), lambda i, j: (i, j, C_SMALL // LANES)),
            pl.BlockSpec((1, CONV_WIDTH - 1, DN_CONV_DIM), lambda i, j: (i, 0, 0)),
            pl.BlockSpec((1, DN_HEADS, DN_DK, DN_DV), lambda i, j: (i, 0, 0, 0)),
            pl.BlockSpec((CONV_WIDTH, DN_CONV_DIM), lambda i, j: (0, 0)),
            pl.BlockSpec((SUBLANES, LANES), lambda i, j: (0, 0)),
            pl.BlockSpec((1, DN_DV), lambda i, j: (0, 0)),
        ],
        out_specs=[
            pl.BlockSpec((1, lc, DN_VW), lambda i, j: (i, j, 0)),
            pl.BlockSpec((1, CONV_WIDTH - 1, DN_CONV_DIM), lambda i, j: (i, 0, 0)),
            pl.BlockSpec((1, DN_HEADS, DN_DK, DN_DV), lambda i, j: (i, 0, 0, 0)),
        ],
        out_shape=[
            jax.ShapeDtypeStruct((b, l, DN_VW), F32),
            jax.ShapeDtypeStruct((b, CONV_WIDTH - 1, DN_CONV_DIM), F32),
            jax.ShapeDtypeStruct((b, DN_HEADS, DN_DK, DN_DV), F32),
        ],
        scratch_shapes=[
            pltpu.VMEM((8 + lp, DN_CONV_DIM), F32),
            pltpu.VMEM((lp, DN_QK), F32), pltpu.VMEM((lp, DN_QK), F32), pltpu.VMEM((lp, DN_VW), F32),
            pltpu.VMEM((lp, DN_VW), F32), pltpu.VMEM((lp, LANES), F32),
            pltpu.VMEM((lp, DN_QK), F32), pltpu.VMEM((lp, DN_QK), F32),
            pltpu.VMEM((lp // DN_CHUNK, LANES, DN_CHUNK), F32),
            pltpu.VMEM((lp, DN_VW), F32),
            pltpu.VMEM((DN_HEADS, DN_DK, DN_DV), F32),
            pltpu.VMEM((lp, DN_VW), F32), pltpu.VMEM((lp, DN_QK), F32),
            pltpu.VMEM((lp, DN_QK), F32), pltpu.VMEM((lp, DN_QK), F32),
            pltpu.VMEM((lp // DN_CHUNK * DN_HEADS, DN_CHUNK, DN_CHUNK), F32),
            pltpu.VMEM((DN_HEADS, lp, 2 * DN_DV), F32),
        ],
        compiler_params=_cparams(("parallel", "arbitrary")),
        name="deltanet",
    )(u3, u3, u3, u3, u3, cbuf, s0, cw, par, nw)


def _ssd_kernel(z_ref, x_ref, bc_ref, sm_ref, cbuf_ref, h0_ref, cw_ref, cb_ref, par_ref, nw_ref, dsk_ref,
                o_ref, cnew_ref, hnew_ref,
                ext_ref, xs_ref, bm_ref, cm_ref, zz_ref, smp_ref, ce_ref, de_ref, ct_ref, yy_ref, ht_ref,
                *, lin, lp, chunk):
    assert chunk == SSM_HEAD_DIM
    l = pl.program_id(1)
    nch = lp // chunk
    gn = SSM_GROUPS * SSM_STATE

    @pl.when(l == 0)
    def _():
        for g in range(SSM_GROUPS):
            hg = h0_ref[0, g * SSM_HPG:(g + 1) * SSM_HPG]
            ht_ref[g] = hg.reshape(SSM_HPG * SSM_HEAD_DIM, SSM_STATE).T

    _fill_ext(ext_ref, cbuf_ref, ((x_ref, 0, SSM_D_INNER), (bc_ref, SSM_D_INNER, 2 * gn)), lin=lin, lp=lp)
    _pad_rows(zz_ref, z_ref, lin=lin, lp=lp)
    _pad_rows(smp_ref, sm_ref, lin=lin, lp=lp)

    for r0 in range(0, lp, ROWS):
        valid = _row_valid(r0, ROWS, lin)
        for c0 in range(0, SSM_CONV_DIM, LANES):
            y = _conv_silu(ext_ref, r0, ROWS, slice(c0, c0 + LANES), cw_ref, cb_ref)
            if valid is not None:
                y = jnp.where(valid, y, 0.0)
            if c0 < SSM_D_INNER:
                xs_ref[r0:r0 + ROWS, c0:c0 + LANES] = y
            elif c0 < SSM_D_INNER + gn:
                bm_ref[r0:r0 + ROWS, c0 - SSM_D_INNER:c0 - SSM_D_INNER + LANES] = y
            else:
                cm_ref[r0:r0 + ROWS, c0 - SSM_D_INNER - gn:c0 - SSM_D_INNER - gn + LANES] = y

    sm = smp_ref[...]
    dt = _softplus(sm + par_ref[1:2, :])
    valid = _row_valid(0, lp, lin)
    if valid is not None:
        dt = jnp.where(valid, dt, 0.0)
    cum = _chunk_cumsum(dt * (-jnp.exp(par_ref[0:1, :])), chunk)
    ct = cum.T
    for c in range(nch):
        ct_ref[c] = ct[:, c * chunk:(c + 1) * chunk]

    spread = _expand_heads(jnp.concatenate([cum, dt], axis=0), SSM_HEADS, SSM_HEAD_DIM, SM_DT)
    ce_ref[...] = spread[:lp]
    de_ref[...] = spread[lp:]

    pw = 2 * SSM_HEAD_DIM
    row2 = lax.broadcasted_iota(jnp.int32, (chunk, 2 * chunk), 0)
    lane2 = lax.broadcasted_iota(jnp.int32, (chunk, 2 * chunk), 1)
    lower2 = (lane2 % chunk) <= row2
    first = lax.broadcasted_iota(jnp.int32, (chunk, pw), 1) < SSM_HEAD_DIM
    groups = range(SSM_GROUPS)
    gw_ = SSM_HPG * SSM_HEAD_DIM

    def chunk_body(c, carry):
        r0 = pl.multiple_of(c * chunk, chunk)
        rs = pl.ds(r0, chunk)
        bg = [bm_ref[rs, g * SSM_STATE:(g + 1) * SSM_STATE] for g in groups]
        cg = [cm_ref[rs, g * SSM_STATE:(g + 1) * SSM_STATE] for g in groups]
        ce = [ce_ref[rs, g * gw_:(g + 1) * gw_] for g in groups]
        de = [de_ref[rs, g * gw_:(g + 1) * gw_] for g in groups]
        xg = [xs_ref[rs, g * gw_:(g + 1) * gw_] for g in groups]
        ht = [ht_ref[g] for g in groups]
        cb = [_dot_nt(cg[g], bg[g]) for g in groups]
        ys = [_dot(cg[g], ht[g]) * jnp.exp(ce[g]) for g in groups]
        for g in groups:
            last = ce[g][chunk - 1:chunk, :]
            ht_ref[g] = ht[g] * jnp.exp(last) + _dot_tn(bg[g], xg[g] * (jnp.exp(last - ce[g]) * de[g]))
        for g in groups:
            cb2 = jnp.concatenate([cb[g], cb[g]], axis=1)
            xd = xg[g] * de[g]
            for p in range(SSM_HPG // 2):
                ps = slice(p * pw, (p + 1) * pw)
                hd = g * SSM_HPG + 2 * p
                ctp = ct_ref[c, SM_DT + hd:SM_DT + hd + 2, :]
                crp = jnp.concatenate([ctp[0:1], ctp[1:2]], axis=1)
                lm = jnp.exp(jnp.where(lower2, ce[g][:, ps] - crp, -jnp.inf))
                xdp = xd[:, ps]
                bd = jnp.concatenate([jnp.where(first, xdp, 0.0), jnp.where(first, 0.0, xdp)], axis=0)
                cols = slice(g * gw_ + p * pw, g * gw_ + (p + 1) * pw)
                yy_ref[rs, cols] = _dot(cb2 * lm, bd) + ys[g][:, ps] + dsk_ref[:, cols] * xg[g][:, ps]
        return carry

    lax.fori_loop(0, nch, chunk_body, 0)

    gw = SSM_D_INNER // SSM_GROUPS
    for r0 in range(0, lp, ROWS):
        n = min(ROWS, lin - r0)
        if n <= 0:
            break
        for g in range(SSM_GROUPS):
            cols = slice(g * gw, (g + 1) * gw)
            t = yy_ref[r0:r0 + ROWS, cols] * _silu(zz_ref[r0:r0 + ROWS, cols])
            t = _rms_rows(t, nw_ref[:, cols])
            o_ref[0, r0:r0 + n, cols] = t[:n]

    @pl.when(l == pl.num_programs(1) - 1)
    def _():
        cnew_ref[0] = ext_ref[5 + lin:8 + lin, :]
        for g in range(SSM_GROUPS):
            hnew_ref[0, g * SSM_HPG:(g + 1) * SSM_HPG] = ht_ref[g].T.reshape(SSM_HPG, SSM_HEAD_DIM, SSM_STATE)


def _ssd(u3, cbuf, h0, cw, cb, par, nw, d, *, lc, lp):
    b, l, _ = u3.shape
    nl = l // lc
    gn2 = 2 * SSM_GROUPS * SSM_STATE
    return pl.pallas_call(
        functools.partial(_ssd_kernel, lin=lc, lp=lp, chunk=SSM_CHUNK),
        grid=(b, nl),
        in_specs=[
            pl.BlockSpec((1, lc, SSM_D_INNER), lambda i, j: (i, j, C_SSZ // SSM_D_INNER)),
            pl.BlockSpec((1, lc, SSM_D_INNER), lambda i, j: (i, j, C_SSX // SSM_D_INNER)),
            pl.BlockSpec((1, lc, gn2), lambda i, j: (i, j, C_SSBC // gn2)),
            pl.BlockSpec((1, lc, LANES), lambda i, j: (i, j, C_SMALL // LANES)),
            pl.BlockSpec((1, CONV_WIDTH - 1, SSM_CONV_DIM), lambda i, j: (i, 0, 0)),
            pl.BlockSpec((1, SSM_HEADS, SSM_HEAD_DIM, SSM_STATE), lambda i, j: (i, 0, 0, 0)),
            pl.BlockSpec((CONV_WIDTH, SSM_CONV_DIM), lambda i, j: (0, 0)),
            pl.BlockSpec((1, SSM_CONV_DIM), lambda i, j: (0, 0)),
            pl.BlockSpec((SUBLANES, LANES), lambda i, j: (0, 0)),
            pl.BlockSpec((1, SSM_D_INNER), lambda i, j: (0, 0)),
            pl.BlockSpec((1, SSM_D_INNER), lambda i, j: (0, 0)),
        ],
        out_specs=[
            pl.BlockSpec((1, lc, SSM_D_INNER), lambda i, j: (i, j, 0)),
            pl.BlockSpec((1, CONV_WIDTH - 1, SSM_CONV_DIM), lambda i, j: (i, 0, 0)),
            pl.BlockSpec((1, SSM_HEADS, SSM_HEAD_DIM, SSM_STATE), lambda i, j: (i, 0, 0, 0)),
        ],
        out_shape=[
            jax.ShapeDtypeStruct((b, l, SSM_D_INNER), F32),
            jax.ShapeDtypeStruct((b, CONV_WIDTH - 1, SSM_CONV_DIM), F32),
            jax.ShapeDtypeStruct((b, SSM_HEADS, SSM_HEAD_DIM, SSM_STATE), F32),
        ],
        scratch_shapes=[
            pltpu.VMEM((8 + lp, SSM_CONV_DIM), F32),
            pltpu.VMEM((lp, SSM_D_INNER), F32),
            pltpu.VMEM((lp, SSM_GROUPS * SSM_STATE), F32), pltpu.VMEM((lp, SSM_GROUPS * SSM_STATE), F32),
            pltpu.VMEM((lp, SSM_D_INNER), F32), pltpu.VMEM((lp, LANES), F32),
            pltpu.VMEM((lp, SSM_D_INNER), F32), pltpu.VMEM((lp, SSM_D_INNER), F32),
            pltpu.VMEM((lp // SSM_CHUNK, LANES, SSM_CHUNK), F32),
            pltpu.VMEM((lp, SSM_D_INNER), F32),
            pltpu.VMEM((SSM_GROUPS, SSM_STATE, SSM_HPG * SSM_HEAD_DIM), F32),
        ],
        compiler_params=_cparams(("parallel", "arbitrary")),
        name="ssd",
    )(u3, u3, u3, u3, cbuf, h0, cw, cb, par, nw, d)


def _strict_upper_stack(n):
    j = lax.broadcasted_iota(jnp.int32, (2 * n, n), 0) % n
    s = lax.broadcasted_iota(jnp.int32, (2 * n, n), 1)
    return jnp.where(j > s, 1.0, 0.0).astype(BF16)


def _rev_excl_cumsum(la, uu):
    hi, lo = _split2(la)
    return jnp.dot(jnp.concatenate([hi, lo], axis=1), uu, preferred_element_type=F32)


def _sbp_kernel(bias_ref, q_ref, k_ref, v_ref, o_ref, *, tq, scale):
    h = pl.program_id(1)
    qi = pl.program_id(2)
    bias = bias_ref[h]
    q = (q_ref[0] * scale).astype(BF16)
    uu = _strict_upper_stack(tq)

    def sweep(blocks, carry, diagonal):
        c, acc = carry
        rows = [pl.ds(kj * tq if isinstance(kj, int) else pl.multiple_of(kj * tq, tq), tq) for kj in blocks]
        zs = [_dot_nt(q, k_ref[0, r, :]) + bias for r in rows]
        lss, las = [], []
        for z in zs:
            l1 = jnp.log(1.0 + jnp.exp(-jnp.abs(z)))
            lss.append(jnp.minimum(z, 0.0) - l1)
            las.append(-jnp.maximum(z, 0.0) - l1)
        if diagonal:
            valid = lax.broadcasted_iota(jnp.int32, (tq, tq), 1) < lax.broadcasted_iota(jnp.int32, (tq, tq), 0)
            las = [jnp.where(valid, la, 0.0) for la in las]
        survs = [_rev_excl_cumsum(la, uu) for la in las]
        for ls, la, surv, r in zip(lss, las, survs, rows):
            att = jnp.exp(ls + surv + c)
            if diagonal:
                att = jnp.where(valid, att, 0.0)
            acc = acc + _dot(att, v_ref[0, r, :])
            c = c + jnp.sum(la, axis=1, keepdims=True)
        return c, acc

    carry = (jnp.zeros((tq, 1), F32), jnp.zeros((tq, SB_HEAD_DIM), F32))
    carry = sweep([qi], carry, True)
    n4 = lax.shift_right_logical(qi, 2)
    carry = lax.fori_loop(0, n4, lambda t, cr: sweep([qi - 1 - 4 * t - i for i in range(4)], cr, False), carry)
    nxt = qi - 1 - 4 * n4
    carry = lax.fori_loop(0, lax.shift_right_logical(qi, 1) & 1, lambda t, cr: sweep([nxt, nxt - 1], cr, False), carry)
    carry = lax.fori_loop(0, qi & 1, lambda t, cr: sweep([0], cr, False), carry)
    o_ref[0] = carry[1]


def _sb_prompt(u3, bias, *, tq=256):
    b, l, _ = u3.shape
    tq = min(tq, l)
    kv = lambda c0: pl.BlockSpec((1, l, SB_HEAD_DIM), lambda i, h, j: (i, 0, c0 // SB_HEAD_DIM + h))
    return pl.pallas_call(
        functools.partial(_sbp_kernel, tq=tq, scale=SB_HEAD_DIM ** -0.5),
        grid=(b, SB_HEADS, l // tq),
        in_specs=[
            pl.BlockSpec(memory_space=pltpu.SMEM),
            pl.BlockSpec((1, tq, SB_HEAD_DIM), lambda i, h, j: (i, j, C_SBQ // SB_HEAD_DIM + h)),
            kv(C_SBK), kv(C_SBV),
        ],
        out_specs=pl.BlockSpec((1, tq, SB_HEAD_DIM), lambda i, h, j: (i, j, h)),
        out_shape=jax.ShapeDtypeStruct((b, l, SB_WIDTH), F32),
        compiler_params=_cparams(("parallel", "parallel", "arbitrary")),
        name="sb_prompt",
    )(bias, u3, u3, u3)


def _kvrows_kernel(k_ref, v_ref, ko_ref, vo_ref, *, tl):
    for h in range(SB_HEADS):
        rows = pl.ds(h, tl, stride=SB_HEADS)
        ko_ref[0, rows, :] = k_ref[0, :, h * SB_HEAD_DIM:(h + 1) * SB_HEAD_DIM]
        vo_ref[0, rows, :] = v_ref[0, :, h * SB_HEAD_DIM:(h + 1) * SB_HEAD_DIM]


def _kv_rows(u3, *, tl=512):
    b, l, _ = u3.shape
    tl = min(tl, l)
    src = lambda c0: pl.BlockSpec((1, tl, SB_WIDTH), lambda i, j: (i, j, c0 // SB_WIDTH))
    dst = pl.BlockSpec((1, tl * SB_HEADS, SB_HEAD_DIM), lambda i, j: (i, j, 0))
    shape = jax.ShapeDtypeStruct((b, l * SB_HEADS, SB_HEAD_DIM), F32)
    return pl.pallas_call(
        functools.partial(_kvrows_kernel, tl=tl),
        grid=(b, l // tl),
        in_specs=[src(C_SBK), src(C_SBV)],
        out_specs=[dst, dst],
        out_shape=[shape, shape],
        compiler_params=_cparams(("parallel", "parallel")),
        name="kv_rows",
    )(u3, u3)


def _sbs_kernel(pt_ref, q_ref, kc_ref, vc_ref, bias_ref, uu_ref, *rest, pp, tq, scale):
    k_refs = rest[:pp]
    v_refs = rest[pp:2 * pp]
    o_ref = rest[2 * pp]
    c_ref = rest[2 * pp + 1]
    p = pl.program_id(1)
    hq = q_ref.shape[1]
    ncol = kc_ref.shape[1]
    q = q_ref[0].astype(BF16)
    bias = bias_ref[...]
    row_head = lax.broadcasted_iota(jnp.int32, (hq, ncol), 0) // tq
    col = lax.broadcasted_iota(jnp.int32, (hq, ncol), 1)
    own = (col % SB_HEADS) == row_head

    def rev_cumsum(la):
        hi, lo = _split2(la)
        return jnp.dot(jnp.concatenate([hi, lo], axis=1), uu_ref[...], preferred_element_type=F32)

    @pl.when(p == 0)
    def _():
        z = _dot_nt(q, kc_ref[0]) * scale + bias
        t = lax.broadcasted_iota(jnp.int32, (hq, ncol), 0) % tq
        valid = own & ((col // SB_HEADS) < t)
        ls = _log_sigmoid(z)
        la = jnp.where(valid, ls - z, 0.0)
        att = jnp.where(valid, jnp.exp(ls + rev_cumsum(la)), 0.0)
        o_ref[0] = _dot(att, vc_ref[0])
        c_ref[...] = jnp.sum(la, axis=1, keepdims=True)

    z = jnp.concatenate([_dot_nt(q, k_refs[j][...]) for j in range(pp)], axis=0)
    z = z * scale + jnp.concatenate([bias] * pp, axis=0)
    valid = jnp.concatenate([own] * pp, axis=0)
    ls = _log_sigmoid(z)
    la = jnp.where(valid, ls - z, 0.0)
    surv = rev_cumsum(la)
    tot = jnp.sum(la, axis=1, keepdims=True)
    cur = c_ref[...]
    cs = []
    for j in range(pp):
        cs.append(cur)
        cur = cur + tot[j * hq:(j + 1) * hq]
    c_ref[...] = cur
    att = jnp.where(valid, jnp.exp(ls + surv + jnp.concatenate(cs, axis=0)), 0.0)
    acc = o_ref[0]
    for j in range(pp):
        acc = acc + _dot(att[j * hq:(j + 1) * hq], v_refs[j][...])
    o_ref[0] = acc


def _sb_sample(q_rows, k_cur, v_cur, bias_rows, cache_k, cache_v, page_table, layer, *, pp=32):
    b, hq, _ = q_rows.shape
    n_pages = page_table.shape[1]
    ncol = cache_k.shape[2]
    pp = math.gcd(pp, n_pages)
    tq = hq // SB_HEADS
    jj = lax.broadcasted_iota(jnp.int32, (2 * ncol, ncol), 0) % ncol
    ss = lax.broadcasted_iota(jnp.int32, (2 * ncol, ncol), 1)
    uu = jnp.where(jj > ss, 1.0, 0.0).astype(BF16)

    def page_spec(j):
        return pl.BlockSpec((None, None, ncol, SB_HEAD_DIM),
                            lambda i, p, pt: (layer, pt[i, n_pages - 1 - (p * pp + j)], 0, 0))

    grid_spec = pltpu.PrefetchScalarGridSpec(
        num_scalar_prefetch=1,
        grid=(b, n_pages // pp),
        in_specs=[
            pl.BlockSpec((1, hq, SB_HEAD_DIM), lambda i, p, pt: (i, 0, 0)),
            pl.BlockSpec((1, ncol, SB_HEAD_DIM), lambda i, p, pt: (i, 0, 0)),
            pl.BlockSpec((1, ncol, SB_HEAD_DIM), lambda i, p, pt: (i, 0, 0)),
            pl.BlockSpec((hq, ncol), lambda i, p, pt: (0, 0)),
            pl.BlockSpec((2 * ncol, ncol), lambda i, p, pt: (0, 0)),
        ] + [page_spec(j) for j in range(pp)] + [page_spec(j) for j in range(pp)],
        out_specs=pl.BlockSpec((1, hq, SB_HEAD_DIM), lambda i, p, pt: (i, 0, 0)),
        scratch_shapes=[pltpu.VMEM((hq, 1), F32)],
    )
    return pl.pallas_call(
        functools.partial(_sbs_kernel, pp=pp, tq=tq, scale=SB_HEAD_DIM ** -0.5),
        grid_spec=grid_spec,
        out_shape=jax.ShapeDtypeStruct((b, hq, SB_HEAD_DIM), F32),
        compiler_params=_cparams(("parallel", "arbitrary")),
        name="sb_sample",
    )(page_table, q_rows, k_cur, v_cur, bias_rows, uu, *([cache_k] * pp), *([cache_v] * pp))


def _mixout_kernel(op_ref, od_ref, os_ref, oa_ref, gt_ref, wb_ref, wo_ref, x_ref, g_ref, o_ref):
    gate = lambda k: gt_ref[:, k * D_MODEL:(k + 1) * D_MODEL].astype(F32)
    br = lambda o_ref_, r0, n: _dot(o_ref_[...], wb_ref[r0:r0 + n, :])
    acc = gate(0) * br(op_ref, 0, POOL_WIDTH)
    acc = acc + gate(1) * br(od_ref, POOL_WIDTH, DN_VW)
    acc = acc + gate(2) * br(os_ref, POOL_WIDTH + DN_VW, SB_WIDTH)
    acc = acc + gate(3) * br(oa_ref, POOL_WIDTH + DN_VW + SB_WIDTH, SSM_D_INNER)
    mix = jnp.dot(acc.astype(BF16), wo_ref[...], preferred_element_type=F32)
    o_ref[...] = x_ref[...] + _rms_rows(mix, g_ref[...])


def _mixout(o_pool, o_dn, o_sb, o_ss, gates, w_br_all, w_out_all, layer, x2, g, *, tm):
    m = x2.shape[0]
    rows = lambda w: pl.BlockSpec((tm, w), lambda i: (i, 0))
    resident = lambda r: pl.BlockSpec((None, r, D_MODEL), lambda i: (layer, 0, 0), pipeline_mode=pl.Buffered(1))
    return pl.pallas_call(
        _mixout_kernel,
        grid=(m // tm,),
        in_specs=[rows(POOL_WIDTH), rows(DN_VW), rows(SB_WIDTH), rows(SSM_D_INNER), rows(N_BRANCH * D_MODEL),
                  resident(w_br_all.shape[1]), resident(D_MODEL), rows(D_MODEL),
                  pl.BlockSpec((1, D_MODEL), lambda i: (0, 0))],
        out_specs=rows(D_MODEL),
        out_shape=jax.ShapeDtypeStruct((m, D_MODEL), F32),
        compiler_params=_cparams(("parallel",)),
        name="mixout",
    )(o_pool, o_dn, o_sb, o_ss, gates, w_br_all, w_out_all, x2, g)


def _mlp_kernel(x_ref, g1_ref, wu_ref, wd_ref, g2_ref, o_ref, h_ref, acc_ref, *, tm):
    f = pl.program_id(1)

    @pl.when(f == 0)
    def _():
        _norm_rows_to(h_ref, x_ref, g1_ref, tm)
        acc_ref[...] = jnp.zeros(acc_ref.shape, F32)

    a = jnp.dot(h_ref[...], wu_ref[...], preferred_element_type=F32)
    a = jnp.square(jnp.maximum(a, 0.0)).astype(BF16)
    acc_ref[...] += jnp.dot(a, wd_ref[...], preferred_element_type=F32)

    @pl.when(f == pl.num_programs(1) - 1)
    def _():
        o_ref[...] = x_ref[...] + _rms_rows(acc_ref[...], g2_ref[...])


def _mlp(x2, g1, wu_all, wd_all, layer, g2, *, tm, tf=1024):
    m = x2.shape[0]
    return pl.pallas_call(
        functools.partial(_mlp_kernel, tm=tm),
        grid=(m // tm, D_FF // tf),
        in_specs=[
            pl.BlockSpec((tm, D_MODEL), lambda i, f: (i, 0)),
            pl.BlockSpec((1, D_MODEL), lambda i, f: (0, 0)),
            pl.BlockSpec((None, D_MODEL, tf), lambda i, f: (layer, 0, f)),
            pl.BlockSpec((None, tf, D_MODEL), lambda i, f: (layer, f, 0)),
            pl.BlockSpec((1, D_MODEL), lambda i, f: (0, 0)),
        ],
        out_specs=pl.BlockSpec((tm, D_MODEL), lambda i, f: (i, 0)),
        out_shape=jax.ShapeDtypeStruct((m, D_MODEL), F32),
        scratch_shapes=[pltpu.VMEM((tm, D_MODEL), BF16), pltpu.VMEM((tm, D_MODEL), F32)],
        compiler_params=_cparams(("parallel", "arbitrary")),
        name="mlp",
    )(x2, g1, wu_all, wd_all, g2)


def _lane_rows(vals, offset):
    out = jnp.zeros((vals.shape[0], LANES), F32)
    for k in range(3):
        o = offset + k * SM_REP
        out = out.at[:, o:o + vals.shape[1]].set(vals.astype(F32))
    return out


def _prepare_params(p):
    depth = p["w_in"].shape[0]
    wt = jnp.transpose(p["w_in"], (0, 2, 1))
    small = wt[:, _O_DNB:_O_SB], wt[:, _O_SSDT:_O_GATE]
    gap = jnp.zeros((depth, SM_REP - SM_GROUP, D_MODEL), wt.dtype)
    w_u = jnp.concatenate(
        [wt[:, :_O_DNB], wt[:, _O_SB:_O_SSDT], *small, gap, *small, gap, *small,
         jnp.zeros((depth, NU - C_SMALL - 2 * SM_REP - SM_GROUP, D_MODEL), wt.dtype)], axis=1).astype(BF16)
    w_g = wt[:, _O_GATE:_O_END].astype(BF16)
    zrow = jnp.zeros((depth, LANES), F32)
    par = lambda a_log, dt_bias, off: jnp.stack(
        [_lane_rows(a_log, off), _lane_rows(dt_bias, off)] + [zrow] * (SUBLANES - 2), axis=1)
    return dict(
        w_u=w_u, w_g=w_g,
        n_mix_pre=p["norm_mix_pre"], n_mix_post=p["norm_mix_post"],
        n_mlp_pre=p["norm_mlp_pre"], n_mlp_post=p["norm_mlp_post"],
        pool_w=p["pool_w"].astype(BF16), pool_scale=p["pool_scale"],
        dn_conv_w=p["dn_conv_w"], dn_par=par(p["dn_a_log"], p["dn_dt_bias"], SM_DECAY), dn_norm_w=p["dn_norm_w"],
        sb_bias=p["sb_bias"],
        ssm_conv_w=p["ssm_conv_w"], ssm_conv_b=p["ssm_conv_b"],
        ss_par=par(p["ssm_a_log"], p["ssm_dt_bias"], SM_DT), ssm_norm_w=p["ssm_norm_w"],
        ssm_d=jnp.repeat(p["ssm_d"].astype(F32), SSM_HEAD_DIM, axis=1),
        w_branch=p["w_branch"].astype(BF16), w_out=p["w_out"].astype(BF16),
        w_up=p["w_up"].astype(BF16), w_down=p["w_down"].astype(BF16),
    )


def _trunk_layer(x, states, pw, layer, *, past=None):
    b, l, _ = x.shape
    m = b * l
    pool_buf, dn_conv, dn_s, ssm_conv, ssm_h = states
    prompt = past is None
    tm_big = min(m, 1024)
    tm = min(m, 512)
    lc = min(l, 256)
    lp = max(lc, 2 * DN_CHUNK)
    row = lambda name: pw[name][layer].reshape(1, -1)

    x2 = x.reshape(m, D_MODEL)
    u2 = _inproj(x2, row("n_mix_pre"), pw["w_u"], layer, tm=tm_big)
    gates = _inproj(x2, row("n_mix_pre"), pw["w_g"], layer, tm=tm_big, gate=True)
    u3 = u2.reshape(b, l, NU)

    pos0 = 0 if prompt else past[2].shape[1] * (past[0].shape[2] // SB_HEADS)
    o_pool, pool_new = _pool(u3, pool_buf, pw["pool_w"][layer], row("pool_scale"), pos0=pos0)
    o_dn, dn_conv_new, dn_s_new = _deltanet(u3, dn_conv, dn_s, pw["dn_conv_w"][layer], pw["dn_par"][layer],
                                            row("dn_norm_w"), lc=lc, lp=lp)
    o_ss, ss_conv_new, ss_h_new = _ssd(u3, ssm_conv, ssm_h, pw["ssm_conv_w"][layer], row("ssm_conv_b"),
                                       pw["ss_par"][layer], row("ssm_norm_w"), row("ssm_d"), lc=lc, lp=lp)

    k_rows, v_rows = _kv_rows(u3)
    sb_bias = pw["sb_bias"][layer]
    if prompt:
        o_sb = _sb_prompt(u3, sb_bias)
    else:
        cache_k, cache_v, page_table = past
        ncol = cache_k.shape[2]
        q = u3[:, :, C_SBQ:C_SBQ + SB_WIDTH].reshape(b, l, SB_HEADS, SB_HEAD_DIM)
        q_rows = jnp.transpose(q, (0, 2, 1, 3)).reshape(b, SB_HEADS * l, SB_HEAD_DIM)
        k_cur = jnp.pad(k_rows, ((0, 0), (0, ncol - l * SB_HEADS), (0, 0)))
        v_cur = jnp.pad(v_rows, ((0, 0), (0, ncol - l * SB_HEADS), (0, 0)))
        bias_rows = jnp.broadcast_to(jnp.repeat(sb_bias, l)[:, None], (SB_HEADS * l, ncol)).astype(F32)
        acc = _sb_sample(q_rows, k_cur, v_cur, bias_rows, cache_k, cache_v, page_table, layer)
        o_sb = jnp.transpose(acc.reshape(b, SB_HEADS, l, SB_HEAD_DIM), (0, 2, 1, 3)).reshape(b, l, SB_WIDTH)

    x2 = _mixout(o_pool.reshape(m, -1), o_dn.reshape(m, -1), o_sb.reshape(m, -1), o_ss.reshape(m, -1),
                 gates, pw["w_branch"], pw["w_out"], layer, x2, row("n_mix_post"), tm=min(m, 256))
    x2 = _mlp(x2, row("n_mlp_pre"), pw["w_up"], pw["w_down"], layer, row("n_mlp_post"), tm=tm)
    new_states = (k_rows.reshape(b, l, SB_HEADS, SB_HEAD_DIM), v_rows.reshape(b, l, SB_HEADS, SB_HEAD_DIM),
                  pool_new, dn_conv_new, dn_s_new, ss_conv_new, ss_h_new)
    return x2.reshape(b, l, D_MODEL), new_states


def kernel(x_prompt, x_sample, cache_sb_k, cache_sb_v, state_pool, state_dn_conv, state_dn_s, state_ssm_conv, state_ssm_h, page_table, norm_mix_pre, norm_mix_post, norm_mlp_pre, norm_mlp_post, w_in, pool_w, pool_scale, dn_conv_w, dn_a_log, dn_dt_bias, dn_norm_w, sb_bias, ssm_conv_w, ssm_conv_b, ssm_a_log, ssm_dt_bias, ssm_d, ssm_norm_w, w_branch, w_out, w_up, w_down):
    pw = _prepare_params(dict(
        norm_mix_pre=norm_mix_pre, norm_mix_post=norm_mix_post, norm_mlp_pre=norm_mlp_pre,
        norm_mlp_post=norm_mlp_post, w_in=w_in, pool_w=pool_w, pool_scale=pool_scale,
        dn_conv_w=dn_conv_w, dn_a_log=dn_a_log, dn_dt_bias=dn_dt_bias, dn_norm_w=dn_norm_w,
        sb_bias=sb_bias, ssm_conv_w=ssm_conv_w, ssm_conv_b=ssm_conv_b, ssm_a_log=ssm_a_log,
        ssm_dt_bias=ssm_dt_bias, ssm_d=ssm_d, ssm_norm_w=ssm_norm_w, w_branch=w_branch,
        w_out=w_out, w_up=w_up, w_down=w_down))
    depth = w_in.shape[0]
    bp = x_prompt.shape[0]
    dt_ = x_prompt.dtype
    zero_states = (jnp.zeros((bp, POOL_BUF, POOL_WIDTH), dt_),
                   jnp.zeros((bp, CONV_WIDTH - 1, DN_CONV_DIM), dt_),
                   jnp.zeros((bp, DN_HEADS, DN_DK, DN_DV), dt_),
                   jnp.zeros((bp, CONV_WIDTH - 1, SSM_CONV_DIM), dt_),
                   jnp.zeros((bp, SSM_HEADS, SSM_HEAD_DIM, SSM_STATE), dt_))
    n_pool, page = cache_sb_k.shape[1], cache_sb_k.shape[2]
    cache_k = cache_sb_k.reshape(depth, n_pool, page * SB_HEADS, SB_HEAD_DIM)
    cache_v = cache_sb_v.reshape(depth, n_pool, page * SB_HEADS, SB_HEAD_DIM)
    y_prompt, y_sample = x_prompt, x_sample
    new_p, new_s = [], []
    for layer in range(depth):
        y_prompt, st_p = _trunk_layer(y_prompt, zero_states, pw, layer)
        sample_states = (state_pool[layer], state_dn_conv[layer], state_dn_s[layer],
                         state_ssm_conv[layer], state_ssm_h[layer])
        y_sample, st_s = _trunk_layer(y_sample, sample_states, pw, layer, past=(cache_k, cache_v, page_table))
        new_p.append(st_p)
        new_s.append(st_s)
    outs_p = [jnp.stack(t) for t in zip(*new_p)]
    outs_s = [jnp.stack(t) for t in zip(*new_s)]
    return (y_prompt, y_sample, *outs_p, *outs_s)
```

```python
import functools
import math

import jax
import jax.numpy as jnp
from jax import lax
from jax.experimental import pallas as pl
from jax.experimental.pallas import tpu as pltpu

F32 = jnp.float32
BF16 = jnp.bfloat16

D_MODEL = 2048
N_BRANCH = 4
RMS_EPS = 1e-6
CONV_WIDTH = 4
POOL_WINDOWS = (2, 4, 8, 16)
POOL_WIDTH = D_MODEL // 4
POOL_GW = POOL_WIDTH // 4
POOL_BUF = max(POOL_WINDOWS) - 1
DN_HEADS = 4
DN_DK = 128
DN_DV = 128
DN_QK = DN_HEADS * DN_DK
DN_VW = DN_HEADS * DN_DV
DN_CONV_DIM = 2 * DN_QK + DN_VW
DN_CHUNK = 64
SB_HEADS = 4
SB_HEAD_DIM = 128
SB_WIDTH = SB_HEADS * SB_HEAD_DIM
SSM_D_INNER = D_MODEL // 2
SSM_HEAD_DIM = 64
SSM_HEADS = SSM_D_INNER // SSM_HEAD_DIM
SSM_GROUPS = 2
SSM_HPG = SSM_HEADS // SSM_GROUPS
SSM_STATE = 128
SSM_CONV_DIM = SSM_D_INNER + 2 * SSM_GROUPS * SSM_STATE
SSM_CHUNK = 64
D_FF = 4 * D_MODEL

LANES = 128
SUBLANES = 8

C_POOL = 0
C_DNQ, C_DNK, C_DNV, C_DNZ = 512, 1024, 1536, 2048
C_SBQ, C_SBK, C_SBV = 2560, 3072, 3584
C_SSZ, C_SSX, C_SSBC = 4096, 5120, 6144
C_SMALL = 6656
NU = 7168
_O_DNB = POOL_WIDTH + DN_CONV_DIM + DN_VW
_O_SB = _O_DNB + 2 * DN_HEADS
_O_SSDT = _O_SB + 3 * SB_WIDTH + SSM_D_INNER + SSM_CONV_DIM
_O_GATE = _O_SSDT + SSM_HEADS
_O_END = _O_GATE + N_BRANCH * D_MODEL
SM_BETA, SM_DECAY, SM_DT = 0, DN_HEADS, 2 * DN_HEADS
SM_GROUP = 2 * DN_HEADS + SSM_HEADS
SM_REP = 32

VMEM_LIMIT_MB = 56


def _cparams(sem, vmem_mb=VMEM_LIMIT_MB):
    return pltpu.CompilerParams(dimension_semantics=sem, vmem_limit_bytes=vmem_mb * 1024 * 1024)


def _sigmoid(x):
    return 1.0 / (1.0 + jnp.exp(-x))


def _silu(x):
    hx = 0.5 * x
    return hx + hx * jnp.tanh(hx)


def _softplus(x):
    return jnp.maximum(x, 0.0) + jnp.log1p(jnp.exp(-jnp.abs(x)))


def _log_sigmoid(x):
    return jnp.minimum(x, 0.0) - jnp.log1p(jnp.exp(-jnp.abs(x)))


def _dot(a, b):
    return jnp.dot(a.astype(BF16), b.astype(BF16), preferred_element_type=F32)


def _dot_nt(a, b):
    return lax.dot_general(a.astype(BF16), b.astype(BF16), (((1,), (1,)), ((), ())), preferred_element_type=F32)


def _dot_tn(a, b):
    return lax.dot_general(a.astype(BF16), b.astype(BF16), (((0,), (0,)), ((), ())), preferred_element_type=F32)


def _split2(a):
    hi = a.astype(BF16)
    lo = (a - hi.astype(F32)).astype(BF16)
    return hi, lo


def _expand_heads(src, n_heads, width, lane0):
    k = lax.broadcasted_iota(jnp.int32, (LANES, n_heads * width), 0)
    n = lax.broadcasted_iota(jnp.int32, (LANES, n_heads * width), 1)
    sel = jnp.where((k % SM_REP - lane0 == n // width) & (k < 3 * SM_REP), 1.0, 0.0).astype(BF16)
    lane = lax.broadcasted_iota(jnp.int32, src.shape, 1)
    hi = src.astype(BF16).astype(F32)
    r1 = src - hi
    mid = r1.astype(BF16).astype(F32)
    lo = r1 - mid
    pieces = jnp.where(lane < SM_REP, hi, jnp.where(lane < 2 * SM_REP, mid, lo))
    return jnp.dot(pieces.astype(BF16), sel, preferred_element_type=F32)


def _dot3(a, b):
    ah, al = _split2(a)
    bh, bl = _split2(b)
    d = lambda x, y: jnp.dot(x, y, preferred_element_type=F32)
    return d(ah, bh) + (d(ah, bl) + d(al, bh))


def _rms_rows(x, g):
    return x * lax.rsqrt(jnp.mean(x * x, axis=-1, keepdims=True) + RMS_EPS) * g


def _norm_rows_to(h_ref, x_ref, g_ref, tm):
    ch = min(tm, 256)

    def body(r, c):
        rs = pl.ds(pl.multiple_of(r * ch, ch), ch)
        h_ref[rs, :] = _rms_rows(x_ref[rs, :], g_ref[...]).astype(h_ref.dtype)
        return c

    lax.fori_loop(0, tm // ch, body, 0)


def _inproj_kernel(x_ref, g_ref, w_ref, o_ref, h_ref, *, tm, gate):
    @pl.when(pl.program_id(1) == 0)
    def _():
        _norm_rows_to(h_ref, x_ref, g_ref, tm)

    acc = _dot_nt(h_ref[...], w_ref[...])
    o_ref[...] = (_sigmoid(acc) if gate else acc).astype(o_ref.dtype)


def _inproj(x2, g, w_all, layer, *, tm, tn=1024, gate=False):
    m = x2.shape[0]
    n = w_all.shape[1]
    return pl.pallas_call(
        functools.partial(_inproj_kernel, tm=tm, gate=gate),
        grid=(m // tm, n // tn),
        in_specs=[
            pl.BlockSpec((tm, D_MODEL), lambda i, j: (i, 0)),
            pl.BlockSpec((1, D_MODEL), lambda i, j: (0, 0)),
            pl.BlockSpec((None, tn, D_MODEL), lambda i, j: (layer, j, 0)),
        ],
        out_specs=pl.BlockSpec((tm, tn), lambda i, j: (i, j)),
        out_shape=jax.ShapeDtypeStruct((m, n), BF16 if gate else F32),
        scratch_shapes=[pltpu.VMEM((tm, D_MODEL), BF16)],
        compiler_params=_cparams(("parallel", "arbitrary")),
        name="inproj_gate" if gate else "inproj",
    )(x2, g, w_all)


def _pool_kernel(u_ref, buf_ref, w_ref, sc_ref, o_ref, new_ref, ext_ref, *, lin, lp, pos0):
    ext_ref[0:16, :] = jnp.zeros((16, POOL_WIDTH), F32)
    ext_ref[1:16, :] = buf_ref[0]
    if lin < lp:
        ext_ref[16:16 + lp, :] = jnp.zeros((lp, POOL_WIDTH), F32)
    ext_ref[16:16 + lin, :] = u_ref[0]
    ch = min(lp, 256)
    for c0 in range(0, lp, ch):
        pos = pos0 + c0 + lax.broadcasted_iota(jnp.int32, (ch, 1), 0)
        for gi, w in enumerate(POOL_WINDOWS):
            cols = slice(gi * POOL_GW, (gi + 1) * POOL_GW)
            s = ext_ref[16 + c0:16 + c0 + ch, cols]
            tot = s
            for k in range(1, w):
                tot = tot + ext_ref[16 + c0 - k:16 + c0 - k + ch, cols]
            cnt = jnp.minimum(pos + 1, w).astype(F32)
            y = tot / cnt - s
            yo = _dot(y, w_ref[gi]) * sc_ref[:, cols]
            n = min(ch, lin - c0)
            o_ref[0, c0:c0 + n, cols] = yo[:n]
    new_ref[0] = ext_ref[1 + lin:16 + lin, :]


def _pool(u3, buf, w, sc, *, pos0):
    b, lin, _ = u3.shape
    lp = max(lin, SUBLANES)
    return pl.pallas_call(
        functools.partial(_pool_kernel, lin=lin, lp=lp, pos0=pos0),
        grid=(b,),
        in_specs=[
            pl.BlockSpec((1, lin, POOL_WIDTH), lambda i: (i, 0, C_POOL // POOL_WIDTH)),
            pl.BlockSpec((1, POOL_BUF, POOL_WIDTH), lambda i: (i, 0, 0)),
            pl.BlockSpec((4, POOL_GW, POOL_GW), lambda i: (0, 0, 0)),
            pl.BlockSpec((1, POOL_WIDTH), lambda i: (0, 0)),
        ],
        out_specs=[
            pl.BlockSpec((1, lin, POOL_WIDTH), lambda i: (i, 0, 0)),
            pl.BlockSpec((1, POOL_BUF, POOL_WIDTH), lambda i: (i, 0, 0)),
        ],
        out_shape=[
            jax.ShapeDtypeStruct((b, lin, POOL_WIDTH), F32),
            jax.ShapeDtypeStruct((b, POOL_BUF, POOL_WIDTH), F32),
        ],
        scratch_shapes=[pltpu.VMEM((16 + lp, POOL_WIDTH), F32)],
        compiler_params=_cparams(("parallel",)),
        name="pool",
    )(u3, buf, w, sc)


ROWS = 64


def _fill_ext(ext_ref, cbuf_ref, parts, *, lin, lp):
    l = pl.program_id(1)

    @pl.when(l == 0)
    def _():
        ext_ref[5:8, :] = cbuf_ref[0]

    @pl.when(l > 0)
    def _():
        ext_ref[5:8, :] = ext_ref[5 + lin:8 + lin, :]

    if lin < lp:
        ext_ref[8:8 + lp, :] = jnp.zeros((lp, ext_ref.shape[1]), F32)
    for ref, c0, width in parts:
        ext_ref[8:8 + lin, c0:c0 + width] = ref[0]


def _conv_silu(ext_ref, r0, n, cols, w_ref, b_ref):
    acc = None
    for i in range(CONV_WIDTH):
        part = ext_ref[5 + r0 + i:5 + r0 + i + n, cols] * w_ref[i:i + 1, cols]
        acc = part if acc is None else acc + part
    if b_ref is not None:
        acc = acc + b_ref[:, cols]
    return _silu(acc)


def _pad_rows(dst_ref, src_ref, *, lin, lp):
    if lin < lp:
        dst_ref[...] = jnp.zeros(dst_ref.shape, F32)
    dst_ref[0:lin, :] = src_ref[0]


def _row_valid(r0, n, lin):
    if r0 + n <= lin:
        return None
    return (r0 + lax.broadcasted_iota(jnp.int32, (n, 1), 0)) < lin


def _chunk_cumsum(x, chunk):
    rin = lax.broadcasted_iota(jnp.int32, x.shape, 0) % chunk
    s = 1
    while s < chunk:
        x = x + jnp.where(rin >= s, pltpu.roll(x, s, 0), 0.0)
        s *= 2
    return x


def _diag_inv2(a0, a1):
    c = a0.shape[0]
    hb = c // 2
    ng = hb // SUBLANES
    sub = lax.broadcasted_iota(jnp.int32, (SUBLANES, 2 * c), 0)
    lane = lax.broadcasted_iota(jnp.int32, (SUBLANES, 2 * c), 1)
    base = (lane // hb) * hb
    lmod = lane - base
    a01 = jnp.concatenate([a0, a1], axis=1)
    odd = ((lax.broadcasted_iota(jnp.int32, (hb, 2 * c), 1) // hb) % 2) == 1
    packed = jnp.where(odd, a01[hb:], a01[:hb])
    racc = [jnp.zeros((SUBLANES, 2 * c), F32) for _ in range(ng)]
    tg = [jnp.zeros((SUBLANES, 2 * c), F32) for _ in range(ng)]
    for j in range(hb):
        gj, rj = divmod(j, SUBLANES)
        t_j = jnp.where(lmod[0:1] == j, 1.0, 0.0) - racc[gj][rj:rj + 1, :]
        tg[gj] = jnp.where(sub == rj, t_j, tg[gj])
        if j == hb - 1:
            break
        for g in range(gj, ng):
            col = jnp.take_along_axis(packed[g * SUBLANES:(g + 1) * SUBLANES], base + j, axis=1)
            racc[g] = racc[g] + col * t_j
    dinv = jnp.concatenate(tg, axis=0)
    bd = jnp.concatenate([jnp.where(odd, 0.0, dinv), jnp.where(odd, dinv, 0.0)], axis=0)
    return [bd[:, :c], bd[:, c:]]


def _dn_kernel(q_ref, k_ref, v_ref, z_ref, sm_ref, cbuf_ref, s0_ref, cw_ref, par_ref, nw_ref,
               o_ref, cnew_ref, snew_ref,
               ext_ref, qn_ref, kn_ref, kb_ref, zz_ref, smp_ref, be_ref, ge_ref, gt_ref, oo_ref, s_ref,
               uc_ref, wc_ref, qe_ref, kt_ref, at_ref, rhs_ref, *, lin, lp, chunk):
    l = pl.program_id(1)
    nch = lp // chunk

    @pl.when(l == 0)
    def _():
        s_ref[...] = s0_ref[0]

    _fill_ext(ext_ref, cbuf_ref, ((q_ref, 0, DN_QK), (k_ref, DN_QK, DN_QK), (v_ref, 2 * DN_QK, DN_VW)), lin=lin, lp=lp)
    _pad_rows(zz_ref, z_ref, lin=lin, lp=lp)
    _pad_rows(smp_ref, sm_ref, lin=lin, lp=lp)

    sm = smp_ref[...]
    beta = _sigmoid(sm)
    g = -jnp.exp(par_ref[0:1, :]) * _softplus(sm + par_ref[1:2, :])
    valid = _row_valid(0, lp, lin)
    if valid is not None:
        beta = jnp.where(valid, beta, 0.0)
        g = jnp.where(valid, g, 0.0)
    gcum = _chunk_cumsum(g, chunk)
    be_ref[...] = _expand_heads(beta, DN_HEADS, DN_DK, SM_BETA)
    ge_ref[...] = _expand_heads(gcum, DN_HEADS, DN_DK, SM_DECAY)
    gt = gcum.T
    for c in range(nch):
        gt_ref[c] = gt[:, c * chunk:(c + 1) * chunk]

    assert ROWS == chunk
    for r0 in range(0, lp, ROWS):
        valid = _row_valid(r0, ROWS, lin)
        rows = slice(r0, r0 + ROWS)
        for h in range(DN_HEADS):
            hs = slice(h * DN_DK, (h + 1) * DN_DK)
            conv = lambda part: _conv_silu(ext_ref, r0, ROWS, slice(part * DN_QK + h * DN_DK, part * DN_QK + (h + 1) * DN_DK), cw_ref, None)
            l2n = lambda y: y * lax.rsqrt(jnp.sum(y * y, axis=-1, keepdims=True) + 1e-6)
            q = l2n(conv(0)) * (DN_DK ** -0.5)
            k = l2n(conv(1))
            v = conv(2)
            if valid is not None:
                q, k, v = (jnp.where(valid, t, 0.0) for t in (q, k, v))
            bt = be_ref[rows, hs]
            ge = ge_ref[rows, hs]
            eg = jnp.exp(ge)
            kb = k * bt
            qn_ref[rows, hs] = q
            kn_ref[rows, hs] = k
            kb_ref[rows, hs] = kb
            qe_ref[rows, hs] = q * eg
            kt_ref[rows, hs] = k * jnp.exp(ge[ROWS - 1:ROWS, :] - ge)
            rhs_ref[h, rows, 0:DN_DV] = v * bt
            rhs_ref[h, rows, DN_DV:2 * DN_DV] = kb * eg

    rid = lax.broadcasted_iota(jnp.int32, (chunk, chunk), 0)
    cid = lax.broadcasted_iota(jnp.int32, (chunk, chunk), 1)
    strict_lower = cid < rid
    lower = cid <= rid
    off = (rid >= chunk // 2) & (cid < chunk // 2)
    heads = range(DN_HEADS)
    hsl = [slice(h * DN_DK, (h + 1) * DN_DK) for h in heads]

    per = 4 if nch % 4 == 0 else 2

    def prep_body(c2, carry):
        items = [(i, h) for i in range(per) for h in heads]
        cidx = [per * c2 + i for i in range(per)]
        rs = [pl.ds(pl.multiple_of(c * chunk, chunk), chunk) for c in cidx]
        kh = [kn_ref[rs[c], hsl[h]] for c, h in items]
        decay = [jnp.exp(jnp.where(lower, ge_ref[rs[c], h * DN_DK:h * DN_DK + chunk]
                                   - gt_ref[cidx[c], SM_DECAY + h:SM_DECAY + h + 1, :], -jnp.inf)) for c, h in items]
        amat = [_dot_nt(kb_ref[rs[c], hsl[h]], kh[n]) * jnp.where(strict_lower, decay[n], 0.0)
                for n, (c, h) in enumerate(items)]
        for n, (c, h) in enumerate(items):
            at_ref[cidx[c] * DN_HEADS + h] = _dot_nt(qn_ref[rs[c], hsl[h]], kh[n]) * decay[n]
        dinv = sum((_diag_inv2(amat[n], amat[n + 1]) for n in range(0, len(items), 2)), [])
        inner = [_dot3(jnp.where(off, a, 0.0), d) for a, d in zip(amat, dinv)]
        tmat = [d - _dot3(d, i) for d, i in zip(dinv, inner)]
        sol = [_dot3(t, rhs_ref[h, rs[c], :]) for t, (c, h) in zip(tmat, items)]
        for s, (c, h) in zip(sol, items):
            uc_ref[rs[c], hsl[h]] = s[:, :DN_DV]
            wc_ref[rs[c], hsl[h]] = s[:, DN_DV:]
        return carry

    assert nch % per == 0
    lax.fori_loop(0, nch // per, prep_body, 0)

    def scan_body(c, carry):
        r0 = pl.multiple_of(c * chunk, chunk)
        rs = pl.ds(r0, chunk)
        glast = ge_ref[pl.ds(r0 + chunk - 1, 1), :]
        sh = [s_ref[h] for h in heads]
        ws = [_dot(wc_ref[rs, hsl[h]], sh[h]) for h in heads]
        qs = [_dot(qe_ref[rs, hsl[h]], sh[h]) for h in heads]
        v_new = [uc_ref[rs, hsl[h]] - ws[h] for h in heads]
        o2 = [_dot(at_ref[c * DN_HEADS + h], v_new[h]) for h in heads]
        kv = [_dot_tn(kt_ref[rs, hsl[h]], v_new[h]) for h in heads]
        for h in heads:
            s_ref[h] = sh[h] * jnp.exp(glast[:, hsl[h]]) + kv[h]
            oo_ref[rs, hsl[h]] = _rms_rows(qs[h] + o2[h], nw_ref[...]) * _silu(zz_ref[rs, hsl[h]])
        return carry

    lax.fori_loop(0, nch, scan_body, 0)
    o_ref[0] = oo_ref[0:lin, :]

    @pl.when(l == pl.num_programs(1) - 1)
    def _():
        cnew_ref[0] = ext_ref[5 + lin:8 + lin, :]
        snew_ref[0] = s_ref[...]


def _deltanet(u3, cbuf, s0, cw, par, nw, *, lc, lp):
    b, l, _ = u3.shape
    nl = l // lc
    blk = lambda c0: pl.BlockSpec((1, lc, 512), lambda i, j: (i, j, c0 // 512))
    return pl.pallas_call(
        functools.partial(_dn_kernel, lin=lc, lp=lp, chunk=DN_CHUNK),
        grid=(b, nl),
        in_specs=[
            blk(C_DNQ), blk(C_DNK), blk(C_DNV), blk(C_DNZ),
            pl.BlockSpec((1, lc, LANES), lambda i, j: (i, j, C_SMALL // LANES)),
            pl.BlockSpec((1, CONV_WIDTH - 1, DN_CONV_DIM), lambda i, j: (i, 0, 0)),
            pl.BlockSpec((1, DN_HEADS, DN_DK, DN_DV), lambda i, j: (i, 0, 0, 0)),
            pl.BlockSpec((CONV_WIDTH, DN_CONV_DIM), lambda i, j: (0, 0)),
            pl.BlockSpec((SUBLANES, LANES), lambda i, j: (0, 0)),
            pl.BlockSpec((1, DN_DV), lambda i, j: (0, 0)),
        ],
        out_specs=[
            pl.BlockSpec((1, lc, DN_VW), lambda i, j: (i, j, 0)),
            pl.BlockSpec((1, CONV_WIDTH - 1, DN_CONV_DIM), lambda i, j: (i, 0, 0)),
            pl.BlockSpec((1, DN_HEADS, DN_DK, DN_DV), lambda i, j: (i, 0, 0, 0)),
        ],
        out_shape=[
            jax.ShapeDtypeStruct((b, l, DN_VW), F32),
            jax.ShapeDtypeStruct((b, CONV_WIDTH - 1, DN_CONV_DIM), F32),
            jax.ShapeDtypeStruct((b, DN_HEADS, DN_DK, DN_DV), F32),
        ],
        scratch_shapes=[
            pltpu.VMEM((8 + lp, DN_CONV_DIM), F32),
            pltpu.VMEM((lp, DN_QK), F32), pltpu.VMEM((lp, DN_QK), F32), pltpu.VMEM((lp, DN_VW), F32),
            pltpu.VMEM((lp, DN_VW), F32), pltpu.VMEM((lp, LANES), F32),
            pltpu.VMEM((lp, DN_QK), F32), pltpu.VMEM((lp, DN_QK), F32),
            pltpu.VMEM((lp // DN_CHUNK, LANES, DN_CHUNK), F32),
            pltpu.VMEM((lp, DN_VW), F32),
            pltpu.VMEM((DN_HEADS, DN_DK, DN_DV), F32),
            pltpu.VMEM((lp, DN_VW), F32), pltpu.VMEM((lp, DN_QK), F32),
            pltpu.VMEM((lp, DN_QK), F32), pltpu.VMEM((lp, DN_QK), F32),
            pltpu.VMEM((lp // DN_CHUNK * DN_HEADS, DN_CHUNK, DN_CHUNK), F32),
            pltpu.VMEM((DN_HEADS, lp, 2 * DN_DV), F32),
        ],
        compiler_params=_cparams(("parallel", "arbitrary")),
        name="deltanet",
    )(u3, u3, u3, u3, u3, cbuf, s0, cw, par, nw)


def _ssd_kernel(z_ref, x_ref, bc_ref, sm_ref, cbuf_ref, h0_ref, cw_ref, cb_ref, par_ref, nw_ref, dsk_ref,
                o_ref, cnew_ref, hnew_ref,
                ext_ref, xs_ref, bm_ref, cm_ref, zz_ref, smp_ref, ce_ref, de_ref, ct_ref, yy_ref, ht_ref,
                *, lin, lp, chunk):
    assert chunk == SSM_HEAD_DIM
    l = pl.program_id(1)
    nch = lp // chunk
    gn = SSM_GROUPS * SSM_STATE

    @pl.when(l == 0)
    def _():
        for g in range(SSM_GROUPS):
            hg = h0_ref[0, g * SSM_HPG:(g + 1) * SSM_HPG]
            ht_ref[g] = hg.reshape(SSM_HPG * SSM_HEAD_DIM, SSM_STATE).T

    _fill_ext(ext_ref, cbuf_ref, ((x_ref, 0, SSM_D_INNER), (bc_ref, SSM_D_INNER, 2 * gn)), lin=lin, lp=lp)
    _pad_rows(zz_ref, z_ref, lin=lin, lp=lp)
    _pad_rows(smp_ref, sm_ref, lin=lin, lp=lp)

    for r0 in range(0, lp, ROWS):
        valid = _row_valid(r0, ROWS, lin)
        for c0 in range(0, SSM_CONV_DIM, LANES):
            y = _conv_silu(ext_ref, r0, ROWS, slice(c0, c0 + LANES), cw_ref, cb_ref)
            if valid is not None:
                y = jnp.where(valid, y, 0.0)
            if c0 < SSM_D_INNER:
                xs_ref[r0:r0 + ROWS, c0:c0 + LANES] = y
            elif c0 < SSM_D_INNER + gn:
                bm_ref[r0:r0 + ROWS, c0 - SSM_D_INNER:c0 - SSM_D_INNER + LANES] = y
            else:
                cm_ref[r0:r0 + ROWS, c0 - SSM_D_INNER - gn:c0 - SSM_D_INNER - gn + LANES] = y

    sm = smp_ref[...]
    dt = _softplus(sm + par_ref[1:2, :])
    valid = _row_valid(0, lp, lin)
    if valid is not None:
        dt = jnp.where(valid, dt, 0.0)
    cum = _chunk_cumsum(dt * (-jnp.exp(par_ref[0:1, :])), chunk)
    ct = cum.T
    for c in range(nch):
        ct_ref[c] = ct[:, c * chunk:(c + 1) * chunk]

    spread = _expand_heads(jnp.concatenate([cum, dt], axis=0), SSM_HEADS, SSM_HEAD_DIM, SM_DT)
    ce_ref[...] = spread[:lp]
    de_ref[...] = spread[lp:]

    pw = 2 * SSM_HEAD_DIM
    row2 = lax.broadcasted_iota(jnp.int32, (chunk, 2 * chunk), 0)
    lane2 = lax.broadcasted_iota(jnp.int32, (chunk, 2 * chunk), 1)
    lower2 = (lane2 % chunk) <= row2
    first = lax.broadcasted_iota(jnp.int32, (chunk, pw), 1) < SSM_HEAD_DIM
    groups = range(SSM_GROUPS)
    gw_ = SSM_HPG * SSM_HEAD_DIM

    def chunk_body(c, carry):
        r0 = pl.multiple_of(c * chunk, chunk)
        rs = pl.ds(r0, chunk)
        bg = [bm_ref[rs, g * SSM_STATE:(g + 1) * SSM_STATE] for g in groups]
        cg = [cm_ref[rs, g * SSM_STATE:(g + 1) * SSM_STATE] for g in groups]
        ce = [ce_ref[rs, g * gw_:(g + 1) * gw_] for g in groups]
        de = [de_ref[rs, g * gw_:(g + 1) * gw_] for g in groups]
        xg = [xs_ref[rs, g * gw_:(g + 1) * gw_] for g in groups]
        ht = [ht_ref[g] for g in groups]
        cb = [_dot_nt(cg[g], bg[g]) for g in groups]
        ys = [_dot(cg[g], ht[g]) * jnp.exp(ce[g]) for g in groups]
        for g in groups:
            last = ce[g][chunk - 1:chunk, :]
            ht_ref[g] = ht[g] * jnp.exp(last) + _dot_tn(bg[g], xg[g] * (jnp.exp(last - ce[g]) * de[g]))
        for g in groups:
            cb2 = jnp.concatenate([cb[g], cb[g]], axis=1)
            xd = xg[g] * de[g]
            for p in range(SSM_HPG // 2):
                ps = slice(p * pw, (p + 1) * pw)
                hd = g * SSM_HPG + 2 * p
                ctp = ct_ref[c, SM_DT + hd:SM_DT + hd + 2, :]
                crp = jnp.concatenate([ctp[0:1], ctp[1:2]], axis=1)
                lm = jnp.exp(jnp.where(lower2, ce[g][:, ps] - crp, -jnp.inf))
                xdp = xd[:, ps]
                bd = jnp.concatenate([jnp.where(first, xdp, 0.0), jnp.where(first, 0.0, xdp)], axis=0)
                cols = slice(g * gw_ + p * pw, g * gw_ + (p + 1) * pw)
                yy_ref[rs, cols] = _dot(cb2 * lm, bd) + ys[g][:, ps] + dsk_ref[:, cols] * xg[g][:, ps]
        return carry

    lax.fori_loop(0, nch, chunk_body, 0)

    gw = SSM_D_INNER // SSM_GROUPS
    for r0 in range(0, lp, ROWS):
        n = min(ROWS, lin - r0)
        if n <= 0:
            break
        for g in range(SSM_GROUPS):
            cols = slice(g * gw, (g + 1) * gw)
            t = yy_ref[r0:r0 + ROWS, cols] * _silu(zz_ref[r0:r0 + ROWS, cols])
            t = _rms_rows(t, nw_ref[:, cols])
            o_ref[0, r0:r0 + n, cols] = t[:n]

    @pl.when(l == pl.num_programs(1) - 1)
    def _():
        cnew_ref[0] = ext_ref[5 + lin:8 + lin, :]
        for g in range(SSM_GROUPS):
            hnew_ref[0, g * SSM_HPG:(g + 1) * SSM_HPG] = ht_ref[g].T.reshape(SSM_HPG, SSM_HEAD_DIM, SSM_STATE)


def _ssd(u3, cbuf, h0, cw, cb, par, nw, d, *, lc, lp):
    b, l, _ = u3.shape
    nl = l // lc
    gn2 = 2 * SSM_GROUPS * SSM_STATE
    return pl.pallas_call(
        functools.partial(_ssd_kernel, lin=lc, lp=lp, chunk=SSM_CHUNK),
        grid=(b, nl),
        in_specs=[
            pl.BlockSpec((1, lc, SSM_D_INNER), lambda i, j: (i, j, C_SSZ // SSM_D_INNER)),
            pl.BlockSpec((1, lc, SSM_D_INNER), lambda i, j: (i, j, C_SSX // SSM_D_INNER)),
            pl.BlockSpec((1, lc, gn2), lambda i, j: (i, j, C_SSBC // gn2)),
            pl.BlockSpec((1, lc, LANES), lambda i, j: (i, j, C_SMALL // LANES)),
            pl.BlockSpec((1, CONV_WIDTH - 1, SSM_CONV_DIM), lambda i, j: (i, 0, 0)),
            pl.BlockSpec((1, SSM_HEADS, SSM_HEAD_DIM, SSM_STATE), lambda i, j: (i, 0, 0, 0)),
            pl.BlockSpec((CONV_WIDTH, SSM_CONV_DIM), lambda i, j: (0, 0)),
            pl.BlockSpec((1, SSM_CONV_DIM), lambda i, j: (0, 0)),
            pl.BlockSpec((SUBLANES, LANES), lambda i, j: (0, 0)),
            pl.BlockSpec((1, SSM_D_INNER), lambda i, j: (0, 0)),
            pl.BlockSpec((1, SSM_D_INNER), lambda i, j: (0, 0)),
        ],
        out_specs=[
            pl.BlockSpec((1, lc, SSM_D_INNER), lambda i, j: (i, j, 0)),
            pl.BlockSpec((1, CONV_WIDTH - 1, SSM_CONV_DIM), lambda i, j: (i, 0, 0)),
            pl.BlockSpec((1, SSM_HEADS, SSM_HEAD_DIM, SSM_STATE), lambda i, j: (i, 0, 0, 0)),
        ],
        out_shape=[
            jax.ShapeDtypeStruct((b, l, SSM_D_INNER), F32),
            jax.ShapeDtypeStruct((b, CONV_WIDTH - 1, SSM_CONV_DIM), F32),
            jax.ShapeDtypeStruct((b, SSM_HEADS, SSM_HEAD_DIM, SSM_STATE), F32),
        ],
        scratch_shapes=[
            pltpu.VMEM((8 + lp, SSM_CONV_DIM), F32),
            pltpu.VMEM((lp, SSM_D_INNER), F32),
            pltpu.VMEM((lp, SSM_GROUPS * SSM_STATE), F32), pltpu.VMEM((lp, SSM_GROUPS * SSM_STATE), F32),
            pltpu.VMEM((lp, SSM_D_INNER), F32), pltpu.VMEM((lp, LANES), F32),
            pltpu.VMEM((lp, SSM_D_INNER), F32), pltpu.VMEM((lp, SSM_D_INNER), F32),
            pltpu.VMEM((lp // SSM_CHUNK, LANES, SSM_CHUNK), F32),
            pltpu.VMEM((lp, SSM_D_INNER), F32),
            pltpu.VMEM((SSM_GROUPS, SSM_STATE, SSM_HPG * SSM_HEAD_DIM), F32),
        ],
        compiler_params=_cparams(("parallel", "arbitrary")),
        name="ssd",
    )(u3, u3, u3, u3, cbuf, h0, cw, cb, par, nw, d)


def _strict_upper_stack(n):
    j = lax.broadcasted_iota(jnp.int32, (2 * n, n), 0) % n
    s = lax.broadcasted_iota(jnp.int32, (2 * n, n), 1)
    return jnp.where(j > s, 1.0, 0.0).astype(BF16)


def _rev_excl_cumsum(la, uu):
    hi, lo = _split2(la)
    return jnp.dot(jnp.concatenate([hi, lo], axis=1), uu, preferred_element_type=F32)


def _sbp_kernel(bias_ref, q_ref, k_ref, v_ref, o_ref, *, tq, scale):
    h = pl.program_id(1)
    qi = pl.program_id(2)
    bias = bias_ref[h]
    q = (q_ref[0] * scale).astype(BF16)
    uu = _strict_upper_stack(tq)

    def sweep(blocks, carry, diagonal):
        c, acc = carry
        rows = [pl.ds(kj * tq if isinstance(kj, int) else pl.multiple_of(kj * tq, tq), tq) for kj in blocks]
        zs = [_dot_nt(q, k_ref[0, r, :]) + bias for r in rows]
        lss, las = [], []
        for z in zs:
            l1 = jnp.log(1.0 + jnp.exp(-jnp.abs(z)))
            lss.append(jnp.minimum(z, 0.0) - l1)
            las.append(-jnp.maximum(z, 0.0) - l1)
        if diagonal:
            valid = lax.broadcasted_iota(jnp.int32, (tq, tq), 1) < lax.broadcasted_iota(jnp.int32, (tq, tq), 0)
            las = [jnp.where(valid, la, 0.0) for la in las]
        survs = [_rev_excl_cumsum(la, uu) for la in las]
        for ls, la, surv, r in zip(lss, las, survs, rows):
            att = jnp.exp(ls + surv + c)
            if diagonal:
                att = jnp.where(valid, att, 0.0)
            acc = acc + _dot(att, v_ref[0, r, :])
            c = c + jnp.sum(la, axis=1, keepdims=True)
        return c, acc

    carry = (jnp.zeros((tq, 1), F32), jnp.zeros((tq, SB_HEAD_DIM), F32))
    carry = sweep([qi], carry, True)
    n4 = lax.shift_right_logical(qi, 2)
    carry = lax.fori_loop(0, n4, lambda t, cr: sweep([qi - 1 - 4 * t - i for i in range(4)], cr, False), carry)
    nxt = qi - 1 - 4 * n4
    carry = lax.fori_loop(0, lax.shift_right_logical(qi, 1) & 1, lambda t, cr: sweep([nxt, nxt - 1], cr, False), carry)
    carry = lax.fori_loop(0, qi & 1, lambda t, cr: sweep([0], cr, False), carry)
    o_ref[0] = carry[1]


def _sb_prompt(u3, bias, *, tq=256):
    b, l, _ = u3.shape
    tq = min(tq, l)
    kv = lambda c0: pl.BlockSpec((1, l, SB_HEAD_DIM), lambda i, h, j: (i, 0, c0 // SB_HEAD_DIM + h))
    return pl.pallas_call(
        functools.partial(_sbp_kernel, tq=tq, scale=SB_HEAD_DIM ** -0.5),
        grid=(b, SB_HEADS, l // tq),
        in_specs=[
            pl.BlockSpec(memory_space=pltpu.SMEM),
            pl.BlockSpec((1, tq, SB_HEAD_DIM), lambda i, h, j: (i, j, C_SBQ // SB_HEAD_DIM + h)),
            kv(C_SBK), kv(C_SBV),
        ],
        out_specs=pl.BlockSpec((1, tq, SB_HEAD_DIM), lambda i, h, j: (i, j, h)),
        out_shape=jax.ShapeDtypeStruct((b, l, SB_WIDTH), F32),
        compiler_params=_cparams(("parallel", "parallel", "arbitrary")),
        name="sb_prompt",
    )(bias, u3, u3, u3)


def _kvrows_kernel(k_ref, v_ref, ko_ref, vo_ref, *, tl):
    for h in range(SB_HEADS):
        rows = pl.ds(h, tl, stride=SB_HEADS)
        ko_ref[0, rows, :] = k_ref[0, :, h * SB_HEAD_DIM:(h + 1) * SB_HEAD_DIM]
        vo_ref[0, rows, :] = v_ref[0, :, h * SB_HEAD_DIM:(h + 1) * SB_HEAD_DIM]


def _kv_rows(u3, *, tl=512):
    b, l, _ = u3.shape
    tl = min(tl, l)
    src = lambda c0: pl.BlockSpec((1, tl, SB_WIDTH), lambda i, j: (i, j, c0 // SB_WIDTH))
    dst = pl.BlockSpec((1, tl * SB_HEADS, SB_HEAD_DIM), lambda i, j: (i, j, 0))
    shape = jax.ShapeDtypeStruct((b, l * SB_HEADS, SB_HEAD_DIM), F32)
    return pl.pallas_call(
        functools.partial(_kvrows_kernel, tl=tl),
        grid=(b, l // tl),
        in_specs=[src(C_SBK), src(C_SBV)],
        out_specs=[dst, dst],
        out_shape=[shape, shape],
        compiler_params=_cparams(("parallel", "parallel")),
        name="kv_rows",
    )(u3, u3)


def _sbs_kernel(pt_ref, q_ref, kc_ref, vc_ref, bias_ref, uu_ref, *rest, pp, tq, scale):
    k_refs = rest[:pp]
    v_refs = rest[pp:2 * pp]
    o_ref = rest[2 * pp]
    c_ref = rest[2 * pp + 1]
    p = pl.program_id(1)
    hq = q_ref.shape[1]
    ncol = kc_ref.shape[1]
    q = q_ref[0].astype(BF16)
    bias = bias_ref[...]
    row_head = lax.broadcasted_iota(jnp.int32, (hq, ncol), 0) // tq
    col = lax.broadcasted_iota(jnp.int32, (hq, ncol), 1)
    own = (col % SB_HEADS) == row_head

    def rev_cumsum(la):
        hi, lo = _split2(la)
        return jnp.dot(jnp.concatenate([hi, lo], axis=1), uu_ref[...], preferred_element_type=F32)

    @pl.when(p == 0)
    def _():
        z = _dot_nt(q, kc_ref[0]) * scale + bias
        t = lax.broadcasted_iota(jnp.int32, (hq, ncol), 0) % tq
        valid = own & ((col // SB_HEADS) < t)
        ls = _log_sigmoid(z)
        la = jnp.where(valid, ls - z, 0.0)
        att = jnp.where(valid, jnp.exp(ls + rev_cumsum(la)), 0.0)
        o_ref[0] = _dot(att, vc_ref[0])
        c_ref[...] = jnp.sum(la, axis=1, keepdims=True)

    z = jnp.concatenate([_dot_nt(q, k_refs[j][...]) for j in range(pp)], axis=0)
    z = z * scale + jnp.concatenate([bias] * pp, axis=0)
    valid = jnp.concatenate([own] * pp, axis=0)
    ls = _log_sigmoid(z)
    la = jnp.where(valid, ls - z, 0.0)
    surv = rev_cumsum(la)
    tot = jnp.sum(la, axis=1, keepdims=True)
    cur = c_ref[...]
    cs = []
    for j in range(pp):
        cs.append(cur)
        cur = cur + tot[j * hq:(j + 1) * hq]
    c_ref[...] = cur
    att = jnp.where(valid, jnp.exp(ls + surv + jnp.concatenate(cs, axis=0)), 0.0)
    acc = o_ref[0]
    for j in range(pp):
        acc = acc + _dot(att[j * hq:(j + 1) * hq], v_refs[j][...])
    o_ref[0] = acc


def _sb_sample(q_rows, k_cur, v_cur, bias_rows, cache_k, cache_v, page_table, layer, *, pp=32):
    b, hq, _ = q_rows.shape
    n_pages = page_table.shape[1]
    ncol = cache_k.shape[2]
    pp = math.gcd(pp, n_pages)
    tq = hq // SB_HEADS
    jj = lax.broadcasted_iota(jnp.int32, (2 * ncol, ncol), 0) % ncol
    ss = lax.broadcasted_iota(jnp.int32, (2 * ncol, ncol), 1)
    uu = jnp.where(jj > ss, 1.0, 0.0).astype(BF16)

    def page_spec(j):
        return pl.BlockSpec((None, None, ncol, SB_HEAD_DIM),
                            lambda i, p, pt: (layer, pt[i, n_pages - 1 - (p * pp + j)], 0, 0))

    grid_spec = pltpu.PrefetchScalarGridSpec(
        num_scalar_prefetch=1,
        grid=(b, n_pages // pp),
        in_specs=[
            pl.BlockSpec((1, hq, SB_HEAD_DIM), lambda i, p, pt: (i, 0, 0)),
            pl.BlockSpec((1, ncol, SB_HEAD_DIM), lambda i, p, pt: (i, 0, 0)),
            pl.BlockSpec((1, ncol, SB_HEAD_DIM), lambda i, p, pt: (i, 0, 0)),
            pl.BlockSpec((hq, ncol), lambda i, p, pt: (0, 0)),
            pl.BlockSpec((2 * ncol, ncol), lambda i, p, pt: (0, 0)),
        ] + [page_spec(j) for j in range(pp)] + [page_spec(j) for j in range(pp)],
        out_specs=pl.BlockSpec((1, hq, SB_HEAD_DIM), lambda i, p, pt: (i, 0, 0)),
        scratch_shapes=[pltpu.VMEM((hq, 1), F32)],
    )
    return pl.pallas_call(
        functools.partial(_sbs_kernel, pp=pp, tq=tq, scale=SB_HEAD_DIM ** -0.5),
        grid_spec=grid_spec,
        out_shape=jax.ShapeDtypeStruct((b, hq, SB_HEAD_DIM), F32),
        compiler_params=_cparams(("parallel", "arbitrary")),
        name="sb_sample",
    )(page_table, q_rows, k_cur, v_cur, bias_rows, uu, *([cache_k] * pp), *([cache_v] * pp))


def _mixout_kernel(op_ref, od_ref, os_ref, oa_ref, gt_ref, wb_ref, wo_ref, x_ref, g_ref, o_ref):
    gate = lambda k: gt_ref[:, k * D_MODEL:(k + 1) * D_MODEL].astype(F32)
    br = lambda o_ref_, r0, n: _dot(o_ref_[...], wb_ref[r0:r0 + n, :])
    acc = gate(0) * br(op_ref, 0, POOL_WIDTH)
    acc = acc + gate(1) * br(od_ref, POOL_WIDTH, DN_VW)
    acc = acc + gate(2) * br(os_ref, POOL_WIDTH + DN_VW, SB_WIDTH)
    acc = acc + gate(3) * br(oa_ref, POOL_WIDTH + DN_VW + SB_WIDTH, SSM_D_INNER)
    mix = jnp.dot(acc.astype(BF16), wo_ref[...], preferred_element_type=F32)
    o_ref[...] = x_ref[...] + _rms_rows(mix, g_ref[...])


def _mixout(o_pool, o_dn, o_sb, o_ss, gates, w_br_all, w_out_all, layer, x2, g, *, tm):
    m = x2.shape[0]
    rows = lambda w: pl.BlockSpec((tm, w), lambda i: (i, 0))
    resident = lambda r: pl.BlockSpec((None, r, D_MODEL), lambda i: (layer, 0, 0), pipeline_mode=pl.Buffered(1))
    return pl.pallas_call(
        _mixout_kernel,
        grid=(m // tm,),
        in_specs=[rows(POOL_WIDTH), rows(DN_VW), rows(SB_WIDTH), rows(SSM_D_INNER), rows(N_BRANCH * D_MODEL),
                  resident(w_br_all.shape[1]), resident(D_MODEL), rows(D_MODEL),
                  pl.BlockSpec((1, D_MODEL), lambda i: (0, 0))],
        out_specs=rows(D_MODEL),
        out_shape=jax.ShapeDtypeStruct((m, D_MODEL), F32),
        compiler_params=_cparams(("parallel",)),
        name="mixout",
    )(o_pool, o_dn, o_sb, o_ss, gates, w_br_all, w_out_all, x2, g)


def _mlp_kernel(x_ref, g1_ref, wu_ref, wd_ref, g2_ref, o_ref, h_ref, acc_ref, *, tm):
    f = pl.program_id(1)

    @pl.when(f == 0)
    def _():
        _norm_rows_to(h_ref, x_ref, g1_ref, tm)
        acc_ref[...] = jnp.zeros(acc_ref.shape, F32)

    a = jnp.dot(h_ref[...], wu_ref[...], preferred_element_type=F32)
    a = jnp.square(jnp.maximum(a, 0.0)).astype(BF16)
    acc_ref[...] += jnp.dot(a, wd_ref[...], preferred_element_type=F32)

    @pl.when(f == pl.num_programs(1) - 1)
    def _():
        o_ref[...] = x_ref[...] + _rms_rows(acc_ref[...], g2_ref[...])


def _mlp(x2, g1, wu_all, wd_all, layer, g2, *, tm, tf=1024):
    m = x2.shape[0]
    return pl.pallas_call(
        functools.partial(_mlp_kernel, tm=tm),
        grid=(m // tm, D_FF // tf),
        in_specs=[
            pl.BlockSpec((tm, D_MODEL), lambda i, f: (i, 0)),
            pl.BlockSpec((1, D_MODEL), lambda i, f: (0, 0)),
            pl.BlockSpec((None, D_MODEL, tf), lambda i, f: (layer, 0, f)),
            pl.BlockSpec((None, tf, D_MODEL), lambda i, f: (layer, f, 0)),
            pl.BlockSpec((1, D_MODEL), lambda i, f: (0, 0)),
        ],
        out_specs=pl.BlockSpec((tm, D_MODEL), lambda i, f: (i, 0)),
        out_shape=jax.ShapeDtypeStruct((m, D_MODEL), F32),
        scratch_shapes=[pltpu.VMEM((tm, D_MODEL), BF16), pltpu.VMEM((tm, D_MODEL), F32)],
        compiler_params=_cparams(("parallel", "arbitrary")),
        name="mlp",
    )(x2, g1, wu_all, wd_all, g2)


def _lane_rows(vals, offset):
    out = jnp.zeros((vals.shape[0], LANES), F32)
    for k in range(3):
        o = offset + k * SM_REP
        out = out.at[:, o:o + vals.shape[1]].set(vals.astype(F32))
    return out


def _prepare_params(p):
    depth = p["w_in"].shape[0]
    wt = jnp.transpose(p["w_in"], (0, 2, 1))
    small = wt[:, _O_DNB:_O_SB], wt[:, _O_SSDT:_O_GATE]
    gap = jnp.zeros((depth, SM_REP - SM_GROUP, D_MODEL), wt.dtype)
    w_u = jnp.concatenate(
        [wt[:, :_O_DNB], wt[:, _O_SB:_O_SSDT], *small, gap, *small, gap, *small,
         jnp.zeros((depth, NU - C_SMALL - 2 * SM_REP - SM_GROUP, D_MODEL), wt.dtype)], axis=1).astype(BF16)
    w_g = wt[:, _O_GATE:_O_END].astype(BF16)
    zrow = jnp.zeros((depth, LANES), F32)
    par = lambda a_log, dt_bias, off: jnp.stack(
        [_lane_rows(a_log, off), _lane_rows(dt_bias, off)] + [zrow] * (SUBLANES - 2), axis=1)
    return dict(
        w_u=w_u, w_g=w_g,
        n_mix_pre=p["norm_mix_pre"], n_mix_post=p["norm_mix_post"],
        n_mlp_pre=p["norm_mlp_pre"], n_mlp_post=p["norm_mlp_post"],
        pool_w=p["pool_w"].astype(BF16), pool_scale=p["pool_scale"],
        dn_conv_w=p["dn_conv_w"], dn_par=par(p["dn_a_log"], p["dn_dt_bias"], SM_DECAY), dn_norm_w=p["dn_norm_w"],
        sb_bias=p["sb_bias"],
        ssm_conv_w=p["ssm_conv_w"], ssm_conv_b=p["ssm_conv_b"],
        ss_par=par(p["ssm_a_log"], p["ssm_dt_bias"], SM_DT), ssm_norm_w=p["ssm_norm_w"],
        ssm_d=jnp.repeat(p["ssm_d"].astype(F32), SSM_HEAD_DIM, axis=1),
        w_branch=p["w_branch"].astype(BF16), w_out=p["w_out"].astype(BF16),
        w_up=p["w_up"].astype(BF16), w_down=p["w_down"].astype(BF16),
    )


def _trunk_layer(x, states, pw, layer, *, past=None):
    b, l, _ = x.shape
    m = b * l
    pool_buf, dn_conv, dn_s, ssm_conv, ssm_h = states
    prompt = past is None
    tm_big = min(m, 1024)
    tm = min(m, 512)
    lc = min(l, 256)
    lp = max(lc, 2 * DN_CHUNK)
    row = lambda name: pw[name][layer].reshape(1, -1)

    x2 = x.reshape(m, D_MODEL)
    u2 = _inproj(x2, row("n_mix_pre"), pw["w_u"], layer, tm=tm_big)
    gates = _inproj(x2, row("n_mix_pre"), pw["w_g"], layer, tm=tm_big, gate=True)
    u3 = u2.reshape(b, l, NU)

    pos0 = 0 if prompt else past[2].shape[1] * (past[0].shape[2] // SB_HEADS)
    o_pool, pool_new = _pool(u3, pool_buf, pw["pool_w"][layer], row("pool_scale"), pos0=pos0)
    o_dn, dn_conv_new, dn_s_new = _deltanet(u3, dn_conv, dn_s, pw["dn_conv_w"][layer], pw["dn_par"][layer],
                                            row("dn_norm_w"), lc=lc, lp=lp)
    o_ss, ss_conv_new, ss_h_new = _ssd(u3, ssm_conv, ssm_h, pw["ssm_conv_w"][layer], row("ssm_conv_b"),
                                       pw["ss_par"][layer], row("ssm_norm_w"), row("ssm_d"),
                                       lc=min(l, 2 * lc), lp=max(min(l, 2 * lc), lp))

    k_rows, v_rows = _kv_rows(u3)
    sb_bias = pw["sb_bias"][layer]
    if prompt:
        o_sb = _sb_prompt(u3, sb_bias)
    else:
        cache_k, cache_v, page_table = past
        ncol = cache_k.shape[2]
        q = u3[:, :, C_SBQ:C_SBQ + SB_WIDTH].reshape(b, l, SB_HEADS, SB_HEAD_DIM)
        q_rows = jnp.transpose(q, (0, 2, 1, 3)).reshape(b, SB_HEADS * l, SB_HEAD_DIM)
        k_cur = jnp.pad(k_rows, ((0, 0), (0, ncol - l * SB_HEADS), (0, 0)))
        v_cur = jnp.pad(v_rows, ((0, 0), (0, ncol - l * SB_HEADS), (0, 0)))
        bias_rows = jnp.broadcast_to(jnp.repeat(sb_bias, l)[:, None], (SB_HEADS * l, ncol)).astype(F32)
        acc = _sb_sample(q_rows, k_cur, v_cur, bias_rows, cache_k, cache_v, page_table, layer)
        o_sb = jnp.transpose(acc.reshape(b, SB_HEADS, l, SB_HEAD_DIM), (0, 2, 1, 3)).reshape(b, l, SB_WIDTH)

    x2 = _mixout(o_pool.reshape(m, -1), o_dn.reshape(m, -1), o_sb.reshape(m, -1), o_ss.reshape(m, -1),
                 gates, pw["w_branch"], pw["w_out"], layer, x2, row("n_mix_post"), tm=min(m, 256))
    x2 = _mlp(x2, row("n_mlp_pre"), pw["w_up"], pw["w_down"], layer, row("n_mlp_post"), tm=tm)
    new_states = (k_rows.reshape(b, l, SB_HEADS, SB_HEAD_DIM), v_rows.reshape(b, l, SB_HEADS, SB_HEAD_DIM),
                  pool_new, dn_conv_new, dn_s_new, ss_conv_new, ss_h_new)
    return x2.reshape(b, l, D_MODEL), new_states


def kernel(x_prompt, x_sample, cache_sb_k, cache_sb_v, state_pool, state_dn_conv, state_dn_s, state_ssm_conv, state_ssm_h, page_table, norm_mix_pre, norm_mix_post, norm_mlp_pre, norm_mlp_post, w_in, pool_w, pool_scale, dn_conv_w, dn_a_log, dn_dt_bias, dn_norm_w, sb_bias, ssm_conv_w, ssm_conv_b, ssm_a_log, ssm_dt_bias, ssm_d, ssm_norm_w, w_branch, w_out, w_up, w_down):
    pw = _prepare_params(dict(
        norm_mix_pre=norm_mix_pre, norm_mix_post=norm_mix_post, norm_mlp_pre=norm_mlp_pre,
        norm_mlp_post=norm_mlp_post, w_in=w_in, pool_w=pool_w, pool_scale=pool_scale,
        dn_conv_w=dn_conv_w, dn_a_log=dn_a_log, dn_dt_bias=dn_dt_bias, dn_norm_w=dn_norm_w,
        sb_bias=sb_bias, ssm_conv_w=ssm_conv_w, ssm_conv_b=ssm_conv_b, ssm_a_log=ssm_a_log,
        ssm_dt_bias=ssm_dt_bias, ssm_d=ssm_d, ssm_norm_w=ssm_norm_w, w_branch=w_branch,
        w_out=w_out, w_up=w_up, w_down=w_down))
    depth = w_in.shape[0]
    bp = x_prompt.shape[0]
    dt_ = x_prompt.dtype
    zero_states = (jnp.zeros((bp, POOL_BUF, POOL_WIDTH), dt_),
                   jnp.zeros((bp, CONV_WIDTH - 1, DN_CONV_DIM), dt_),
                   jnp.zeros((bp, DN_HEADS, DN_DK, DN_DV), dt_),
                   jnp.zeros((bp, CONV_WIDTH - 1, SSM_CONV_DIM), dt_),
                   jnp.zeros((bp, SSM_HEADS, SSM_HEAD_DIM, SSM_STATE), dt_))
    n_pool, page = cache_sb_k.shape[1], cache_sb_k.shape[2]
    cache_k = cache_sb_k.reshape(depth, n_pool, page * SB_HEADS, SB_HEAD_DIM)
    cache_v = cache_sb_v.reshape(depth, n_pool, page * SB_HEADS, SB_HEAD_DIM)
    y_prompt, y_sample = x_prompt, x_sample
    new_p, new_s = [], []
    for layer in range(depth):
        y_prompt, st_p = _trunk_layer(y_prompt, zero_states, pw, layer)
        sample_states = (state_pool[layer], state_dn_conv[layer], state_dn_s[layer],
                         state_ssm_conv[layer], state_ssm_h[layer])
        y_sample, st_s = _trunk_layer(y_sample, sample_states, pw, layer, past=(cache_k, cache_v, page_table))
        new_p.append(st_p)
        new_s.append(st_s)
    outs_p = [jnp.stack(t) for t in zip(*new_p)]
    outs_s = [jnp.stack(t) for t in zip(*new_s)]
    return (y_prompt, y_sample, *outs_p, *outs_s)
```

```python
import functools
import math

import jax
import jax.numpy as jnp
from jax import lax
from jax.experimental import pallas as pl
from jax.experimental.pallas import tpu as pltpu

F32 = jnp.float32
BF16 = jnp.bfloat16

D_MODEL = 2048
N_BRANCH = 4
RMS_EPS = 1e-6
CONV_WIDTH = 4
POOL_WINDOWS = (2, 4, 8, 16)
POOL_WIDTH = D_MODEL // 4
POOL_GW = POOL_WIDTH // 4
POOL_BUF = max(POOL_WINDOWS) - 1
DN_HEADS = 4
DN_DK = 128
DN_DV = 128
DN_QK = DN_HEADS * DN_DK
DN_VW = DN_HEADS * DN_DV
DN_CONV_DIM = 2 * DN_QK + DN_VW
DN_CHUNK = 64
SB_HEADS = 4
SB_HEAD_DIM = 128
SB_WIDTH = SB_HEADS * SB_HEAD_DIM
SSM_D_INNER = D_MODEL // 2
SSM_HEAD_DIM = 64
SSM_HEADS = SSM_D_INNER // SSM_HEAD_DIM
SSM_GROUPS = 2
SSM_HPG = SSM_HEADS // SSM_GROUPS
SSM_STATE = 128
SSM_CONV_DIM = SSM_D_INNER + 2 * SSM_GROUPS * SSM_STATE
SSM_CHUNK = 64
D_FF = 4 * D_MODEL

LANES = 128
SUBLANES = 8

C_POOL = 0
C_DNQ, C_DNK, C_DNV, C_DNZ = 512, 1024, 1536, 2048
C_SBQ, C_SBK, C_SBV = 2560, 3072, 3584
C_SSZ, C_SSX, C_SSBC = 4096, 5120, 6144
C_SMALL = 6656
NU = 7168
_O_DNB = POOL_WIDTH + DN_CONV_DIM + DN_VW
_O_SB = _O_DNB + 2 * DN_HEADS
_O_SSDT = _O_SB + 3 * SB_WIDTH + SSM_D_INNER + SSM_CONV_DIM
_O_GATE = _O_SSDT + SSM_HEADS
_O_END = _O_GATE + N_BRANCH * D_MODEL
SM_BETA, SM_DECAY, SM_DT = 0, DN_HEADS, 2 * DN_HEADS
SM_GROUP = 2 * DN_HEADS + SSM_HEADS
SM_REP = 32

VMEM_LIMIT_MB = 56


def _cparams(sem, vmem_mb=VMEM_LIMIT_MB):
    return pltpu.CompilerParams(dimension_semantics=sem, vmem_limit_bytes=vmem_mb * 1024 * 1024)


def _sigmoid(x):
    return 1.0 / (1.0 + jnp.exp(-x))


def _silu(x):
    hx = 0.5 * x
    return hx + hx * jnp.tanh(hx)


def _softplus(x):
    return jnp.maximum(x, 0.0) + jnp.log1p(jnp.exp(-jnp.abs(x)))


def _log_sigmoid(x):
    return jnp.minimum(x, 0.0) - jnp.log1p(jnp.exp(-jnp.abs(x)))


def _dot(a, b):
    return jnp.dot(a.astype(BF16), b.astype(BF16), preferred_element_type=F32)


def _dot_nt(a, b):
    return lax.dot_general(a.astype(BF16), b.astype(BF16), (((1,), (1,)), ((), ())), preferred_element_type=F32)


def _dot_tn(a, b):
    return lax.dot_general(a.astype(BF16), b.astype(BF16), (((0,), (0,)), ((), ())), preferred_element_type=F32)


def _split2(a):
    hi = a.astype(BF16)
    lo = (a - hi.astype(F32)).astype(BF16)
    return hi, lo


def _expand_heads(src, n_heads, width, lane0):
    k = lax.broadcasted_iota(jnp.int32, (LANES, n_heads * width), 0)
    n = lax.broadcasted_iota(jnp.int32, (LANES, n_heads * width), 1)
    sel = jnp.where((k % SM_REP - lane0 == n // width) & (k < 3 * SM_REP), 1.0, 0.0).astype(BF16)
    lane = lax.broadcasted_iota(jnp.int32, src.shape, 1)
    hi = src.astype(BF16).astype(F32)
    r1 = src - hi
    mid = r1.astype(BF16).astype(F32)
    lo = r1 - mid
    pieces = jnp.where(lane < SM_REP, hi, jnp.where(lane < 2 * SM_REP, mid, lo))
    return jnp.dot(pieces.astype(BF16), sel, preferred_element_type=F32)


def _dot3(a, b):
    ah, al = _split2(a)
    bh, bl = _split2(b)
    d = lambda x, y: jnp.dot(x, y, preferred_element_type=F32)
    return d(ah, bh) + (d(ah, bl) + d(al, bh))


def _rms_rows(x, g):
    return x * lax.rsqrt(jnp.mean(x * x, axis=-1, keepdims=True) + RMS_EPS) * g


def _norm_rows_to(h_ref, x_ref, g_ref, tm):
    ch = min(tm, 256)

    def body(r, c):
        rs = pl.ds(pl.multiple_of(r * ch, ch), ch)
        h_ref[rs, :] = _rms_rows(x_ref[rs, :], g_ref[...]).astype(h_ref.dtype)
        return c

    lax.fori_loop(0, tm // ch, body, 0)


def _inproj_kernel(x_ref, g_ref, w_ref, o_ref, h_ref, *, tm, gate):
    @pl.when(pl.program_id(1) == 0)
    def _():
        _norm_rows_to(h_ref, x_ref, g_ref, tm)

    acc = _dot_nt(h_ref[...], w_ref[...])
    o_ref[...] = (_sigmoid(acc) if gate else acc).astype(o_ref.dtype)


def _inproj(x2, g, w_all, layer, *, tm, tn=1024, gate=False):
    m = x2.shape[0]
    n = w_all.shape[1]
    return pl.pallas_call(
        functools.partial(_inproj_kernel, tm=tm, gate=gate),
        grid=(m // tm, n // tn),
        in_specs=[
            pl.BlockSpec((tm, D_MODEL), lambda i, j: (i, 0)),
            pl.BlockSpec((1, D_MODEL), lambda i, j: (0, 0)),
            pl.BlockSpec((None, tn, D_MODEL), lambda i, j: (layer, j, 0)),
        ],
        out_specs=pl.BlockSpec((tm, tn), lambda i, j: (i, j)),
        out_shape=jax.ShapeDtypeStruct((m, n), BF16 if gate else F32),
        scratch_shapes=[pltpu.VMEM((tm, D_MODEL), BF16)],
        compiler_params=_cparams(("parallel", "arbitrary")),
        name="inproj_gate" if gate else "inproj",
    )(x2, g, w_all)


def _pool_kernel(u_ref, buf_ref, w_ref, sc_ref, o_ref, new_ref, ext_ref, *, lin, lp, pos0):
    ext_ref[0:16, :] = jnp.zeros((16, POOL_WIDTH), F32)
    ext_ref[1:16, :] = buf_ref[0]
    if lin < lp:
        ext_ref[16:16 + lp, :] = jnp.zeros((lp, POOL_WIDTH), F32)
    ext_ref[16:16 + lin, :] = u_ref[0]
    ch = min(lp, 256)
    for c0 in range(0, lp, ch):
        pos = pos0 + c0 + lax.broadcasted_iota(jnp.int32, (ch, 1), 0)
        for gi, w in enumerate(POOL_WINDOWS):
            cols = slice(gi * POOL_GW, (gi + 1) * POOL_GW)
            s = ext_ref[16 + c0:16 + c0 + ch, cols]
            tot = s
            for k in range(1, w):
                tot = tot + ext_ref[16 + c0 - k:16 + c0 - k + ch, cols]
            cnt = jnp.minimum(pos + 1, w).astype(F32)
            y = tot / cnt - s
            yo = _dot(y, w_ref[gi]) * sc_ref[:, cols]
            n = min(ch, lin - c0)
            o_ref[0, c0:c0 + n, cols] = yo[:n]
    new_ref[0] = ext_ref[1 + lin:16 + lin, :]


def _pool(u3, buf, w, sc, *, pos0):
    b, lin, _ = u3.shape
    lp = max(lin, SUBLANES)
    return pl.pallas_call(
        functools.partial(_pool_kernel, lin=lin, lp=lp, pos0=pos0),
        grid=(b,),
        in_specs=[
            pl.BlockSpec((1, lin, POOL_WIDTH), lambda i: (i, 0, C_POOL // POOL_WIDTH)),
            pl.BlockSpec((1, POOL_BUF, POOL_WIDTH), lambda i: (i, 0, 0)),
            pl.BlockSpec((4, POOL_GW, POOL_GW), lambda i: (0, 0, 0)),
            pl.BlockSpec((1, POOL_WIDTH), lambda i: (0, 0)),
        ],
        out_specs=[
            pl.BlockSpec((1, lin, POOL_WIDTH), lambda i: (i, 0, 0)),
            pl.BlockSpec((1, POOL_BUF, POOL_WIDTH), lambda i: (i, 0, 0)),
        ],
        out_shape=[
            jax.ShapeDtypeStruct((b, lin, POOL_WIDTH), F32),
            jax.ShapeDtypeStruct((b, POOL_BUF, POOL_WIDTH), F32),
        ],
        scratch_shapes=[pltpu.VMEM((16 + lp, POOL_WIDTH), F32)],
        compiler_params=_cparams(("parallel",)),
        name="pool",
    )(u3, buf, w, sc)


ROWS = 64


def _fill_ext(ext_ref, cbuf_ref, parts, *, lin, lp):
    l = pl.program_id(1)

    @pl.when(l == 0)
    def _():
        ext_ref[5:8, :] = cbuf_ref[0]

    @pl.when(l > 0)
    def _():
        ext_ref[5:8, :] = ext_ref[5 + lin:8 + lin, :]

    if lin < lp:
        ext_ref[8:8 + lp, :] = jnp.zeros((lp, ext_ref.shape[1]), F32)
    for ref, c0, width in parts:
        ext_ref[8:8 + lin, c0:c0 + width] = ref[0]


def _conv_silu(ext_ref, r0, n, cols, w_ref, b_ref):
    acc = None
    for i in range(CONV_WIDTH):
        part = ext_ref[5 + r0 + i:5 + r0 + i + n, cols] * w_ref[i:i + 1, cols]
        acc = part if acc is None else acc + part
    if b_ref is not None:
        acc = acc + b_ref[:, cols]
    return _silu(acc)


def _pad_rows(dst_ref, src_ref, *, lin, lp):
    if lin < lp:
        dst_ref[...] = jnp.zeros(dst_ref.shape, F32)
    dst_ref[0:lin, :] = src_ref[0]


def _row_valid(r0, n, lin):
    if r0 + n <= lin:
        return None
    return (r0 + lax.broadcasted_iota(jnp.int32, (n, 1), 0)) < lin


def _chunk_cumsum(x, chunk):
    rin = lax.broadcasted_iota(jnp.int32, x.shape, 0) % chunk
    s = 1
    while s < chunk:
        x = x + jnp.where(rin >= s, pltpu.roll(x, s, 0), 0.0)
        s *= 2
    return x


def _diag_inv2(a0, a1):
    c = a0.shape[0]
    hb = c // 2
    ng = hb // SUBLANES
    sub = lax.broadcasted_iota(jnp.int32, (SUBLANES, 2 * c), 0)
    lane = lax.broadcasted_iota(jnp.int32, (SUBLANES, 2 * c), 1)
    base = (lane // hb) * hb
    lmod = lane - base
    a01 = jnp.concatenate([a0, a1], axis=1)
    odd = ((lax.broadcasted_iota(jnp.int32, (hb, 2 * c), 1) // hb) % 2) == 1
    packed = jnp.where(odd, a01[hb:], a01[:hb])
    racc = [jnp.zeros((SUBLANES, 2 * c), F32) for _ in range(ng)]
    tg = [jnp.zeros((SUBLANES, 2 * c), F32) for _ in range(ng)]
    for j in range(hb):
        gj, rj = divmod(j, SUBLANES)
        t_j = jnp.where(lmod[0:1] == j, 1.0, 0.0) - racc[gj][rj:rj + 1, :]
        tg[gj] = jnp.where(sub == rj, t_j, tg[gj])
        if j == hb - 1:
            break
        for g in range(gj, ng):
            col = jnp.take_along_axis(packed[g * SUBLANES:(g + 1) * SUBLANES], base + j, axis=1)
            racc[g] = racc[g] + col * t_j
    dinv = jnp.concatenate(tg, axis=0)
    bd = jnp.concatenate([jnp.where(odd, 0.0, dinv), jnp.where(odd, dinv, 0.0)], axis=0)
    return [bd[:, :c], bd[:, c:]]


def _dn_kernel(q_ref, k_ref, v_ref, z_ref, sm_ref, cbuf_ref, s0_ref, cw_ref, par_ref, nw_ref,
               o_ref, cnew_ref, snew_ref,
               ext_ref, qn_ref, kn_ref, kb_ref, zz_ref, smp_ref, be_ref, ge_ref, gt_ref, oo_ref, s_ref,
               uc_ref, wc_ref, qe_ref, kt_ref, at_ref, rhs_ref, *, lin, lp, chunk):
    l = pl.program_id(1)
    nch = lp // chunk

    @pl.when(l == 0)
    def _():
        s_ref[...] = s0_ref[0]

    _fill_ext(ext_ref, cbuf_ref, ((q_ref, 0, DN_QK), (k_ref, DN_QK, DN_QK), (v_ref, 2 * DN_QK, DN_VW)), lin=lin, lp=lp)
    _pad_rows(zz_ref, z_ref, lin=lin, lp=lp)
    _pad_rows(smp_ref, sm_ref, lin=lin, lp=lp)

    sm = smp_ref[...]
    beta = _sigmoid(sm)
    g = -jnp.exp(par_ref[0:1, :]) * _softplus(sm + par_ref[1:2, :])
    valid = _row_valid(0, lp, lin)
    if valid is not None:
        beta = jnp.where(valid, beta, 0.0)
        g = jnp.where(valid, g, 0.0)
    gcum = _chunk_cumsum(g, chunk)
    be_ref[...] = _expand_heads(beta, DN_HEADS, DN_DK, SM_BETA)
    ge_ref[...] = _expand_heads(gcum, DN_HEADS, DN_DK, SM_DECAY)
    gt = gcum.T
    for c in range(nch):
        gt_ref[c] = gt[:, c * chunk:(c + 1) * chunk]

    assert ROWS == chunk
    for r0 in range(0, lp, ROWS):
        valid = _row_valid(r0, ROWS, lin)
        rows = slice(r0, r0 + ROWS)
        for h in range(DN_HEADS):
            hs = slice(h * DN_DK, (h + 1) * DN_DK)
            conv = lambda part: _conv_silu(ext_ref, r0, ROWS, slice(part * DN_QK + h * DN_DK, part * DN_QK + (h + 1) * DN_DK), cw_ref, None)
            l2n = lambda y: y * lax.rsqrt(jnp.sum(y * y, axis=-1, keepdims=True) + 1e-6)
            q = l2n(conv(0)) * (DN_DK ** -0.5)
            k = l2n(conv(1))
            v = conv(2)
            if valid is not None:
                q, k, v = (jnp.where(valid, t, 0.0) for t in (q, k, v))
            bt = be_ref[rows, hs]
            ge = ge_ref[rows, hs]
            eg = jnp.exp(ge)
            kb = k * bt
            qn_ref[rows, hs] = q
            kn_ref[rows, hs] = k
            kb_ref[rows, hs] = kb
            qe_ref[rows, hs] = q * eg
            kt_ref[rows, hs] = k * jnp.exp(ge[ROWS - 1:ROWS, :] - ge)
            rhs_ref[h, rows, 0:DN_DV] = v * bt
            rhs_ref[h, rows, DN_DV:2 * DN_DV] = kb * eg

    rid = lax.broadcasted_iota(jnp.int32, (chunk, chunk), 0)
    cid = lax.broadcasted_iota(jnp.int32, (chunk, chunk), 1)
    strict_lower = cid < rid
    lower = cid <= rid
    off = (rid >= chunk // 2) & (cid < chunk // 2)
    heads = range(DN_HEADS)
    hsl = [slice(h * DN_DK, (h + 1) * DN_DK) for h in heads]

    per = 4 if nch % 4 == 0 else 2

    def prep_body(c2, carry):
        items = [(i, h) for i in range(per) for h in heads]
        cidx = [per * c2 + i for i in range(per)]
        rs = [pl.ds(pl.multiple_of(c * chunk, chunk), chunk) for c in cidx]
        kh = [kn_ref[rs[c], hsl[h]] for c, h in items]
        decay = [jnp.exp(jnp.where(lower, ge_ref[rs[c], h * DN_DK:h * DN_DK + chunk]
                                   - gt_ref[cidx[c], SM_DECAY + h:SM_DECAY + h + 1, :], -jnp.inf)) for c, h in items]
        amat = [_dot_nt(kb_ref[rs[c], hsl[h]], kh[n]) * jnp.where(strict_lower, decay[n], 0.0)
                for n, (c, h) in enumerate(items)]
        for n, (c, h) in enumerate(items):
            at_ref[cidx[c] * DN_HEADS + h] = _dot_nt(qn_ref[rs[c], hsl[h]], kh[n]) * decay[n]
        dinv = sum((_diag_inv2(amat[n], amat[n + 1]) for n in range(0, len(items), 2)), [])
        inner = [_dot3(jnp.where(off, a, 0.0), d) for a, d in zip(amat, dinv)]
        tmat = [d - _dot3(d, i) for d, i in zip(dinv, inner)]
        sol = [_dot3(t, rhs_ref[h, rs[c], :]) for t, (c, h) in zip(tmat, items)]
        for s, (c, h) in zip(sol, items):
            uc_ref[rs[c], hsl[h]] = s[:, :DN_DV]
            wc_ref[rs[c], hsl[h]] = s[:, DN_DV:]
        return carry

    assert nch % per == 0
    lax.fori_loop(0, nch // per, prep_body, 0)

    def scan_body(c, carry):
        r0 = pl.multiple_of(c * chunk, chunk)
        rs = pl.ds(r0, chunk)
        glast = ge_ref[pl.ds(r0 + chunk - 1, 1), :]
        sh = [s_ref[h] for h in heads]
        ws = [_dot(wc_ref[rs, hsl[h]], sh[h]) for h in heads]
        qs = [_dot(qe_ref[rs, hsl[h]], sh[h]) for h in heads]
        v_new = [uc_ref[rs, hsl[h]] - ws[h] for h in heads]
        o2 = [_dot(at_ref[c * DN_HEADS + h], v_new[h]) for h in heads]
        kv = [_dot_tn(kt_ref[rs, hsl[h]], v_new[h]) for h in heads]
        for h in heads:
            s_ref[h] = sh[h] * jnp.exp(glast[:, hsl[h]]) + kv[h]
            oo_ref[rs, hsl[h]] = _rms_rows(qs[h] + o2[h], nw_ref[...]) * _silu(zz_ref[rs, hsl[h]])
        return carry

    lax.fori_loop(0, nch, scan_body, 0)
    o_ref[0] = oo_ref[0:lin, :]

    @pl.when(l == pl.num_programs(1) - 1)
    def _():
        cnew_ref[0] = ext_ref[5 + lin:8 + lin, :]
        snew_ref[0] = s_ref[...]


def _deltanet(u3, cbuf, s0, cw, par, nw, *, lc, lp):
    b, l, _ = u3.shape
    nl = l // lc
    blk = lambda c0: pl.BlockSpec((1, lc, 512), lambda i, j: (i, j, c0 // 512))
    return pl.pallas_call(
        functools.partial(_dn_kernel, lin=lc, lp=lp, chunk=DN_CHUNK),
        grid=(b, nl),
        in_specs=[
            blk(C_DNQ), blk(C_DNK), blk(C_DNV), blk(C_DNZ),
            pl.BlockSpec((1, lc, LANES), lambda i, j: (i, j, C_SMALL // LANES)),
            pl.BlockSpec((1, CONV_WIDTH - 1, DN_CONV_DIM), lambda i, j: (i, 0, 0)),
            pl.BlockSpec((1, DN_HEADS, DN_DK, DN_DV), lambda i, j: (i, 0, 0, 0)),
            pl.BlockSpec((CONV_WIDTH, DN_CONV_DIM), lambda i, j: (0, 0)),
            pl.BlockSpec((SUBLANES, LANES), lambda i, j: (0, 0)),
            pl.BlockSpec((1, DN_DV), lambda i, j: (0, 0)),
        ],
        out_specs=[
            pl.BlockSpec((1, lc, DN_VW), lambda i, j: (i, j, 0)),
            pl.BlockSpec((1, CONV_WIDTH - 1, DN_CONV_DIM), lambda i, j: (i, 0, 0)),
            pl.BlockSpec((1, DN_HEADS, DN_DK, DN_DV), lambda i, j: (i, 0, 0, 0)),
        ],
        out_shape=[
            jax.ShapeDtypeStruct((b, l, DN_VW), F32),
            jax.ShapeDtypeStruct((b, CONV_WIDTH - 1, DN_CONV_DIM), F32),
            jax.ShapeDtypeStruct((b, DN_HEADS, DN_DK, DN_DV), F32),
        ],
        scratch_shapes=[
            pltpu.VMEM((8 + lp, DN_CONV_DIM), F32),
            pltpu.VMEM((lp, DN_QK), F32), pltpu.VMEM((lp, DN_QK), F32), pltpu.VMEM((lp, DN_VW), F32),
            pltpu.VMEM((lp, DN_VW), F32), pltpu.VMEM((lp, LANES), F32),
            pltpu.VMEM((lp, DN_QK), F32), pltpu.VMEM((lp, DN_QK), F32),
            pltpu.VMEM((lp // DN_CHUNK, LANES, DN_CHUNK), F32),
            pltpu.VMEM((lp, DN_VW), F32),
            pltpu.VMEM((DN_HEADS, DN_DK, DN_DV), F32),
            pltpu.VMEM((lp, DN_VW), F32), pltpu.VMEM((lp, DN_QK), F32),
            pltpu.VMEM((lp, DN_QK), F32), pltpu.VMEM((lp, DN_QK), F32),
            pltpu.VMEM((lp // DN_CHUNK * DN_HEADS, DN_CHUNK, DN_CHUNK), F32),
            pltpu.VMEM((DN_HEADS, lp, 2 * DN_DV), F32),
        ],
        compiler_params=_cparams(("parallel", "arbitrary")),
        name="deltanet",
    )(u3, u3, u3, u3, u3, cbuf, s0, cw, par, nw)


def _ssd_kernel(z_ref, x_ref, bc_ref, sm_ref, cbuf_ref, h0_ref, cw_ref, cb_ref, par_ref, nw_ref, dsk_ref,
                o_ref, cnew_ref, hnew_ref,
                ext_ref, xs_ref, bm_ref, cm_ref, zz_ref, smp_ref, ce_ref, de_ref, ct_ref, yy_ref, ht_ref,
                *, lin, lp, chunk):
    assert chunk == SSM_HEAD_DIM
    l = pl.program_id(1)
    nch = lp // chunk
    gn = SSM_GROUPS * SSM_STATE

    @pl.when(l == 0)
    def _():
        for g in range(SSM_GROUPS):
            hg = h0_ref[0, g * SSM_HPG:(g + 1) * SSM_HPG]
            ht_ref[g] = hg.reshape(SSM_HPG * SSM_HEAD_DIM, SSM_STATE).T

    _fill_ext(ext_ref, cbuf_ref, ((x_ref, 0, SSM_D_INNER), (bc_ref, SSM_D_INNER, 2 * gn)), lin=lin, lp=lp)
    _pad_rows(zz_ref, z_ref, lin=lin, lp=lp)
    _pad_rows(smp_ref, sm_ref, lin=lin, lp=lp)

    for r0 in range(0, lp, ROWS):
        valid = _row_valid(r0, ROWS, lin)
        for c0 in range(0, SSM_CONV_DIM, LANES):
            y = _conv_silu(ext_ref, r0, ROWS, slice(c0, c0 + LANES), cw_ref, cb_ref)
            if valid is not None:
                y = jnp.where(valid, y, 0.0)
            if c0 < SSM_D_INNER:
                xs_ref[r0:r0 + ROWS, c0:c0 + LANES] = y
            elif c0 < SSM_D_INNER + gn:
                bm_ref[r0:r0 + ROWS, c0 - SSM_D_INNER:c0 - SSM_D_INNER + LANES] = y
            else:
                cm_ref[r0:r0 + ROWS, c0 - SSM_D_INNER - gn:c0 - SSM_D_INNER - gn + LANES] = y

    sm = smp_ref[...]
    dt = _softplus(sm + par_ref[1:2, :])
    valid = _row_valid(0, lp, lin)
    if valid is not None:
        dt = jnp.where(valid, dt, 0.0)
    cum = _chunk_cumsum(dt * (-jnp.exp(par_ref[0:1, :])), chunk)
    ct = cum.T
    for c in range(nch):
        ct_ref[c] = ct[:, c * chunk:(c + 1) * chunk]

    spread = _expand_heads(jnp.concatenate([cum, dt], axis=0), SSM_HEADS, SSM_HEAD_DIM, SM_DT)
    ce_ref[...] = spread[:lp]
    de_ref[...] = spread[lp:]

    pw = 2 * SSM_HEAD_DIM
    row2 = lax.broadcasted_iota(jnp.int32, (chunk, 2 * chunk), 0)
    lane2 = lax.broadcasted_iota(jnp.int32, (chunk, 2 * chunk), 1)
    lower2 = (lane2 % chunk) <= row2
    first = lax.broadcasted_iota(jnp.int32, (chunk, pw), 1) < SSM_HEAD_DIM
    groups = range(SSM_GROUPS)
    gw_ = SSM_HPG * SSM_HEAD_DIM

    def chunk_body(c, carry):
        r0 = pl.multiple_of(c * chunk, chunk)
        rs = pl.ds(r0, chunk)
        bg = [bm_ref[rs, g * SSM_STATE:(g + 1) * SSM_STATE] for g in groups]
        cg = [cm_ref[rs, g * SSM_STATE:(g + 1) * SSM_STATE] for g in groups]
        ce = [ce_ref[rs, g * gw_:(g + 1) * gw_] for g in groups]
        de = [de_ref[rs, g * gw_:(g + 1) * gw_] for g in groups]
        xg = [xs_ref[rs, g * gw_:(g + 1) * gw_] for g in groups]
        ht = [ht_ref[g] for g in groups]
        cb = [_dot_nt(cg[g], bg[g]) for g in groups]
        ys = [_dot(cg[g], ht[g]) * jnp.exp(ce[g]) for g in groups]
        for g in groups:
            last = ce[g][chunk - 1:chunk, :]
            ht_ref[g] = ht[g] * jnp.exp(last) + _dot_tn(bg[g], xg[g] * (jnp.exp(last - ce[g]) * de[g]))
        for g in groups:
            cb2 = jnp.concatenate([cb[g], cb[g]], axis=1)
            xd = xg[g] * de[g]
            for p in range(SSM_HPG // 2):
                ps = slice(p * pw, (p + 1) * pw)
                hd = g * SSM_HPG + 2 * p
                ctp = ct_ref[c, SM_DT + hd:SM_DT + hd + 2, :]
                crp = jnp.concatenate([ctp[0:1], ctp[1:2]], axis=1)
                lm = jnp.exp(jnp.where(lower2, ce[g][:, ps] - crp, -jnp.inf))
                xdp = xd[:, ps]
                bd = jnp.concatenate([jnp.where(first, xdp, 0.0), jnp.where(first, 0.0, xdp)], axis=0)
                cols = slice(g * gw_ + p * pw, g * gw_ + (p + 1) * pw)
                yy_ref[rs, cols] = _dot(cb2 * lm, bd) + ys[g][:, ps] + dsk_ref[:, cols] * xg[g][:, ps]
        return carry

    lax.fori_loop(0, nch, chunk_body, 0)

    gw = SSM_D_INNER // SSM_GROUPS
    for r0 in range(0, lp, ROWS):
        n = min(ROWS, lin - r0)
        if n <= 0:
            break
        for g in range(SSM_GROUPS):
            cols = slice(g * gw, (g + 1) * gw)
            t = yy_ref[r0:r0 + ROWS, cols] * _silu(zz_ref[r0:r0 + ROWS, cols])
            t = _rms_rows(t, nw_ref[:, cols])
            o_ref[0, r0:r0 + n, cols] = t[:n]

    @pl.when(l == pl.num_programs(1) - 1)
    def _():
        cnew_ref[0] = ext_ref[5 + lin:8 + lin, :]
        for g in range(SSM_GROUPS):
            hnew_ref[0, g * SSM_HPG:(g + 1) * SSM_HPG] = ht_ref[g].T.reshape(SSM_HPG, SSM_HEAD_DIM, SSM_STATE)


def _ssd(u3, cbuf, h0, cw, cb, par, nw, d, *, lc, lp):
    b, l, _ = u3.shape
    nl = l // lc
    gn2 = 2 * SSM_GROUPS * SSM_STATE
    return pl.pallas_call(
        functools.partial(_ssd_kernel, lin=lc, lp=lp, chunk=SSM_CHUNK),
        grid=(b, nl),
        in_specs=[
            pl.BlockSpec((1, lc, SSM_D_INNER), lambda i, j: (i, j, C_SSZ // SSM_D_INNER)),
            pl.BlockSpec((1, lc, SSM_D_INNER), lambda i, j: (i, j, C_SSX // SSM_D_INNER)),
            pl.BlockSpec((1, lc, gn2), lambda i, j: (i, j, C_SSBC // gn2)),
            pl.BlockSpec((1, lc, LANES), lambda i, j: (i, j, C_SMALL // LANES)),
            pl.BlockSpec((1, CONV_WIDTH - 1, SSM_CONV_DIM), lambda i, j: (i, 0, 0)),
            pl.BlockSpec((1, SSM_HEADS, SSM_HEAD_DIM, SSM_STATE), lambda i, j: (i, 0, 0, 0)),
            pl.BlockSpec((CONV_WIDTH, SSM_CONV_DIM), lambda i, j: (0, 0)),
            pl.BlockSpec((1, SSM_CONV_DIM), lambda i, j: (0, 0)),
            pl.BlockSpec((SUBLANES, LANES), lambda i, j: (0, 0)),
            pl.BlockSpec((1, SSM_D_INNER), lambda i, j: (0, 0)),
            pl.BlockSpec((1, SSM_D_INNER), lambda i, j: (0, 0)),
        ],
        out_specs=[
            pl.BlockSpec((1, lc, SSM_D_INNER), lambda i, j: (i, j, 0)),
            pl.BlockSpec((1, CONV_WIDTH - 1, SSM_CONV_DIM), lambda i, j: (i, 0, 0)),
            pl.BlockSpec((1, SSM_HEADS, SSM_HEAD_DIM, SSM_STATE), lambda i, j: (i, 0, 0, 0)),
        ],
        out_shape=[
            jax.ShapeDtypeStruct((b, l, SSM_D_INNER), F32),
            jax.ShapeDtypeStruct((b, CONV_WIDTH - 1, SSM_CONV_DIM), F32),
            jax.ShapeDtypeStruct((b, SSM_HEADS, SSM_HEAD_DIM, SSM_STATE), F32),
        ],
        scratch_shapes=[
            pltpu.VMEM((8 + lp, SSM_CONV_DIM), F32),
            pltpu.VMEM((lp, SSM_D_INNER), F32),
            pltpu.VMEM((lp, SSM_GROUPS * SSM_STATE), F32), pltpu.VMEM((lp, SSM_GROUPS * SSM_STATE), F32),
            pltpu.VMEM((lp, SSM_D_INNER), F32), pltpu.VMEM((lp, LANES), F32),
            pltpu.VMEM((lp, SSM_D_INNER), F32), pltpu.VMEM((lp, SSM_D_INNER), F32),
            pltpu.VMEM((lp // SSM_CHUNK, LANES, SSM_CHUNK), F32),
            pltpu.VMEM((lp, SSM_D_INNER), F32),
            pltpu.VMEM((SSM_GROUPS, SSM_STATE, SSM_HPG * SSM_HEAD_DIM), F32),
        ],
        compiler_params=_cparams(("parallel", "arbitrary")),
        name="ssd",
    )(u3, u3, u3, u3, cbuf, h0, cw, cb, par, nw, d)


def _strict_upper_stack(n):
    j = lax.broadcasted_iota(jnp.int32, (2 * n, n), 0) % n
    s = lax.broadcasted_iota(jnp.int32, (2 * n, n), 1)
    return jnp.where(j > s, 1.0, 0.0).astype(BF16)


def _rev_excl_cumsum(la, uu):
    hi, lo = _split2(la)
    return jnp.dot(jnp.concatenate([hi, lo], axis=1), uu, preferred_element_type=F32)


def _sbp_kernel(bias_ref, q_ref, k_ref, v_ref, o_ref, *, tq, scale):
    h = pl.program_id(1)
    qi = pl.program_id(2)
    bias = bias_ref[h]
    q = (q_ref[0] * scale).astype(BF16)
    uu = _strict_upper_stack(tq)

    def sweep(blocks, carry, diagonal):
        c, acc = carry
        rows = [pl.ds(kj * tq if isinstance(kj, int) else pl.multiple_of(kj * tq, tq), tq) for kj in blocks]
        zs = [_dot_nt(q, k_ref[0, r, :]) + bias for r in rows]
        lss, las = [], []
        for z in zs:
            l1 = jnp.log(1.0 + jnp.exp(-jnp.abs(z)))
            lss.append(jnp.minimum(z, 0.0) - l1)
            las.append(-jnp.maximum(z, 0.0) - l1)
        if diagonal:
            valid = lax.broadcasted_iota(jnp.int32, (tq, tq), 1) < lax.broadcasted_iota(jnp.int32, (tq, tq), 0)
            las = [jnp.where(valid, la, 0.0) for la in las]
        survs = [_rev_excl_cumsum(la, uu) for la in las]
        for ls, la, surv, r in zip(lss, las, survs, rows):
            att = jnp.exp(ls + surv + c)
            if diagonal:
                att = jnp.where(valid, att, 0.0)
            acc = acc + _dot(att, v_ref[0, r, :])
            c = c + jnp.sum(la, axis=1, keepdims=True)
        return c, acc

    carry = (jnp.zeros((tq, 1), F32), jnp.zeros((tq, SB_HEAD_DIM), F32))
    carry = sweep([qi], carry, True)
    n4 = lax.shift_right_logical(qi, 2)
    carry = lax.fori_loop(0, n4, lambda t, cr: sweep([qi - 1 - 4 * t - i for i in range(4)], cr, False), carry)
    nxt = qi - 1 - 4 * n4
    carry = lax.fori_loop(0, lax.shift_right_logical(qi, 1) & 1, lambda t, cr: sweep([nxt, nxt - 1], cr, False), carry)
    carry = lax.fori_loop(0, qi & 1, lambda t, cr: sweep([0], cr, False), carry)
    o_ref[0] = carry[1]


def _sb_prompt(u3, bias, *, tq=256):
    b, l, _ = u3.shape
    tq = min(tq, l)
    kv = lambda c0: pl.BlockSpec((1, l, SB_HEAD_DIM), lambda i, h, j: (i, 0, c0 // SB_HEAD_DIM + h))
    return pl.pallas_call(
        functools.partial(_sbp_kernel, tq=tq, scale=SB_HEAD_DIM ** -0.5),
        grid=(b, SB_HEADS, l // tq),
        in_specs=[
            pl.BlockSpec(memory_space=pltpu.SMEM),
            pl.BlockSpec((1, tq, SB_HEAD_DIM), lambda i, h, j: (i, j, C_SBQ // SB_HEAD_DIM + h)),
            kv(C_SBK), kv(C_SBV),
        ],
        out_specs=pl.BlockSpec((1, tq, SB_HEAD_DIM), lambda i, h, j: (i, j, h)),
        out_shape=jax.ShapeDtypeStruct((b, l, SB_WIDTH), F32),
        compiler_params=_cparams(("parallel", "parallel", "arbitrary")),
        name="sb_prompt",
    )(bias, u3, u3, u3)


def _kvrows_kernel(k_ref, v_ref, ko_ref, vo_ref, *, tl):
    for h in range(SB_HEADS):
        rows = pl.ds(h, tl, stride=SB_HEADS)
        ko_ref[0, rows, :] = k_ref[0, :, h * SB_HEAD_DIM:(h + 1) * SB_HEAD_DIM]
        vo_ref[0, rows, :] = v_ref[0, :, h * SB_HEAD_DIM:(h + 1) * SB_HEAD_DIM]


def _kv_rows(u3, *, tl=512):
    b, l, _ = u3.shape
    tl = min(tl, l)
    src = lambda c0: pl.BlockSpec((1, tl, SB_WIDTH), lambda i, j: (i, j, c0 // SB_WIDTH))
    dst = pl.BlockSpec((1, tl * SB_HEADS, SB_HEAD_DIM), lambda i, j: (i, j, 0))
    shape = jax.ShapeDtypeStruct((b, l * SB_HEADS, SB_HEAD_DIM), F32)
    return pl.pallas_call(
        functools.partial(_kvrows_kernel, tl=tl),
        grid=(b, l // tl),
        in_specs=[src(C_SBK), src(C_SBV)],
        out_specs=[dst, dst],
        out_shape=[shape, shape],
        compiler_params=_cparams(("parallel", "parallel")),
        name="kv_rows",
    )(u3, u3)


def _sbs_kernel(pt_ref, q_ref, kc_ref, vc_ref, bias_ref, uu_ref, *rest, pp, tq, scale):
    k_refs = rest[:pp]
    v_refs = rest[pp:2 * pp]
    o_ref = rest[2 * pp]
    c_ref = rest[2 * pp + 1]
    p = pl.program_id(1)
    hq = q_ref.shape[1]
    ncol = kc_ref.shape[1]
    q = q_ref[0].astype(BF16)
    bias = bias_ref[...]
    row_head = lax.broadcasted_iota(jnp.int32, (hq, ncol), 0) // tq
    col = lax.broadcasted_iota(jnp.int32, (hq, ncol), 1)
    own = (col % SB_HEADS) == row_head

    def rev_cumsum(la):
        hi, lo = _split2(la)
        return jnp.dot(jnp.concatenate([hi, lo], axis=1), uu_ref[...], preferred_element_type=F32)

    @pl.when(p == 0)
    def _():
        z = _dot_nt(q, kc_ref[0]) * scale + bias
        t = lax.broadcasted_iota(jnp.int32, (hq, ncol), 0) % tq
        valid = own & ((col // SB_HEADS) < t)
        ls = _log_sigmoid(z)
        la = jnp.where(valid, ls - z, 0.0)
        att = jnp.where(valid, jnp.exp(ls + rev_cumsum(la)), 0.0)
        o_ref[0] = _dot(att, vc_ref[0])
        c_ref[...] = jnp.sum(la, axis=1, keepdims=True)

    z = jnp.concatenate([_dot_nt(q, k_refs[j][...]) for j in range(pp)], axis=0)
    z = z * scale + jnp.concatenate([bias] * pp, axis=0)
    valid = jnp.concatenate([own] * pp, axis=0)
    ls = _log_sigmoid(z)
    la = jnp.where(valid, ls - z, 0.0)
    surv = rev_cumsum(la)
    tot = jnp.sum(la, axis=1, keepdims=True)
    cur = c_ref[...]
    cs = []
    for j in range(pp):
        cs.append(cur)
        cur = cur + tot[j * hq:(j + 1) * hq]
    c_ref[...] = cur
    att = jnp.where(valid, jnp.exp(ls + surv + jnp.concatenate(cs, axis=0)), 0.0)
    acc = o_ref[0]
    for j in range(pp):
        acc = acc + _dot(att[j * hq:(j + 1) * hq], v_refs[j][...])
    o_ref[0] = acc


def _sb_sample(q_rows, k_cur, v_cur, bias_rows, cache_k, cache_v, page_table, layer, *, pp=32):
    b, hq, _ = q_rows.shape
    n_pages = page_table.shape[1]
    ncol = cache_k.shape[2]
    pp = math.gcd(pp, n_pages)
    tq = hq // SB_HEADS
    jj = lax.broadcasted_iota(jnp.int32, (2 * ncol, ncol), 0) % ncol
    ss = lax.broadcasted_iota(jnp.int32, (2 * ncol, ncol), 1)
    uu = jnp.where(jj > ss, 1.0, 0.0).astype(BF16)

    def page_spec(j):
        return pl.BlockSpec((None, None, ncol, SB_HEAD_DIM),
                            lambda i, p, pt: (layer, pt[i, n_pages - 1 - (p * pp + j)], 0, 0))

    grid_spec = pltpu.PrefetchScalarGridSpec(
        num_scalar_prefetch=1,
        grid=(b, n_pages // pp),
        in_specs=[
            pl.BlockSpec((1, hq, SB_HEAD_DIM), lambda i, p, pt: (i, 0, 0)),
            pl.BlockSpec((1, ncol, SB_HEAD_DIM), lambda i, p, pt: (i, 0, 0)),
            pl.BlockSpec((1, ncol, SB_HEAD_DIM), lambda i, p, pt: (i, 0, 0)),
            pl.BlockSpec((hq, ncol), lambda i, p, pt: (0, 0)),
            pl.BlockSpec((2 * ncol, ncol), lambda i, p, pt: (0, 0)),
        ] + [page_spec(j) for j in range(pp)] + [page_spec(j) for j in range(pp)],
        out_specs=pl.BlockSpec((1, hq, SB_HEAD_DIM), lambda i, p, pt: (i, 0, 0)),
        scratch_shapes=[pltpu.VMEM((hq, 1), F32)],
    )
    return pl.pallas_call(
        functools.partial(_sbs_kernel, pp=pp, tq=tq, scale=SB_HEAD_DIM ** -0.5),
        grid_spec=grid_spec,
        out_shape=jax.ShapeDtypeStruct((b, hq, SB_HEAD_DIM), F32),
        compiler_params=_cparams(("parallel", "arbitrary")),
        name="sb_sample",
    )(page_table, q_rows, k_cur, v_cur, bias_rows, uu, *([cache_k] * pp), *([cache_v] * pp))


def _mixout_kernel(op_ref, od_ref, os_ref, oa_ref, gt_ref, wb_ref, wo_ref, x_ref, g_ref, o_ref):
    gate = lambda k: gt_ref[:, k * D_MODEL:(k + 1) * D_MODEL].astype(F32)
    br = lambda o_ref_, r0, n: _dot(o_ref_[...], wb_ref[r0:r0 + n, :])
    acc = gate(0) * br(op_ref, 0, POOL_WIDTH)
    acc = acc + gate(1) * br(od_ref, POOL_WIDTH, DN_VW)
    acc = acc + gate(2) * br(os_ref, POOL_WIDTH + DN_VW, SB_WIDTH)
    acc = acc + gate(3) * br(oa_ref, POOL_WIDTH + DN_VW + SB_WIDTH, SSM_D_INNER)
    mix = jnp.dot(acc.astype(BF16), wo_ref[...], preferred_element_type=F32)
    o_ref[...] = x_ref[...] + _rms_rows(mix, g_ref[...])


def _mixout(o_pool, o_dn, o_sb, o_ss, gates, w_br_all, w_out_all, layer, x2, g, *, tm):
    m = x2.shape[0]
    rows = lambda w: pl.BlockSpec((tm, w), lambda i: (i, 0))
    resident = lambda r: pl.BlockSpec((None, r, D_MODEL), lambda i: (layer, 0, 0), pipeline_mode=pl.Buffered(1))
    return pl.pallas_call(
        _mixout_kernel,
        grid=(m // tm,),
        in_specs=[rows(POOL_WIDTH), rows(DN_VW), rows(SB_WIDTH), rows(SSM_D_INNER), rows(N_BRANCH * D_MODEL),
                  resident(w_br_all.shape[1]), resident(D_MODEL), rows(D_MODEL),
                  pl.BlockSpec((1, D_MODEL), lambda i: (0, 0))],
        out_specs=rows(D_MODEL),
        out_shape=jax.ShapeDtypeStruct((m, D_MODEL), F32),
        compiler_params=_cparams(("parallel",)),
        name="mixout",
    )(o_pool, o_dn, o_sb, o_ss, gates, w_br_all, w_out_all, x2, g)


def _mlp_kernel(x_ref, g1_ref, wu_ref, wd_ref, g2_ref, o_ref, h_ref, acc_ref, *, tm):
    f = pl.program_id(1)

    @pl.when(f == 0)
    def _():
        _norm_rows_to(h_ref, x_ref, g1_ref, tm)
        acc_ref[...] = jnp.zeros(acc_ref.shape, F32)

    a = jnp.dot(h_ref[...], wu_ref[...], preferred_element_type=F32)
    a = jnp.square(jnp.maximum(a, 0.0)).astype(BF16)
    acc_ref[...] += jnp.dot(a, wd_ref[...], preferred_element_type=F32)

    @pl.when(f == pl.num_programs(1) - 1)
    def _():
        o_ref[...] = x_ref[...] + _rms_rows(acc_ref[...], g2_ref[...])


def _mlp(x2, g1, wu_all, wd_all, layer, g2, *, tm, tf=1024):
    m = x2.shape[0]
    return pl.pallas_call(
        functools.partial(_mlp_kernel, tm=tm),
        grid=(m // tm, D_FF // tf),
        in_specs=[
            pl.BlockSpec((tm, D_MODEL), lambda i, f: (i, 0)),
            pl.BlockSpec((1, D_MODEL), lambda i, f: (0, 0)),
            pl.BlockSpec((None, D_MODEL, tf), lambda i, f: (layer, 0, f)),
            pl.BlockSpec((None, tf, D_MODEL), lambda i, f: (layer, f, 0)),
            pl.BlockSpec((1, D_MODEL), lambda i, f: (0, 0)),
        ],
        out_specs=pl.BlockSpec((tm, D_MODEL), lambda i, f: (i, 0)),
        out_shape=jax.ShapeDtypeStruct((m, D_MODEL), F32),
        scratch_shapes=[pltpu.VMEM((tm, D_MODEL), BF16), pltpu.VMEM((tm, D_MODEL), F32)],
        compiler_params=_cparams(("parallel", "arbitrary")),
        name="mlp",
    )(x2, g1, wu_all, wd_all, g2)


def _lane_rows(vals, offset):
    out = jnp.zeros((vals.shape[0], LANES), F32)
    for k in range(3):
        o = offset + k * SM_REP
        out = out.at[:, o:o + vals.shape[1]].set(vals.astype(F32))
    return out


def _prepare_params(p):
    depth = p["w_in"].shape[0]
    wt = jnp.transpose(p["w_in"], (0, 2, 1))
    small = wt[:, _O_DNB:_O_SB], wt[:, _O_SSDT:_O_GATE]
    gap = jnp.zeros((depth, SM_REP - SM_GROUP, D_MODEL), wt.dtype)
    w_u = jnp.concatenate(
        [wt[:, :_O_DNB], wt[:, _O_SB:_O_SSDT], *small, gap, *small, gap, *small,
         jnp.zeros((depth, NU - C_SMALL - 2 * SM_REP - SM_GROUP, D_MODEL), wt.dtype)], axis=1).astype(BF16)
    w_g = wt[:, _O_GATE:_O_END].astype(BF16)
    zrow = jnp.zeros((depth, LANES), F32)
    par = lambda a_log, dt_bias, off: jnp.stack(
        [_lane_rows(a_log, off), _lane_rows(dt_bias, off)] + [zrow] * (SUBLANES - 2), axis=1)
    return dict(
        w_u=w_u, w_g=w_g,
        n_mix_pre=p["norm_mix_pre"], n_mix_post=p["norm_mix_post"],
        n_mlp_pre=p["norm_mlp_pre"], n_mlp_post=p["norm_mlp_post"],
        pool_w=p["pool_w"].astype(BF16), pool_scale=p["pool_scale"],
        dn_conv_w=p["dn_conv_w"], dn_par=par(p["dn_a_log"], p["dn_dt_bias"], SM_DECAY), dn_norm_w=p["dn_norm_w"],
        sb_bias=p["sb_bias"],
        ssm_conv_w=p["ssm_conv_w"], ssm_conv_b=p["ssm_conv_b"],
        ss_par=par(p["ssm_a_log"], p["ssm_dt_bias"], SM_DT), ssm_norm_w=p["ssm_norm_w"],
        ssm_d=jnp.repeat(p["ssm_d"].astype(F32), SSM_HEAD_DIM, axis=1),
        w_branch=p["w_branch"].astype(BF16), w_out=p["w_out"].astype(BF16),
        w_up=p["w_up"].astype(BF16), w_down=p["w_down"].astype(BF16),
    )


def _trunk_layer(x, states, pw, layer, *, past=None):
    b, l, _ = x.shape
    m = b * l
    pool_buf, dn_conv, dn_s, ssm_conv, ssm_h = states
    prompt = past is None
    tm_big = min(m, 1024)
    tm = min(m, 512)
    lc = min(l, 256)
    lp = max(lc, 2 * DN_CHUNK)
    row = lambda name: pw[name][layer].reshape(1, -1)

    x2 = x.reshape(m, D_MODEL)
    u2 = _inproj(x2, row("n_mix_pre"), pw["w_u"], layer, tm=tm_big, tn=NU // 4)
    gates = _inproj(x2, row("n_mix_pre"), pw["w_g"], layer, tm=tm_big, tn=D_MODEL, gate=True)
    u3 = u2.reshape(b, l, NU)

    pos0 = 0 if prompt else past[2].shape[1] * (past[0].shape[2] // SB_HEADS)
    o_pool, pool_new = _pool(u3, pool_buf, pw["pool_w"][layer], row("pool_scale"), pos0=pos0)
    o_dn, dn_conv_new, dn_s_new = _deltanet(u3, dn_conv, dn_s, pw["dn_conv_w"][layer], pw["dn_par"][layer],
                                            row("dn_norm_w"), lc=lc, lp=lp)
    o_ss, ss_conv_new, ss_h_new = _ssd(u3, ssm_conv, ssm_h, pw["ssm_conv_w"][layer], row("ssm_conv_b"),
                                       pw["ss_par"][layer], row("ssm_norm_w"), row("ssm_d"), lc=lc, lp=lp)

    k_rows, v_rows = _kv_rows(u3)
    sb_bias = pw["sb_bias"][layer]
    if prompt:
        o_sb = _sb_prompt(u3, sb_bias)
    else:
        cache_k, cache_v, page_table = past
        ncol = cache_k.shape[2]
        q = u3[:, :, C_SBQ:C_SBQ + SB_WIDTH].reshape(b, l, SB_HEADS, SB_HEAD_DIM)
        q_rows = jnp.transpose(q, (0, 2, 1, 3)).reshape(b, SB_HEADS * l, SB_HEAD_DIM)
        k_cur = jnp.pad(k_rows, ((0, 0), (0, ncol - l * SB_HEADS), (0, 0)))
        v_cur = jnp.pad(v_rows, ((0, 0), (0, ncol - l * SB_HEADS), (0, 0)))
        bias_rows = jnp.broadcast_to(jnp.repeat(sb_bias, l)[:, None], (SB_HEADS * l, ncol)).astype(F32)
        acc = _sb_sample(q_rows, k_cur, v_cur, bias_rows, cache_k, cache_v, page_table, layer)
        o_sb = jnp.transpose(acc.reshape(b, SB_HEADS, l, SB_HEAD_DIM), (0, 2, 1, 3)).reshape(b, l, SB_WIDTH)

    x2 = _mixout(o_pool.reshape(m, -1), o_dn.reshape(m, -1), o_sb.reshape(m, -1), o_ss.reshape(m, -1),
                 gates, pw["w_branch"], pw["w_out"], layer, x2, row("n_mix_post"), tm=min(m, 256))
    x2 = _mlp(x2, row("n_mlp_pre"), pw["w_up"], pw["w_down"], layer, row("n_mlp_post"), tm=tm)
    new_states = (k_rows.reshape(b, l, SB_HEADS, SB_HEAD_DIM), v_rows.reshape(b, l, SB_HEADS, SB_HEAD_DIM),
                  pool_new, dn_conv_new, dn_s_new, ss_conv_new, ss_h_new)
    return x2.reshape(b, l, D_MODEL), new_states


def kernel(x_prompt, x_sample, cache_sb_k, cache_sb_v, state_pool, state_dn_conv, state_dn_s, state_ssm_conv, state_ssm_h, page_table, norm_mix_pre, norm_mix_post, norm_mlp_pre, norm_mlp_post, w_in, pool_w, pool_scale, dn_conv_w, dn_a_log, dn_dt_bias, dn_norm_w, sb_bias, ssm_conv_w, ssm_conv_b, ssm_a_log, ssm_dt_bias, ssm_d, ssm_norm_w, w_branch, w_out, w_up, w_down):
    pw = _prepare_params(dict(
        norm_mix_pre=norm_mix_pre, norm_mix_post=norm_mix_post, norm_mlp_pre=norm_mlp_pre,
        norm_mlp_post=norm_mlp_post, w_in=w_in, pool_w=pool_w, pool_scale=pool_scale,
        dn_conv_w=dn_conv_w, dn_a_log=dn_a_log, dn_dt_bias=dn_dt_bias, dn_norm_w=dn_norm_w,
        sb_bias=sb_bias, ssm_conv_w=ssm_conv_w, ssm_conv_b=ssm_conv_b, ssm_a_log=ssm_a_log,
        ssm_dt_bias=ssm_dt_bias, ssm_d=ssm_d, ssm_norm_w=ssm_norm_w, w_branch=w_branch,
        w_out=w_out, w_up=w_up, w_down=w_down))
    depth = w_in.shape[0]
    bp = x_prompt.shape[0]
    dt_ = x_prompt.dtype
    zero_states = (jnp.zeros((bp, POOL_BUF, POOL_WIDTH), dt_),
                   jnp.zeros((bp, CONV_WIDTH - 1, DN_CONV_DIM), dt_),
                   jnp.zeros((bp, DN_HEADS, DN_DK, DN_DV), dt_),
                   jnp.zeros((bp, CONV_WIDTH - 1, SSM_CONV_DIM), dt_),
                   jnp.zeros((bp, SSM_HEADS, SSM_HEAD_DIM, SSM_STATE), dt_))
    n_pool, page = cache_sb_k.shape[1], cache_sb_k.shape[2]
    cache_k = cache_sb_k.reshape(depth, n_pool, page * SB_HEADS, SB_HEAD_DIM)
    cache_v = cache_sb_v.reshape(depth, n_pool, page * SB_HEADS, SB_HEAD_DIM)
    y_prompt, y_sample = x_prompt, x_sample
    new_p, new_s = [], []
    for layer in range(depth):
        y_prompt, st_p = _trunk_layer(y_prompt, zero_states, pw, layer)
        sample_states = (state_pool[layer], state_dn_conv[layer], state_dn_s[layer],
                         state_ssm_conv[layer], state_ssm_h[layer])
        y_sample, st_s = _trunk_layer(y_sample, sample_states, pw, layer, past=(cache_k, cache_v, page_table))
        new_p.append(st_p)
        new_s.append(st_s)
    outs_p = [jnp.stack(t) for t in zip(*new_p)]
    outs_s = [jnp.stack(t) for t in zip(*new_s)]
    return (y_prompt, y_sample, *outs_p, *outs_s)
```

```python
import functools
import math

import jax
import jax.numpy as jnp
from jax import lax
from jax.experimental import pallas as pl
from jax.experimental.pallas import tpu as pltpu

F32 = jnp.float32
BF16 = jnp.bfloat16

D_MODEL = 2048
N_BRANCH = 4
RMS_EPS = 1e-6
CONV_WIDTH = 4
POOL_WINDOWS = (2, 4, 8, 16)
POOL_WIDTH = D_MODEL // 4
POOL_GW = POOL_WIDTH // 4
POOL_BUF = max(POOL_WINDOWS) - 1
DN_HEADS = 4
DN_DK = 128
DN_DV = 128
DN_QK = DN_HEADS * DN_DK
DN_VW = DN_HEADS * DN_DV
DN_CONV_DIM = 2 * DN_QK + DN_VW
DN_CHUNK = 64
SB_HEADS = 4
SB_HEAD_DIM = 128
SB_WIDTH = SB_HEADS * SB_HEAD_DIM
SSM_D_INNER = D_MODEL // 2
SSM_HEAD_DIM = 64
SSM_HEADS = SSM_D_INNER // SSM_HEAD_DIM
SSM_GROUPS = 2
SSM_HPG = SSM_HEADS // SSM_GROUPS
SSM_STATE = 128
SSM_CONV_DIM = SSM_D_INNER + 2 * SSM_GROUPS * SSM_STATE
SSM_CHUNK = 64
D_FF = 4 * D_MODEL

LANES = 128
SUBLANES = 8

C_POOL = 0
C_DNQ, C_DNK, C_DNV, C_DNZ = 512, 1024, 1536, 2048
C_SBQ, C_SBK, C_SBV = 2560, 3072, 3584
C_SSZ, C_SSX, C_SSBC = 4096, 5120, 6144
C_SMALL = 6656
NU = 7168
_O_DNB = POOL_WIDTH + DN_CONV_DIM + DN_VW
_O_SB = _O_DNB + 2 * DN_HEADS
_O_SSDT = _O_SB + 3 * SB_WIDTH + SSM_D_INNER + SSM_CONV_DIM
_O_GATE = _O_SSDT + SSM_HEADS
_O_END = _O_GATE + N_BRANCH * D_MODEL
SM_BETA, SM_DECAY, SM_DT = 0, DN_HEADS, 2 * DN_HEADS
SM_GROUP = 2 * DN_HEADS + SSM_HEADS
SM_REP = 32

VMEM_LIMIT_MB = 56


def _cparams(sem, vmem_mb=VMEM_LIMIT_MB):
    return pltpu.CompilerParams(dimension_semantics=sem, vmem_limit_bytes=vmem_mb * 1024 * 1024)


def _sigmoid(x):
    return 1.0 / (1.0 + jnp.exp(-x))


def _silu(x):
    hx = 0.5 * x
    return hx + hx * jnp.tanh(hx)


def _softplus(x):
    return jnp.maximum(x, 0.0) + jnp.log1p(jnp.exp(-jnp.abs(x)))


def _log_sigmoid(x):
    return jnp.minimum(x, 0.0) - jnp.log1p(jnp.exp(-jnp.abs(x)))


def _dot(a, b):
    return jnp.dot(a.astype(BF16), b.astype(BF16), preferred_element_type=F32)


def _dot_nt(a, b):
    return lax.dot_general(a.astype(BF16), b.astype(BF16), (((1,), (1,)), ((), ())), preferred_element_type=F32)


def _dot_tn(a, b):
    return lax.dot_general(a.astype(BF16), b.astype(BF16), (((0,), (0,)), ((), ())), preferred_element_type=F32)


def _split2(a):
    hi = a.astype(BF16)
    lo = (a - hi.astype(F32)).astype(BF16)
    return hi, lo


def _expand_heads(src, n_heads, width, lane0):
    k = lax.broadcasted_iota(jnp.int32, (LANES, n_heads * width), 0)
    n = lax.broadcasted_iota(jnp.int32, (LANES, n_heads * width), 1)
    sel = jnp.where((k % SM_REP - lane0 == n // width) & (k < 3 * SM_REP), 1.0, 0.0).astype(BF16)
    lane = lax.broadcasted_iota(jnp.int32, src.shape, 1)
    hi = src.astype(BF16).astype(F32)
    r1 = src - hi
    mid = r1.astype(BF16).astype(F32)
    lo = r1 - mid
    pieces = jnp.where(lane < SM_REP, hi, jnp.where(lane < 2 * SM_REP, mid, lo))
    return jnp.dot(pieces.astype(BF16), sel, preferred_element_type=F32)


def _dot3(a, b):
    ah, al = _split2(a)
    bh, bl = _split2(b)
    d = lambda x, y: jnp.dot(x, y, preferred_element_type=F32)
    return d(ah, bh) + (d(ah, bl) + d(al, bh))


def _rms_rows(x, g):
    return x * lax.rsqrt(jnp.mean(x * x, axis=-1, keepdims=True) + RMS_EPS) * g


def _norm_rows_to(h_ref, x_ref, g_ref, tm):
    ch = min(tm, 256)

    def body(r, c):
        rs = pl.ds(pl.multiple_of(r * ch, ch), ch)
        h_ref[rs, :] = _rms_rows(x_ref[rs, :], g_ref[...]).astype(h_ref.dtype)
        return c

    lax.fori_loop(0, tm // ch, body, 0)


def _inproj_kernel(x_ref, g_ref, w_ref, o_ref, h_ref, *, tm, gate):
    @pl.when(pl.program_id(1) == 0)
    def _():
        _norm_rows_to(h_ref, x_ref, g_ref, tm)

    acc = _dot_nt(h_ref[...], w_ref[...])
    o_ref[...] = (_sigmoid(acc) if gate else acc).astype(o_ref.dtype)


def _inproj(x2, g, w_all, layer, *, tm, tn=1024, gate=False):
    m = x2.shape[0]
    n = w_all.shape[1]
    return pl.pallas_call(
        functools.partial(_inproj_kernel, tm=tm, gate=gate),
        grid=(m // tm, n // tn),
        in_specs=[
            pl.BlockSpec((tm, D_MODEL), lambda i, j: (i, 0)),
            pl.BlockSpec((1, D_MODEL), lambda i, j: (0, 0)),
            pl.BlockSpec((None, tn, D_MODEL), lambda i, j: (layer, j, 0)),
        ],
        out_specs=pl.BlockSpec((tm, tn), lambda i, j: (i, j)),
        out_shape=jax.ShapeDtypeStruct((m, n), BF16 if gate else F32),
        scratch_shapes=[pltpu.VMEM((tm, D_MODEL), BF16)],
        compiler_params=_cparams(("parallel", "arbitrary")),
        name="inproj_gate" if gate else "inproj",
    )(x2, g, w_all)


def _pool_kernel(u_ref, buf_ref, w_ref, sc_ref, o_ref, new_ref, ext_ref, *, lin, lp, pos0):
    ext_ref[0:16, :] = jnp.zeros((16, POOL_WIDTH), F32)
    ext_ref[1:16, :] = buf_ref[0]
    if lin < lp:
        ext_ref[16:16 + lp, :] = jnp.zeros((lp, POOL_WIDTH), F32)
    ext_ref[16:16 + lin, :] = u_ref[0]
    ch = min(lp, 256)
    for c0 in range(0, lp, ch):
        pos = pos0 + c0 + lax.broadcasted_iota(jnp.int32, (ch, 1), 0)
        for gi, w in enumerate(POOL_WINDOWS):
            cols = slice(gi * POOL_GW, (gi + 1) * POOL_GW)
            s = ext_ref[16 + c0:16 + c0 + ch, cols]
            tot = s
            for k in range(1, w):
                tot = tot + ext_ref[16 + c0 - k:16 + c0 - k + ch, cols]
            cnt = jnp.minimum(pos + 1, w).astype(F32)
            y = tot / cnt - s
            yo = _dot(y, w_ref[gi]) * sc_ref[:, cols]
            n = min(ch, lin - c0)
            o_ref[0, c0:c0 + n, cols] = yo[:n]
    new_ref[0] = ext_ref[1 + lin:16 + lin, :]


def _pool(u3, buf, w, sc, *, pos0):
    b, lin, _ = u3.shape
    lp = max(lin, SUBLANES)
    return pl.pallas_call(
        functools.partial(_pool_kernel, lin=lin, lp=lp, pos0=pos0),
        grid=(b,),
        in_specs=[
            pl.BlockSpec((1, lin, POOL_WIDTH), lambda i: (i, 0, C_POOL // POOL_WIDTH)),
            pl.BlockSpec((1, POOL_BUF, POOL_WIDTH), lambda i: (i, 0, 0)),
            pl.BlockSpec((4, POOL_GW, POOL_GW), lambda i: (0, 0, 0)),
            pl.BlockSpec((1, POOL_WIDTH), lambda i: (0, 0)),
        ],
        out_specs=[
            pl.BlockSpec((1, lin, POOL_WIDTH), lambda i: (i, 0, 0)),
            pl.BlockSpec((1, POOL_BUF, POOL_WIDTH), lambda i: (i, 0, 0)),
        ],
        out_shape=[
            jax.ShapeDtypeStruct((b, lin, POOL_WIDTH), F32),
            jax.ShapeDtypeStruct((b, POOL_BUF, POOL_WIDTH), F32),
        ],
        scratch_shapes=[pltpu.VMEM((16 + lp, POOL_WIDTH), F32)],
        compiler_params=_cparams(("parallel",)),
        name="pool",
    )(u3, buf, w, sc)


ROWS = 64


def _fill_ext(ext_ref, cbuf_ref, parts, *, lin, lp):
    l = pl.program_id(1)

    @pl.when(l == 0)
    def _():
        ext_ref[5:8, :] = cbuf_ref[0]

    @pl.when(l > 0)
    def _():
        ext_ref[5:8, :] = ext_ref[5 + lin:8 + lin, :]

    if lin < lp:
        ext_ref[8:8 + lp, :] = jnp.zeros((lp, ext_ref.shape[1]), F32)
    for ref, c0, width in parts:
        ext_ref[8:8 + lin, c0:c0 + width] = ref[0]


def _conv_silu(ext_ref, r0, n, cols, w_ref, b_ref):
    acc = None
    for i in range(CONV_WIDTH):
        part = ext_ref[5 + r0 + i:5 + r0 + i + n, cols] * w_ref[i:i + 1, cols]
        acc = part if acc is None else acc + part
    if b_ref is not None:
        acc = acc + b_ref[:, cols]
    return _silu(acc)


def _pad_rows(dst_ref, src_ref, *, lin, lp):
    if lin < lp:
        dst_ref[...] = jnp.zeros(dst_ref.shape, F32)
    dst_ref[0:lin, :] = src_ref[0]


def _row_valid(r0, n, lin):
    if r0 + n <= lin:
        return None
    return (r0 + lax.broadcasted_iota(jnp.int32, (n, 1), 0)) < lin


def _chunk_cumsum(x, chunk):
    rin = lax.broadcasted_iota(jnp.int32, x.shape, 0) % chunk
    s = 1
    while s < chunk:
        x = x + jnp.where(rin >= s, pltpu.roll(x, s, 0), 0.0)
        s *= 2
    return x


def _diag_inv2(a0, a1):
    c = a0.shape[0]
    hb = c // 2
    ng = hb // SUBLANES
    sub = lax.broadcasted_iota(jnp.int32, (SUBLANES, 2 * c), 0)
    lane = lax.broadcasted_iota(jnp.int32, (SUBLANES, 2 * c), 1)
    base = (lane // hb) * hb
    lmod = lane - base
    a01 = jnp.concatenate([a0, a1], axis=1)
    odd = ((lax.broadcasted_iota(jnp.int32, (hb, 2 * c), 1) // hb) % 2) == 1
    packed = jnp.where(odd, a01[hb:], a01[:hb])
    racc = [jnp.zeros((SUBLANES, 2 * c), F32) for _ in range(ng)]
    tg = [jnp.zeros((SUBLANES, 2 * c), F32) for _ in range(ng)]
    for j in range(hb):
        gj, rj = divmod(j, SUBLANES)
        t_j = jnp.where(lmod[0:1] == j, 1.0, 0.0) - racc[gj][rj:rj + 1, :]
        tg[gj] = jnp.where(sub == rj, t_j, tg[gj])
        if j == hb - 1:
            break
        for g in range(gj, ng):
            col = jnp.take_along_axis(packed[g * SUBLANES:(g + 1) * SUBLANES], base + j, axis=1)
            racc[g] = racc[g] + col * t_j
    dinv = jnp.concatenate(tg, axis=0)
    bd = jnp.concatenate([jnp.where(odd, 0.0, dinv), jnp.where(odd, dinv, 0.0)], axis=0)
    return [bd[:, :c], bd[:, c:]]


def _dn_kernel(q_ref, k_ref, v_ref, z_ref, sm_ref, cbuf_ref, s0_ref, cw_ref, par_ref, nw_ref,
               o_ref, cnew_ref, snew_ref,
               ext_ref, qn_ref, kn_ref, kb_ref, zz_ref, smp_ref, be_ref, ge_ref, gt_ref, oo_ref, s_ref,
               uc_ref, wc_ref, qe_ref, kt_ref, at_ref, rhs_ref, *, lin, lp, chunk):
    l = pl.program_id(1)
    nch = lp // chunk

    @pl.when(l == 0)
    def _():
        s_ref[...] = s0_ref[0]

    _fill_ext(ext_ref, cbuf_ref, ((q_ref, 0, DN_QK), (k_ref, DN_QK, DN_QK), (v_ref, 2 * DN_QK, DN_VW)), lin=lin, lp=lp)
    _pad_rows(zz_ref, z_ref, lin=lin, lp=lp)
    _pad_rows(smp_ref, sm_ref, lin=lin, lp=lp)

    sm = smp_ref[...]
    beta = _sigmoid(sm)
    g = -jnp.exp(par_ref[0:1, :]) * _softplus(sm + par_ref[1:2, :])
    valid = _row_valid(0, lp, lin)
    if valid is not None:
        beta = jnp.where(valid, beta, 0.0)
        g = jnp.where(valid, g, 0.0)
    gcum = _chunk_cumsum(g, chunk)
    be_ref[...] = _expand_heads(beta, DN_HEADS, DN_DK, SM_BETA)
    ge_ref[...] = _expand_heads(gcum, DN_HEADS, DN_DK, SM_DECAY)
    gt = gcum.T
    for c in range(nch):
        gt_ref[c] = gt[:, c * chunk:(c + 1) * chunk]

    assert ROWS == chunk
    for r0 in range(0, lp, ROWS):
        valid = _row_valid(r0, ROWS, lin)
        rows = slice(r0, r0 + ROWS)
        for h in range(DN_HEADS):
            hs = slice(h * DN_DK, (h + 1) * DN_DK)
            conv = lambda part: _conv_silu(ext_ref, r0, ROWS, slice(part * DN_QK + h * DN_DK, part * DN_QK + (h + 1) * DN_DK), cw_ref, None)
            l2n = lambda y: y * lax.rsqrt(jnp.sum(y * y, axis=-1, keepdims=True) + 1e-6)
            q = l2n(conv(0)) * (DN_DK ** -0.5)
            k = l2n(conv(1))
            v = conv(2)
            if valid is not None:
                q, k, v = (jnp.where(valid, t, 0.0) for t in (q, k, v))
            bt = be_ref[rows, hs]
            ge = ge_ref[rows, hs]
            eg = jnp.exp(ge)
            kb = k * bt
            qn_ref[rows, hs] = q
            kn_ref[rows, hs] = k
            kb_ref[rows, hs] = kb
            qe_ref[rows, hs] = q * eg
            kt_ref[rows, hs] = k * jnp.exp(ge[ROWS - 1:ROWS, :] - ge)
            rhs_ref[h, rows, 0:DN_DV] = v * bt
            rhs_ref[h, rows, DN_DV:2 * DN_DV] = kb * eg

    rid = lax.broadcasted_iota(jnp.int32, (chunk, chunk), 0)
    cid = lax.broadcasted_iota(jnp.int32, (chunk, chunk), 1)
    strict_lower = cid < rid
    lower = cid <= rid
    off = (rid >= chunk // 2) & (cid < chunk // 2)
    heads = range(DN_HEADS)
    hsl = [slice(h * DN_DK, (h + 1) * DN_DK) for h in heads]

    per = 4 if nch % 4 == 0 else 2

    def prep_body(c2, carry):
        items = [(i, h) for i in range(per) for h in heads]
        cidx = [per * c2 + i for i in range(per)]
        rs = [pl.ds(pl.multiple_of(c * chunk, chunk), chunk) for c in cidx]
        kh = [kn_ref[rs[c], hsl[h]] for c, h in items]
        decay = [jnp.exp(jnp.where(lower, ge_ref[rs[c], h * DN_DK:h * DN_DK + chunk]
                                   - gt_ref[cidx[c], SM_DECAY + h:SM_DECAY + h + 1, :], -jnp.inf)) for c, h in items]
        amat = [_dot_nt(kb_ref[rs[c], hsl[h]], kh[n]) * jnp.where(strict_lower, decay[n], 0.0)
                for n, (c, h) in enumerate(items)]
        for n, (c, h) in enumerate(items):
            at_ref[cidx[c] * DN_HEADS + h] = _dot_nt(qn_ref[rs[c], hsl[h]], kh[n]) * decay[n]
        dinv = sum((_diag_inv2(amat[n], amat[n + 1]) for n in range(0, len(items), 2)), [])
        inner = [_dot3(jnp.where(off, a, 0.0), d) for a, d in zip(amat, dinv)]
        tmat = [d - _dot3(d, i) for d, i in zip(dinv, inner)]
        sol = [_dot3(t, rhs_ref[h, rs[c], :]) for t, (c, h) in zip(tmat, items)]
        for s, (c, h) in zip(sol, items):
            uc_ref[rs[c], hsl[h]] = s[:, :DN_DV]
            wc_ref[rs[c], hsl[h]] = s[:, DN_DV:]
        return carry

    assert nch % per == 0
    lax.fori_loop(0, nch // per, prep_body, 0)

    def scan_body(c, carry):
        r0 = pl.multiple_of(c * chunk, chunk)
        rs = pl.ds(r0, chunk)
        glast = ge_ref[pl.ds(r0 + chunk - 1, 1), :]
        sh = [s_ref[h] for h in heads]
        ws = [_dot(wc_ref[rs, hsl[h]], sh[h]) for h in heads]
        qs = [_dot(qe_ref[rs, hsl[h]], sh[h]) for h in heads]
        v_new = [uc_ref[rs, hsl[h]] - ws[h] for h in heads]
        o2 = [_dot(at_ref[c * DN_HEADS + h], v_new[h]) for h in heads]
        kv = [_dot_tn(kt_ref[rs, hsl[h]], v_new[h]) for h in heads]
        for h in heads:
            s_ref[h] = sh[h] * jnp.exp(glast[:, hsl[h]]) + kv[h]
            oo_ref[rs, hsl[h]] = _rms_rows(qs[h] + o2[h], nw_ref[...]) * _silu(zz_ref[rs, hsl[h]])
        return carry

    lax.fori_loop(0, nch, scan_body, 0)
    o_ref[0] = oo_ref[0:lin, :]

    @pl.when(l == pl.num_programs(1) - 1)
    def _():
        cnew_ref[0] = ext_ref[5 + lin:8 + lin, :]
        snew_ref[0] = s_ref[...]


def _deltanet(u3, cbuf, s0, cw, par, nw, *, lc, lp):
    b, l, _ = u3.shape
    nl = l // lc
    blk = lambda c0: pl.BlockSpec((1, lc, 512), lambda i, j: (i, j, c0 // 512))
    return pl.pallas_call(
        functools.partial(_dn_kernel, lin=lc, lp=lp, chunk=DN_CHUNK),
        grid=(b, nl),
        in_specs=[
            blk(C_DNQ), blk(C_DNK), blk(C_DNV), blk(C_DNZ),
            pl.BlockSpec((1, lc, LANES), lambda i, j: (i, j, C_SMALL // LANES)),
            pl.BlockSpec((1, CONV_WIDTH - 1, DN_CONV_DIM), lambda i, j: (i, 0, 0)),
            pl.BlockSpec((1, DN_HEADS, DN_DK, DN_DV), lambda i, j: (i, 0, 0, 0)),
            pl.BlockSpec((CONV_WIDTH, DN_CONV_DIM), lambda i, j: (0, 0)),
            pl.BlockSpec((SUBLANES, LANES), lambda i, j: (0, 0)),
            pl.BlockSpec((1, DN_DV), lambda i, j: (0, 0)),
        ],
        out_specs=[
            pl.BlockSpec((1, lc, DN_VW), lambda i, j: (i, j, 0)),
            pl.BlockSpec((1, CONV_WIDTH - 1, DN_CONV_DIM), lambda i, j: (i, 0, 0)),
            pl.BlockSpec((1, DN_HEADS, DN_DK, DN_DV), lambda i, j: (i, 0, 0, 0)),
        ],
        out_shape=[
            jax.ShapeDtypeStruct((b, l, DN_VW), F32),
            jax.ShapeDtypeStruct((b, CONV_WIDTH - 1, DN_CONV_DIM), F32),
            jax.ShapeDtypeStruct((b, DN_HEADS, DN_DK, DN_DV), F32),
        ],
        scratch_shapes=[
            pltpu.VMEM((8 + lp, DN_CONV_DIM), F32),
            pltpu.VMEM((lp, DN_QK), F32), pltpu.VMEM((lp, DN_QK), F32), pltpu.VMEM((lp, DN_VW), F32),
            pltpu.VMEM((lp, DN_VW), F32), pltpu.VMEM((lp, LANES), F32),
            pltpu.VMEM((lp, DN_QK), F32), pltpu.VMEM((lp, DN_QK), F32),
            pltpu.VMEM((lp // DN_CHUNK, LANES, DN_CHUNK), F32),
            pltpu.VMEM((lp, DN_VW), F32),
            pltpu.VMEM((DN_HEADS, DN_DK, DN_DV), F32),
            pltpu.VMEM((lp, DN_VW), F32), pltpu.VMEM((lp, DN_QK), F32),
            pltpu.VMEM((lp, DN_QK), F32), pltpu.VMEM((lp, DN_QK), F32),
            pltpu.VMEM((lp // DN_CHUNK * DN_HEADS, DN_CHUNK, DN_CHUNK), F32),
            pltpu.VMEM((DN_HEADS, lp, 2 * DN_DV), F32),
        ],
        compiler_params=_cparams(("parallel", "arbitrary")),
        name="deltanet",
    )(u3, u3, u3, u3, u3, cbuf, s0, cw, par, nw)


def _ssd_kernel(z_ref, x_ref, bc_ref, sm_ref, cbuf_ref, h0_ref, cw_ref, cb_ref, par_ref, nw_ref, dsk_ref,
                o_ref, cnew_ref, hnew_ref,
                ext_ref, xs_ref, bm_ref, cm_ref, zz_ref, smp_ref, ce_ref, de_ref, ct_ref, yy_ref, ht_ref,
                *, lin, lp, chunk):
    assert chunk == SSM_HEAD_DIM
    l = pl.program_id(1)
    nch = lp // chunk
    gn = SSM_GROUPS * SSM_STATE

    @pl.when(l == 0)
    def _():
        for g in range(SSM_GROUPS):
            hg = h0_ref[0, g * SSM_HPG:(g + 1) * SSM_HPG]
            ht_ref[g] = hg.reshape(SSM_HPG * SSM_HEAD_DIM, SSM_STATE).T

    _fill_ext(ext_ref, cbuf_ref, ((x_ref, 0, SSM_D_INNER), (bc_ref, SSM_D_INNER, 2 * gn)), lin=lin, lp=lp)
    _pad_rows(zz_ref, z_ref, lin=lin, lp=lp)
    _pad_rows(smp_ref, sm_ref, lin=lin, lp=lp)

    for r0 in range(0, lp, ROWS):
        valid = _row_valid(r0, ROWS, lin)
        for c0 in range(0, SSM_CONV_DIM, LANES):
            y = _conv_silu(ext_ref, r0, ROWS, slice(c0, c0 + LANES), cw_ref, cb_ref)
            if valid is not None:
                y = jnp.where(valid, y, 0.0)
            if c0 < SSM_D_INNER:
                xs_ref[r0:r0 + ROWS, c0:c0 + LANES] = y
            elif c0 < SSM_D_INNER + gn:
                bm_ref[r0:r0 + ROWS, c0 - SSM_D_INNER:c0 - SSM_D_INNER + LANES] = y
            else:
                cm_ref[r0:r0 + ROWS, c0 - SSM_D_INNER - gn:c0 - SSM_D_INNER - gn + LANES] = y

    sm = smp_ref[...]
    dt = _softplus(sm + par_ref[1:2, :])
    valid = _row_valid(0, lp, lin)
    if valid is not None:
        dt = jnp.where(valid, dt, 0.0)
    cum = _chunk_cumsum(dt * (-jnp.exp(par_ref[0:1, :])), chunk)
    ct = cum.T
    for c in range(nch):
        ct_ref[c] = ct[:, c * chunk:(c + 1) * chunk]

    spread = _expand_heads(jnp.concatenate([cum, dt], axis=0), SSM_HEADS, SSM_HEAD_DIM, SM_DT)
    ce_ref[...] = spread[:lp]
    de_ref[...] = spread[lp:]

    pw = 2 * SSM_HEAD_DIM
    row2 = lax.broadcasted_iota(jnp.int32, (chunk, 2 * chunk), 0)
    lane2 = lax.broadcasted_iota(jnp.int32, (chunk, 2 * chunk), 1)
    lower2 = (lane2 % chunk) <= row2
    first = lax.broadcasted_iota(jnp.int32, (chunk, pw), 1) < SSM_HEAD_DIM
    groups = range(SSM_GROUPS)
    gw_ = SSM_HPG * SSM_HEAD_DIM

    def chunk_body(c, carry):
        r0 = pl.multiple_of(c * chunk, chunk)
        rs = pl.ds(r0, chunk)
        bg = [bm_ref[rs, g * SSM_STATE:(g + 1) * SSM_STATE] for g in groups]
        cg = [cm_ref[rs, g * SSM_STATE:(g + 1) * SSM_STATE] for g in groups]
        ce = [ce_ref[rs, g * gw_:(g + 1) * gw_] for g in groups]
        de = [de_ref[rs, g * gw_:(g + 1) * gw_] for g in groups]
        xg = [xs_ref[rs, g * gw_:(g + 1) * gw_] for g in groups]
        ht = [ht_ref[g] for g in groups]
        cb = [_dot_nt(cg[g], bg[g]) for g in groups]
        ys = [_dot(cg[g], ht[g]) * jnp.exp(ce[g]) for g in groups]
        for g in groups:
            last = ce[g][chunk - 1:chunk, :]
            ht_ref[g] = ht[g] * jnp.exp(last) + _dot_tn(bg[g], xg[g] * (jnp.exp(last - ce[g]) * de[g]))
        for g in groups:
            cb2 = jnp.concatenate([cb[g], cb[g]], axis=1)
            xd = xg[g] * de[g]
            for p in range(SSM_HPG // 2):
                ps = slice(p * pw, (p + 1) * pw)
                hd = g * SSM_HPG + 2 * p
                ctp = ct_ref[c, SM_DT + hd:SM_DT + hd + 2, :]
                crp = jnp.concatenate([ctp[0:1], ctp[1:2]], axis=1)
                lm = jnp.exp(jnp.where(lower2, ce[g][:, ps] - crp, -jnp.inf))
                xdp = xd[:, ps]
                bd = jnp.concatenate([jnp.where(first, xdp, 0.0), jnp.where(first, 0.0, xdp)], axis=0)
                cols = slice(g * gw_ + p * pw, g * gw_ + (p + 1) * pw)
                yy_ref[rs, cols] = _dot(cb2 * lm, bd) + ys[g][:, ps] + dsk_ref[:, cols] * xg[g][:, ps]
        return carry

    lax.fori_loop(0, nch, chunk_body, 0)

    gw = SSM_D_INNER // SSM_GROUPS
    for r0 in range(0, lp, ROWS):
        n = min(ROWS, lin - r0)
        if n <= 0:
            break
        for g in range(SSM_GROUPS):
            cols = slice(g * gw, (g + 1) * gw)
            t = yy_ref[r0:r0 + ROWS, cols] * _silu(zz_ref[r0:r0 + ROWS, cols])
            t = _rms_rows(t, nw_ref[:, cols])
            o_ref[0, r0:r0 + n, cols] = t[:n]

    @pl.when(l == pl.num_programs(1) - 1)
    def _():
        cnew_ref[0] = ext_ref[5 + lin:8 + lin, :]
        for g in range(SSM_GROUPS):
            hnew_ref[0, g * SSM_HPG:(g + 1) * SSM_HPG] = ht_ref[g].T.reshape(SSM_HPG, SSM_HEAD_DIM, SSM_STATE)


def _ssd(u3, cbuf, h0, cw, cb, par, nw, d, *, lc, lp):
    b, l, _ = u3.shape
    nl = l // lc
    gn2 = 2 * SSM_GROUPS * SSM_STATE
    return pl.pallas_call(
        functools.partial(_ssd_kernel, lin=lc, lp=lp, chunk=SSM_CHUNK),
        grid=(b, nl),
        in_specs=[
            pl.BlockSpec((1, lc, SSM_D_INNER), lambda i, j: (i, j, C_SSZ // SSM_D_INNER)),
            pl.BlockSpec((1, lc, SSM_D_INNER), lambda i, j: (i, j, C_SSX // SSM_D_INNER)),
            pl.BlockSpec((1, lc, gn2), lambda i, j: (i, j, C_SSBC // gn2)),
            pl.BlockSpec((1, lc, LANES), lambda i, j: (i, j, C_SMALL // LANES)),
            pl.BlockSpec((1, CONV_WIDTH - 1, SSM_CONV_DIM), lambda i, j: (i, 0, 0)),
            pl.BlockSpec((1, SSM_HEADS, SSM_HEAD_DIM, SSM_STATE), lambda i, j: (i, 0, 0, 0)),
            pl.BlockSpec((CONV_WIDTH, SSM_CONV_DIM), lambda i, j: (0, 0)),
            pl.BlockSpec((1, SSM_CONV_DIM), lambda i, j: (0, 0)),
            pl.BlockSpec((SUBLANES, LANES), lambda i, j: (0, 0)),
            pl.BlockSpec((1, SSM_D_INNER), lambda i, j: (0, 0)),
            pl.BlockSpec((1, SSM_D_INNER), lambda i, j: (0, 0)),
        ],
        out_specs=[
            pl.BlockSpec((1, lc, SSM_D_INNER), lambda i, j: (i, j, 0)),
            pl.BlockSpec((1, CONV_WIDTH - 1, SSM_CONV_DIM), lambda i, j: (i, 0, 0)),
            pl.BlockSpec((1, SSM_HEADS, SSM_HEAD_DIM, SSM_STATE), lambda i, j: (i, 0, 0, 0)),
        ],
        out_shape=[
            jax.ShapeDtypeStruct((b, l, SSM_D_INNER), F32),
            jax.ShapeDtypeStruct((b, CONV_WIDTH - 1, SSM_CONV_DIM), F32),
            jax.ShapeDtypeStruct((b, SSM_HEADS, SSM_HEAD_DIM, SSM_STATE), F32),
        ],
        scratch_shapes=[
            pltpu.VMEM((8 + lp, SSM_CONV_DIM), F32),
            pltpu.VMEM((lp, SSM_D_INNER), F32),
            pltpu.VMEM((lp, SSM_GROUPS * SSM_STATE), F32), pltpu.VMEM((lp, SSM_GROUPS * SSM_STATE), F32),
            pltpu.VMEM((lp, SSM_D_INNER), F32), pltpu.VMEM((lp, LANES), F32),
            pltpu.VMEM((lp, SSM_D_INNER), F32), pltpu.VMEM((lp, SSM_D_INNER), F32),
            pltpu.VMEM((lp // SSM_CHUNK, LANES, SSM_CHUNK), F32),
            pltpu.VMEM((lp, SSM_D_INNER), F32),
            pltpu.VMEM((SSM_GROUPS, SSM_STATE, SSM_HPG * SSM_HEAD_DIM), F32),
        ],
        compiler_params=_cparams(("parallel", "arbitrary")),
        name="ssd",
    )(u3, u3, u3, u3, cbuf, h0, cw, cb, par, nw, d)


def _strict_upper_stack(n):
    j = lax.broadcasted_iota(jnp.int32, (2 * n, n), 0) % n
    s = lax.broadcasted_iota(jnp.int32, (2 * n, n), 1)
    return jnp.where(j > s, 1.0, 0.0).astype(BF16)


def _rev_excl_cumsum(la, uu):
    hi, lo = _split2(la)
    return jnp.dot(jnp.concatenate([hi, lo], axis=1), uu, preferred_element_type=F32)


def _sbp_kernel(bias_ref, q_ref, k_ref, v_ref, o_ref, *, tq, scale):
    h = pl.program_id(1)
    qi = pl.program_id(2)
    bias = bias_ref[h]
    q = (q_ref[0] * scale).astype(BF16)
    uu = _strict_upper_stack(tq)

    def sweep(blocks, carry, diagonal):
        c, acc = carry
        rows = [pl.ds(kj * tq if isinstance(kj, int) else pl.multiple_of(kj * tq, tq), tq) for kj in blocks]
        zs = [_dot_nt(q, k_ref[0, r, :]) + bias for r in rows]
        lss, las = [], []
        for z in zs:
            l1 = jnp.log(1.0 + jnp.exp(-jnp.abs(z)))
            lss.append(jnp.minimum(z, 0.0) - l1)
            las.append(-jnp.maximum(z, 0.0) - l1)
        if diagonal:
            valid = lax.broadcasted_iota(jnp.int32, (tq, tq), 1) < lax.broadcasted_iota(jnp.int32, (tq, tq), 0)
            las = [jnp.where(valid, la, 0.0) for la in las]
        survs = [_rev_excl_cumsum(la, uu) for la in las]
        for ls, la, surv, r in zip(lss, las, survs, rows):
            att = jnp.exp(ls + surv + c)
            if diagonal:
                att = jnp.where(valid, att, 0.0)
            acc = acc + _dot(att, v_ref[0, r, :])
            c = c + jnp.sum(la, axis=1, keepdims=True)
        return c, acc

    carry = (jnp.zeros((tq, 1), F32), jnp.zeros((tq, SB_HEAD_DIM), F32))
    carry = sweep([qi], carry, True)
    n4 = lax.shift_right_logical(qi, 2)
    carry = lax.fori_loop(0, n4, lambda t, cr: sweep([qi - 1 - 4 * t - i for i in range(4)], cr, False), carry)
    nxt = qi - 1 - 4 * n4
    carry = lax.fori_loop(0, lax.shift_right_logical(qi, 1) & 1, lambda t, cr: sweep([nxt, nxt - 1], cr, False), carry)
    carry = lax.fori_loop(0, qi & 1, lambda t, cr: sweep([0], cr, False), carry)
    o_ref[0] = carry[1]


def _sb_prompt(u3, bias, *, tq=256):
    b, l, _ = u3.shape
    tq = min(tq, l)
    kv = lambda c0: pl.BlockSpec((1, l, SB_HEAD_DIM), lambda i, h, j: (i, 0, c0 // SB_HEAD_DIM + h))
    return pl.pallas_call(
        functools.partial(_sbp_kernel, tq=tq, scale=SB_HEAD_DIM ** -0.5),
        grid=(b, SB_HEADS, l // tq),
        in_specs=[
            pl.BlockSpec(memory_space=pltpu.SMEM),
            pl.BlockSpec((1, tq, SB_HEAD_DIM), lambda i, h, j: (i, j, C_SBQ // SB_HEAD_DIM + h)),
            kv(C_SBK), kv(C_SBV),
        ],
        out_specs=pl.BlockSpec((1, tq, SB_HEAD_DIM), lambda i, h, j: (i, j, h)),
        out_shape=jax.ShapeDtypeStruct((b, l, SB_WIDTH), F32),
        compiler_params=_cparams(("parallel", "parallel", "arbitrary")),
        name="sb_prompt",
    )(bias, u3, u3, u3)


def _kvrows_kernel(k_ref, v_ref, ko_ref, vo_ref, *, tl):
    for h in range(SB_HEADS):
        rows = pl.ds(h, tl, stride=SB_HEADS)
        ko_ref[0, rows, :] = k_ref[0, :, h * SB_HEAD_DIM:(h + 1) * SB_HEAD_DIM]
        vo_ref[0, rows, :] = v_ref[0, :, h * SB_HEAD_DIM:(h + 1) * SB_HEAD_DIM]


def _kv_rows(u3, *, tl=512):
    b, l, _ = u3.shape
    tl = min(tl, l)
    src = lambda c0: pl.BlockSpec((1, tl, SB_WIDTH), lambda i, j: (i, j, c0 // SB_WIDTH))
    dst = pl.BlockSpec((1, tl * SB_HEADS, SB_HEAD_DIM), lambda i, j: (i, j, 0))
    shape = jax.ShapeDtypeStruct((b, l * SB_HEADS, SB_HEAD_DIM), F32)
    return pl.pallas_call(
        functools.partial(_kvrows_kernel, tl=tl),
        grid=(b, l // tl),
        in_specs=[src(C_SBK), src(C_SBV)],
        out_specs=[dst, dst],
        out_shape=[shape, shape],
        compiler_params=_cparams(("parallel", "parallel")),
        name="kv_rows",
    )(u3, u3)


def _sbs_kernel(pt_ref, q_ref, kc_ref, vc_ref, bias_ref, uu_ref, *rest, pp, tq, scale):
    k_refs = rest[:pp]
    v_refs = rest[pp:2 * pp]
    o_ref = rest[2 * pp]
    c_ref = rest[2 * pp + 1]
    p = pl.program_id(1)
    hq = q_ref.shape[1]
    ncol = kc_ref.shape[1]
    q = q_ref[0].astype(BF16)
    bias = bias_ref[...]
    row_head = lax.broadcasted_iota(jnp.int32, (hq, ncol), 0) // tq
    col = lax.broadcasted_iota(jnp.int32, (hq, ncol), 1)
    own = (col % SB_HEADS) == row_head

    def rev_cumsum(la):
        hi, lo = _split2(la)
        return jnp.dot(jnp.concatenate([hi, lo], axis=1), uu_ref[...], preferred_element_type=F32)

    @pl.when(p == 0)
    def _():
        z = _dot_nt(q, kc_ref[0]) * scale + bias
        t = lax.broadcasted_iota(jnp.int32, (hq, ncol), 0) % tq
        valid = own & ((col // SB_HEADS) < t)
        ls = _log_sigmoid(z)
        la = jnp.where(valid, ls - z, 0.0)
        att = jnp.where(valid, jnp.exp(ls + rev_cumsum(la)), 0.0)
        o_ref[0] = _dot(att, vc_ref[0])
        c_ref[...] = jnp.sum(la, axis=1, keepdims=True)

    z = jnp.concatenate([_dot_nt(q, k_refs[j][...]) for j in range(pp)], axis=0)
    z = z * scale + jnp.concatenate([bias] * pp, axis=0)
    valid = jnp.concatenate([own] * pp, axis=0)
    ls = _log_sigmoid(z)
    la = jnp.where(valid, ls - z, 0.0)
    surv = rev_cumsum(la)
    tot = jnp.sum(la, axis=1, keepdims=True)
    cur = c_ref[...]
    cs = []
    for j in range(pp):
        cs.append(cur)
        cur = cur + tot[j * hq:(j + 1) * hq]
    c_ref[...] = cur
    att = jnp.where(valid, jnp.exp(ls + surv + jnp.concatenate(cs, axis=0)), 0.0)
    acc = o_ref[0]
    for j in range(pp):
        acc = acc + _dot(att[j * hq:(j + 1) * hq], v_refs[j][...])
    o_ref[0] = acc


def _sb_sample(q_rows, k_cur, v_cur, bias_rows, cache_k, cache_v, page_table, layer, *, pp=32):
    b, hq, _ = q_rows.shape
    n_pages = page_table.shape[1]
    ncol = cache_k.shape[2]
    pp = math.gcd(pp, n_pages)
    tq = hq // SB_HEADS
    jj = lax.broadcasted_iota(jnp.int32, (2 * ncol, ncol), 0) % ncol
    ss = lax.broadcasted_iota(jnp.int32, (2 * ncol, ncol), 1)
    uu = jnp.where(jj > ss, 1.0, 0.0).astype(BF16)

    def page_spec(j):
        return pl.BlockSpec((None, None, ncol, SB_HEAD_DIM),
                            lambda i, p, pt: (layer, pt[i, n_pages - 1 - (p * pp + j)], 0, 0))

    grid_spec = pltpu.PrefetchScalarGridSpec(
        num_scalar_prefetch=1,
        grid=(b, n_pages // pp),
        in_specs=[
            pl.BlockSpec((1, hq, SB_HEAD_DIM), lambda i, p, pt: (i, 0, 0)),
            pl.BlockSpec((1, ncol, SB_HEAD_DIM), lambda i, p, pt: (i, 0, 0)),
            pl.BlockSpec((1, ncol, SB_HEAD_DIM), lambda i, p, pt: (i, 0, 0)),
            pl.BlockSpec((hq, ncol), lambda i, p, pt: (0, 0)),
            pl.BlockSpec((2 * ncol, ncol), lambda i, p, pt: (0, 0)),
        ] + [page_spec(j) for j in range(pp)] + [page_spec(j) for j in range(pp)],
        out_specs=pl.BlockSpec((1, hq, SB_HEAD_DIM), lambda i, p, pt: (i, 0, 0)),
        scratch_shapes=[pltpu.VMEM((hq, 1), F32)],
    )
    return pl.pallas_call(
        functools.partial(_sbs_kernel, pp=pp, tq=tq, scale=SB_HEAD_DIM ** -0.5),
        grid_spec=grid_spec,
        out_shape=jax.ShapeDtypeStruct((b, hq, SB_HEAD_DIM), F32),
        compiler_params=_cparams(("parallel", "arbitrary")),
        name="sb_sample",
    )(page_table, q_rows, k_cur, v_cur, bias_rows, uu, *([cache_k] * pp), *([cache_v] * pp))


def _mixout_kernel(op_ref, od_ref, os_ref, oa_ref, gt_ref, wb_ref, wo_ref, x_ref, g_ref, o_ref):
    gate = lambda k: gt_ref[:, k * D_MODEL:(k + 1) * D_MODEL].astype(F32)
    br = lambda o_ref_, r0, n: _dot(o_ref_[...], wb_ref[r0:r0 + n, :])
    acc = gate(0) * br(op_ref, 0, POOL_WIDTH)
    acc = acc + gate(1) * br(od_ref, POOL_WIDTH, DN_VW)
    acc = acc + gate(2) * br(os_ref, POOL_WIDTH + DN_VW, SB_WIDTH)
    acc = acc + gate(3) * br(oa_ref, POOL_WIDTH + DN_VW + SB_WIDTH, SSM_D_INNER)
    mix = jnp.dot(acc.astype(BF16), wo_ref[...], preferred_element_type=F32)
    o_ref[...] = x_ref[...] + _rms_rows(mix, g_ref[...])


def _mixout(o_pool, o_dn, o_sb, o_ss, gates, w_br_all, w_out_all, layer, x2, g, *, tm):
    m = x2.shape[0]
    rows = lambda w: pl.BlockSpec((tm, w), lambda i: (i, 0))
    resident = lambda r: pl.BlockSpec((None, r, D_MODEL), lambda i: (layer, 0, 0), pipeline_mode=pl.Buffered(1))
    return pl.pallas_call(
        _mixout_kernel,
        grid=(m // tm,),
        in_specs=[rows(POOL_WIDTH), rows(DN_VW), rows(SB_WIDTH), rows(SSM_D_INNER), rows(N_BRANCH * D_MODEL),
                  resident(w_br_all.shape[1]), resident(D_MODEL), rows(D_MODEL),
                  pl.BlockSpec((1, D_MODEL), lambda i: (0, 0))],
        out_specs=rows(D_MODEL),
        out_shape=jax.ShapeDtypeStruct((m, D_MODEL), F32),
        compiler_params=_cparams(("parallel",)),
        name="mixout",
    )(o_pool, o_dn, o_sb, o_ss, gates, w_br_all, w_out_all, x2, g)


def _mlp_kernel(x_ref, g1_ref, wu_ref, wd_ref, g2_ref, o_ref, h_ref, acc_ref, *, tm):
    f = pl.program_id(1)

    @pl.when(f == 0)
    def _():
        _norm_rows_to(h_ref, x_ref, g1_ref, tm)
        acc_ref[...] = jnp.zeros(acc_ref.shape, F32)

    a = jnp.dot(h_ref[...], wu_ref[...], preferred_element_type=F32)
    a = jnp.square(jnp.maximum(a, 0.0)).astype(BF16)
    acc_ref[...] += jnp.dot(a, wd_ref[...], preferred_element_type=F32)

    @pl.when(f == pl.num_programs(1) - 1)
    def _():
        o_ref[...] = x_ref[...] + _rms_rows(acc_ref[...], g2_ref[...])


def _mlp(x2, g1, wu_all, wd_all, layer, g2, *, tm, tf=1024):
    m = x2.shape[0]
    return pl.pallas_call(
        functools.partial(_mlp_kernel, tm=tm),
        grid=(m // tm, D_FF // tf),
        in_specs=[
            pl.BlockSpec((tm, D_MODEL), lambda i, f: (i, 0), pipeline_mode=pl.Buffered(1)),
            pl.BlockSpec((1, D_MODEL), lambda i, f: (0, 0)),
            pl.BlockSpec((None, D_MODEL, tf), lambda i, f: (layer, 0, f)),
            pl.BlockSpec((None, tf, D_MODEL), lambda i, f: (layer, f, 0)),
            pl.BlockSpec((1, D_MODEL), lambda i, f: (0, 0)),
        ],
        out_specs=pl.BlockSpec((tm, D_MODEL), lambda i, f: (i, 0)),
        out_shape=jax.ShapeDtypeStruct((m, D_MODEL), F32),
        scratch_shapes=[pltpu.VMEM((tm, D_MODEL), BF16), pltpu.VMEM((tm, D_MODEL), F32)],
        compiler_params=_cparams(("parallel", "arbitrary")),
        name="mlp",
    )(x2, g1, wu_all, wd_all, g2)


def _lane_rows(vals, offset):
    out = jnp.zeros((vals.shape[0], LANES), F32)
    for k in range(3):
        o = offset + k * SM_REP
        out = out.at[:, o:o + vals.shape[1]].set(vals.astype(F32))
    return out


def _prepare_params(p):
    depth = p["w_in"].shape[0]
    wt = jnp.transpose(p["w_in"], (0, 2, 1))
    small = wt[:, _O_DNB:_O_SB], wt[:, _O_SSDT:_O_GATE]
    gap = jnp.zeros((depth, SM_REP - SM_GROUP, D_MODEL), wt.dtype)
    w_u = jnp.concatenate(
        [wt[:, :_O_DNB], wt[:, _O_SB:_O_SSDT], *small, gap, *small, gap, *small,
         jnp.zeros((depth, NU - C_SMALL - 2 * SM_REP - SM_GROUP, D_MODEL), wt.dtype)], axis=1).astype(BF16)
    w_g = wt[:, _O_GATE:_O_END].astype(BF16)
    zrow = jnp.zeros((depth, LANES), F32)
    par = lambda a_log, dt_bias, off: jnp.stack(
        [_lane_rows(a_log, off), _lane_rows(dt_bias, off)] + [zrow] * (SUBLANES - 2), axis=1)
    return dict(
        w_u=w_u, w_g=w_g,
        n_mix_pre=p["norm_mix_pre"], n_mix_post=p["norm_mix_post"],
        n_mlp_pre=p["norm_mlp_pre"], n_mlp_post=p["norm_mlp_post"],
        pool_w=p["pool_w"].astype(BF16), pool_scale=p["pool_scale"],
        dn_conv_w=p["dn_conv_w"], dn_par=par(p["dn_a_log"], p["dn_dt_bias"], SM_DECAY), dn_norm_w=p["dn_norm_w"],
        sb_bias=p["sb_bias"],
        ssm_conv_w=p["ssm_conv_w"], ssm_conv_b=p["ssm_conv_b"],
        ss_par=par(p["ssm_a_log"], p["ssm_dt_bias"], SM_DT), ssm_norm_w=p["ssm_norm_w"],
        ssm_d=jnp.repeat(p["ssm_d"].astype(F32), SSM_HEAD_DIM, axis=1),
        w_branch=p["w_branch"].astype(BF16), w_out=p["w_out"].astype(BF16),
        w_up=p["w_up"].astype(BF16), w_down=p["w_down"].astype(BF16),
    )


def _trunk_layer(x, states, pw, layer, *, past=None):
    b, l, _ = x.shape
    m = b * l
    pool_buf, dn_conv, dn_s, ssm_conv, ssm_h = states
    prompt = past is None
    tm_big = min(m, 1024)
    tm = min(m, 512)
    lc = min(l, 256)
    lp = max(lc, 2 * DN_CHUNK)
    row = lambda name: pw[name][layer].reshape(1, -1)

    x2 = x.reshape(m, D_MODEL)
    u2 = _inproj(x2, row("n_mix_pre"), pw["w_u"], layer, tm=tm_big, tn=NU // 4)
    gates = _inproj(x2, row("n_mix_pre"), pw["w_g"], layer, tm=tm_big, tn=D_MODEL, gate=True)
    u3 = u2.reshape(b, l, NU)

    pos0 = 0 if prompt else past[2].shape[1] * (past[0].shape[2] // SB_HEADS)
    o_pool, pool_new = _pool(u3, pool_buf, pw["pool_w"][layer], row("pool_scale"), pos0=pos0)
    o_dn, dn_conv_new, dn_s_new = _deltanet(u3, dn_conv, dn_s, pw["dn_conv_w"][layer], pw["dn_par"][layer],
                                            row("dn_norm_w"), lc=lc, lp=lp)
    o_ss, ss_conv_new, ss_h_new = _ssd(u3, ssm_conv, ssm_h, pw["ssm_conv_w"][layer], row("ssm_conv_b"),
                                       pw["ss_par"][layer], row("ssm_norm_w"), row("ssm_d"), lc=lc, lp=lp)

    k_rows, v_rows = _kv_rows(u3)
    sb_bias = pw["sb_bias"][layer]
    if prompt:
        o_sb = _sb_prompt(u3, sb_bias)
    else:
        cache_k, cache_v, page_table = past
        ncol = cache_k.shape[2]
        q = u3[:, :, C_SBQ:C_SBQ + SB_WIDTH].reshape(b, l, SB_HEADS, SB_HEAD_DIM)
        q_rows = jnp.transpose(q, (0, 2, 1, 3)).reshape(b, SB_HEADS * l, SB_HEAD_DIM)
        k_cur = jnp.pad(k_rows, ((0, 0), (0, ncol - l * SB_HEADS), (0, 0)))
        v_cur = jnp.pad(v_rows, ((0, 0), (0, ncol - l * SB_HEADS), (0, 0)))
        bias_rows = jnp.broadcast_to(jnp.repeat(sb_bias, l)[:, None], (SB_HEADS * l, ncol)).astype(F32)
        acc = _sb_sample(q_rows, k_cur, v_cur, bias_rows, cache_k, cache_v, page_table, layer)
        o_sb = jnp.transpose(acc.reshape(b, SB_HEADS, l, SB_HEAD_DIM), (0, 2, 1, 3)).reshape(b, l, SB_WIDTH)

    x2 = _mixout(o_pool.reshape(m, -1), o_dn.reshape(m, -1), o_sb.reshape(m, -1), o_ss.reshape(m, -1),
                 gates, pw["w_branch"], pw["w_out"], layer, x2, row("n_mix_post"), tm=min(m, 256))
    x2 = _mlp(x2, row("n_mlp_pre"), pw["w_up"], pw["w_down"], layer, row("n_mlp_post"), tm=tm_big, tf=512)
    new_states = (k_rows.reshape(b, l, SB_HEADS, SB_HEAD_DIM), v_rows.reshape(b, l, SB_HEADS, SB_HEAD_DIM),
                  pool_new, dn_conv_new, dn_s_new, ss_conv_new, ss_h_new)
    return x2.reshape(b, l, D_MODEL), new_states


def kernel(x_prompt, x_sample, cache_sb_k, cache_sb_v, state_pool, state_dn_conv, state_dn_s, state_ssm_conv, state_ssm_h, page_table, norm_mix_pre, norm_mix_post, norm_mlp_pre, norm_mlp_post, w_in, pool_w, pool_scale, dn_conv_w, dn_a_log, dn_dt_bias, dn_norm_w, sb_bias, ssm_conv_w, ssm_conv_b, ssm_a_log, ssm_dt_bias, ssm_d, ssm_norm_w, w_branch, w_out, w_up, w_down):
    pw = _prepare_params(dict(
        norm_mix_pre=norm_mix_pre, norm_mix_post=norm_mix_post, norm_mlp_pre=norm_mlp_pre,
        norm_mlp_post=norm_mlp_post, w_in=w_in, pool_w=pool_w, pool_scale=pool_scale,
        dn_conv_w=dn_conv_w, dn_a_log=dn_a_log, dn_dt_bias=dn_dt_bias, dn_norm_w=dn_norm_w,
        sb_bias=sb_bias, ssm_conv_w=ssm_conv_w, ssm_conv_b=ssm_conv_b, ssm_a_log=ssm_a_log,
        ssm_dt_bias=ssm_dt_bias, ssm_d=ssm_d, ssm_norm_w=ssm_norm_w, w_branch=w_branch,
        w_out=w_out, w_up=w_up, w_down=w_down))
    depth = w_in.shape[0]
    bp = x_prompt.shape[0]
    dt_ = x_prompt.dtype
    zero_states = (jnp.zeros((bp, POOL_BUF, POOL_WIDTH), dt_),
                   jnp.zeros((bp, CONV_WIDTH - 1, DN_CONV_DIM), dt_),
                   jnp.zeros((bp, DN_HEADS, DN_DK, DN_DV), dt_),
                   jnp.zeros((bp, CONV_WIDTH - 1, SSM_CONV_DIM), dt_),
                   jnp.zeros((bp, SSM_HEADS, SSM_HEAD_DIM, SSM_STATE), dt_))
    n_pool, page = cache_sb_k.shape[1], cache_sb_k.shape[2]
    cache_k = cache_sb_k.reshape(depth, n_pool, page * SB_HEADS, SB_HEAD_DIM)
    cache_v = cache_sb_v.reshape(depth, n_pool, page * SB_HEADS, SB_HEAD_DIM)
    y_prompt, y_sample = x_prompt, x_sample
    new_p, new_s = [], []
    for layer in range(depth):
        y_prompt, st_p = _trunk_layer(y_prompt, zero_states, pw, layer)
        sample_states = (state_pool[layer], state_dn_conv[layer], state_dn_s[layer],
                         state_ssm_conv[layer], state_ssm_h[layer])
        y_sample, st_s = _trunk_layer(y_sample, sample_states, pw, layer, past=(cache_k, cache_v, page_table))
        new_p.append(st_p)
        new_s.append(st_s)
    outs_p = [jnp.stack(t) for t in zip(*new_p)]
    outs_s = [jnp.stack(t) for t in zip(*new_s)]
    return (y_prompt, y_sample, *outs_p, *outs_s)
```
